```python
import math
import jax, jax.numpy as jnp
from jax import lax
import numpy as np

D_MODEL = 1024
BATCH = 2
SEQ = 8192
DEPTH = 4
DEC_BATCH = 128
DEC_SEQ = 4
PAST_LEN = 8192
PAGE_SIZE = 128

A_HEAD_DIM = 64
A_WIDTH = D_MODEL // 2
A_HEADS = A_WIDTH // A_HEAD_DIM
A_GROUPS = 2
A_HPG = A_HEADS // A_GROUPS
A_STATE = 64
A_CONV = 4
A_CONV_DIM = A_WIDTH + 2 * A_GROUPS * A_STATE
A_CHUNK = 128
B_CH = 16
B_WIDTH = D_MODEL // 2
B_GROUPS = B_WIDTH // B_CH
B_STATE = 64
C_HEAD_DIM = 64
C_WIDTH = D_MODEL // 2
C_HEADS = C_WIDTH // C_HEAD_DIM
C_KV_HEADS = 2
C_GQA = C_HEADS // C_KV_HEADS
C_KV_WIDTH = C_KV_HEADS * C_HEAD_DIM
WINDOW = 128
REL_BUCKETS = 32
REL_MAX_DIST = 128
D_FF = -(-8 * D_MODEL // (3 * 256)) * 256
N_BRANCH = 3
EPS = 1e-6
OFF_Z = 0
OFF_XBC = OFF_Z + A_WIDTH
OFF_DT = OFF_XBC + A_CONV_DIM
OFF_U = OFF_DT + A_HEADS
OFF_Q = OFF_U + B_WIDTH
OFF_K = OFF_Q + C_WIDTH
OFF_V = OFF_K + C_KV_WIDTH
OFF_GATE = OFF_V + C_KV_WIDTH
IN_COLS = OFF_GATE + N_BRANCH * D_MODEL

kernel_name = 'hybrid_ssd_s5_swa_step'


def rmsnorm(x, w):
    xf = x.astype(jnp.float32)
    y = xf * lax.rsqrt(jnp.mean(xf * xf, axis=-1, keepdims=True) + EPS)
    return (y * w.astype(jnp.float32)).astype(x.dtype)


def causal_conv(xbc, buf, w, b):
    full = jnp.concatenate([buf.astype(xbc.dtype), xbc], axis=1)
    out = lax.conv_general_dilated(full, w[:, None, :].astype(xbc.dtype), window_strides=(1,), padding='VALID', dimension_numbers=('NWC', 'WIO', 'NWC'), feature_group_count=xbc.shape[-1])
    return out + b.astype(xbc.dtype), full[:, full.shape[1] - (A_CONV - 1):]


def ssd_scan(x, dt, a, bm, cm, h0):
    bsz, l, g, k, p = x.shape
    n = bm.shape[-1]
    q = A_CHUNK if l % A_CHUNK == 0 else l
    c = l // q
    xs = (x * dt[..., None]).reshape(bsz, c, q, g, k, p)
    a_cum = jnp.cumsum((dt * a).reshape(bsz, c, q, g, k), axis=2)
    bc = bm.reshape(bsz, c, q, g, n)
    cc = cm.reshape(bsz, c, q, g, n)
    seg = a_cum[:, :, :, None] - a_cum[:, :, None, :]
    tri = jnp.tril(jnp.ones((q, q), dtype=bool))[:, :, None, None]
    decay_in = jnp.exp(jnp.where(tri, seg, -jnp.inf))
    cb = jnp.einsum('bcqgn,bcsgn->bcqsg', cc, bc)
    y_diag = jnp.einsum('bcqsgk,bcsgkp->bcqgkp', cb[..., None] * decay_in, xs)
    decay_end = jnp.exp(a_cum[:, :, -1:] - a_cum)
    chunk_states = jnp.einsum('bcsgn,bcsgkp->bcgkpn', bc, xs * decay_end[..., None])
    chunk_decay = jnp.exp(a_cum[:, :, -1])

    def step(h, inp):
        dec, st = inp
        return dec[..., None, None] * h + st, h

    h_last, h_prev = lax.scan(step, h0, (jnp.moveaxis(chunk_decay, 1, 0), jnp.moveaxis(chunk_states, 1, 0)))
    h_prev = jnp.moveaxis(h_prev, 0, 1)
    y_off = jnp.einsum('bcqgn,bcgkpn->bcqgkp', cc, h_prev) * jnp.exp(a_cum)[..., None]
    return (y_diag + y_off).reshape(bsz, l, g, k, p), h_last


def ssd_mixer(z, xbc, dt_raw, conv_buf, h0, conv_w, conv_b, a_log, dt_bias, d, norm_w):
    bsz, l = z.shape[:2]
    xbc_c, new_buf = causal_conv(xbc, conv_buf, conv_w, conv_b)
    xbc_c = jax.nn.silu(xbc_c.astype(jnp.float32))
    x = xbc_c[..., :A_WIDTH].reshape(bsz, l, A_GROUPS, A_HPG, A_HEAD_DIM)
    bm = xbc_c[..., A_WIDTH:A_WIDTH + A_GROUPS * A_STATE].reshape(bsz, l, A_GROUPS, A_STATE)
    cm = xbc_c[..., A_WIDTH + A_GROUPS * A_STATE:].reshape(bsz, l, A_GROUPS, A_STATE)
    dt = jax.nn.softplus(dt_raw.astype(jnp.float32) + dt_bias.astype(jnp.float32)).reshape(bsz, l, A_GROUPS, A_HPG)
    a = -jnp.exp(a_log.astype(jnp.float32)).reshape(A_GROUPS, A_HPG)
    h0 = h0.astype(jnp.float32).reshape(bsz, A_GROUPS, A_HPG, A_HEAD_DIM, A_STATE)
    y, h = ssd_scan(x, dt, a, bm, cm, h0)
    y = y + d.astype(jnp.float32).reshape(A_GROUPS, A_HPG)[..., None] * x
    y = y.reshape(bsz, l, A_WIDTH) * jax.nn.silu(z.astype(jnp.float32))
    return rmsnorm(y, norm_w), new_buf, h.reshape(bsz, A_HEADS, A_HEAD_DIM, A_STATE)


def lin_combine(e1, e2):
    a1, b1 = e1
    a2, b2 = e2
    return a1 * a2, a2 * b1 + b2


def s5_mixer(u, h_re, h_im, lam_re, lam_im, log_dt, b_re, b_im, c_re, c_im, d, w_glu):
    bsz, l = u.shape[:2]
    uf = u.astype(jnp.float32).reshape(bsz, l, B_GROUPS, B_CH)
    lam = lax.complex(lam_re.astype(jnp.float32), lam_im.astype(jnp.float32))
    step = jnp.exp(log_dt.astype(jnp.float32))[:, None]
    a_bar = jnp.exp(lam * step)
    b_bar = ((a_bar - 1.0) / lam)[:, :, None] * lax.complex(b_re.astype(jnp.float32), b_im.astype(jnp.float32))
    c_mat = lax.complex(c_re.astype(jnp.float32), c_im.astype(jnp.float32))
    h0 = lax.complex(h_re.astype(jnp.float32), h_im.astype(jnp.float32))
    bu = jnp.einsum('gph,blgh->blgp', b_bar, uf.astype(jnp.complex64))
    a_seq = jnp.broadcast_to(a_bar, bu.shape)
    a_cum, h = lax.associative_scan(lin_combine, (a_seq, bu), axis=1)
    h = h + a_cum * h0[:, None]
    y = jnp.real(jnp.einsum('ghp,blgp->blgh', c_mat, h)) + d.astype(jnp.float32) * uf
    g = jax.nn.gelu(y.reshape(bsz, l, B_WIDTH))
    gv = g @ w_glu.astype(jnp.float32)
    return gv[..., :B_WIDTH] * jax.nn.sigmoid(gv[..., B_WIDTH:]), h[:, -1]


def rel_bucket(n):
    n = jnp.maximum(n, 0)
    exact = REL_BUCKETS // 2
    nf = jnp.maximum(n, 1).astype(jnp.float32)
    large = exact + (jnp.log(nf / exact) / math.log(REL_MAX_DIST / exact) * (REL_BUCKETS - exact)).astype(jnp.int32)
    large = jnp.minimum(large, REL_BUCKETS - 1)
    return jnp.where(n < exact, n, large)


def rel_bias_for(n, table):
    bias = table[rel_bucket(n)].astype(jnp.float32)
    return jnp.moveaxis(bias, -1, 0).reshape(C_KV_HEADS, C_GQA, *n.shape)


def sink_softmax(s, sink):
    sk = sink[:, :, None, None]
    m = jnp.maximum(jnp.max(s, axis=-1, keepdims=True), sk)
    p = jnp.exp(s - m)
    return p / (jnp.sum(p, axis=-1, keepdims=True) + jnp.exp(sk - m))


def attn_prompt(q, k, v, table, sinks):
    bsz, l = q.shape[:2]
    nb = l // WINDOW
    qb = q.astype(jnp.float32).reshape(bsz, nb, WINDOW, C_KV_HEADS, C_GQA, C_HEAD_DIM)

    def band(t):
        tb = t.astype(jnp.float32).reshape(bsz, nb, WINDOW, C_KV_HEADS, C_HEAD_DIM)
        prev = jnp.pad(tb[:, :-1], ((0, 0), (1, 0), (0, 0), (0, 0), (0, 0)))
        return jnp.concatenate([prev, tb], axis=2)

    kb, vb = band(k), band(v)
    qi = jnp.arange(WINDOW)[:, None]
    kj = jnp.arange(2 * WINDOW)[None, :]
    dist = qi + WINDOW - kj
    valid = (dist >= 0) & (dist < WINDOW)
    first = (jnp.arange(nb)[:, None, None] == 0) & (kj[None] < WINDOW)
    mask = valid[None] & jnp.logical_not(first)
    s = jnp.einsum('bnqhgd,bnkhd->bnhgqk', qb, kb) * (C_HEAD_DIM ** -0.5) + rel_bias_for(dist, table)
    s = jnp.where(mask[None, :, None, None], s, -jnp.inf)
    p = sink_softmax(s, sinks.astype(jnp.float32).reshape(C_KV_HEADS, C_GQA))
    o = jnp.einsum('bnhgqk,bnkhd->bnqhgd', p, vb)
    return o.reshape(bsz, l, C_WIDTH)


def attn_sample(q, k, v, k_buf, v_buf, table, sinks):
    bsz, t = q.shape[:2]
    lb = k_buf.shape[1]
    kk = jnp.concatenate([k_buf.astype(jnp.float32), k.astype(jnp.float32)], axis=1)
    vv = jnp.concatenate([v_buf.astype(jnp.float32), v.astype(jnp.float32)], axis=1)
    dist = (lb + jnp.arange(t))[:, None] - jnp.arange(lb + t)[None, :]
    mask = (dist >= 0) & (dist < WINDOW)
    s = jnp.einsum('bqhgd,bkhd->bhgqk', q.astype(jnp.float32), kk) * (C_HEAD_DIM ** -0.5) + rel_bias_for(dist, table)
    s = jnp.where(mask, s, -jnp.inf)
    p = sink_softmax(s, sinks.astype(jnp.float32).reshape(C_KV_HEADS, C_GQA))
    o = jnp.einsum('bhgqk,bkhd->bqhgd', p, vv)
    return o.reshape(bsz, t, C_WIDTH), kk[:, kk.shape[1] - lb:], vv[:, vv.shape[1] - lb:]


def block(x, ssd_h, conv_buf, s5_re, s5_im, k_buf, v_buf,
          norm1_w, w_in, conv_w, conv_b, ssd_a_log, ssd_dt_bias, ssd_d, ssd_norm_w,
          s5_lam_re, s5_lam_im, s5_log_dt, s5_b_re, s5_b_im, s5_c_re, s5_c_im, s5_d, s5_w_glu,
          attn_sinks, w_br_a, w_br_b, w_br_c, w_out, norm2_w, w_ffn_up, w_ffn_down, rel_bias):
    bsz, l, _ = x.shape
    h = rmsnorm(x, norm1_w)
    proj = h @ w_in
    y_a, conv_new, ssd_new = ssd_mixer(proj[..., OFF_Z:OFF_XBC], proj[..., OFF_XBC:OFF_DT], proj[..., OFF_DT:OFF_U], conv_buf, ssd_h, conv_w, conv_b, ssd_a_log, ssd_dt_bias, ssd_d, ssd_norm_w)
    y_b, s5_new = s5_mixer(proj[..., OFF_U:OFF_Q], s5_re, s5_im, s5_lam_re, s5_lam_im, s5_log_dt, s5_b_re, s5_b_im, s5_c_re, s5_c_im, s5_d, s5_w_glu)
    q = proj[..., OFF_Q:OFF_K].reshape(bsz, l, C_KV_HEADS, C_GQA, C_HEAD_DIM)
    k = proj[..., OFF_K:OFF_V].reshape(bsz, l, C_KV_HEADS, C_HEAD_DIM)
    v = proj[..., OFF_V:OFF_GATE].reshape(bsz, l, C_KV_HEADS, C_HEAD_DIM)
    if k_buf is None:
        y_c = attn_prompt(q, k, v, rel_bias, attn_sinks)
        keep = min(WINDOW, l)
        k_new, v_new = k[:, l - keep:], v[:, l - keep:]
    else:
        y_c, k_new, v_new = attn_sample(q, k, v, k_buf, v_buf, rel_bias, attn_sinks)
    gates = jax.nn.sigmoid(proj[..., OFF_GATE:].astype(jnp.float32)).reshape(bsz, l, N_BRANCH, D_MODEL)
    mixed = gates[:, :, 0] * (y_a @ w_br_a) + gates[:, :, 1] * (y_b @ w_br_b) + gates[:, :, 2] * (y_c @ w_br_c)
    x = x + (mixed @ w_out).astype(x.dtype)
    h2 = rmsnorm(x, norm2_w)
    gu = h2 @ w_ffn_up
    x = x + ((jax.nn.silu(gu[..., :D_FF]) * gu[..., D_FF:]) @ w_ffn_down).astype(x.dtype)
    sd = x.dtype
    return x, (ssd_new.astype(sd), conv_new.astype(sd), jnp.real(s5_new).astype(sd), jnp.imag(s5_new).astype(sd), k_new.astype(sd), v_new.astype(sd))


def setup_inputs(seed: int = 0) -> dict:
    key = jax.random.key(seed)
    keys = iter(jax.random.split(key, 48))
    f32 = jnp.float32

    def nrm(shape, scale):
        return scale * jax.random.normal(next(keys), shape, f32)

    def unif(shape, lo, hi):
        return jax.random.uniform(next(keys), shape, f32, lo, hi)

    win_buf = min(WINDOW, PAST_LEN)
    x_prompt = nrm((BATCH, SEQ, D_MODEL), 1.0)
    x_sample = nrm((DEC_BATCH, DEC_SEQ, D_MODEL), 1.0)
    state_ssd = nrm((DEPTH, DEC_BATCH, A_HEADS, A_HEAD_DIM, A_STATE), 0.5)
    state_conv = nrm((DEPTH, DEC_BATCH, A_CONV - 1, A_CONV_DIM), 1.0)
    state_s5_re = nrm((DEPTH, DEC_BATCH, B_GROUPS, B_STATE), 0.3)
    state_s5_im = nrm((DEPTH, DEC_BATCH, B_GROUPS, B_STATE), 0.3)
    cache_k = nrm((DEPTH, DEC_BATCH, win_buf, C_KV_HEADS, C_HEAD_DIM), 1.0)
    cache_v = nrm((DEPTH, DEC_BATCH, win_buf, C_KV_HEADS, C_HEAD_DIM), 1.0)
    norm1_w = 1.0 + nrm((DEPTH, D_MODEL), 0.02)
    w_in = nrm((DEPTH, D_MODEL, IN_COLS), D_MODEL ** -0.5)
    conv_w = nrm((DEPTH, A_CONV, A_CONV_DIM), A_CONV ** -0.5)
    conv_b = nrm((DEPTH, A_CONV_DIM), 0.02)
    ssd_a_log = jnp.log(unif((DEPTH, A_HEADS), 1.0, 16.0))
    dt_init = jnp.exp(unif((DEPTH, A_HEADS), math.log(1e-3), math.log(1e-1)))
    ssd_dt_bias = dt_init + jnp.log(-jnp.expm1(-dt_init))
    ssd_d = 1.0 + nrm((DEPTH, A_HEADS), 0.02)
    ssd_norm_w = 1.0 + nrm((DEPTH, A_WIDTH), 0.02)
    s5_lam_re = -0.5 + nrm((DEPTH, B_GROUPS, B_STATE), 0.01)
    s5_lam_im = jnp.pi * jnp.arange(B_STATE, dtype=f32) + nrm((DEPTH, B_GROUPS, B_STATE), 0.01)
    s5_log_dt = unif((DEPTH, B_GROUPS), math.log(1e-3), math.log(1e-1))
    s5_b_re = nrm((DEPTH, B_GROUPS, B_STATE, B_CH), (2 * B_CH) ** -0.5)
    s5_b_im = nrm((DEPTH, B_GROUPS, B_STATE, B_CH), (2 * B_CH) ** -0.5)
    s5_c_re = nrm((DEPTH, B_GROUPS, B_CH, B_STATE), B_STATE ** -0.5)
    s5_c_im = nrm((DEPTH, B_GROUPS, B_CH, B_STATE), B_STATE ** -0.5)
    s5_d = nrm((DEPTH, B_GROUPS, B_CH), 1.0)
    s5_w_glu = nrm((DEPTH, B_WIDTH, 2 * B_WIDTH), B_WIDTH ** -0.5)
    attn_sinks = nrm((DEPTH, C_HEADS), 0.5)
    w_br_a = nrm((DEPTH, A_WIDTH, D_MODEL), A_WIDTH ** -0.5)
    w_br_b = nrm((DEPTH, B_WIDTH, D_MODEL), B_WIDTH ** -0.5)
    w_br_c = nrm((DEPTH, C_WIDTH, D_MODEL), C_WIDTH ** -0.5)
    w_out = nrm((DEPTH, D_MODEL, D_MODEL), D_MODEL ** -0.5)
    norm2_w = 1.0 + nrm((DEPTH, D_MODEL), 0.02)
    w_ffn_up = nrm((DEPTH, D_MODEL, 2 * D_FF), D_MODEL ** -0.5)
    w_ffn_down = nrm((DEPTH, D_FF, D_MODEL), D_FF ** -0.5)
    rel_bias = nrm((REL_BUCKETS, C_HEADS), 0.5)
    final_norm_w = 1.0 + nrm((D_MODEL,), 0.02)
    return {'x_prompt': x_prompt, 'x_sample': x_sample, 'state_ssd': state_ssd, 'state_conv': state_conv,
            'state_s5_re': state_s5_re, 'state_s5_im': state_s5_im, 'cache_k': cache_k, 'cache_v': cache_v,
            'norm1_w': norm1_w, 'w_in': w_in, 'conv_w': conv_w, 'conv_b': conv_b, 'ssd_a_log': ssd_a_log,
            'ssd_dt_bias': ssd_dt_bias, 'ssd_d': ssd_d, 'ssd_norm_w': ssd_norm_w, 's5_lam_re': s5_lam_re,
            's5_lam_im': s5_lam_im, 's5_log_dt': s5_log_dt, 's5_b_re': s5_b_re, 's5_b_im': s5_b_im,
            's5_c_re': s5_c_re, 's5_c_im': s5_c_im, 's5_d': s5_d, 's5_w_glu': s5_w_glu, 'attn_sinks': attn_sinks,
            'w_br_a': w_br_a, 'w_br_b': w_br_b, 'w_br_c': w_br_c, 'w_out': w_out, 'norm2_w': norm2_w,
            'w_ffn_up': w_ffn_up, 'w_ffn_down': w_ffn_down, 'rel_bias': rel_bias, 'final_norm_w': final_norm_w}


def reference(x_prompt, x_sample, state_ssd, state_conv, state_s5_re, state_s5_im, cache_k, cache_v,
              norm1_w, w_in, conv_w, conv_b, ssd_a_log, ssd_dt_bias, ssd_d, ssd_norm_w,
              s5_lam_re, s5_lam_im, s5_log_dt, s5_b_re, s5_b_im, s5_c_re, s5_c_im, s5_d, s5_w_glu,
              attn_sinks, w_br_a, w_br_b, w_br_c, w_out, norm2_w, w_ffn_up, w_ffn_down, rel_bias, final_norm_w):
    bp = x_prompt.shape[0]
    sd = x_prompt.dtype
    zero_ssd = jnp.zeros((bp, A_HEADS, A_HEAD_DIM, A_STATE), sd)
    zero_conv = jnp.zeros((bp, A_CONV - 1, A_CONV_DIM), sd)
    zero_s5 = jnp.zeros((bp, B_GROUPS, B_STATE), sd)
    xp, xs = x_prompt, x_sample
    new_p, new_s = [], []
    for i in range(DEPTH):
        lw = (norm1_w[i], w_in[i], conv_w[i], conv_b[i], ssd_a_log[i], ssd_dt_bias[i], ssd_d[i], ssd_norm_w[i],
              s5_lam_re[i], s5_lam_im[i], s5_log_dt[i], s5_b_re[i], s5_b_im[i], s5_c_re[i], s5_c_im[i], s5_d[i], s5_w_glu[i],
              attn_sinks[i], w_br_a[i], w_br_b[i], w_br_c[i], w_out[i], norm2_w[i], w_ffn_up[i], w_ffn_down[i])
        xp, st_p = block(xp, zero_ssd, zero_conv, zero_s5, zero_s5, None, None, *lw, rel_bias)
        xs, st_s = block(xs, state_ssd[i], state_conv[i], state_s5_re[i], state_s5_im[i], cache_k[i], cache_v[i], *lw, rel_bias)
        new_p.append(st_p)
        new_s.append(st_s)

    def stack(states, j):
        return jnp.stack([s[j] for s in states], axis=0)

    y_prompt = rmsnorm(xp, final_norm_w)
    y_sample = rmsnorm(xs, final_norm_w)
    return (y_prompt, y_sample,
            stack(new_p, 0), stack(new_p, 1), stack(new_p, 2), stack(new_p, 3), stack(new_p, 4), stack(new_p, 5),
            stack(new_s, 0), stack(new_s, 1), stack(new_s, 2), stack(new_s, 3), stack(new_s, 4), stack(new_s, 5))
```

```python
import functools
import math

import numpy as np
import jax
import jax.numpy as jnp
from jax import lax
from jax.experimental import pallas as pl
from jax.experimental.pallas import tpu as pltpu

F32 = jnp.float32
BF16 = jnp.bfloat16
HIGHEST = lax.Precision.HIGHEST

D_MODEL = 1024
A_HEAD_DIM = 64
A_WIDTH = 512
A_HEADS = 8
A_GROUPS = 2
A_STATE = 64
A_CONV = 4
A_CONV_DIM = 768
A_CHUNK = 128
B_CH = 16
B_WIDTH = 512
B_GROUPS = 32
B_STATE = 64
C_HEAD_DIM = 64
C_WIDTH = 512
C_HEADS = 8
C_KV_HEADS = 2
C_GQA = 4
C_KV_WIDTH = 128
WINDOW = 128
REL_BUCKETS = 32
REL_MAX_DIST = 128
D_FF = 2816
EPS = 1e-6

P_Z = 0
P_XBC = P_Z + A_WIDTH
P_U = P_XBC + A_CONV_DIM
P_Q = P_U + B_WIDTH
P_K = P_Q + C_WIDTH
P_V = P_K + C_KV_WIDTH
P_DT = P_V + C_KV_WIDTH
P_COLS = P_DT + 128

LANE = 128
SUBLANE = 8
S5_Q = 8
S5_OCT = 8
N_OCT = B_GROUPS // S5_OCT
S5_STATE_COLS = B_GROUPS * B_STATE
SAMPLE_ROWS = 8
NEG = -1e30
VMEM_LIMIT = 56 * 1024 * 1024
FF_CHUNK = 256


def _cparams(*sem):
    return pltpu.CompilerParams(dimension_semantics=sem, vmem_limit_bytes=VMEM_LIMIT)


def _rms(x, w):
    return x * lax.rsqrt(jnp.mean(x * x, axis=-1, keepdims=True) + EPS) * w


def _dot(a, b):
    return jnp.dot(a, b, preferred_element_type=F32)


def _dot_nt(a, b, precision=None):
    return lax.dot_general(a, b, (((1,), (1,)), ((), ())), preferred_element_type=F32, precision=precision)


def _dot_tn(a, b):
    return lax.dot_general(a, b, (((0,), (0,)), ((), ())), preferred_element_type=F32)


def _full(shape):
    n = len(shape)
    return pl.BlockSpec(shape, lambda *_: (0,) * n)


def _inproj_body(x_ref, nw_ref, w_ref, z_ref, xbc_ref, u_ref, q_ref, k_ref, v_ref, dt_ref):
    h = _rms(x_ref[...], nw_ref[...]).astype(BF16)

    def proj(lo, hi):
        return _dot(h, w_ref[:, lo:hi])

    z_ref[...] = proj(P_Z, P_XBC).astype(BF16)
    xbc_ref[...] = proj(P_XBC, P_U)
    u_ref[...] = proj(P_U, P_Q).astype(BF16)
    q_ref[...] = proj(P_Q, P_K).astype(BF16)
    k_ref[...] = proj(P_K, P_V)
    v_ref[...] = proj(P_V, P_DT)
    dt_ref[...] = proj(P_DT, P_COLS)


def _inproj(x, nw, w, tm):
    t = x.shape[0]
    widths = [(A_WIDTH, BF16), (A_CONV_DIM, F32), (B_WIDTH, BF16), (C_WIDTH, BF16),
              (C_KV_WIDTH, F32), (C_KV_WIDTH, F32), (LANE, F32)]
    return pl.pallas_call(
        _inproj_body,
        grid=(t // tm,),
        in_specs=[pl.BlockSpec((tm, D_MODEL), lambda i: (i, 0)), _full((1, D_MODEL)), _full((D_MODEL, P_COLS))],
        out_specs=[pl.BlockSpec((tm, wd), lambda i: (i, 0)) for wd, _ in widths],
        out_shape=[jax.ShapeDtypeStruct((t, wd), dt) for wd, dt in widths],
        compiler_params=_cparams("parallel"),
        name="inproj",
    )(x, nw, w)


def _ssd_body(*refs, q, valid, has_init, nchunks):
    if has_init:
        (xbc_ref, z_ref, dt_ref, cw_ref, cb_ref, alog_ref, dtb_ref, dexp_ref, nw_ref, h0_ref, conv0_ref,
         y_ref, hout_ref, h_scr, tail_scr) = refs
    else:
        (xbc_ref, z_ref, dt_ref, cw_ref, cb_ref, alog_ref, dtb_ref, dexp_ref, nw_ref,
         y_ref, hout_ref, h_scr, tail_scr) = refs
    c = pl.program_id(1)

    @pl.when(c == 0)
    def _():
        if has_init:
            h_scr[...] = h0_ref[0]
            tail_scr[...] = conv0_ref[...]
        else:
            h_scr[...] = jnp.zeros_like(h_scr)
            tail_scr[...] = jnp.zeros_like(tail_scr)

    xbc = xbc_ref[...]
    ext = jnp.concatenate([tail_scr[...], xbc], axis=0)
    acc = cb_ref[...] + cw_ref[A_CONV - 1:A_CONV, :] * xbc
    for j in range(1, A_CONV):
        acc = acc + cw_ref[A_CONV - 1 - j:A_CONV - j, :] * pltpu.roll(ext, j, axis=0)[SUBLANE:SUBLANE + q]
    tail_scr[...] = xbc[q - SUBLANE:q]
    xc = jax.nn.silu(acc)
    x = xc[:, :A_WIDTH]
    bm = xc[:, A_WIDTH:A_WIDTH + A_GROUPS * A_STATE].astype(BF16)
    cm = xc[:, A_WIDTH + A_GROUPS * A_STATE:].astype(BF16)

    dt = jax.nn.softplus(dt_ref[...] + dtb_ref[...])
    if valid < q:
        dt = jnp.where(lax.broadcasted_iota(jnp.int32, dt.shape, 0) < valid, dt, 0.0)
    dta = dt * (-jnp.exp(alog_ref[...]))
    row = lax.broadcasted_iota(jnp.int32, (q, q), 0)
    col = lax.broadcasted_iota(jnp.int32, (q, q), 1)
    tri = row >= col
    a_cum = jnp.dot(tri.astype(F32), dta, preferred_element_type=F32, precision=HIGHEST)

    def expand(v):
        return jnp.concatenate([jnp.broadcast_to(v[:, k:k + 1], (q, A_HEAD_DIM)) for k in range(A_HEADS)], axis=1)

    a_cum_e = expand(a_cum)
    xs = x * expand(dt)
    xs_b = xs.astype(BF16)
    a_cum_t = a_cum.T

    h = h_scr[...]
    h_b = h.astype(BF16)
    hpg = A_HEADS // A_GROUPS
    gw = hpg * A_HEAD_DIM
    y_diag, y_off = [], []
    for g in range(A_GROUPS):
        cg = cm[:, g * A_STATE:(g + 1) * A_STATE]
        bg = bm[:, g * A_STATE:(g + 1) * A_STATE]
        cb = _dot_nt(cg, bg)
        for kk in range(hpg):
            k = g * hpg + kk
            seg = a_cum[:, k:k + 1] - a_cum_t[k:k + 1, :]
            decay = jnp.exp(jnp.where(tri, seg, -jnp.inf))
            y_diag.append(_dot((cb * decay).astype(BF16), xs_b[:, k * A_HEAD_DIM:(k + 1) * A_HEAD_DIM]))
        y_off.append(_dot_nt(cg, h_b[g * gw:(g + 1) * gw]))
    y = (jnp.concatenate(y_diag, axis=1) + jnp.concatenate(y_off, axis=1) * jnp.exp(a_cum_e)
         + dexp_ref[...] * x)
    y = y * jax.nn.silu(z_ref[...].astype(F32))
    y_ref[...] = _rms(y, nw_ref[...]).astype(BF16)

    last = a_cum_e[q - 1:q, :]
    xs_dec = (xs * jnp.exp(last - a_cum_e)).astype(BF16)
    dec_col = jnp.exp(a_cum_e.T[:, q - 1:q])
    upd = [_dot_tn(xs_dec[:, g * gw:(g + 1) * gw], bm[:, g * A_STATE:(g + 1) * A_STATE]) for g in range(A_GROUPS)]
    h_new = dec_col * h + jnp.concatenate(upd, axis=0)
    h_scr[...] = h_new

    @pl.when(c == nchunks - 1)
    def _():
        hout_ref[0] = h_new


def _ssd(xbc, z, dt, lw, nb, nchunks, q, valid, h0=None, conv0=None):
    has_init = h0 is not None
    rows = nb * nchunks * q
    rmap = lambda b, c: (b * nchunks + c, 0)
    in_specs = [pl.BlockSpec((q, A_CONV_DIM), rmap), pl.BlockSpec((q, A_WIDTH), rmap), pl.BlockSpec((q, LANE), rmap),
                _full((A_CONV, A_CONV_DIM)), _full((1, A_CONV_DIM)), _full((1, LANE)), _full((1, LANE)),
                _full((1, A_WIDTH)), _full((1, A_WIDTH))]
    args = [xbc, z, dt, lw["conv_w"], lw["conv_b"], lw["a_log"], lw["dt_bias"], lw["d_exp"], lw["ssd_norm_w"]]
    if has_init:
        in_specs += [pl.BlockSpec((1, A_WIDTH, A_STATE), lambda b, c: (b, 0, 0)),
                     pl.BlockSpec((SUBLANE, A_CONV_DIM), lambda b, c: (b, 0))]
        args += [h0, conv0]
    return pl.pallas_call(
        functools.partial(_ssd_body, q=q, valid=valid, has_init=has_init, nchunks=nchunks),
        grid=(nb, nchunks),
        in_specs=in_specs,
        out_specs=[pl.BlockSpec((q, A_WIDTH), rmap), pl.BlockSpec((1, A_WIDTH, A_STATE), lambda b, c: (b, 0, 0))],
        out_shape=[jax.ShapeDtypeStruct((rows, A_WIDTH), BF16), jax.ShapeDtypeStruct((nb, A_WIDTH, A_STATE), F32)],
        scratch_shapes=[pltpu.VMEM((A_WIDTH, A_STATE), F32), pltpu.VMEM((SUBLANE, A_CONV_DIM), F32)],
        compiler_params=_cparams("parallel", "arbitrary"),
        name="ssd",
    )(*args)


def _s5w_body(lre_ref, lim_ref, ldt_ref, btr_ref, bti_ref, cr_ref, ci_ref,
              bst_ref, bsts_ref, cst_ref, kd_ref, ap_ref, cd_scr, bb_scr, *, valid):
    h = B_STATE
    lr, li = lre_ref[0], lim_ref[0]
    step = jnp.exp(ldt_ref[0])
    d = lax.broadcasted_iota(jnp.int32, (2 * SUBLANE, h), 0).astype(F32)
    er = jnp.exp(lr * step * d)
    ang = li * step * d
    ar, ai = er * jnp.cos(ang), er * jnp.sin(ang)
    nr, ni = ar[1:2] - 1.0, ai[1:2]
    den = lr * lr + li * li
    wr, wi = (nr * lr + ni * li) / den, (ni * lr - nr * li) / den
    btr, bti = btr_ref[0], bti_ref[0]
    bbr, bbi = wr * btr - wi * bti, wr * bti + wi * btr
    cr, ci = cr_ref[0], ci_ref[0]
    bb_scr[:, 0:h] = bbr
    bb_scr[:, h:2 * h] = bbi
    for s in range(S5_Q):
        rs = slice(s * B_CH, (s + 1) * B_CH)
        e = S5_Q - 1 - s
        bst_ref[0, rs, 0:h] = ar[e:e + 1] * bbr - ai[e:e + 1] * bbi
        bst_ref[0, rs, h:2 * h] = ai[e:e + 1] * bbr + ar[e:e + 1] * bbi
        if s < valid:
            e = valid - 1 - s
            bsts_ref[0, rs, 0:h] = ar[e:e + 1] * bbr - ai[e:e + 1] * bbi
            bsts_ref[0, rs, h:2 * h] = ai[e:e + 1] * bbr + ar[e:e + 1] * bbi
        else:
            bsts_ref[0, rs, :] = jnp.zeros((B_CH, 2 * h), F32)
        cst_ref[0, rs, 0:h] = cr * ar[s + 1:s + 2] - ci * ai[s + 1:s + 2]
        cst_ref[0, rs, h:2 * h] = -(cr * ai[s + 1:s + 2] + ci * ar[s + 1:s + 2])
        cd_scr[rs, 0:h] = cr * ar[s:s + 1] - ci * ai[s:s + 1]
        cd_scr[rs, h:2 * h] = -(cr * ai[s:s + 1] + ci * ar[s:s + 1])
    kd_ref[0] = _dot_nt(cd_scr[...], bb_scr[...], precision=HIGHEST)
    ap_ref[0, 0:1, 0:h] = ar[S5_Q:S5_Q + 1]
    ap_ref[0, 0:1, h:2 * h] = ai[S5_Q:S5_Q + 1]
    ap_ref[0, 1:2, 0:h] = ar[valid:valid + 1]
    ap_ref[0, 1:2, h:2 * h] = ai[valid:valid + 1]


def _s5_weights(lam_re, lam_im, log_dt, b_re, b_im, c_re, c_im, valid):
    g, h = B_GROUPS, B_STATE
    qr = S5_Q * B_CH
    vec = pl.BlockSpec((1, 1, h), lambda i: (i, 0, 0))
    mat = pl.BlockSpec((1, B_CH, h), lambda i: (i, 0, 0))
    big = pl.BlockSpec((1, qr, 2 * h), lambda i: (i, 0, 0))
    bst, bsts, cst, kd, ap = pl.pallas_call(
        functools.partial(_s5w_body, valid=valid),
        grid=(g,),
        in_specs=[vec, vec, vec, mat, mat, mat, mat],
        out_specs=[big, big, big, pl.BlockSpec((1, qr, B_CH), lambda i: (i, 0, 0)),
                   pl.BlockSpec((1, 2, 2 * h), lambda i: (i, 0, 0))],
        out_shape=[jax.ShapeDtypeStruct((g, qr, 2 * h), F32)] * 3
        + [jax.ShapeDtypeStruct((g, qr, B_CH), F32), jax.ShapeDtypeStruct((g, 2, 2 * h), F32)],
        scratch_shapes=[pltpu.VMEM((qr, 2 * h), F32), pltpu.VMEM((B_CH, 2 * h), F32)],
        compiler_params=_cparams("parallel"),
        name="s5_weights",
    )(lam_re.reshape(g, 1, h), lam_im.reshape(g, 1, h), jnp.broadcast_to(log_dt.reshape(g, 1, 1), (g, 1, h)),
      jnp.swapaxes(b_re, 1, 2), jnp.swapaxes(b_im, 1, 2), c_re, c_im)

    eye = jnp.eye(S5_OCT, dtype=F32)
    q = S5_Q

    def state_in(m):
        m = m.reshape(N_OCT, S5_OCT, q, B_CH, h)
        return jnp.einsum("kgscp,gf->ksgcfp", m, eye).reshape(N_OCT, q * LANE, S5_OCT * h).astype(BF16)

    def state_out(m):
        m = m.reshape(N_OCT, S5_OCT, q, B_CH, h)
        return jnp.einsum("kgjhp,gf->kgpjfh", m, eye).reshape(N_OCT, S5_OCT * h, q * LANE).astype(BF16)

    s_idx = np.arange(q)[:, None]
    j_idx = np.arange(q)[None, :]
    lag = np.clip(j_idx - s_idx, 0, q - 1)
    causal = jnp.asarray((j_idx >= s_idx).astype(np.float32))
    kt = kd.reshape(g, q, B_CH, B_CH)[:, lag] * causal[None, :, :, None, None]
    kt = kt.reshape(N_OCT, S5_OCT, q, q, B_CH, B_CH)
    toep = jnp.einsum("kgsjhc,gf->ksgcjfh", kt, eye).reshape(N_OCT, q * LANE, q * LANE).astype(BF16)
    return dict(
        bst_re=state_in(bst[..., :h]), bst_im=state_in(bst[..., h:]),
        bsts_re=state_in(bsts[..., :h]), bsts_im=state_in(bsts[..., h:]),
        cst_re=state_out(cst[..., :h]), cst_im=state_out(cst[..., h:]),
        toep=toep,
        aq_re=ap[:, 0, :h].reshape(1, S5_STATE_COLS), aq_im=ap[:, 0, h:].reshape(1, S5_STATE_COLS),
        av_re=ap[:, 1, :h].reshape(1, S5_STATE_COLS), av_im=ap[:, 1, h:].reshape(1, S5_STATE_COLS),
    )


def _oct_rows(u_ref, k8):
    return jnp.concatenate(
        [u_ref[:, j * B_WIDTH + k8 * LANE:j * B_WIDTH + (k8 + 1) * LANE] for j in range(S5_Q)], axis=1)


def _s5_state_body(u_ref, bre_ref, bim_ref, sre_ref, sim_ref):
    w = S5_OCT * B_STATE
    for k8 in range(N_OCT):
        uo = _oct_rows(u_ref, k8)
        sre_ref[:, k8 * w:(k8 + 1) * w] = _dot(uo, bre_ref[k8])
        sim_ref[:, k8 * w:(k8 + 1) * w] = _dot(uo, bim_ref[k8])


def _s5_state(u2, bre, bim, rt):
    r = u2.shape[0]
    wspec = _full((N_OCT, S5_Q * LANE, S5_OCT * B_STATE))
    ospec = pl.BlockSpec((rt, S5_STATE_COLS), lambda i: (i, 0))
    return pl.pallas_call(
        _s5_state_body,
        grid=(r // rt,),
        in_specs=[pl.BlockSpec((rt, S5_Q * B_WIDTH), lambda i: (i, 0)), wspec, wspec],
        out_specs=[ospec, ospec],
        out_shape=[jax.ShapeDtypeStruct((r, S5_STATE_COLS), F32)] * 2,
        compiler_params=_cparams("parallel"),
        name="s5_state",
    )(u2, bre, bim)


def _s5_scan_body(sre_ref, sim_ref, ar_ref, ai_ref, hre_ref, him_ref, fre_ref, fim_ref, cr_scr, ci_scr, *, rb, nblk):
    i = pl.program_id(1)

    @pl.when(i == 0)
    def _():
        cr_scr[...] = jnp.zeros_like(cr_scr)
        ci_scr[...] = jnp.zeros_like(ci_scr)

    ar, ai = ar_ref[...], ai_ref[...]

    def step(r, carry):
        hr, hi = carry
        hre_ref[pl.ds(r, 1), :] = hr
        him_ref[pl.ds(r, 1), :] = hi
        sr, si = sre_ref[pl.ds(r, 1), :], sim_ref[pl.ds(r, 1), :]
        return ar * hr - ai * hi + sr, ai * hr + ar * hi + si

    hr, hi = lax.fori_loop(0, rb, step, (cr_scr[...], ci_scr[...]))
    cr_scr[...] = hr
    ci_scr[...] = hi

    @pl.when(i == nblk - 1)
    def _():
        fre_ref[0] = hr
        fim_ref[0] = hi


def _s5_scan(sre, sim, ar, ai, nb, rb):
    r = sre.shape[0]
    nblk = r // nb // rb
    blk = pl.BlockSpec((rb, S5_STATE_COLS), lambda b, i: (b * nblk + i, 0))
    vec = _full((1, S5_STATE_COLS))
    fin = pl.BlockSpec((1, 1, S5_STATE_COLS), lambda b, i: (b, 0, 0))
    return pl.pallas_call(
        functools.partial(_s5_scan_body, rb=rb, nblk=nblk),
        grid=(nb, nblk),
        in_specs=[blk, blk, vec, vec],
        out_specs=[blk, blk, fin, fin],
        out_shape=[jax.ShapeDtypeStruct((r, S5_STATE_COLS), F32)] * 2
        + [jax.ShapeDtypeStruct((nb, 1, S5_STATE_COLS), F32)] * 2,
        scratch_shapes=[pltpu.VMEM((1, S5_STATE_COLS), F32)] * 2,
        compiler_params=_cparams("parallel", "arbitrary"),
        name="s5_scan",
    )(sre, sim, ar, ai)


def _s5_final_body(u_ref, bre_ref, bim_ref, hre_ref, him_ref, ar_ref, ai_ref, fre_ref, fim_ref):
    w = S5_OCT * B_STATE
    hr, hi = hre_ref[...], him_ref[...]
    ar, ai = ar_ref[...], ai_ref[...]
    pre, pim = ar * hr - ai * hi, ai * hr + ar * hi
    for k8 in range(N_OCT):
        uo = _oct_rows(u_ref, k8)
        fre_ref[:, k8 * w:(k8 + 1) * w] = pre[:, k8 * w:(k8 + 1) * w] + _dot(uo, bre_ref[k8])
        fim_ref[:, k8 * w:(k8 + 1) * w] = pim[:, k8 * w:(k8 + 1) * w] + _dot(uo, bim_ref[k8])


def _s5_final(u2, bre, bim, hre, him, ar, ai):
    r = u2.shape[0]
    wspec = _full((N_OCT, S5_Q * LANE, S5_OCT * B_STATE))
    st = _full((r, S5_STATE_COLS))
    vec = _full((1, S5_STATE_COLS))
    return pl.pallas_call(
        _s5_final_body,
        grid=(1,),
        in_specs=[_full((r, S5_Q * B_WIDTH)), wspec, wspec, st, st, vec, vec],
        out_specs=[st, st],
        out_shape=[jax.ShapeDtypeStruct((r, S5_STATE_COLS), F32)] * 2,
        compiler_params=_cparams("arbitrary"),
        name="s5_final",
    )(u2, bre, bim, hre, him, ar, ai)


def _s5_out_body(u_ref, hre_ref, him_ref, toep_ref, cre_ref, cim_ref, y_ref):
    w = S5_OCT * B_STATE
    for k8 in range(N_OCT):
        uo = _oct_rows(u_ref, k8)
        yo = (_dot(uo, toep_ref[k8])
              + _dot(hre_ref[:, k8 * w:(k8 + 1) * w].astype(BF16), cre_ref[k8])
              + _dot(him_ref[:, k8 * w:(k8 + 1) * w].astype(BF16), cim_ref[k8]))
        for j in range(S5_Q):
            y_ref[:, j * B_WIDTH + k8 * LANE:j * B_WIDTH + (k8 + 1) * LANE] = yo[:, j * LANE:(j + 1) * LANE]


def _s5_out(u2, hre, him, toep, cre, cim, rt):
    r = u2.shape[0]
    ublk = pl.BlockSpec((rt, S5_Q * B_WIDTH), lambda i: (i, 0))
    hblk = pl.BlockSpec((rt, S5_STATE_COLS), lambda i: (i, 0))
    return pl.pallas_call(
        _s5_out_body,
        grid=(r // rt,),
        in_specs=[ublk, hblk, hblk, _full((N_OCT, S5_Q * LANE, S5_Q * LANE)),
                  _full((N_OCT, S5_OCT * B_STATE, S5_Q * LANE)), _full((N_OCT, S5_OCT * B_STATE, S5_Q * LANE))],
        out_specs=ublk,
        out_shape=jax.ShapeDtypeStruct((r, S5_Q * B_WIDTH), F32),
        compiler_params=_cparams("parallel"),
        name="s5_out",
    )(u2, hre, him, toep, cre, cim)


def _bias_body(bucket_ref, table_ref, o_ref):
    bucket = bucket_ref[...]
    for hh in range(C_HEADS):
        acc = jnp.zeros((1, WINDOW), F32)
        for b in range(REL_BUCKETS):
            acc = jnp.where(bucket == b, table_ref[b, hh], acc)
        o_ref[hh:hh + 1, :] = acc


def _bias_by_distance(table):
    n = np.arange(WINDOW)
    exact = REL_BUCKETS // 2
    nf = np.maximum(n, 1).astype(np.float32)
    large = exact + (np.log(nf / np.float32(exact)) / np.float32(math.log(REL_MAX_DIST / exact))
                     * np.float32(REL_BUCKETS - exact)).astype(np.int32)
    bucket = np.where(n < exact, n, np.minimum(large, REL_BUCKETS - 1)).astype(np.int32)
    return pl.pallas_call(
        _bias_body,
        in_specs=[pl.BlockSpec(memory_space=pltpu.VMEM), pl.BlockSpec(memory_space=pltpu.SMEM)],
        out_specs=pl.BlockSpec(memory_space=pltpu.VMEM),
        out_shape=jax.ShapeDtypeStruct((C_HEADS, WINDOW), F32),
        name="rel_bias",
    )(jnp.asarray(bucket).reshape(1, WINDOW), table)


def _banded(bd, dist, ok):
    vals = bd[:, np.clip(dist, 0, WINDOW - 1)]
    vals = jnp.where(jnp.asarray(ok)[None], vals, NEG)
    r, k = dist.shape
    return vals.reshape(C_KV_HEADS, C_GQA * r, k)


def _attn_prompt_body(q_ref, kc_ref, kp_ref, vc_ref, vp_ref, bias_ref, sink_ref, o_ref):
    for hk in range(C_KV_HEADS):
        ks = slice(hk * C_HEAD_DIM, (hk + 1) * C_HEAD_DIM)
        qs = jnp.concatenate([q_ref[:, (hk * C_GQA + g) * C_HEAD_DIM:(hk * C_GQA + g + 1) * C_HEAD_DIM]
                              for g in range(C_GQA)], axis=0)
        kcat = jnp.concatenate([kp_ref[:, ks], kc_ref[:, ks]], axis=0).astype(BF16)
        vcat = jnp.concatenate([vp_ref[:, ks], vc_ref[:, ks]], axis=0).astype(BF16)
        s = _dot_nt(qs, kcat) * (C_HEAD_DIM ** -0.5) + bias_ref[0, hk]
        sk = sink_ref[hk]
        m = jnp.maximum(jnp.max(s, axis=-1, keepdims=True), sk)
        p = jnp.exp(s - m)
        den = jnp.sum(p, axis=-1, keepdims=True) + jnp.exp(sk - m)
        o = _dot(p.astype(BF16), vcat) / den
        for g in range(C_GQA):
            hq = hk * C_GQA + g
            o_ref[:, hq * C_HEAD_DIM:(hq + 1) * C_HEAD_DIM] = o[g * WINDOW:(g + 1) * WINDOW].astype(BF16)


def _attn_prompt(q, k, v, bias2, sink_col, nb, nblk):
    cur = lambda b, n: (b * nblk + n, 0)
    prev = lambda b, n: (b * nblk + jnp.maximum(n - 1, 0), 0)
    kv = lambda m: pl.BlockSpec((WINDOW, C_KV_WIDTH), m)
    return pl.pallas_call(
        _attn_prompt_body,
        grid=(nb, nblk),
        in_specs=[pl.BlockSpec((WINDOW, C_WIDTH), cur), kv(cur), kv(prev), kv(cur), kv(prev),
                  pl.BlockSpec((1, C_KV_HEADS, C_GQA * WINDOW, 2 * WINDOW), lambda b, n: (jnp.minimum(n, 1), 0, 0, 0)),
                  _full((C_KV_HEADS, C_GQA * WINDOW, 1))],
        out_specs=pl.BlockSpec((WINDOW, C_WIDTH), cur),
        out_shape=jax.ShapeDtypeStruct((nb * nblk * WINDOW, C_WIDTH), BF16),
        compiler_params=_cparams("parallel", "arbitrary"),
        name="attn_prompt",
    )(q, k, k, v, v, bias2, sink_col)


def _attn_sample_body(q_ref, kn_ref, vn_ref, kb_ref, vb_ref, b1_ref, b2_ref, sink_ref, o_ref):
    r = SAMPLE_ROWS
    for hk in range(C_KV_HEADS):
        ks = slice(hk * C_HEAD_DIM, (hk + 1) * C_HEAD_DIM)
        qs = jnp.concatenate([q_ref[:, (hk * C_GQA + g) * C_HEAD_DIM:(hk * C_GQA + g + 1) * C_HEAD_DIM]
                              for g in range(C_GQA)], axis=0)
        s1 = _dot_nt(qs, kb_ref[0, :, ks].astype(BF16)) * (C_HEAD_DIM ** -0.5) + b1_ref[hk]
        s2 = _dot_nt(qs, kn_ref[:, ks].astype(BF16)) * (C_HEAD_DIM ** -0.5) + b2_ref[hk]
        sk = sink_ref[hk]
        m = jnp.maximum(jnp.maximum(jnp.max(s1, axis=-1, keepdims=True), jnp.max(s2, axis=-1, keepdims=True)), sk)
        p1, p2 = jnp.exp(s1 - m), jnp.exp(s2 - m)
        den = jnp.sum(p1, axis=-1, keepdims=True) + jnp.sum(p2, axis=-1, keepdims=True) + jnp.exp(sk - m)
        o = (_dot(p1.astype(BF16), vb_ref[0, :, ks].astype(BF16))
             + _dot(p2.astype(BF16), vn_ref[:, ks].astype(BF16))) / den
        for g in range(C_GQA):
            hq = hk * C_GQA + g
            o_ref[:, hq * C_HEAD_DIM:(hq + 1) * C_HEAD_DIM] = o[g * r:(g + 1) * r].astype(BF16)


def _attn_sample(q, kn, vn, kbuf, vbuf, b1, b2, sink_col, nsb):
    r = SAMPLE_ROWS
    lb = kbuf.shape[1]
    rows = lambda w: pl.BlockSpec((r, w), lambda i: (i, 0))
    buf = pl.BlockSpec((1, lb, C_KV_WIDTH), lambda i: (i, 0, 0))
    return pl.pallas_call(
        _attn_sample_body,
        grid=(nsb,),
        in_specs=[rows(C_WIDTH), rows(C_KV_WIDTH), rows(C_KV_WIDTH), buf, buf,
                  _full((C_KV_HEADS, C_GQA * r, lb)), _full((C_KV_HEADS, C_GQA * r, r)),
                  _full((C_KV_HEADS, C_GQA * r, 1))],
        out_specs=rows(C_WIDTH),
        out_shape=jax.ShapeDtypeStruct((nsb * r, C_WIDTH), BF16),
        compiler_params=_cparams("parallel"),
        name="attn_sample",
    )(q, kn, vn, kbuf, vbuf, b1, b2, sink_col)


def _merge_body(x_ref, ya_ref, yb_ref, u_ref, yc_ref, nw_ref, wg_ref, d5_ref, wglu_ref,
                wa_ref, wb_ref, wc_ref, wo_ref, o_ref):
    x = x_ref[...]
    h = _rms(x, nw_ref[...]).astype(BF16)
    g = jax.nn.gelu(yb_ref[...] + d5_ref[...] * u_ref[...].astype(F32))
    gv = _dot(g.astype(BF16), wglu_ref[...])
    yb = (gv[:, :B_WIDTH] * jax.nn.sigmoid(gv[:, B_WIDTH:])).astype(BF16)

    def gate(i):
        return jax.nn.sigmoid(_dot(h, wg_ref[:, i * D_MODEL:(i + 1) * D_MODEL]))

    mixed = gate(0) * _dot(ya_ref[...], wa_ref[...])
    mixed = mixed + gate(1) * _dot(yb, wb_ref[...])
    mixed = mixed + gate(2) * _dot(yc_ref[...], wc_ref[...])
    o_ref[...] = x + _dot(mixed.astype(BF16), wo_ref[...])


def _merge(x, ya, yb, u, yc, lw, tm):
    t = x.shape[0]
    rows = lambda w: pl.BlockSpec((tm, w), lambda i: (i, 0))
    return pl.pallas_call(
        _merge_body,
        grid=(t // tm,),
        in_specs=[rows(D_MODEL), rows(A_WIDTH), rows(B_WIDTH), rows(B_WIDTH), rows(C_WIDTH),
                  _full((1, D_MODEL)), _full((D_MODEL, 3 * D_MODEL)), _full((1, B_WIDTH)),
                  _full((B_WIDTH, 2 * B_WIDTH)), _full((A_WIDTH, D_MODEL)), _full((B_WIDTH, D_MODEL)),
                  _full((C_WIDTH, D_MODEL)), _full((D_MODEL, D_MODEL))],
        out_specs=rows(D_MODEL),
        out_shape=jax.ShapeDtypeStruct((t, D_MODEL), F32),
        compiler_params=_cparams("parallel"),
        name="merge",
    )(x, ya, yb, u, yc, lw["norm1_w"], lw["w_gate"], lw["s5_d"], lw["w_glu"],
      lw["w_br_a"], lw["w_br_b"], lw["w_br_c"], lw["w_out"])


def _ffn_body(x_ref, nw_ref, wup_ref, wdn_ref, fnw_ref, o_ref, act_scr, *, final_norm):
    x = x_ref[...]
    h = _rms(x, nw_ref[...]).astype(BF16)
    for c in range(D_FF // FF_CHUNK):
        lo, hi = c * FF_CHUNK, (c + 1) * FF_CHUNK
        a = _dot(h, wup_ref[:, lo:hi])
        b = _dot(h, wup_ref[:, D_FF + lo:D_FF + hi])
        act_scr[:, lo:hi] = (jax.nn.silu(a) * b).astype(BF16)
    y = x + _dot(act_scr[...], wdn_ref[...])
    if final_norm:
        y = _rms(y, fnw_ref[...])
    o_ref[...] = y


def _ffn(x, lw, fnw, tm, final_norm):
    t = x.shape[0]
    rows = pl.BlockSpec((tm, D_MODEL), lambda i: (i, 0))
    return pl.pallas_call(
        functools.partial(_ffn_body, final_norm=final_norm),
        grid=(t // tm,),
        in_specs=[rows, _full((1, D_MODEL)), _full((D_MODEL, 2 * D_FF)), _full((D_FF, D_MODEL)), _full((1, D_MODEL))],
        out_specs=rows,
        out_shape=jax.ShapeDtypeStruct((t, D_MODEL), F32),
        scratch_shapes=[pltpu.VMEM((tm, D_FF), BF16)],
        compiler_params=_cparams("parallel"),
        name="ffn",
    )(x, lw["norm2_w"], lw["w_ffn_up"], lw["w_ffn_down"], fnw)


def _row_tile(t, pref):
    tm = pref
    while t % tm:
        tm //= 2
    return tm


def kernel(x_prompt, x_sample, state_ssd, state_conv, state_s5_re, state_s5_im, cache_k, cache_v, norm1_w, w_in, conv_w, conv_b, ssd_a_log, ssd_dt_bias, ssd_d, ssd_norm_w, s5_lam_re, s5_lam_im, s5_log_dt, s5_b_re, s5_b_im, s5_c_re, s5_c_im, s5_d, s5_w_glu, attn_sinks, w_br_a, w_br_b, w_br_c, w_out, norm2_w, w_ffn_up, w_ffn_down, rel_bias, final_norm_w):
    nb, seq, _ = x_prompt.shape
    nsb, valid, _ = x_sample.shape
    depth = w_in.shape[0]
    lb = cache_k.shape[2]
    r = SAMPLE_ROWS
    assert seq % A_CHUNK == 0 and seq % WINDOW == 0 and seq % (S5_Q * SUBLANE) == 0
    assert A_CONV - 1 <= valid <= r and lb >= valid and lb == WINDOW
    nchunk = seq // A_CHUNK
    tp = nb * seq
    ts = nsb * r

    bd = _bias_by_distance(rel_bias)
    qi = np.arange(WINDOW)[:, None]
    kj = np.arange(2 * WINDOW)[None, :]
    dist = qi + WINDOW - kj
    ok = (dist >= 0) & (dist < WINDOW)
    bias_p = jnp.stack([_banded(bd, dist, ok & (kj >= WINDOW)), _banded(bd, dist, ok)], axis=0)
    ti = np.arange(r)[:, None]
    dist1 = lb + ti - np.arange(lb)[None, :]
    dist2 = ti - np.arange(r)[None, :]
    bias_s1 = _banded(bd, dist1, (dist1 >= 0) & (dist1 < WINDOW))
    bias_s2 = _banded(bd, dist2, (dist2 >= 0) & (dist2 < WINDOW))
    fnw = final_norm_w.reshape(1, D_MODEL)

    xp = x_prompt.reshape(tp, D_MODEL)
    xs = jnp.pad(x_sample, ((0, 0), (0, r - valid), (0, 0))).reshape(ts, D_MODEL)
    tm_p = _row_tile(tp, 512)
    tm_s = _row_tile(ts, 512)
    rt_p = _row_tile(tp // S5_Q, 256)
    rb_p = _row_tile(seq // S5_Q, 128)

    new_p, new_s = [], []
    for i in range(depth):
        wi = w_in[i]
        off = [0, 512, 1280, 1288, 1800, 2312, 2440, 2568, 5640]
        w_pack = jnp.concatenate(
            [wi[:, off[0]:off[1]], wi[:, off[1]:off[2]], wi[:, off[3]:off[4]], wi[:, off[4]:off[5]],
             wi[:, off[5]:off[6]], wi[:, off[6]:off[7]],
             jnp.pad(wi[:, off[2]:off[3]], ((0, 0), (0, LANE - A_HEADS)))], axis=1).astype(BF16)
        sinks = attn_sinks[i].reshape(C_KV_HEADS, C_GQA, 1)
        lw = dict(
            norm1_w=norm1_w[i].reshape(1, D_MODEL), w_gate=wi[:, off[7]:off[8]].astype(BF16),
            conv_w=conv_w[i], conv_b=conv_b[i].reshape(1, A_CONV_DIM),
            a_log=jnp.pad(ssd_a_log[i], (0, LANE - A_HEADS)).reshape(1, LANE),
            dt_bias=jnp.pad(ssd_dt_bias[i], (0, LANE - A_HEADS)).reshape(1, LANE),
            d_exp=jnp.repeat(ssd_d[i], A_HEAD_DIM).reshape(1, A_WIDTH),
            ssd_norm_w=ssd_norm_w[i].reshape(1, A_WIDTH),
            s5_d=s5_d[i].reshape(1, B_WIDTH), w_glu=s5_w_glu[i].astype(BF16),
            w_br_a=w_br_a[i].astype(BF16), w_br_b=w_br_b[i].astype(BF16), w_br_c=w_br_c[i].astype(BF16),
            w_out=w_out[i].astype(BF16), norm2_w=norm2_w[i].reshape(1, D_MODEL),
            w_ffn_up=w_ffn_up[i].astype(BF16), w_ffn_down=w_ffn_down[i].astype(BF16),
        )
        s5w = _s5_weights(s5_lam_re[i], s5_lam_im[i], s5_log_dt[i], s5_b_re[i], s5_b_im[i],
                          s5_c_re[i], s5_c_im[i], valid)
        last = i == depth - 1

        z, xbc, u, q, k, v, dt = _inproj(xp, lw["norm1_w"], w_pack, tm_p)
        ya, ssd_h = _ssd(xbc, z, dt, lw, nb, nchunk, A_CHUNK, A_CHUNK)
        u2 = u.reshape(tp // S5_Q, S5_Q * B_WIDTH)
        sre, sim = _s5_state(u2, s5w["bst_re"], s5w["bst_im"], rt_p)
        hre, him, fre, fim = _s5_scan(sre, sim, s5w["aq_re"], s5w["aq_im"], nb, rb_p)
        yb = _s5_out(u2, hre, him, s5w["toep"], s5w["cst_re"], s5w["cst_im"], rt_p).reshape(tp, B_WIDTH)
        yc = _attn_prompt(q, k, v, bias_p, jnp.broadcast_to(sinks[:, :, None, :], (C_KV_HEADS, C_GQA, WINDOW, 1))
                          .reshape(C_KV_HEADS, C_GQA * WINDOW, 1), nb, seq // WINDOW)
        x1 = _merge(xp, ya, yb, u, yc, lw, tm_p)
        xp = _ffn(x1, lw, fnw, tm_p, last)
        new_p.append((
            ssd_h.reshape(nb, A_HEADS, A_HEAD_DIM, A_STATE),
            xbc.reshape(nb, seq, A_CONV_DIM)[:, seq - (A_CONV - 1):],
            fre.reshape(nb, B_GROUPS, B_STATE), fim.reshape(nb, B_GROUPS, B_STATE),
            k.reshape(nb, seq, C_KV_HEADS, C_HEAD_DIM)[:, seq - WINDOW:],
            v.reshape(nb, seq, C_KV_HEADS, C_HEAD_DIM)[:, seq - WINDOW:]))

        z, xbc, u, q, k, v, dt = _inproj(xs, lw["norm1_w"], w_pack, tm_s)
        conv0 = jnp.pad(state_conv[i], ((0, 0), (r - (A_CONV - 1), 0), (0, 0))).reshape(ts, A_CONV_DIM)
        ya, ssd_h = _ssd(xbc, z, dt, lw, nsb, 1, r, valid,
                         h0=state_ssd[i].reshape(nsb, A_WIDTH, A_STATE), conv0=conv0)
        u2 = u.reshape(nsb, S5_Q * B_WIDTH)
        hre = state_s5_re[i].reshape(nsb, S5_STATE_COLS)
        him = state_s5_im[i].reshape(nsb, S5_STATE_COLS)
        fre, fim = _s5_final(u2, s5w["bsts_re"], s5w["bsts_im"], hre, him, s5w["av_re"], s5w["av_im"])
        yb = _s5_out(u2, hre, him, s5w["toep"], s5w["cst_re"], s5w["cst_im"], _row_tile(nsb, 256)).reshape(ts, B_WIDTH)
        kbuf = cache_k[i].reshape(nsb, lb, C_KV_WIDTH)
        vbuf = cache_v[i].reshape(nsb, lb, C_KV_WIDTH)
        yc = _attn_sample(q, k, v, kbuf, vbuf, bias_s1, bias_s2,
                          jnp.broadcast_to(sinks[:, :, None, :], (C_KV_HEADS, C_GQA, r, 1))
                          .reshape(C_KV_HEADS, C_GQA * r, 1), nsb)
        x1 = _merge(xs, ya, yb, u, yc, lw, tm_s)
        xs = _ffn(x1, lw, fnw, tm_s, last)
        kn = k.reshape(nsb, r, C_KV_WIDTH)[:, :valid]
        vn = v.reshape(nsb, r, C_KV_WIDTH)[:, :valid]
        new_s.append((
            ssd_h.reshape(nsb, A_HEADS, A_HEAD_DIM, A_STATE),
            xbc.reshape(nsb, r, A_CONV_DIM)[:, valid - (A_CONV - 1):valid],
            fre.reshape(nsb, B_GROUPS, B_STATE), fim.reshape(nsb, B_GROUPS, B_STATE),
            jnp.concatenate([kbuf[:, valid:], kn], axis=1).reshape(nsb, lb, C_KV_HEADS, C_HEAD_DIM),
            jnp.concatenate([vbuf[:, valid:], vn], axis=1).reshape(nsb, lb, C_KV_HEADS, C_HEAD_DIM)))

    def stack(states, j):
        return jnp.stack([s[j] for s in states], axis=0)

    y_prompt = xp.reshape(nb, seq, D_MODEL)
    y_sample = xs.reshape(nsb, r, D_MODEL)[:, :valid]
    return (y_prompt, y_sample,
            stack(new_p, 0), stack(new_p, 1), stack(new_p, 2), stack(new_p, 3), stack(new_p, 4), stack(new_p, 5),
            stack(new_s, 0), stack(new_s, 1), stack(new_s, 2), stack(new_s, 3), stack(new_s, 4), stack(new_s, 5))
```

```python
import functools
import math

import numpy as np
import jax
import jax.numpy as jnp
from jax import lax
from jax.experimental import pallas as pl
from jax.experimental.pallas import tpu as pltpu

F32 = jnp.float32
BF16 = jnp.bfloat16
HIGHEST = lax.Precision.HIGHEST

D_MODEL = 1024
A_HEAD_DIM = 64
A_WIDTH = 512
A_HEADS = 8
A_GROUPS = 2
A_STATE = 64
A_CONV = 4
A_CONV_DIM = 768
A_CHUNK = 128
B_CH = 16
B_WIDTH = 512
B_GROUPS = 32
B_STATE = 64
C_HEAD_DIM = 64
C_WIDTH = 512
C_HEADS = 8
C_KV_HEADS = 2
C_GQA = 4
C_KV_WIDTH = 128
WINDOW = 128
REL_BUCKETS = 32
REL_MAX_DIST = 128
D_FF = 2816
EPS = 1e-6

P_Z = 0
P_XBC = P_Z + A_WIDTH
P_U = P_XBC + A_CONV_DIM
P_Q = P_U + B_WIDTH
P_K = P_Q + C_WIDTH
P_V = P_K + C_KV_WIDTH
P_DT = P_V + C_KV_WIDTH
P_COLS = P_DT + 128

LANE = 128
SUBLANE = 8
S5_Q = 8
S5_OCT = 8
N_OCT = B_GROUPS // S5_OCT
S5_STATE_COLS = B_GROUPS * B_STATE
SAMPLE_ROWS = 8
NEG = -1e30
VMEM_LIMIT = 56 * 1024 * 1024
FF_CHUNK = 256


def _cparams(*sem):
    return pltpu.CompilerParams(dimension_semantics=sem, vmem_limit_bytes=VMEM_LIMIT)


def _rms(x, w):
    return x * lax.rsqrt(jnp.mean(x * x, axis=-1, keepdims=True) + EPS) * w


def _dot(a, b):
    return jnp.dot(a, b, preferred_element_type=F32)


def _dot_nt(a, b, precision=None):
    return lax.dot_general(a, b, (((1,), (1,)), ((), ())), preferred_element_type=F32, precision=precision)


def _dot_tn(a, b):
    return lax.dot_general(a, b, (((0,), (0,)), ((), ())), preferred_element_type=F32)


def _full(shape):
    n = len(shape)
    return pl.BlockSpec(shape, lambda *_: (0,) * n)


def _inproj_body(x_ref, nw_ref, w_ref, z_ref, xbc_ref, u_ref, q_ref, k_ref, v_ref, dt_ref):
    h = _rms(x_ref[...], nw_ref[...]).astype(BF16)

    def proj(lo, hi):
        return _dot(h, w_ref[:, lo:hi])

    z_ref[...] = proj(P_Z, P_XBC).astype(BF16)
    xbc_ref[...] = proj(P_XBC, P_U)
    u_ref[...] = proj(P_U, P_Q)
    q_ref[...] = proj(P_Q, P_K).astype(BF16)
    k_ref[...] = proj(P_K, P_V)
    v_ref[...] = proj(P_V, P_DT)
    dt_ref[...] = proj(P_DT, P_COLS)


def _inproj(x, nw, w, tm):
    t = x.shape[0]
    widths = [(A_WIDTH, BF16), (A_CONV_DIM, F32), (B_WIDTH, F32), (C_WIDTH, BF16),
              (C_KV_WIDTH, F32), (C_KV_WIDTH, F32), (LANE, F32)]
    return pl.pallas_call(
        _inproj_body,
        grid=(t // tm,),
        in_specs=[pl.BlockSpec((tm, D_MODEL), lambda i: (i, 0)), _full((1, D_MODEL)), _full((D_MODEL, P_COLS))],
        out_specs=[pl.BlockSpec((tm, wd), lambda i: (i, 0)) for wd, _ in widths],
        out_shape=[jax.ShapeDtypeStruct((t, wd), dt) for wd, dt in widths],
        compiler_params=_cparams("parallel"),
        name="inproj",
    )(x, nw, w)


def _ssd_body(*refs, q, valid, has_init, nchunks):
    if has_init:
        (xbc_ref, z_ref, dt_ref, cw_ref, cb_ref, alog_ref, dtb_ref, dexp_ref, nw_ref, h0_ref, conv0_ref,
         y_ref, hout_ref, h_scr, tail_scr) = refs
    else:
        (xbc_ref, z_ref, dt_ref, cw_ref, cb_ref, alog_ref, dtb_ref, dexp_ref, nw_ref,
         y_ref, hout_ref, h_scr, tail_scr) = refs
    c = pl.program_id(1)

    @pl.when(c == 0)
    def _():
        if has_init:
            h_scr[...] = h0_ref[0]
            tail_scr[...] = conv0_ref[...]
        else:
            h_scr[...] = jnp.zeros_like(h_scr)
            tail_scr[...] = jnp.zeros_like(tail_scr)

    xbc = xbc_ref[...]
    ext = jnp.concatenate([tail_scr[...], xbc], axis=0)
    acc = cb_ref[...] + cw_ref[A_CONV - 1:A_CONV, :] * xbc
    for j in range(1, A_CONV):
        acc = acc + cw_ref[A_CONV - 1 - j:A_CONV - j, :] * pltpu.roll(ext, j, axis=0)[SUBLANE:SUBLANE + q]
    tail_scr[...] = xbc[q - SUBLANE:q]
    xc = jax.nn.silu(acc)
    x = xc[:, :A_WIDTH]
    bm = xc[:, A_WIDTH:A_WIDTH + A_GROUPS * A_STATE].astype(BF16)
    cm = xc[:, A_WIDTH + A_GROUPS * A_STATE:].astype(BF16)

    dt = jax.nn.softplus(dt_ref[...] + dtb_ref[...])
    if valid < q:
        dt = jnp.where(lax.broadcasted_iota(jnp.int32, dt.shape, 0) < valid, dt, 0.0)
    dta = dt * (-jnp.exp(alog_ref[...]))
    row = lax.broadcasted_iota(jnp.int32, (q, q), 0)
    col = lax.broadcasted_iota(jnp.int32, (q, q), 1)
    tri = row >= col
    a_cum = jnp.dot(tri.astype(F32), dta, preferred_element_type=F32, precision=HIGHEST)

    def expand(v):
        return jnp.concatenate([jnp.broadcast_to(v[:, k:k + 1], (q, A_HEAD_DIM)) for k in range(A_HEADS)], axis=1)

    a_cum_e = expand(a_cum)
    xs = x * expand(dt)
    xs_b = xs.astype(BF16)
    a_cum_t = a_cum.T

    h = h_scr[...]
    h_b = h.astype(BF16)
    hpg = A_HEADS // A_GROUPS
    gw = hpg * A_HEAD_DIM
    y_diag, y_off = [], []
    for g in range(A_GROUPS):
        cg = cm[:, g * A_STATE:(g + 1) * A_STATE]
        bg = bm[:, g * A_STATE:(g + 1) * A_STATE]
        cb = _dot_nt(cg, bg)
        for kk in range(hpg):
            k = g * hpg + kk
            seg = a_cum[:, k:k + 1] - a_cum_t[k:k + 1, :]
            decay = jnp.exp(jnp.where(tri, seg, -jnp.inf))
            y_diag.append(_dot((cb * decay).astype(BF16), xs_b[:, k * A_HEAD_DIM:(k + 1) * A_HEAD_DIM]))
        y_off.append(_dot_nt(cg, h_b[g * gw:(g + 1) * gw]))
    y = (jnp.concatenate(y_diag, axis=1) + jnp.concatenate(y_off, axis=1) * jnp.exp(a_cum_e)
         + dexp_ref[...] * x)
    y = y * jax.nn.silu(z_ref[...].astype(F32))
    y_ref[...] = _rms(y, nw_ref[...]).astype(BF16)

    last = a_cum_e[q - 1:q, :]
    xs_dec = (xs * jnp.exp(last - a_cum_e)).astype(BF16)
    dec_col = jnp.exp(a_cum_e.T[:, q - 1:q])
    upd = [_dot_tn(xs_dec[:, g * gw:(g + 1) * gw], bm[:, g * A_STATE:(g + 1) * A_STATE]) for g in range(A_GROUPS)]
    h_new = dec_col * h + jnp.concatenate(upd, axis=0)
    h_scr[...] = h_new

    @pl.when(c == nchunks - 1)
    def _():
        hout_ref[0] = h_new


def _ssd(xbc, z, dt, lw, nb, nchunks, q, valid, h0=None, conv0=None):
    has_init = h0 is not None
    rows = nb * nchunks * q
    rmap = lambda b, c: (b * nchunks + c, 0)
    in_specs = [pl.BlockSpec((q, A_CONV_DIM), rmap), pl.BlockSpec((q, A_WIDTH), rmap), pl.BlockSpec((q, LANE), rmap),
                _full((A_CONV, A_CONV_DIM)), _full((1, A_CONV_DIM)), _full((1, LANE)), _full((1, LANE)),
                _full((1, A_WIDTH)), _full((1, A_WIDTH))]
    args = [xbc, z, dt, lw["conv_w"], lw["conv_b"], lw["a_log"], lw["dt_bias"], lw["d_exp"], lw["ssd_norm_w"]]
    if has_init:
        in_specs += [pl.BlockSpec((1, A_WIDTH, A_STATE), lambda b, c: (b, 0, 0)),
                     pl.BlockSpec((SUBLANE, A_CONV_DIM), lambda b, c: (b, 0))]
        args += [h0, conv0]
    return pl.pallas_call(
        functools.partial(_ssd_body, q=q, valid=valid, has_init=has_init, nchunks=nchunks),
        grid=(nb, nchunks),
        in_specs=in_specs,
        out_specs=[pl.BlockSpec((q, A_WIDTH), rmap), pl.BlockSpec((1, A_WIDTH, A_STATE), lambda b, c: (b, 0, 0))],
        out_shape=[jax.ShapeDtypeStruct((rows, A_WIDTH), BF16), jax.ShapeDtypeStruct((nb, A_WIDTH, A_STATE), F32)],
        scratch_shapes=[pltpu.VMEM((A_WIDTH, A_STATE), F32), pltpu.VMEM((SUBLANE, A_CONV_DIM), F32)],
        compiler_params=_cparams("parallel", "arbitrary"),
        name="ssd",
    )(*args)


def _s5w_body(lre_ref, lim_ref, ldt_ref, btr_ref, bti_ref, cr_ref, ci_ref,
              bre_ref, bim_ref, bsre_ref, bsim_ref, cre_ref, cim_ref, toep_ref, ap_ref, cd_scr, bb_scr, *, valid):
    h, q = B_STATE, S5_Q
    for ref in (bre_ref, bim_ref, bsre_ref, bsim_ref, cre_ref, cim_ref, toep_ref):
        ref[...] = jnp.zeros(ref.shape, BF16)
    r_i = lax.broadcasted_iota(jnp.int32, (q * B_CH, q * LANE), 0)
    c_i = lax.broadcasted_iota(jnp.int32, (q * B_CH, q * LANE), 1)
    d = lax.broadcasted_iota(jnp.int32, (2 * SUBLANE, h), 0).astype(F32)
    for g8 in range(S5_OCT):
        lr, li = lre_ref[g8], lim_ref[g8]
        step = jnp.exp(ldt_ref[g8])
        er = jnp.exp(lr * step * d)
        ang = li * step * d
        ar, ai = er * jnp.cos(ang), er * jnp.sin(ang)
        nr, ni = ar[1:2] - 1.0, ai[1:2]
        den = lr * lr + li * li
        wr, wi = (nr * lr + ni * li) / den, (ni * lr - nr * li) / den
        btr, bti = btr_ref[g8], bti_ref[g8]
        bbr, bbi = wr * btr - wi * bti, wr * bti + wi * btr
        cr, ci = cr_ref[g8], ci_ref[g8]
        bb_scr[:, 0:h] = bbr
        bb_scr[:, h:2 * h] = bbi
        cols = slice(g8 * h, (g8 + 1) * h)
        for s in range(q):
            rows = slice(s * LANE + g8 * B_CH, s * LANE + (g8 + 1) * B_CH)
            e = q - 1 - s
            bre_ref[0, rows, cols] = (ar[e:e + 1] * bbr - ai[e:e + 1] * bbi).astype(BF16)
            bim_ref[0, rows, cols] = (ai[e:e + 1] * bbr + ar[e:e + 1] * bbi).astype(BF16)
            if s < valid:
                e = valid - 1 - s
                bsre_ref[0, rows, cols] = (ar[e:e + 1] * bbr - ai[e:e + 1] * bbi).astype(BF16)
                bsim_ref[0, rows, cols] = (ai[e:e + 1] * bbr + ar[e:e + 1] * bbi).astype(BF16)
            cre_ref[0, rows, cols] = (cr * ar[s + 1:s + 2] - ci * ai[s + 1:s + 2]).astype(BF16)
            cim_ref[0, rows, cols] = (-(cr * ai[s + 1:s + 2] + ci * ar[s + 1:s + 2])).astype(BF16)
            cd_scr[s * B_CH:(s + 1) * B_CH, 0:h] = cr * ar[s:s + 1] - ci * ai[s:s + 1]
            cd_scr[s * B_CH:(s + 1) * B_CH, h:2 * h] = -(cr * ai[s:s + 1] + ci * ar[s:s + 1])
        kdt = _dot_nt(bb_scr[...], cd_scr[...], precision=HIGHEST)
        place = (c_i == (r_i >> 4) * LANE + g8 * B_CH + (r_i & (B_CH - 1))).astype(F32)
        slab = jnp.dot(kdt, place, preferred_element_type=F32, precision=HIGHEST).astype(BF16)
        for s in range(q):
            toep_ref[0, s * LANE + g8 * B_CH:s * LANE + (g8 + 1) * B_CH, s * LANE:] = slab[:, 0:(q - s) * LANE]
        ap_ref[g8, 0:1, 0:h] = ar[q:q + 1]
        ap_ref[g8, 0:1, h:2 * h] = ai[q:q + 1]
        ap_ref[g8, 1:2, 0:h] = ar[valid:valid + 1]
        ap_ref[g8, 1:2, h:2 * h] = ai[valid:valid + 1]


def _s5_weights(lam_re, lam_im, log_dt, b_re, b_im, c_re, c_im, valid):
    g, h, q = B_GROUPS, B_STATE, S5_Q
    vec = pl.BlockSpec((S5_OCT, 1, h), lambda i: (i, 0, 0))
    mat = pl.BlockSpec((S5_OCT, B_CH, h), lambda i: (i, 0, 0))
    st = pl.BlockSpec((1, q * LANE, S5_OCT * h), lambda i: (i, 0, 0))
    st_shape = jax.ShapeDtypeStruct((N_OCT, q * LANE, S5_OCT * h), BF16)
    bre, bim, bsre, bsim, cre, cim, toep, ap = pl.pallas_call(
        functools.partial(_s5w_body, valid=valid),
        grid=(N_OCT,),
        in_specs=[vec, vec, vec, mat, mat, mat, mat],
        out_specs=[st] * 6 + [pl.BlockSpec((1, q * LANE, q * LANE), lambda i: (i, 0, 0)),
                              pl.BlockSpec((S5_OCT, 2, 2 * h), lambda i: (i, 0, 0))],
        out_shape=[st_shape] * 6 + [jax.ShapeDtypeStruct((N_OCT, q * LANE, q * LANE), BF16),
                                    jax.ShapeDtypeStruct((g, 2, 2 * h), F32)],
        scratch_shapes=[pltpu.VMEM((q * B_CH, 2 * h), F32), pltpu.VMEM((B_CH, 2 * h), F32)],
        compiler_params=_cparams("parallel"),
        name="s5_weights",
    )(lam_re.reshape(g, 1, h), lam_im.reshape(g, 1, h), jnp.broadcast_to(log_dt.reshape(g, 1, 1), (g, 1, h)),
      jnp.swapaxes(b_re, 1, 2), jnp.swapaxes(b_im, 1, 2), c_re, c_im)
    return dict(
        bst_re=bre, bst_im=bim, bsts_re=bsre, bsts_im=bsim, cst_re=cre, cst_im=cim, toep=toep,
        aq_re=ap[:, 0, :h].reshape(1, S5_STATE_COLS), aq_im=ap[:, 0, h:].reshape(1, S5_STATE_COLS),
        av_re=ap[:, 1, :h].reshape(1, S5_STATE_COLS), av_im=ap[:, 1, h:].reshape(1, S5_STATE_COLS),
    )


def _oct_rows(u_ref, k8):
    return jnp.concatenate(
        [u_ref[:, j, k8 * LANE:(k8 + 1) * LANE] for j in range(S5_Q)], axis=1).astype(BF16)


def _s5_state_body(u_ref, bre_ref, bim_ref, sre_ref, sim_ref):
    w = S5_OCT * B_STATE
    for k8 in range(N_OCT):
        uo = _oct_rows(u_ref, k8)
        sre_ref[:, k8 * w:(k8 + 1) * w] = _dot(uo, bre_ref[k8])
        sim_ref[:, k8 * w:(k8 + 1) * w] = _dot(uo, bim_ref[k8])


def _s5_state(u2, bre, bim, rt):
    r = u2.shape[0]
    wspec = _full((N_OCT, S5_Q * LANE, S5_OCT * B_STATE))
    ospec = pl.BlockSpec((rt, S5_STATE_COLS), lambda i: (i, 0))
    return pl.pallas_call(
        _s5_state_body,
        grid=(r // rt,),
        in_specs=[pl.BlockSpec((rt, S5_Q, B_WIDTH), lambda i: (i, 0, 0)), wspec, wspec],
        out_specs=[ospec, ospec],
        out_shape=[jax.ShapeDtypeStruct((r, S5_STATE_COLS), F32)] * 2,
        compiler_params=_cparams("parallel"),
        name="s5_state",
    )(u2, bre, bim)


def _s5_scan_body(sre_ref, sim_ref, ar_ref, ai_ref, hre_ref, him_ref, fre_ref, fim_ref, cr_scr, ci_scr, *, rb, nblk):
    i = pl.program_id(1)

    @pl.when(i == 0)
    def _():
        cr_scr[...] = jnp.zeros_like(cr_scr)
        ci_scr[...] = jnp.zeros_like(ci_scr)

    ar, ai = ar_ref[...], ai_ref[...]

    def step(r, carry):
        hr, hi = carry
        hre_ref[pl.ds(r, 1), :] = hr
        him_ref[pl.ds(r, 1), :] = hi
        sr, si = sre_ref[pl.ds(r, 1), :], sim_ref[pl.ds(r, 1), :]
        return ar * hr - ai * hi + sr, ai * hr + ar * hi + si

    hr, hi = lax.fori_loop(0, rb, step, (cr_scr[...], ci_scr[...]))
    cr_scr[...] = hr
    ci_scr[...] = hi

    @pl.when(i == nblk - 1)
    def _():
        fre_ref[0] = hr
        fim_ref[0] = hi


def _s5_scan(sre, sim, ar, ai, nb, rb):
    r = sre.shape[0]
    nblk = r // nb // rb
    blk = pl.BlockSpec((rb, S5_STATE_COLS), lambda b, i: (b * nblk + i, 0))
    vec = _full((1, S5_STATE_COLS))
    fin = pl.BlockSpec((1, 1, S5_STATE_COLS), lambda b, i: (b, 0, 0))
    return pl.pallas_call(
        functools.partial(_s5_scan_body, rb=rb, nblk=nblk),
        grid=(nb, nblk),
        in_specs=[blk, blk, vec, vec],
        out_specs=[blk, blk, fin, fin],
        out_shape=[jax.ShapeDtypeStruct((r, S5_STATE_COLS), F32)] * 2
        + [jax.ShapeDtypeStruct((nb, 1, S5_STATE_COLS), F32)] * 2,
        scratch_shapes=[pltpu.VMEM((1, S5_STATE_COLS), F32)] * 2,
        compiler_params=_cparams("parallel", "arbitrary"),
        name="s5_scan",
    )(sre, sim, ar, ai)


def _s5_final_body(u_ref, bre_ref, bim_ref, hre_ref, him_ref, ar_ref, ai_ref, fre_ref, fim_ref):
    w = S5_OCT * B_STATE
    hr, hi = hre_ref[...], him_ref[...]
    ar, ai = ar_ref[...], ai_ref[...]
    pre, pim = ar * hr - ai * hi, ai * hr + ar * hi
    for k8 in range(N_OCT):
        uo = _oct_rows(u_ref, k8)
        fre_ref[:, k8 * w:(k8 + 1) * w] = pre[:, k8 * w:(k8 + 1) * w] + _dot(uo, bre_ref[k8])
        fim_ref[:, k8 * w:(k8 + 1) * w] = pim[:, k8 * w:(k8 + 1) * w] + _dot(uo, bim_ref[k8])


def _s5_final(u2, bre, bim, hre, him, ar, ai):
    r = u2.shape[0]
    wspec = _full((N_OCT, S5_Q * LANE, S5_OCT * B_STATE))
    st = _full((r, S5_STATE_COLS))
    vec = _full((1, S5_STATE_COLS))
    return pl.pallas_call(
        _s5_final_body,
        grid=(1,),
        in_specs=[_full((r, S5_Q, B_WIDTH)), wspec, wspec, st, st, vec, vec],
        out_specs=[st, st],
        out_shape=[jax.ShapeDtypeStruct((r, S5_STATE_COLS), F32)] * 2,
        compiler_params=_cparams("arbitrary"),
        name="s5_final",
    )(u2, bre, bim, hre, him, ar, ai)


def _s5_out_body(u_ref, hre_ref, him_ref, toep_ref, cre_ref, cim_ref, y_ref):
    w = S5_OCT * B_STATE
    for k8 in range(N_OCT):
        uo = _oct_rows(u_ref, k8)
        yo = (_dot(uo, toep_ref[k8])
              + _dot_nt(hre_ref[:, k8 * w:(k8 + 1) * w].astype(BF16), cre_ref[k8])
              + _dot_nt(him_ref[:, k8 * w:(k8 + 1) * w].astype(BF16), cim_ref[k8]))
        for j in range(S5_Q):
            y_ref[:, j, k8 * LANE:(k8 + 1) * LANE] = yo[:, j * LANE:(j + 1) * LANE]


def _s5_out(u2, hre, him, toep, cre, cim, rt):
    r = u2.shape[0]
    ublk = pl.BlockSpec((rt, S5_Q, B_WIDTH), lambda i: (i, 0, 0))
    hblk = pl.BlockSpec((rt, S5_STATE_COLS), lambda i: (i, 0))
    return pl.pallas_call(
        _s5_out_body,
        grid=(r // rt,),
        in_specs=[ublk, hblk, hblk, _full((N_OCT, S5_Q * LANE, S5_Q * LANE)),
                  _full((N_OCT, S5_Q * LANE, S5_OCT * B_STATE)), _full((N_OCT, S5_Q * LANE, S5_OCT * B_STATE))],
        out_specs=ublk,
        out_shape=jax.ShapeDtypeStruct((r, S5_Q, B_WIDTH), F32),
        compiler_params=_cparams("parallel"),
        name="s5_out",
    )(u2, hre, him, toep, cre, cim)


def _band_body(bucket_ref, table_ref, o_ref):
    bucket = bucket_ref[...]
    col = lax.broadcasted_iota(jnp.int32, (WINDOW, 2 * WINDOW), 1)
    for hq in range(C_HEADS):
        base = jnp.full((1, 2 * WINDOW), NEG, F32)
        for b in range(REL_BUCKETS):
            base = jnp.where(bucket == b, table_ref[b, hq], base)
        tile = pltpu.roll(jnp.broadcast_to(base, (WINDOW, 2 * WINDOW)), 0, 1, stride=1, stride_axis=0)
        hk, g = divmod(hq, C_GQA)
        o_ref[1, hk, g * WINDOW:(g + 1) * WINDOW, :] = tile
        o_ref[0, hk, g * WINDOW:(g + 1) * WINDOW, :] = jnp.where(col >= WINDOW, tile, NEG)


def _bias_band(table):
    lane = np.arange(2 * WINDOW)
    n = WINDOW - lane
    exact = REL_BUCKETS // 2
    nf = np.maximum(n, 1).astype(np.float32)
    large = exact + (np.log(nf / np.float32(exact)) / np.float32(math.log(REL_MAX_DIST / exact))
                     * np.float32(REL_BUCKETS - exact)).astype(np.int32)
    bucket = np.where(n < exact, n, np.minimum(large, REL_BUCKETS - 1))
    bucket = np.where((n >= 0) & (n < WINDOW), bucket, -1).astype(np.int32)
    return pl.pallas_call(
        _band_body,
        in_specs=[pl.BlockSpec(memory_space=pltpu.VMEM), pl.BlockSpec(memory_space=pltpu.SMEM)],
        out_specs=pl.BlockSpec(memory_space=pltpu.VMEM),
        out_shape=jax.ShapeDtypeStruct((2, C_KV_HEADS, C_GQA * WINDOW, 2 * WINDOW), F32),
        name="rel_bias_band",
    )(jnp.asarray(bucket).reshape(1, 2 * WINDOW), table)


def _attn_prompt_body(q_ref, kc_ref, kp_ref, vc_ref, vp_ref, b0_ref, br_ref, sink_ref, o_ref, *, nblk):
    for hk in range(C_KV_HEADS):
        ks = slice(hk * C_HEAD_DIM, (hk + 1) * C_HEAD_DIM)
        kext = jnp.concatenate([kp_ref[:, ks], kc_ref[:, ks]], axis=0).astype(BF16)
        vext = jnp.concatenate([vp_ref[:, ks], vc_ref[:, ks]], axis=0).astype(BF16)
        sk = sink_ref[hk]
        for i in range(nblk):
            rows = slice(i * WINDOW, (i + 1) * WINDOW)
            qs = jnp.concatenate([q_ref[rows, (hk * C_GQA + g) * C_HEAD_DIM:(hk * C_GQA + g + 1) * C_HEAD_DIM]
                                  for g in range(C_GQA)], axis=0)
            bias = b0_ref[0, hk] if i == 0 else br_ref[0, hk]
            s = _dot_nt(qs, kext[i * WINDOW:(i + 2) * WINDOW]) * (C_HEAD_DIM ** -0.5) + bias
            m = jnp.maximum(jnp.max(s, axis=-1, keepdims=True), sk)
            p = jnp.exp(s - m)
            den = jnp.sum(p, axis=-1, keepdims=True) + jnp.exp(sk - m)
            o = _dot(p.astype(BF16), vext[i * WINDOW:(i + 2) * WINDOW]) / den
            for g in range(C_GQA):
                hq = hk * C_GQA + g
                o_ref[rows, hq * C_HEAD_DIM:(hq + 1) * C_HEAD_DIM] = o[g * WINDOW:(g + 1) * WINDOW].astype(BF16)


def _attn_prompt(q, k, v, band, sink_col, nb, nblk_seq, nblk):
    steps = nblk_seq // nblk
    cur = lambda b, n: (b * steps + n, 0)
    prev = lambda b, n: (b * nblk_seq + jnp.maximum(n * nblk - 1, 0), 0)
    bspec = lambda m: pl.BlockSpec((1, C_KV_HEADS, C_GQA * WINDOW, 2 * WINDOW), m)
    return pl.pallas_call(
        functools.partial(_attn_prompt_body, nblk=nblk),
        grid=(nb, steps),
        in_specs=[pl.BlockSpec((nblk * WINDOW, C_WIDTH), cur),
                  pl.BlockSpec((nblk * WINDOW, C_KV_WIDTH), cur), pl.BlockSpec((WINDOW, C_KV_WIDTH), prev),
                  pl.BlockSpec((nblk * WINDOW, C_KV_WIDTH), cur), pl.BlockSpec((WINDOW, C_KV_WIDTH), prev),
                  bspec(lambda b, n: (jnp.minimum(n, 1), 0, 0, 0)), bspec(lambda b, n: (1, 0, 0, 0)),
                  _full((C_KV_HEADS, C_GQA * WINDOW, 1))],
        out_specs=pl.BlockSpec((nblk * WINDOW, C_WIDTH), cur),
        out_shape=jax.ShapeDtypeStruct((nb * nblk_seq * WINDOW, C_WIDTH), BF16),
        compiler_params=_cparams("parallel", "arbitrary"),
        name="attn_prompt",
    )(q, k, k, v, v, band, band, sink_col)


def _attn_sample_body(q_ref, kn_ref, vn_ref, kb_ref, vb_ref, b1_ref, b2_ref, sink_ref, o_ref):
    r = SAMPLE_ROWS
    for hk in range(C_KV_HEADS):
        ks = slice(hk * C_HEAD_DIM, (hk + 1) * C_HEAD_DIM)
        qs = jnp.concatenate([q_ref[:, (hk * C_GQA + g) * C_HEAD_DIM:(hk * C_GQA + g + 1) * C_HEAD_DIM]
                              for g in range(C_GQA)], axis=0)
        s1 = _dot_nt(qs, kb_ref[0, :, ks].astype(BF16)) * (C_HEAD_DIM ** -0.5) + b1_ref[hk]
        s2 = _dot_nt(qs, kn_ref[:, ks].astype(BF16)) * (C_HEAD_DIM ** -0.5) + b2_ref[hk]
        sk = sink_ref[hk]
        m = jnp.maximum(jnp.maximum(jnp.max(s1, axis=-1, keepdims=True), jnp.max(s2, axis=-1, keepdims=True)), sk)
        p1, p2 = jnp.exp(s1 - m), jnp.exp(s2 - m)
        den = jnp.sum(p1, axis=-1, keepdims=True) + jnp.sum(p2, axis=-1, keepdims=True) + jnp.exp(sk - m)
        o = (_dot(p1.astype(BF16), vb_ref[0, :, ks].astype(BF16))
             + _dot(p2.astype(BF16), vn_ref[:, ks].astype(BF16))) / den
        for g in range(C_GQA):
            hq = hk * C_GQA + g
            o_ref[:, hq * C_HEAD_DIM:(hq + 1) * C_HEAD_DIM] = o[g * r:(g + 1) * r].astype(BF16)


def _attn_sample(q, kn, vn, kbuf, vbuf, b1, b2, sink_col, nsb):
    r = SAMPLE_ROWS
    lb = kbuf.shape[1]
    rows = lambda w: pl.BlockSpec((r, w), lambda i: (i, 0))
    buf = pl.BlockSpec((1, lb, C_KV_WIDTH), lambda i: (i, 0, 0))
    return pl.pallas_call(
        _attn_sample_body,
        grid=(nsb,),
        in_specs=[rows(C_WIDTH), rows(C_KV_WIDTH), rows(C_KV_WIDTH), buf, buf,
                  _full((C_KV_HEADS, C_GQA * r, lb)), _full((C_KV_HEADS, C_GQA * r, r)),
                  _full((C_KV_HEADS, C_GQA * r, 1))],
        out_specs=rows(C_WIDTH),
        out_shape=jax.ShapeDtypeStruct((nsb * r, C_WIDTH), BF16),
        compiler_params=_cparams("parallel"),
        name="attn_sample",
    )(q, kn, vn, kbuf, vbuf, b1, b2, sink_col)


def _merge_body(x_ref, ya_ref, yb_ref, u_ref, yc_ref, nw_ref, wg_ref, d5_ref, wglu_ref,
                wa_ref, wb_ref, wc_ref, wo_ref, o_ref):
    x = x_ref[...]
    h = _rms(x, nw_ref[...]).astype(BF16)
    g = jax.nn.gelu(yb_ref[...] + d5_ref[...] * u_ref[...].astype(F32))
    gv = _dot(g.astype(BF16), wglu_ref[...])
    yb = (gv[:, :B_WIDTH] * jax.nn.sigmoid(gv[:, B_WIDTH:])).astype(BF16)

    def gate(i):
        return jax.nn.sigmoid(_dot(h, wg_ref[:, i * D_MODEL:(i + 1) * D_MODEL]))

    mixed = gate(0) * _dot(ya_ref[...], wa_ref[...])
    mixed = mixed + gate(1) * _dot(yb, wb_ref[...])
    mixed = mixed + gate(2) * _dot(yc_ref[...], wc_ref[...])
    o_ref[...] = x + _dot(mixed.astype(BF16), wo_ref[...])


def _merge(x, ya, yb, u, yc, lw, tm):
    t = x.shape[0]
    rows = lambda w: pl.BlockSpec((tm, w), lambda i: (i, 0))
    return pl.pallas_call(
        _merge_body,
        grid=(t // tm,),
        in_specs=[rows(D_MODEL), rows(A_WIDTH), rows(B_WIDTH), rows(B_WIDTH), rows(C_WIDTH),
                  _full((1, D_MODEL)), _full((D_MODEL, 3 * D_MODEL)), _full((1, B_WIDTH)),
                  _full((B_WIDTH, 2 * B_WIDTH)), _full((A_WIDTH, D_MODEL)), _full((B_WIDTH, D_MODEL)),
                  _full((C_WIDTH, D_MODEL)), _full((D_MODEL, D_MODEL))],
        out_specs=rows(D_MODEL),
        out_shape=jax.ShapeDtypeStruct((t, D_MODEL), F32),
        compiler_params=_cparams("parallel"),
        name="merge",
    )(x, ya, yb, u, yc, lw["norm1_w"], lw["w_gate"], lw["s5_d"], lw["w_glu"],
      lw["w_br_a"], lw["w_br_b"], lw["w_br_c"], lw["w_out"])


def _ffn_body(x_ref, nw_ref, wup_ref, wdn_ref, fnw_ref, o_ref, act_scr, *, final_norm):
    x = x_ref[...]
    h = _rms(x, nw_ref[...]).astype(BF16)
    for c in range(D_FF // FF_CHUNK):
        lo, hi = c * FF_CHUNK, (c + 1) * FF_CHUNK
        a = _dot(h, wup_ref[:, lo:hi])
        b = _dot(h, wup_ref[:, D_FF + lo:D_FF + hi])
        act_scr[:, lo:hi] = (jax.nn.silu(a) * b).astype(BF16)
    y = x + _dot(act_scr[...], wdn_ref[...])
    if final_norm:
        y = _rms(y, fnw_ref[...])
    o_ref[...] = y


def _ffn(x, lw, fnw, tm, final_norm):
    t = x.shape[0]
    rows = pl.BlockSpec((tm, D_MODEL), lambda i: (i, 0))
    return pl.pallas_call(
        functools.partial(_ffn_body, final_norm=final_norm),
        grid=(t // tm,),
        in_specs=[rows, _full((1, D_MODEL)), _full((D_MODEL, 2 * D_FF)), _full((D_FF, D_MODEL)), _full((1, D_MODEL))],
        out_specs=rows,
        out_shape=jax.ShapeDtypeStruct((t, D_MODEL), F32),
        scratch_shapes=[pltpu.VMEM((tm, D_FF), BF16)],
        compiler_params=_cparams("parallel"),
        name="ffn",
    )(x, lw["norm2_w"], lw["w_ffn_up"], lw["w_ffn_down"], fnw)


def _row_tile(t, pref):
    tm = pref
    while t % tm:
        tm //= 2
    return tm


def kernel(x_prompt, x_sample, state_ssd, state_conv, state_s5_re, state_s5_im, cache_k, cache_v, norm1_w, w_in, conv_w, conv_b, ssd_a_log, ssd_dt_bias, ssd_d, ssd_norm_w, s5_lam_re, s5_lam_im, s5_log_dt, s5_b_re, s5_b_im, s5_c_re, s5_c_im, s5_d, s5_w_glu, attn_sinks, w_br_a, w_br_b, w_br_c, w_out, norm2_w, w_ffn_up, w_ffn_down, rel_bias, final_norm_w):
    nb, seq, _ = x_prompt.shape
    nsb, valid, _ = x_sample.shape
    depth = w_in.shape[0]
    lb = cache_k.shape[2]
    r = SAMPLE_ROWS
    assert seq % A_CHUNK == 0 and seq % WINDOW == 0 and seq % (S5_Q * SUBLANE) == 0
    assert A_CONV - 1 <= valid <= r and lb >= valid and lb == WINDOW
    nchunk = seq // A_CHUNK
    tp = nb * seq
    ts = nsb * r

    band = _bias_band(rel_bias)
    band_s = band[1].reshape(C_KV_HEADS, C_GQA, WINDOW, 2 * WINDOW)[:, :, :r]
    bias_s1 = band_s[..., :lb].reshape(C_KV_HEADS, C_GQA * r, lb)
    bias_s2 = band_s[..., lb:lb + r].reshape(C_KV_HEADS, C_GQA * r, r)
    fnw = final_norm_w.reshape(1, D_MODEL)

    xp = x_prompt.reshape(tp, D_MODEL)
    xs = jnp.pad(x_sample, ((0, 0), (0, r - valid), (0, 0))).reshape(ts, D_MODEL)
    tm_p = _row_tile(tp, 512)
    tm_s = _row_tile(ts, 512)
    rt_p = _row_tile(tp // S5_Q, 256)
    rb_p = _row_tile(seq // S5_Q, 128)
    attn_blk = _row_tile(seq // WINDOW, 4)

    new_p, new_s = [], []
    for i in range(depth):
        wi = w_in[i]
        off = [0, 512, 1280, 1288, 1800, 2312, 2440, 2568, 5640]
        w_pack = jnp.concatenate(
            [wi[:, off[0]:off[1]], wi[:, off[1]:off[2]], wi[:, off[3]:off[4]], wi[:, off[4]:off[5]],
             wi[:, off[5]:off[6]], wi[:, off[6]:off[7]],
             jnp.pad(wi[:, off[2]:off[3]], ((0, 0), (0, LANE - A_HEADS)))], axis=1).astype(BF16)
        sinks = attn_sinks[i].reshape(C_KV_HEADS, C_GQA, 1)
        lw = dict(
            norm1_w=norm1_w[i].reshape(1, D_MODEL), w_gate=wi[:, off[7]:off[8]].astype(BF16),
            conv_w=conv_w[i], conv_b=conv_b[i].reshape(1, A_CONV_DIM),
            a_log=jnp.pad(ssd_a_log[i], (0, LANE - A_HEADS)).reshape(1, LANE),
            dt_bias=jnp.pad(ssd_dt_bias[i], (0, LANE - A_HEADS)).reshape(1, LANE),
            d_exp=jnp.repeat(ssd_d[i], A_HEAD_DIM).reshape(1, A_WIDTH),
            ssd_norm_w=ssd_norm_w[i].reshape(1, A_WIDTH),
            s5_d=s5_d[i].reshape(1, B_WIDTH), w_glu=s5_w_glu[i].astype(BF16),
            w_br_a=w_br_a[i].astype(BF16), w_br_b=w_br_b[i].astype(BF16), w_br_c=w_br_c[i].astype(BF16),
            w_out=w_out[i].astype(BF16), norm2_w=norm2_w[i].reshape(1, D_MODEL),
            w_ffn_up=w_ffn_up[i].astype(BF16), w_ffn_down=w_ffn_down[i].astype(BF16),
        )
        s5w = _s5_weights(s5_lam_re[i], s5_lam_im[i], s5_log_dt[i], s5_b_re[i], s5_b_im[i],
                          s5_c_re[i], s5_c_im[i], valid)
        last = i == depth - 1

        z, xbc, u, q, k, v, dt = _inproj(xp, lw["norm1_w"], w_pack, tm_p)
        ya, ssd_h = _ssd(xbc, z, dt, lw, nb, nchunk, A_CHUNK, A_CHUNK)
        u2 = u.reshape(tp // S5_Q, S5_Q, B_WIDTH)
        sre, sim = _s5_state(u2, s5w["bst_re"], s5w["bst_im"], rt_p)
        hre, him, fre, fim = _s5_scan(sre, sim, s5w["aq_re"], s5w["aq_im"], nb, rb_p)
        yb = _s5_out(u2, hre, him, s5w["toep"], s5w["cst_re"], s5w["cst_im"], rt_p).reshape(tp, B_WIDTH)
        yc = _attn_prompt(q, k, v, band, jnp.broadcast_to(sinks[:, :, None, :], (C_KV_HEADS, C_GQA, WINDOW, 1))
                          .reshape(C_KV_HEADS, C_GQA * WINDOW, 1), nb, seq // WINDOW, attn_blk)
        x1 = _merge(xp, ya, yb, u, yc, lw, tm_p)
        xp = _ffn(x1, lw, fnw, tm_p, last)
        new_p.append((
            ssd_h.reshape(nb, A_HEADS, A_HEAD_DIM, A_STATE),
            xbc.reshape(nb, seq, A_CONV_DIM)[:, seq - (A_CONV - 1):],
            fre.reshape(nb, B_GROUPS, B_STATE), fim.reshape(nb, B_GROUPS, B_STATE),
            k.reshape(nb, seq, C_KV_HEADS, C_HEAD_DIM)[:, seq - WINDOW:],
            v.reshape(nb, seq, C_KV_HEADS, C_HEAD_DIM)[:, seq - WINDOW:]))

        z, xbc, u, q, k, v, dt = _inproj(xs, lw["norm1_w"], w_pack, tm_s)
        conv0 = jnp.pad(state_conv[i], ((0, 0), (r - (A_CONV - 1), 0), (0, 0))).reshape(ts, A_CONV_DIM)
        ya, ssd_h = _ssd(xbc, z, dt, lw, nsb, 1, r, valid,
                         h0=state_ssd[i].reshape(nsb, A_WIDTH, A_STATE), conv0=conv0)
        u2 = u.reshape(nsb, S5_Q, B_WIDTH)
        hre = state_s5_re[i].reshape(nsb, S5_STATE_COLS)
        him = state_s5_im[i].reshape(nsb, S5_STATE_COLS)
        fre, fim = _s5_final(u2, s5w["bsts_re"], s5w["bsts_im"], hre, him, s5w["av_re"], s5w["av_im"])
        yb = _s5_out(u2, hre, him, s5w["toep"], s5w["cst_re"], s5w["cst_im"], _row_tile(nsb, 256)).reshape(ts, B_WIDTH)
        kbuf = cache_k[i].reshape(nsb, lb, C_KV_WIDTH)
        vbuf = cache_v[i].reshape(nsb, lb, C_KV_WIDTH)
        yc = _attn_sample(q, k, v, kbuf, vbuf, bias_s1, bias_s2,
                          jnp.broadcast_to(sinks[:, :, None, :], (C_KV_HEADS, C_GQA, r, 1))
                          .reshape(C_KV_HEADS, C_GQA * r, 1), nsb)
        x1 = _merge(xs, ya, yb, u, yc, lw, tm_s)
        xs = _ffn(x1, lw, fnw, tm_s, last)
        kn = k.reshape(nsb, r, C_KV_WIDTH)[:, :valid]
        vn = v.reshape(nsb, r, C_KV_WIDTH)[:, :valid]
        new_s.append((
            ssd_h.reshape(nsb, A_HEADS, A_HEAD_DIM, A_STATE),
            xbc.reshape(nsb, r, A_CONV_DIM)[:, valid - (A_CONV - 1):valid],
            fre.reshape(nsb, B_GROUPS, B_STATE), fim.reshape(nsb, B_GROUPS, B_STATE),
            jnp.concatenate([kbuf[:, valid:], kn], axis=1).reshape(nsb, lb, C_KV_HEADS, C_HEAD_DIM),
            jnp.concatenate([vbuf[:, valid:], vn], axis=1).reshape(nsb, lb, C_KV_HEADS, C_HEAD_DIM)))

    def stack(states, j):
        return jnp.stack([s[j] for s in states], axis=0)

    y_prompt = xp.reshape(nb, seq, D_MODEL)
    y_sample = xs.reshape(nsb, r, D_MODEL)[:, :valid]
    return (y_prompt, y_sample,
            stack(new_p, 0), stack(new_p, 1), stack(new_p, 2), stack(new_p, 3), stack(new_p, 4), stack(new_p, 5),
            stack(new_s, 0), stack(new_s, 1), stack(new_s, 2), stack(new_s, 3), stack(new_s, 4), stack(new_s, 5))
```

```python
import functools
import math

import numpy as np
import jax
import jax.numpy as jnp
from jax import lax
from jax.experimental import pallas as pl
from jax.experimental.pallas import tpu as pltpu

F32 = jnp.float32
BF16 = jnp.bfloat16
HIGHEST = lax.Precision.HIGHEST

D_MODEL = 1024
A_HEAD_DIM = 64
A_WIDTH = 512
A_HEADS = 8
A_GROUPS = 2
A_STATE = 64
A_CONV = 4
A_CONV_DIM = 768
A_CHUNK = 128
B_CH = 16
B_WIDTH = 512
B_GROUPS = 32
B_STATE = 64
C_HEAD_DIM = 64
C_WIDTH = 512
C_HEADS = 8
C_KV_HEADS = 2
C_GQA = 4
C_KV_WIDTH = 128
WINDOW = 128
REL_BUCKETS = 32
REL_MAX_DIST = 128
D_FF = 2816
EPS = 1e-6

P_Z = 0
P_XBC = P_Z + A_WIDTH
P_U = P_XBC + A_CONV_DIM
P_Q = P_U + B_WIDTH
P_K = P_Q + C_WIDTH
P_V = P_K + C_KV_WIDTH
P_DT = P_V + C_KV_WIDTH
P_COLS = P_DT + 128

LANE = 128
SUBLANE = 8
S5_Q = 8
S5_OCT = 8
N_OCT = B_GROUPS // S5_OCT
S5_STATE_COLS = B_GROUPS * B_STATE
SAMPLE_ROWS = 8
NEG = -1e30
VMEM_LIMIT = 56 * 1024 * 1024
FF_CHUNK = 256


def _cparams(*sem):
    return pltpu.CompilerParams(dimension_semantics=sem, vmem_limit_bytes=VMEM_LIMIT)


def _rms(x, w):
    return x * lax.rsqrt(jnp.mean(x * x, axis=-1, keepdims=True) + EPS) * w


def _dot(a, b):
    return jnp.dot(a, b, preferred_element_type=F32)


def _dot_nt(a, b, precision=None):
    return lax.dot_general(a, b, (((1,), (1,)), ((), ())), preferred_element_type=F32, precision=precision)


def _dot_tn(a, b):
    return lax.dot_general(a, b, (((0,), (0,)), ((), ())), preferred_element_type=F32)


def _full(shape):
    n = len(shape)
    return pl.BlockSpec(shape, lambda *_: (0,) * n)


def _inproj_body(x_ref, nw_ref, w_ref, z_ref, xbc_ref, u_ref, q_ref, k_ref, v_ref, dt_ref):
    h = _rms(x_ref[...], nw_ref[...]).astype(BF16)

    def proj(lo, hi):
        return _dot(h, w_ref[:, lo:hi])

    z_ref[...] = proj(P_Z, P_XBC).astype(BF16)
    xbc_ref[...] = proj(P_XBC, P_U)
    u_ref[...] = proj(P_U, P_Q)
    q_ref[...] = proj(P_Q, P_K).astype(BF16)
    k_ref[...] = proj(P_K, P_V)
    v_ref[...] = proj(P_V, P_DT)
    dt_ref[...] = proj(P_DT, P_COLS)


def _inproj(x, nw, w, tm):
    t = x.shape[0]
    widths = [(A_WIDTH, BF16), (A_CONV_DIM, F32), (B_WIDTH, F32), (C_WIDTH, BF16),
              (C_KV_WIDTH, F32), (C_KV_WIDTH, F32), (LANE, F32)]
    return pl.pallas_call(
        _inproj_body,
        grid=(t // tm,),
        in_specs=[pl.BlockSpec((tm, D_MODEL), lambda i: (i, 0)), _full((1, D_MODEL)), _full((D_MODEL, P_COLS))],
        out_specs=[pl.BlockSpec((tm, wd), lambda i: (i, 0)) for wd, _ in widths],
        out_shape=[jax.ShapeDtypeStruct((t, wd), dt) for wd, dt in widths],
        compiler_params=_cparams("parallel"),
        name="inproj",
    )(x, nw, w)


def _ssd_body(*refs, q, valid, has_init, nchunks):
    if has_init:
        (xbc_ref, z_ref, dt_ref, cw_ref, cb_ref, alog_ref, dtb_ref, dexp_ref, nw_ref, h0_ref, conv0_ref,
         y_ref, hout_ref, h_scr, tail_scr) = refs
    else:
        (xbc_ref, z_ref, dt_ref, cw_ref, cb_ref, alog_ref, dtb_ref, dexp_ref, nw_ref,
         y_ref, hout_ref, h_scr, tail_scr) = refs
    c = pl.program_id(1)

    @pl.when(c == 0)
    def _():
        if has_init:
            h_scr[...] = h0_ref[0]
            tail_scr[...] = conv0_ref[...]
        else:
            h_scr[...] = jnp.zeros_like(h_scr)
            tail_scr[...] = jnp.zeros_like(tail_scr)

    xbc = xbc_ref[...]
    ext = jnp.concatenate([tail_scr[...], xbc], axis=0)
    acc = cb_ref[...] + cw_ref[A_CONV - 1:A_CONV, :] * xbc
    for j in range(1, A_CONV):
        acc = acc + cw_ref[A_CONV - 1 - j:A_CONV - j, :] * pltpu.roll(ext, j, axis=0)[SUBLANE:SUBLANE + q]
    tail_scr[...] = xbc[q - SUBLANE:q]
    xc = jax.nn.silu(acc)
    x = xc[:, :A_WIDTH]
    bm = xc[:, A_WIDTH:A_WIDTH + A_GROUPS * A_STATE].astype(BF16)
    cm = xc[:, A_WIDTH + A_GROUPS * A_STATE:].astype(BF16)

    dt = jax.nn.softplus(dt_ref[...] + dtb_ref[...])
    if valid < q:
        dt = jnp.where(lax.broadcasted_iota(jnp.int32, dt.shape, 0) < valid, dt, 0.0)
    dta = dt * (-jnp.exp(alog_ref[...]))
    row = lax.broadcasted_iota(jnp.int32, (q, q), 0)
    col = lax.broadcasted_iota(jnp.int32, (q, q), 1)
    tri = row >= col
    a_cum = jnp.dot(tri.astype(F32), dta, preferred_element_type=F32, precision=HIGHEST)

    def expand(v):
        return jnp.concatenate([jnp.broadcast_to(v[:, k:k + 1], (q, A_HEAD_DIM)) for k in range(A_HEADS)], axis=1)

    a_cum_e = expand(a_cum)
    xs = x * expand(dt)
    xs_b = xs.astype(BF16)
    a_cum_t = a_cum.T

    h = h_scr[...]
    h_b = h.astype(BF16)
    hpg = A_HEADS // A_GROUPS
    gw = hpg * A_HEAD_DIM
    y_diag, y_off = [], []
    for g in range(A_GROUPS):
        cg = cm[:, g * A_STATE:(g + 1) * A_STATE]
        bg = bm[:, g * A_STATE:(g + 1) * A_STATE]
        cb = _dot_nt(cg, bg)
        for kk in range(hpg):
            k = g * hpg + kk
            seg = a_cum[:, k:k + 1] - a_cum_t[k:k + 1, :]
            decay = jnp.exp(jnp.where(tri, seg, -jnp.inf))
            y_diag.append(_dot((cb * decay).astype(BF16), xs_b[:, k * A_HEAD_DIM:(k + 1) * A_HEAD_DIM]))
        y_off.append(_dot_nt(cg, h_b[g * gw:(g + 1) * gw]))
    y = (jnp.concatenate(y_diag, axis=1) + jnp.concatenate(y_off, axis=1) * jnp.exp(a_cum_e)
         + dexp_ref[...] * x)
    y = y * jax.nn.silu(z_ref[...].astype(F32))
    y_ref[...] = _rms(y, nw_ref[...]).astype(BF16)

    last = a_cum_e[q - 1:q, :]
    xs_dec = (xs * jnp.exp(last - a_cum_e)).astype(BF16)
    dec_col = jnp.exp(a_cum_e.T[:, q - 1:q])
    upd = [_dot_tn(xs_dec[:, g * gw:(g + 1) * gw], bm[:, g * A_STATE:(g + 1) * A_STATE]) for g in range(A_GROUPS)]
    h_new = dec_col * h + jnp.concatenate(upd, axis=0)
    h_scr[...] = h_new

    @pl.when(c == nchunks - 1)
    def _():
        hout_ref[0] = h_new


def _ssd(xbc, z, dt, lw, nb, nchunks, q, valid, h0=None, conv0=None):
    has_init = h0 is not None
    rows = nb * nchunks * q
    rmap = lambda b, c: (b * nchunks + c, 0)
    in_specs = [pl.BlockSpec((q, A_CONV_DIM), rmap), pl.BlockSpec((q, A_WIDTH), rmap), pl.BlockSpec((q, LANE), rmap),
                _full((A_CONV, A_CONV_DIM)), _full((1, A_CONV_DIM)), _full((1, LANE)), _full((1, LANE)),
                _full((1, A_WIDTH)), _full((1, A_WIDTH))]
    args = [xbc, z, dt, lw["conv_w"], lw["conv_b"], lw["a_log"], lw["dt_bias"], lw["d_exp"], lw["ssd_norm_w"]]
    if has_init:
        in_specs += [pl.BlockSpec((1, A_WIDTH, A_STATE), lambda b, c: (b, 0, 0)),
                     pl.BlockSpec((SUBLANE, A_CONV_DIM), lambda b, c: (b, 0))]
        args += [h0, conv0]
    return pl.pallas_call(
        functools.partial(_ssd_body, q=q, valid=valid, has_init=has_init, nchunks=nchunks),
        grid=(nb, nchunks),
        in_specs=in_specs,
        out_specs=[pl.BlockSpec((q, A_WIDTH), rmap), pl.BlockSpec((1, A_WIDTH, A_STATE), lambda b, c: (b, 0, 0))],
        out_shape=[jax.ShapeDtypeStruct((rows, A_WIDTH), BF16), jax.ShapeDtypeStruct((nb, A_WIDTH, A_STATE), F32)],
        scratch_shapes=[pltpu.VMEM((A_WIDTH, A_STATE), F32), pltpu.VMEM((SUBLANE, A_CONV_DIM), F32)],
        compiler_params=_cparams("parallel", "arbitrary"),
        name="ssd",
    )(*args)


def _s5w_body(lre_ref, lim_ref, ldt_ref, btr_ref, bti_ref, cr_ref, ci_ref,
              bre_ref, bim_ref, bsre_ref, bsim_ref, cre_ref, cim_ref, toep_ref, ap_ref, cd_scr, bb_scr, *, valid):
    h, q = B_STATE, S5_Q
    for ref in (bre_ref, bim_ref, bsre_ref, bsim_ref, cre_ref, cim_ref, toep_ref):
        ref[...] = jnp.zeros(ref.shape, BF16)
    r_i = lax.broadcasted_iota(jnp.int32, (q * B_CH, q * LANE), 0)
    c_i = lax.broadcasted_iota(jnp.int32, (q * B_CH, q * LANE), 1)
    d = lax.broadcasted_iota(jnp.int32, (2 * SUBLANE, h), 0).astype(F32)
    for g8 in range(S5_OCT):
        lr, li = lre_ref[g8], lim_ref[g8]
        step = jnp.exp(ldt_ref[g8])
        er = jnp.exp(lr * step * d)
        ang = li * step * d
        ar, ai = er * jnp.cos(ang), er * jnp.sin(ang)
        nr, ni = ar[1:2] - 1.0, ai[1:2]
        den = lr * lr + li * li
        wr, wi = (nr * lr + ni * li) / den, (ni * lr - nr * li) / den
        btr, bti = btr_ref[g8], bti_ref[g8]
        bbr, bbi = wr * btr - wi * bti, wr * bti + wi * btr
        cr, ci = cr_ref[g8], ci_ref[g8]
        bb_scr[:, 0:h] = bbr
        bb_scr[:, h:2 * h] = bbi
        cols = slice(g8 * h, (g8 + 1) * h)
        for s in range(q):
            rows = slice(s * LANE + g8 * B_CH, s * LANE + (g8 + 1) * B_CH)
            e = q - 1 - s
            bre_ref[0, rows, cols] = (ar[e:e + 1] * bbr - ai[e:e + 1] * bbi).astype(BF16)
            bim_ref[0, rows, cols] = (ai[e:e + 1] * bbr + ar[e:e + 1] * bbi).astype(BF16)
            if s < valid:
                e = valid - 1 - s
                bsre_ref[0, rows, cols] = (ar[e:e + 1] * bbr - ai[e:e + 1] * bbi).astype(BF16)
                bsim_ref[0, rows, cols] = (ai[e:e + 1] * bbr + ar[e:e + 1] * bbi).astype(BF16)
            cre_ref[0, rows, cols] = (cr * ar[s + 1:s + 2] - ci * ai[s + 1:s + 2]).astype(BF16)
            cim_ref[0, rows, cols] = (-(cr * ai[s + 1:s + 2] + ci * ar[s + 1:s + 2])).astype(BF16)
            cd_scr[s * B_CH:(s + 1) * B_CH, 0:h] = cr * ar[s:s + 1] - ci * ai[s:s + 1]
            cd_scr[s * B_CH:(s + 1) * B_CH, h:2 * h] = -(cr * ai[s:s + 1] + ci * ar[s:s + 1])
        kdt = _dot_nt(bb_scr[...], cd_scr[...], precision=HIGHEST)
        place = (c_i == (r_i >> 4) * LANE + g8 * B_CH + (r_i & (B_CH - 1))).astype(F32)
        slab = jnp.dot(kdt, place, preferred_element_type=F32, precision=HIGHEST).astype(BF16)
        for s in range(q):
            toep_ref[0, s * LANE + g8 * B_CH:s * LANE + (g8 + 1) * B_CH, s * LANE:] = slab[:, 0:(q - s) * LANE]
        ap_ref[g8, 0:1, 0:h] = ar[q:q + 1]
        ap_ref[g8, 0:1, h:2 * h] = ai[q:q + 1]
        ap_ref[g8, 1:2, 0:h] = ar[valid:valid + 1]
        ap_ref[g8, 1:2, h:2 * h] = ai[valid:valid + 1]


def _s5_weights(lam_re, lam_im, log_dt, b_re, b_im, c_re, c_im, valid):
    g, h, q = B_GROUPS, B_STATE, S5_Q
    vec = pl.BlockSpec((S5_OCT, 1, h), lambda i: (i, 0, 0))
    mat = pl.BlockSpec((S5_OCT, B_CH, h), lambda i: (i, 0, 0))
    st = pl.BlockSpec((1, q * LANE, S5_OCT * h), lambda i: (i, 0, 0))
    st_shape = jax.ShapeDtypeStruct((N_OCT, q * LANE, S5_OCT * h), BF16)
    bre, bim, bsre, bsim, cre, cim, toep, ap = pl.pallas_call(
        functools.partial(_s5w_body, valid=valid),
        grid=(N_OCT,),
        in_specs=[vec, vec, vec, mat, mat, mat, mat],
        out_specs=[st] * 6 + [pl.BlockSpec((1, q * LANE, q * LANE), lambda i: (i, 0, 0)),
                              pl.BlockSpec((S5_OCT, 2, 2 * h), lambda i: (i, 0, 0))],
        out_shape=[st_shape] * 6 + [jax.ShapeDtypeStruct((N_OCT, q * LANE, q * LANE), BF16),
                                    jax.ShapeDtypeStruct((g, 2, 2 * h), F32)],
        scratch_shapes=[pltpu.VMEM((q * B_CH, 2 * h), F32), pltpu.VMEM((B_CH, 2 * h), F32)],
        compiler_params=_cparams("parallel"),
        name="s5_weights",
    )(lam_re.reshape(g, 1, h), lam_im.reshape(g, 1, h), jnp.broadcast_to(log_dt.reshape(g, 1, 1), (g, 1, h)),
      jnp.swapaxes(b_re, 1, 2), jnp.swapaxes(b_im, 1, 2), c_re, c_im)
    return dict(
        bst_re=bre, bst_im=bim, bsts_re=bsre, bsts_im=bsim, cst_re=cre, cst_im=cim, toep=toep,
        aq_re=ap[:, 0, :h].reshape(1, S5_STATE_COLS), aq_im=ap[:, 0, h:].reshape(1, S5_STATE_COLS),
        av_re=ap[:, 1, :h].reshape(1, S5_STATE_COLS), av_im=ap[:, 1, h:].reshape(1, S5_STATE_COLS),
    )


def _oct_rows(u_ref, k8):
    return jnp.concatenate(
        [u_ref[:, j, k8 * LANE:(k8 + 1) * LANE] for j in range(S5_Q)], axis=1).astype(BF16)


def _s5_state_body(u_ref, bre_ref, bim_ref, sre_ref, sim_ref):
    w = S5_OCT * B_STATE
    for k8 in range(N_OCT):
        uo = _oct_rows(u_ref, k8)
        sre_ref[:, k8 * w:(k8 + 1) * w] = _dot(uo, bre_ref[k8])
        sim_ref[:, k8 * w:(k8 + 1) * w] = _dot(uo, bim_ref[k8])


def _s5_state(u2, bre, bim, rt):
    r = u2.shape[0]
    wspec = _full((N_OCT, S5_Q * LANE, S5_OCT * B_STATE))
    ospec = pl.BlockSpec((rt, S5_STATE_COLS), lambda i: (i, 0))
    return pl.pallas_call(
        _s5_state_body,
        grid=(r // rt,),
        in_specs=[pl.BlockSpec((rt, S5_Q, B_WIDTH), lambda i: (i, 0, 0)), wspec, wspec],
        out_specs=[ospec, ospec],
        out_shape=[jax.ShapeDtypeStruct((r, S5_STATE_COLS), F32)] * 2,
        compiler_params=_cparams("parallel"),
        name="s5_state",
    )(u2, bre, bim)


def _s5_scan_body(sre_ref, sim_ref, ar_ref, ai_ref, hre_ref, him_ref, fre_ref, fim_ref, cr_scr, ci_scr, *, rb, nblk):
    i = pl.program_id(1)

    @pl.when(i == 0)
    def _():
        cr_scr[...] = jnp.zeros_like(cr_scr)
        ci_scr[...] = jnp.zeros_like(ci_scr)

    ar, ai = ar_ref[...], ai_ref[...]

    def step(r, carry):
        hr, hi = carry
        hre_ref[pl.ds(r, 1), :] = hr
        him_ref[pl.ds(r, 1), :] = hi
        sr, si = sre_ref[pl.ds(r, 1), :], sim_ref[pl.ds(r, 1), :]
        return ar * hr - ai * hi + sr, ai * hr + ar * hi + si

    hr, hi = lax.fori_loop(0, rb, step, (cr_scr[...], ci_scr[...]))
    cr_scr[...] = hr
    ci_scr[...] = hi

    @pl.when(i == nblk - 1)
    def _():
        fre_ref[0] = hr
        fim_ref[0] = hi


def _s5_scan(sre, sim, ar, ai, nb, rb):
    r = sre.shape[0]
    nblk = r // nb // rb
    blk = pl.BlockSpec((rb, S5_STATE_COLS), lambda b, i: (b * nblk + i, 0))
    vec = _full((1, S5_STATE_COLS))
    fin = pl.BlockSpec((1, 1, S5_STATE_COLS), lambda b, i: (b, 0, 0))
    return pl.pallas_call(
        functools.partial(_s5_scan_body, rb=rb, nblk=nblk),
        grid=(nb, nblk),
        in_specs=[blk, blk, vec, vec],
        out_specs=[blk, blk, fin, fin],
        out_shape=[jax.ShapeDtypeStruct((r, S5_STATE_COLS), F32)] * 2
        + [jax.ShapeDtypeStruct((nb, 1, S5_STATE_COLS), F32)] * 2,
        scratch_shapes=[pltpu.VMEM((1, S5_STATE_COLS), F32)] * 2,
        compiler_params=_cparams("parallel", "arbitrary"),
        name="s5_scan",
    )(sre, sim, ar, ai)


def _s5_final_body(u_ref, bre_ref, bim_ref, hre_ref, him_ref, ar_ref, ai_ref, fre_ref, fim_ref):
    w = S5_OCT * B_STATE
    hr, hi = hre_ref[...], him_ref[...]
    ar, ai = ar_ref[...], ai_ref[...]
    pre, pim = ar * hr - ai * hi, ai * hr + ar * hi
    for k8 in range(N_OCT):
        uo = _oct_rows(u_ref, k8)
        fre_ref[:, k8 * w:(k8 + 1) * w] = pre[:, k8 * w:(k8 + 1) * w] + _dot(uo, bre_ref[k8])
        fim_ref[:, k8 * w:(k8 + 1) * w] = pim[:, k8 * w:(k8 + 1) * w] + _dot(uo, bim_ref[k8])


def _s5_final(u2, bre, bim, hre, him, ar, ai):
    r = u2.shape[0]
    wspec = _full((N_OCT, S5_Q * LANE, S5_OCT * B_STATE))
    st = _full((r, S5_STATE_COLS))
    vec = _full((1, S5_STATE_COLS))
    return pl.pallas_call(
        _s5_final_body,
        grid=(1,),
        in_specs=[_full((r, S5_Q, B_WIDTH)), wspec, wspec, st, st, vec, vec],
        out_specs=[st, st],
        out_shape=[jax.ShapeDtypeStruct((r, S5_STATE_COLS), F32)] * 2,
        compiler_params=_cparams("arbitrary"),
        name="s5_final",
    )(u2, bre, bim, hre, him, ar, ai)


def _s5_out_body(u_ref, hre_ref, him_ref, toep_ref, cre_ref, cim_ref, y_ref):
    w = S5_OCT * B_STATE
    for k8 in range(N_OCT):
        uo = _oct_rows(u_ref, k8)
        yo = (_dot(uo, toep_ref[k8])
              + _dot_nt(hre_ref[:, k8 * w:(k8 + 1) * w].astype(BF16), cre_ref[k8])
              + _dot_nt(him_ref[:, k8 * w:(k8 + 1) * w].astype(BF16), cim_ref[k8]))
        for j in range(S5_Q):
            y_ref[:, j, k8 * LANE:(k8 + 1) * LANE] = yo[:, j * LANE:(j + 1) * LANE]


def _s5_out(u2, hre, him, toep, cre, cim, rt):
    r = u2.shape[0]
    ublk = pl.BlockSpec((rt, S5_Q, B_WIDTH), lambda i: (i, 0, 0))
    hblk = pl.BlockSpec((rt, S5_STATE_COLS), lambda i: (i, 0))
    return pl.pallas_call(
        _s5_out_body,
        grid=(r // rt,),
        in_specs=[ublk, hblk, hblk, _full((N_OCT, S5_Q * LANE, S5_Q * LANE)),
                  _full((N_OCT, S5_Q * LANE, S5_OCT * B_STATE)), _full((N_OCT, S5_Q * LANE, S5_OCT * B_STATE))],
        out_specs=ublk,
        out_shape=jax.ShapeDtypeStruct((r, S5_Q, B_WIDTH), F32),
        compiler_params=_cparams("parallel"),
        name="s5_out",
    )(u2, hre, him, toep, cre, cim)


def _band_body(bucket_ref, table_ref, o_ref):
    bucket = bucket_ref[...]
    col = lax.broadcasted_iota(jnp.int32, (WINDOW, 2 * WINDOW), 1)
    for hq in range(C_HEADS):
        base = jnp.full((1, 2 * WINDOW), NEG, F32)
        for b in range(REL_BUCKETS):
            base = jnp.where(bucket == b, table_ref[b, hq], base)
        tile = pltpu.roll(jnp.broadcast_to(base, (WINDOW, 2 * WINDOW)), 0, 1, stride=1, stride_axis=0)
        o_ref[1, hq] = tile.T
        o_ref[0, hq] = jnp.where(col >= WINDOW, tile, NEG).T


def _bias_band(table):
    lane = np.arange(2 * WINDOW)
    n = WINDOW - lane
    exact = REL_BUCKETS // 2
    nf = np.maximum(n, 1).astype(np.float32)
    large = exact + (np.log(nf / np.float32(exact)) / np.float32(math.log(REL_MAX_DIST / exact))
                     * np.float32(REL_BUCKETS - exact)).astype(np.int32)
    bucket = np.where(n < exact, n, np.minimum(large, REL_BUCKETS - 1))
    bucket = np.where((n >= 0) & (n < WINDOW), bucket, -1).astype(np.int32)
    return pl.pallas_call(
        _band_body,
        in_specs=[pl.BlockSpec(memory_space=pltpu.VMEM), pl.BlockSpec(memory_space=pltpu.SMEM)],
        out_specs=pl.BlockSpec(memory_space=pltpu.VMEM),
        out_shape=jax.ShapeDtypeStruct((2, C_HEADS, 2 * WINDOW, WINDOW), F32),
        name="rel_bias_band",
    )(jnp.asarray(bucket).reshape(1, 2 * WINDOW), table)


def _attn_prompt_body(q_ref, kc_ref, kp_ref, vc_ref, vp_ref, b0_ref, br_ref, sink_ref, o_ref, *, nblk):
    k_all = jnp.concatenate([kp_ref[...], kc_ref[...]], axis=0) * (C_HEAD_DIM ** -0.5)
    v_all = jnp.concatenate([vp_ref[...], vc_ref[...]], axis=0)
    swapped = (pltpu.roll(k_all, C_HEAD_DIM, 1), pltpu.roll(v_all, C_HEAD_DIM, 1))
    low = lax.broadcasted_iota(jnp.int32, k_all.shape, 1) < C_HEAD_DIM
    for hk in range(C_KV_HEADS):
        k_lo = k_all if hk == 0 else swapped[0]
        k_hi = swapped[0] if hk == 0 else k_all
        kz = (jnp.where(low, k_lo, 0.0).astype(BF16), jnp.where(low, 0.0, k_hi).astype(BF16))
        vz = (v_all if hk == 0 else swapped[1])[:, :C_HEAD_DIM].astype(BF16)
        for g in range(C_GQA):
            hq = hk * C_GQA + g
            sk = sink_ref[hq]
            for i in range(nblk):
                keys = slice(i * WINDOW, (i + 2) * WINDOW)
                qt = q_ref[i * WINDOW:(i + 1) * WINDOW, (hq // 2) * LANE:(hq // 2 + 1) * LANE]
                bias = b0_ref[0, hq] if i == 0 else br_ref[0, hq]
                s = _dot_nt(kz[hq % 2][keys], qt) + bias
                m = jnp.maximum(jnp.max(s, axis=0, keepdims=True), sk)
                p = jnp.exp(s - m)
                den = jnp.sum(p, axis=0, keepdims=True) + jnp.exp(sk - m)
                o = _dot_tn(vz[keys], p.astype(BF16)) / den
                o_ref[hq * C_HEAD_DIM:(hq + 1) * C_HEAD_DIM, i * WINDOW:(i + 1) * WINDOW] = o.astype(BF16)


def _attn_prompt(q, k, v, band, sinks, nb, nblk_seq, nblk):
    steps = nblk_seq // nblk
    cur = lambda b, n: (b * steps + n, 0)
    prev = lambda b, n: (b * nblk_seq + jnp.maximum(n * nblk - 1, 0), 0)
    bspec = lambda m: pl.BlockSpec((1, C_HEADS, 2 * WINDOW, WINDOW), m)
    return pl.pallas_call(
        functools.partial(_attn_prompt_body, nblk=nblk),
        grid=(nb, steps),
        in_specs=[pl.BlockSpec((nblk * WINDOW, C_WIDTH), cur),
                  pl.BlockSpec((nblk * WINDOW, C_KV_WIDTH), cur), pl.BlockSpec((WINDOW, C_KV_WIDTH), prev),
                  pl.BlockSpec((nblk * WINDOW, C_KV_WIDTH), cur), pl.BlockSpec((WINDOW, C_KV_WIDTH), prev),
                  bspec(lambda b, n: (jnp.minimum(n, 1), 0, 0, 0)), bspec(lambda b, n: (1, 0, 0, 0)),
                  pl.BlockSpec(memory_space=pltpu.SMEM)],
        out_specs=pl.BlockSpec((C_WIDTH, nblk * WINDOW), lambda b, n: (0, b * steps + n)),
        out_shape=jax.ShapeDtypeStruct((C_WIDTH, nb * nblk_seq * WINDOW), BF16),
        compiler_params=_cparams("parallel", "arbitrary"),
        name="attn_prompt",
    )(q, k, k, v, v, band, band, sinks)


def _attn_sample_body(q_ref, kn_ref, vn_ref, kb_ref, vb_ref, b1_ref, b2_ref, sink_ref, o_ref):
    r = SAMPLE_ROWS
    for hk in range(C_KV_HEADS):
        ks = slice(hk * C_HEAD_DIM, (hk + 1) * C_HEAD_DIM)
        qs = jnp.concatenate([q_ref[:, (hk * C_GQA + g) * C_HEAD_DIM:(hk * C_GQA + g + 1) * C_HEAD_DIM]
                              for g in range(C_GQA)], axis=0)
        s1 = _dot_nt(qs, kb_ref[0, :, ks].astype(BF16)) * (C_HEAD_DIM ** -0.5) + b1_ref[hk]
        s2 = _dot_nt(qs, kn_ref[:, ks].astype(BF16)) * (C_HEAD_DIM ** -0.5) + b2_ref[hk]
        sk = sink_ref[hk]
        m = jnp.maximum(jnp.maximum(jnp.max(s1, axis=-1, keepdims=True), jnp.max(s2, axis=-1, keepdims=True)), sk)
        p1, p2 = jnp.exp(s1 - m), jnp.exp(s2 - m)
        den = jnp.sum(p1, axis=-1, keepdims=True) + jnp.sum(p2, axis=-1, keepdims=True) + jnp.exp(sk - m)
        o = (_dot(p1.astype(BF16), vb_ref[0, :, ks].astype(BF16))
             + _dot(p2.astype(BF16), vn_ref[:, ks].astype(BF16))) / den
        for g in range(C_GQA):
            hq = hk * C_GQA + g
            o_ref[:, hq * C_HEAD_DIM:(hq + 1) * C_HEAD_DIM] = o[g * r:(g + 1) * r].astype(BF16)


def _attn_sample(q, kn, vn, kbuf, vbuf, b1, b2, sink_col, nsb):
    r = SAMPLE_ROWS
    lb = kbuf.shape[1]
    rows = lambda w: pl.BlockSpec((r, w), lambda i: (i, 0))
    buf = pl.BlockSpec((1, lb, C_KV_WIDTH), lambda i: (i, 0, 0))
    return pl.pallas_call(
        _attn_sample_body,
        grid=(nsb,),
        in_specs=[rows(C_WIDTH), rows(C_KV_WIDTH), rows(C_KV_WIDTH), buf, buf,
                  _full((C_KV_HEADS, C_GQA * r, lb)), _full((C_KV_HEADS, C_GQA * r, r)),
                  _full((C_KV_HEADS, C_GQA * r, 1))],
        out_specs=rows(C_WIDTH),
        out_shape=jax.ShapeDtypeStruct((nsb * r, C_WIDTH), BF16),
        compiler_params=_cparams("parallel"),
        name="attn_sample",
    )(q, kn, vn, kbuf, vbuf, b1, b2, sink_col)


def _merge_body(x_ref, ya_ref, yb_ref, u_ref, yc_ref, nw_ref, wg_ref, d5_ref, wglu_ref,
                wa_ref, wb_ref, wc_ref, wo_ref, o_ref, *, yc_features_major):
    x = x_ref[...]
    h = _rms(x, nw_ref[...]).astype(BF16)
    g = jax.nn.gelu(yb_ref[...] + d5_ref[...] * u_ref[...].astype(F32))
    gv = _dot(g.astype(BF16), wglu_ref[...])
    yb = (gv[:, :B_WIDTH] * jax.nn.sigmoid(gv[:, B_WIDTH:])).astype(BF16)

    def gate(i):
        return jax.nn.sigmoid(_dot(h, wg_ref[:, i * D_MODEL:(i + 1) * D_MODEL]))

    mixed = gate(0) * _dot(ya_ref[...], wa_ref[...])
    mixed = mixed + gate(1) * _dot(yb, wb_ref[...])
    yc_proj = _dot_tn(yc_ref[...], wc_ref[...]) if yc_features_major else _dot(yc_ref[...], wc_ref[...])
    mixed = mixed + gate(2) * yc_proj
    o_ref[...] = x + _dot(mixed.astype(BF16), wo_ref[...])


def _merge(x, ya, yb, u, yc, lw, tm, yc_features_major):
    t = x.shape[0]
    rows = lambda w: pl.BlockSpec((tm, w), lambda i: (i, 0))
    yc_spec = pl.BlockSpec((C_WIDTH, tm), lambda i: (0, i)) if yc_features_major else rows(C_WIDTH)
    return pl.pallas_call(
        functools.partial(_merge_body, yc_features_major=yc_features_major),
        grid=(t // tm,),
        in_specs=[rows(D_MODEL), rows(A_WIDTH), rows(B_WIDTH), rows(B_WIDTH), yc_spec,
                  _full((1, D_MODEL)), _full((D_MODEL, 3 * D_MODEL)), _full((1, B_WIDTH)),
                  _full((B_WIDTH, 2 * B_WIDTH)), _full((A_WIDTH, D_MODEL)), _full((B_WIDTH, D_MODEL)),
                  _full((C_WIDTH, D_MODEL)), _full((D_MODEL, D_MODEL))],
        out_specs=rows(D_MODEL),
        out_shape=jax.ShapeDtypeStruct((t, D_MODEL), F32),
        compiler_params=_cparams("parallel"),
        name="merge",
    )(x, ya, yb, u, yc, lw["norm1_w"], lw["w_gate"], lw["s5_d"], lw["w_glu"],
      lw["w_br_a"], lw["w_br_b"], lw["w_br_c"], lw["w_out"])


def _ffn_body(x_ref, nw_ref, wup_ref, wdn_ref, fnw_ref, o_ref, act_scr, *, final_norm):
    x = x_ref[...]
    h = _rms(x, nw_ref[...]).astype(BF16)
    for c in range(D_FF // FF_CHUNK):
        lo, hi = c * FF_CHUNK, (c + 1) * FF_CHUNK
        a = _dot(h, wup_ref[:, lo:hi])
        b = _dot(h, wup_ref[:, D_FF + lo:D_FF + hi])
        act_scr[:, lo:hi] = (jax.nn.silu(a) * b).astype(BF16)
    y = x + _dot(act_scr[...], wdn_ref[...])
    if final_norm:
        y = _rms(y, fnw_ref[...])
    o_ref[...] = y


def _ffn(x, lw, fnw, tm, final_norm):
    t = x.shape[0]
    rows = pl.BlockSpec((tm, D_MODEL), lambda i: (i, 0))
    return pl.pallas_call(
        functools.partial(_ffn_body, final_norm=final_norm),
        grid=(t // tm,),
        in_specs=[rows, _full((1, D_MODEL)), _full((D_MODEL, 2 * D_FF)), _full((D_FF, D_MODEL)), _full((1, D_MODEL))],
        out_specs=rows,
        out_shape=jax.ShapeDtypeStruct((t, D_MODEL), F32),
        scratch_shapes=[pltpu.VMEM((tm, D_FF), BF16)],
        compiler_params=_cparams("parallel"),
        name="ffn",
    )(x, lw["norm2_w"], lw["w_ffn_up"], lw["w_ffn_down"], fnw)


def _row_tile(t, pref):
    tm = pref
    while t % tm:
        tm //= 2
    return tm


def kernel(x_prompt, x_sample, state_ssd, state_conv, state_s5_re, state_s5_im, cache_k, cache_v, norm1_w, w_in, conv_w, conv_b, ssd_a_log, ssd_dt_bias, ssd_d, ssd_norm_w, s5_lam_re, s5_lam_im, s5_log_dt, s5_b_re, s5_b_im, s5_c_re, s5_c_im, s5_d, s5_w_glu, attn_sinks, w_br_a, w_br_b, w_br_c, w_out, norm2_w, w_ffn_up, w_ffn_down, rel_bias, final_norm_w):
    nb, seq, _ = x_prompt.shape
    nsb, valid, _ = x_sample.shape
    depth = w_in.shape[0]
    lb = cache_k.shape[2]
    r = SAMPLE_ROWS
    assert seq % A_CHUNK == 0 and seq % WINDOW == 0 and seq % (S5_Q * SUBLANE) == 0
    assert A_CONV - 1 <= valid <= r and lb >= valid and lb == WINDOW
    nchunk = seq // A_CHUNK
    tp = nb * seq
    ts = nsb * r

    band = _bias_band(rel_bias)
    band_s = jnp.swapaxes(band[1, :, :, :r], 1, 2).reshape(C_KV_HEADS, C_GQA * r, 2 * WINDOW)
    bias_s1 = band_s[..., :lb]
    bias_s2 = band_s[..., lb:lb + r]
    fnw = final_norm_w.reshape(1, D_MODEL)

    xp = x_prompt.reshape(tp, D_MODEL)
    xs = jnp.pad(x_sample, ((0, 0), (0, r - valid), (0, 0))).reshape(ts, D_MODEL)
    tm_p = _row_tile(tp, 512)
    tm_s = _row_tile(ts, 512)
    rt_p = _row_tile(tp // S5_Q, 256)
    rb_p = _row_tile(seq // S5_Q, 128)
    attn_blk = _row_tile(seq // WINDOW, 4)

    new_p, new_s = [], []
    for i in range(depth):
        wi = w_in[i]
        off = [0, 512, 1280, 1288, 1800, 2312, 2440, 2568, 5640]
        w_pack = jnp.concatenate(
            [wi[:, off[0]:off[1]], wi[:, off[1]:off[2]], wi[:, off[3]:off[4]], wi[:, off[4]:off[5]],
             wi[:, off[5]:off[6]], wi[:, off[6]:off[7]],
             jnp.pad(wi[:, off[2]:off[3]], ((0, 0), (0, LANE - A_HEADS)))], axis=1).astype(BF16)
        sinks = attn_sinks[i].reshape(C_KV_HEADS, C_GQA, 1)
        lw = dict(
            norm1_w=norm1_w[i].reshape(1, D_MODEL), w_gate=wi[:, off[7]:off[8]].astype(BF16),
            conv_w=conv_w[i], conv_b=conv_b[i].reshape(1, A_CONV_DIM),
            a_log=jnp.pad(ssd_a_log[i], (0, LANE - A_HEADS)).reshape(1, LANE),
            dt_bias=jnp.pad(ssd_dt_bias[i], (0, LANE - A_HEADS)).reshape(1, LANE),
            d_exp=jnp.repeat(ssd_d[i], A_HEAD_DIM).reshape(1, A_WIDTH),
            ssd_norm_w=ssd_norm_w[i].reshape(1, A_WIDTH),
            s5_d=s5_d[i].reshape(1, B_WIDTH), w_glu=s5_w_glu[i].astype(BF16),
            w_br_a=w_br_a[i].astype(BF16), w_br_b=w_br_b[i].astype(BF16), w_br_c=w_br_c[i].astype(BF16),
            w_out=w_out[i].astype(BF16), norm2_w=norm2_w[i].reshape(1, D_MODEL),
            w_ffn_up=w_ffn_up[i].astype(BF16), w_ffn_down=w_ffn_down[i].astype(BF16),
        )
        s5w = _s5_weights(s5_lam_re[i], s5_lam_im[i], s5_log_dt[i], s5_b_re[i], s5_b_im[i],
                          s5_c_re[i], s5_c_im[i], valid)
        last = i == depth - 1

        z, xbc, u, q, k, v, dt = _inproj(xp, lw["norm1_w"], w_pack, tm_p)
        ya, ssd_h = _ssd(xbc, z, dt, lw, nb, nchunk, A_CHUNK, A_CHUNK)
        u2 = u.reshape(tp // S5_Q, S5_Q, B_WIDTH)
        sre, sim = _s5_state(u2, s5w["bst_re"], s5w["bst_im"], rt_p)
        hre, him, fre, fim = _s5_scan(sre, sim, s5w["aq_re"], s5w["aq_im"], nb, rb_p)
        yb = _s5_out(u2, hre, him, s5w["toep"], s5w["cst_re"], s5w["cst_im"], rt_p).reshape(tp, B_WIDTH)
        yc = _attn_prompt(q, k, v, band, attn_sinks[i], nb, seq // WINDOW, attn_blk)
        x1 = _merge(xp, ya, yb, u, yc, lw, tm_p, True)
        xp = _ffn(x1, lw, fnw, tm_p, last)
        new_p.append((
            ssd_h.reshape(nb, A_HEADS, A_HEAD_DIM, A_STATE),
            xbc.reshape(nb, seq, A_CONV_DIM)[:, seq - (A_CONV - 1):],
            fre.reshape(nb, B_GROUPS, B_STATE), fim.reshape(nb, B_GROUPS, B_STATE),
            k.reshape(nb, seq, C_KV_HEADS, C_HEAD_DIM)[:, seq - WINDOW:],
            v.reshape(nb, seq, C_KV_HEADS, C_HEAD_DIM)[:, seq - WINDOW:]))

        z, xbc, u, q, k, v, dt = _inproj(xs, lw["norm1_w"], w_pack, tm_s)
        conv0 = jnp.pad(state_conv[i], ((0, 0), (r - (A_CONV - 1), 0), (0, 0))).reshape(ts, A_CONV_DIM)
        ya, ssd_h = _ssd(xbc, z, dt, lw, nsb, 1, r, valid,
                         h0=state_ssd[i].reshape(nsb, A_WIDTH, A_STATE), conv0=conv0)
        u2 = u.reshape(nsb, S5_Q, B_WIDTH)
        hre = state_s5_re[i].reshape(nsb, S5_STATE_COLS)
        him = state_s5_im[i].reshape(nsb, S5_STATE_COLS)
        fre, fim = _s5_final(u2, s5w["bsts_re"], s5w["bsts_im"], hre, him, s5w["av_re"], s5w["av_im"])
        yb = _s5_out(u2, hre, him, s5w["toep"], s5w["cst_re"], s5w["cst_im"], _row_tile(nsb, 256)).reshape(ts, B_WIDTH)
        kbuf = cache_k[i].reshape(nsb, lb, C_KV_WIDTH)
        vbuf = cache_v[i].reshape(nsb, lb, C_KV_WIDTH)
        yc = _attn_sample(q, k, v, kbuf, vbuf, bias_s1, bias_s2,
                          jnp.broadcast_to(sinks[:, :, None, :], (C_KV_HEADS, C_GQA, r, 1))
                          .reshape(C_KV_HEADS, C_GQA * r, 1), nsb)
        x1 = _merge(xs, ya, yb, u, yc, lw, tm_s, False)
        xs = _ffn(x1, lw, fnw, tm_s, last)
        kn = k.reshape(nsb, r, C_KV_WIDTH)[:, :valid]
        vn = v.reshape(nsb, r, C_KV_WIDTH)[:, :valid]
        new_s.append((
            ssd_h.reshape(nsb, A_HEADS, A_HEAD_DIM, A_STATE),
            xbc.reshape(nsb, r, A_CONV_DIM)[:, valid - (A_CONV - 1):valid],
            fre.reshape(nsb, B_GROUPS, B_STATE), fim.reshape(nsb, B_GROUPS, B_STATE),
            jnp.concatenate([kbuf[:, valid:], kn], axis=1).reshape(nsb, lb, C_KV_HEADS, C_HEAD_DIM),
            jnp.concatenate([vbuf[:, valid:], vn], axis=1).reshape(nsb, lb, C_KV_HEADS, C_HEAD_DIM)))

    def stack(states, j):
        return jnp.stack([s[j] for s in states], axis=0)

    y_prompt = xp.reshape(nb, seq, D_MODEL)
    y_sample = xs.reshape(nsb, r, D_MODEL)[:, :valid]
    return (y_prompt, y_sample,
            stack(new_p, 0), stack(new_p, 1), stack(new_p, 2), stack(new_p, 3), stack(new_p, 4), stack(new_p, 5),
            stack(new_s, 0), stack(new_s, 1), stack(new_s, 2), stack(new_s, 3), stack(new_s, 4), stack(new_s, 5))
```

```python
import functools
import math

import numpy as np
import jax
import jax.numpy as jnp
from jax import lax
from jax.experimental import pallas as pl
from jax.experimental.pallas import tpu as pltpu

F32 = jnp.float32
BF16 = jnp.bfloat16
HIGHEST = lax.Precision.HIGHEST

D_MODEL = 1024
A_HEAD_DIM = 64
A_WIDTH = 512
A_HEADS = 8
A_GROUPS = 2
A_STATE = 64
A_CONV = 4
A_CONV_DIM = 768
A_CHUNK = 128
B_CH = 16
B_WIDTH = 512
B_GROUPS = 32
B_STATE = 64
C_HEAD_DIM = 64
C_WIDTH = 512
C_HEADS = 8
C_KV_HEADS = 2
C_GQA = 4
C_KV_WIDTH = 128
WINDOW = 128
REL_BUCKETS = 32
REL_MAX_DIST = 128
D_FF = 2816
EPS = 1e-6

P_Z = 0
P_XBC = P_Z + A_WIDTH
P_U = P_XBC + A_CONV_DIM
P_Q = P_U + B_WIDTH
P_K = P_Q + C_WIDTH
P_V = P_K + C_KV_WIDTH
P_DT = P_V + C_KV_WIDTH
P_COLS = P_DT + 128

LANE = 128
SUBLANE = 8
S5_Q = 8
S5_OCT = 8
N_OCT = B_GROUPS // S5_OCT
S5_STATE_COLS = B_GROUPS * B_STATE
NEG = -1e30
VMEM_LIMIT = 56 * 1024 * 1024
FF_CHUNK = 256


def _cparams(*sem):
    return pltpu.CompilerParams(dimension_semantics=sem, vmem_limit_bytes=VMEM_LIMIT)


def _rms(x, w):
    return x * lax.rsqrt(jnp.mean(x * x, axis=-1, keepdims=True) + EPS) * w


def _dot(a, b):
    return jnp.dot(a, b, preferred_element_type=F32)


def _dot_nt(a, b, precision=None):
    return lax.dot_general(a, b, (((1,), (1,)), ((), ())), preferred_element_type=F32, precision=precision)


def _dot_tn(a, b):
    return lax.dot_general(a, b, (((0,), (0,)), ((), ())), preferred_element_type=F32)


def _full(shape):
    n = len(shape)
    return pl.BlockSpec(shape, lambda *_: (0,) * n)


def _inproj_body(x_ref, nw_ref, w_ref, z_ref, xbc_ref, u_ref, q_ref, k_ref, v_ref, dt_ref):
    h = _rms(x_ref[...], nw_ref[...]).astype(BF16)

    def proj(lo, hi):
        return _dot(h, w_ref[:, lo:hi])

    z_ref[...] = proj(P_Z, P_XBC).astype(BF16)
    xbc_ref[...] = proj(P_XBC, P_U)
    u_ref[...] = proj(P_U, P_Q)
    q_ref[...] = proj(P_Q, P_K).astype(BF16)
    k_ref[...] = proj(P_K, P_V)
    v_ref[...] = proj(P_V, P_DT)
    dt_ref[...] = proj(P_DT, P_COLS)


def _inproj(x, nw, w, tm):
    t = x.shape[0]
    widths = [(A_WIDTH, BF16), (A_CONV_DIM, F32), (B_WIDTH, F32), (C_WIDTH, BF16),
              (C_KV_WIDTH, F32), (C_KV_WIDTH, F32), (LANE, F32)]
    return pl.pallas_call(
        _inproj_body,
        grid=(t // tm,),
        in_specs=[pl.BlockSpec((tm, D_MODEL), lambda i: (i, 0)), _full((1, D_MODEL)), _full((D_MODEL, P_COLS))],
        out_specs=[pl.BlockSpec((tm, wd), lambda i: (i, 0)) for wd, _ in widths],
        out_shape=[jax.ShapeDtypeStruct((t, wd), dt) for wd, dt in widths],
        compiler_params=_cparams("parallel"),
        name="inproj",
    )(x, nw, w)


def _ssd_body(xbc_ref, z_ref, dt_ref, cw_ref, cb_ref, alog_ref, dtb_ref, dexp_ref, nw_ref,
              y_ref, hout_ref, h_scr, tail_scr, *, q, nchunks):
    c = pl.program_id(1)

    @pl.when(c == 0)
    def _():
        h_scr[...] = jnp.zeros_like(h_scr)
        tail_scr[...] = jnp.zeros_like(tail_scr)

    xbc = xbc_ref[...]
    ext = jnp.concatenate([tail_scr[...], xbc], axis=0)
    acc = cb_ref[...] + cw_ref[A_CONV - 1:A_CONV, :] * xbc
    for j in range(1, A_CONV):
        acc = acc + cw_ref[A_CONV - 1 - j:A_CONV - j, :] * pltpu.roll(ext, j, axis=0)[SUBLANE:SUBLANE + q]
    tail_scr[...] = xbc[q - SUBLANE:q]
    xc = jax.nn.silu(acc)
    x = xc[:, :A_WIDTH]
    bm = xc[:, A_WIDTH:A_WIDTH + A_GROUPS * A_STATE].astype(BF16)
    cm = xc[:, A_WIDTH + A_GROUPS * A_STATE:].astype(BF16)

    dt = jax.nn.softplus(dt_ref[...] + dtb_ref[...])
    dta = dt * (-jnp.exp(alog_ref[...]))
    row = lax.broadcasted_iota(jnp.int32, (q, q), 0)
    col = lax.broadcasted_iota(jnp.int32, (q, q), 1)
    tri = row >= col
    a_cum = jnp.dot(tri.astype(F32), dta, preferred_element_type=F32, precision=HIGHEST)

    def expand(v):
        return jnp.concatenate([jnp.broadcast_to(v[:, k:k + 1], (q, A_HEAD_DIM)) for k in range(A_HEADS)], axis=1)

    a_cum_e = expand(a_cum)
    xs = x * expand(dt)
    xs_b = xs.astype(BF16)
    a_cum_t = a_cum.T

    h = h_scr[...]
    h_b = h.astype(BF16)
    hpg = A_HEADS // A_GROUPS
    gw = hpg * A_HEAD_DIM
    y_diag, y_off = [], []
    for g in range(A_GROUPS):
        cg = cm[:, g * A_STATE:(g + 1) * A_STATE]
        bg = bm[:, g * A_STATE:(g + 1) * A_STATE]
        cb = _dot_nt(cg, bg)
        for kk in range(hpg):
            k = g * hpg + kk
            seg = a_cum[:, k:k + 1] - a_cum_t[k:k + 1, :]
            decay = jnp.exp(jnp.where(tri, seg, -jnp.inf))
            y_diag.append(_dot((cb * decay).astype(BF16), xs_b[:, k * A_HEAD_DIM:(k + 1) * A_HEAD_DIM]))
        y_off.append(_dot_nt(cg, h_b[g * gw:(g + 1) * gw]))
    y = (jnp.concatenate(y_diag, axis=1) + jnp.concatenate(y_off, axis=1) * jnp.exp(a_cum_e)
         + dexp_ref[...] * x)
    y = y * jax.nn.silu(z_ref[...].astype(F32))
    y_ref[...] = _rms(y, nw_ref[...]).astype(BF16)

    last = a_cum_e[q - 1:q, :]
    xs_dec = (xs * jnp.exp(last - a_cum_e)).astype(BF16)
    dec_col = jnp.exp(a_cum_e.T[:, q - 1:q])
    upd = [_dot_tn(xs_dec[:, g * gw:(g + 1) * gw], bm[:, g * A_STATE:(g + 1) * A_STATE]) for g in range(A_GROUPS)]
    h_new = dec_col * h + jnp.concatenate(upd, axis=0)
    h_scr[...] = h_new

    @pl.when(c == nchunks - 1)
    def _():
        hout_ref[0] = h_new


def _ssd(xbc, z, dt, lw, nb, nchunks, q):
    rows = nb * nchunks * q
    rmap = lambda b, c: (b * nchunks + c, 0)
    in_specs = [pl.BlockSpec((q, A_CONV_DIM), rmap), pl.BlockSpec((q, A_WIDTH), rmap), pl.BlockSpec((q, LANE), rmap),
                _full((A_CONV, A_CONV_DIM)), _full((1, A_CONV_DIM)), _full((1, LANE)), _full((1, LANE)),
                _full((1, A_WIDTH)), _full((1, A_WIDTH))]
    return pl.pallas_call(
        functools.partial(_ssd_body, q=q, nchunks=nchunks),
        grid=(nb, nchunks),
        in_specs=in_specs,
        out_specs=[pl.BlockSpec((q, A_WIDTH), rmap), pl.BlockSpec((1, A_WIDTH, A_STATE), lambda b, c: (b, 0, 0))],
        out_shape=[jax.ShapeDtypeStruct((rows, A_WIDTH), BF16), jax.ShapeDtypeStruct((nb, A_WIDTH, A_STATE), F32)],
        scratch_shapes=[pltpu.VMEM((A_WIDTH, A_STATE), F32), pltpu.VMEM((SUBLANE, A_CONV_DIM), F32)],
        compiler_params=_cparams("parallel", "arbitrary"),
        name="ssd",
    )(xbc, z, dt, lw["conv_w"], lw["conv_b"], lw["a_log"], lw["dt_bias"], lw["d_exp"], lw["ssd_norm_w"])


def _ssd_sample_body(*refs, valid, aliased):
    n_in = 11 + (2 if aliased else 0)
    (xbc_ref, z_ref, dt_ref, conv0_ref, h0_ref, cw_ref, cb_ref, alog_ref, dtb_ref, dexp_ref, nwb_ref) = refs[:11]
    y_ref, conv_out_ref, h_out_ref = refs[n_in:n_in + 3]
    xs_scr, dx_scr, z_scr, y_scr, bt_scr, ct_scr, da_scr = refs[n_in + 3:]
    k = pl.program_id(0)
    ns = LANE

    @pl.when(k == 0)
    def _():
        rows = [conv0_ref[0, r] for r in range(A_CONV - 1)] + [xbc_ref[t * ns:(t + 1) * ns, :] for t in range(valid)]
        for t in range(valid):
            acc = cb_ref[...]
            for j in range(A_CONV):
                acc = acc + cw_ref[A_CONV - 1 - j:A_CONV - j, :] * rows[t + A_CONV - 1 - j]
            xc = jax.nn.silu(acc)
            dt = jax.nn.softplus(dt_ref[t * ns:(t + 1) * ns, :] + dtb_ref[...])
            dt_t = dt.T
            da_scr[t] = jnp.exp((dt * (-jnp.exp(alog_ref[...]))).T)
            x = xc[:, :A_WIDTH]
            x_t = x.T
            for kk in range(A_HEADS):
                hs = slice(kk * A_HEAD_DIM, (kk + 1) * A_HEAD_DIM)
                xs_scr[t, hs, :] = x_t[hs] * dt_t[kk:kk + 1, :]
            dx_scr[t] = (x * dexp_ref[...]).T
            bt_scr[t] = xc[:, A_WIDTH:A_WIDTH + A_GROUPS * A_STATE].T
            ct_scr[t] = xc[:, A_WIDTH + A_GROUPS * A_STATE:].T
            z_scr[t] = jax.nn.silu(z_ref[t * ns:(t + 1) * ns, :].astype(F32)).T
        for r in range(A_CONV - 1):
            conv_out_ref[0, r] = rows[valid + r]

    g0 = pl.multiple_of((k // (A_HEADS // A_GROUPS)) * A_STATE, A_STATE)

    def per_p(p, carry):
        h = h0_ref[0, 0, p]
        row = k * A_HEAD_DIM + p
        for t in range(valid):
            da = da_scr[t, pl.ds(k, 1), :]
            xr = xs_scr[t, pl.ds(row, 1), :]
            h = da * h + xr * bt_scr[t, pl.ds(g0, A_STATE), :]
            y_scr[t, pl.ds(row, 1), :] = jnp.sum(h * ct_scr[t, pl.ds(g0, A_STATE), :], axis=0, keepdims=True)
        h_out_ref[0, 0, p] = h
        return carry

    lax.fori_loop(0, A_HEAD_DIM, per_p, 0, unroll=2)

    @pl.when(k == A_HEADS - 1)
    def _():
        for t in range(valid):
            y = (y_scr[t] + dx_scr[t]) * z_scr[t]
            y = y * lax.rsqrt(jnp.mean(y * y, axis=0, keepdims=True) + EPS) * nwb_ref[...]
            y_ref[:, t * ns:(t + 1) * ns] = y.astype(BF16)


def _ssd_sample(xbc, z, dt, conv_all, h_all, lw, layer, valid, conv_prev, h_prev):
    aliased = conv_prev is not None
    rows = valid * LANE
    hblk = pl.BlockSpec((1, 1, A_HEAD_DIM, A_STATE, LANE), lambda k: (layer, k, 0, 0, 0))
    cblk = pl.BlockSpec((1, A_CONV - 1, LANE, A_CONV_DIM), lambda k: (layer, 0, 0, 0))
    in_specs = [_full((rows, A_CONV_DIM)), _full((rows, A_WIDTH)), _full((rows, LANE)), cblk, hblk,
                _full((A_CONV, A_CONV_DIM)), _full((1, A_CONV_DIM)), _full((1, LANE)), _full((1, LANE)),
                _full((1, A_WIDTH)), _full((A_WIDTH, LANE))]
    args = [xbc, z, dt, conv_all, h_all, lw["conv_w"], lw["conv_b"], lw["a_log"], lw["dt_bias"], lw["d_exp"],
            lw["ssd_norm_wb"]]
    aliases = {}
    if aliased:
        in_specs += [pl.BlockSpec(memory_space=pl.ANY)] * 2
        aliases = {len(args): 1, len(args) + 1: 2}
        args += [conv_prev, h_prev]
    big = lambda n: pltpu.VMEM((valid, n, LANE), F32)
    return pl.pallas_call(
        functools.partial(_ssd_sample_body, valid=valid, aliased=aliased),
        grid=(A_HEADS,),
        in_specs=in_specs,
        out_specs=[_full((A_WIDTH, rows)), cblk, hblk],
        out_shape=[jax.ShapeDtypeStruct((A_WIDTH, rows), BF16), jax.ShapeDtypeStruct(conv_all.shape, F32),
                   jax.ShapeDtypeStruct(h_all.shape, F32)],
        scratch_shapes=[big(A_WIDTH), big(A_WIDTH), big(A_WIDTH), big(A_WIDTH), big(LANE), big(LANE), big(LANE)],
        input_output_aliases=aliases,
        compiler_params=_cparams("arbitrary"),
        name="ssd_sample",
    )(*args)


def _s5w_body(lre_ref, lim_ref, ldt_ref, btr_ref, bti_ref, cr_ref, ci_ref,
              bre_ref, bim_ref, bsre_ref, bsim_ref, cre_ref, cim_ref, toep_ref, ap_ref, cd_scr, bb_scr, *, valid):
    h, q = B_STATE, S5_Q
    for ref in (bre_ref, bim_ref, bsre_ref, bsim_ref, cre_ref, cim_ref, toep_ref):
        ref[...] = jnp.zeros(ref.shape, BF16)
    r_i = lax.broadcasted_iota(jnp.int32, (q * B_CH, q * LANE), 0)
    c_i = lax.broadcasted_iota(jnp.int32, (q * B_CH, q * LANE), 1)
    d = lax.broadcasted_iota(jnp.int32, (2 * SUBLANE, h), 0).astype(F32)
    for g8 in range(S5_OCT):
        lr, li = lre_ref[g8], lim_ref[g8]
        step = jnp.exp(ldt_ref[g8])
        er = jnp.exp(lr * step * d)
        ang = li * step * d
        ar, ai = er * jnp.cos(ang), er * jnp.sin(ang)
        nr, ni = ar[1:2] - 1.0, ai[1:2]
        den = lr * lr + li * li
        wr, wi = (nr * lr + ni * li) / den, (ni * lr - nr * li) / den
        btr, bti = btr_ref[g8], bti_ref[g8]
        bbr, bbi = wr * btr - wi * bti, wr * bti + wi * btr
        cr, ci = cr_ref[g8], ci_ref[g8]
        bb_scr[:, 0:h] = bbr
        bb_scr[:, h:2 * h] = bbi
        cols = slice(g8 * h, (g8 + 1) * h)
        for s in range(q):
            rows = slice(s * LANE + g8 * B_CH, s * LANE + (g8 + 1) * B_CH)
            e = q - 1 - s
            bre_ref[0, rows, cols] = (ar[e:e + 1] * bbr - ai[e:e + 1] * bbi).astype(BF16)
            bim_ref[0, rows, cols] = (ai[e:e + 1] * bbr + ar[e:e + 1] * bbi).astype(BF16)
            if s < valid:
                e = valid - 1 - s
                bsre_ref[0, rows, cols] = (ar[e:e + 1] * bbr - ai[e:e + 1] * bbi).astype(BF16)
                bsim_ref[0, rows, cols] = (ai[e:e + 1] * bbr + ar[e:e + 1] * bbi).astype(BF16)
            cre_ref[0, rows, cols] = (cr * ar[s + 1:s + 2] - ci * ai[s + 1:s + 2]).astype(BF16)
            cim_ref[0, rows, cols] = (-(cr * ai[s + 1:s + 2] + ci * ar[s + 1:s + 2])).astype(BF16)
            cd_scr[s * B_CH:(s + 1) * B_CH, 0:h] = cr * ar[s:s + 1] - ci * ai[s:s + 1]
            cd_scr[s * B_CH:(s + 1) * B_CH, h:2 * h] = -(cr * ai[s:s + 1] + ci * ar[s:s + 1])
        kdt = _dot_nt(bb_scr[...], cd_scr[...], precision=HIGHEST)
        place = (c_i == (r_i >> 4) * LANE + g8 * B_CH + (r_i & (B_CH - 1))).astype(F32)
        slab = jnp.dot(kdt, place, preferred_element_type=F32, precision=HIGHEST).astype(BF16)
        for s in range(q):
            toep_ref[0, s * LANE + g8 * B_CH:s * LANE + (g8 + 1) * B_CH, s * LANE:] = slab[:, 0:(q - s) * LANE]
        ap_ref[g8, 0:1, 0:h] = ar[q:q + 1]
        ap_ref[g8, 0:1, h:2 * h] = ai[q:q + 1]
        ap_ref[g8, 1:2, 0:h] = ar[valid:valid + 1]
        ap_ref[g8, 1:2, h:2 * h] = ai[valid:valid + 1]


def _s5_weights(lam_re, lam_im, log_dt, b_re, b_im, c_re, c_im, valid):
    g, h, q = B_GROUPS, B_STATE, S5_Q
    vec = pl.BlockSpec((S5_OCT, 1, h), lambda i: (i, 0, 0))
    mat = pl.BlockSpec((S5_OCT, B_CH, h), lambda i: (i, 0, 0))
    st = pl.BlockSpec((1, q * LANE, S5_OCT * h), lambda i: (i, 0, 0))
    st_shape = jax.ShapeDtypeStruct((N_OCT, q * LANE, S5_OCT * h), BF16)
    bre, bim, bsre, bsim, cre, cim, toep, ap = pl.pallas_call(
        functools.partial(_s5w_body, valid=valid),
        grid=(N_OCT,),
        in_specs=[vec, vec, vec, mat, mat, mat, mat],
        out_specs=[st] * 6 + [pl.BlockSpec((1, q * LANE, q * LANE), lambda i: (i, 0, 0)),
                              pl.BlockSpec((S5_OCT, 2, 2 * h), lambda i: (i, 0, 0))],
        out_shape=[st_shape] * 6 + [jax.ShapeDtypeStruct((N_OCT, q * LANE, q * LANE), BF16),
                                    jax.ShapeDtypeStruct((g, 2, 2 * h), F32)],
        scratch_shapes=[pltpu.VMEM((q * B_CH, 2 * h), F32), pltpu.VMEM((B_CH, 2 * h), F32)],
        compiler_params=_cparams("parallel"),
        name="s5_weights",
    )(lam_re.reshape(g, 1, h), lam_im.reshape(g, 1, h), jnp.broadcast_to(log_dt.reshape(g, 1, 1), (g, 1, h)),
      jnp.swapaxes(b_re, 1, 2), jnp.swapaxes(b_im, 1, 2), c_re, c_im)
    return dict(
        bst_re=bre, bst_im=bim, bsts_re=bsre, bsts_im=bsim, cst_re=cre, cst_im=cim, toep=toep,
        aq_re=ap[:, 0, :h].reshape(1, S5_STATE_COLS), aq_im=ap[:, 0, h:].reshape(1, S5_STATE_COLS),
        av_re=ap[:, 1, :h].reshape(1, S5_STATE_COLS), av_im=ap[:, 1, h:].reshape(1, S5_STATE_COLS),
    )


def _oct_rows(u_ref, k8):
    return jnp.concatenate(
        [u_ref[:, j, k8 * LANE:(k8 + 1) * LANE] for j in range(S5_Q)], axis=1).astype(BF16)


def _s5_state_body(u_ref, bre_ref, bim_ref, sre_ref, sim_ref):
    w = S5_OCT * B_STATE
    for k8 in range(N_OCT):
        uo = _oct_rows(u_ref, k8)
        sre_ref[:, k8 * w:(k8 + 1) * w] = _dot(uo, bre_ref[k8])
        sim_ref[:, k8 * w:(k8 + 1) * w] = _dot(uo, bim_ref[k8])


def _s5_state(u2, bre, bim, rt):
    r = u2.shape[0]
    wspec = _full((N_OCT, S5_Q * LANE, S5_OCT * B_STATE))
    ospec = pl.BlockSpec((rt, S5_STATE_COLS), lambda i: (i, 0))
    return pl.pallas_call(
        _s5_state_body,
        grid=(r // rt,),
        in_specs=[pl.BlockSpec((rt, S5_Q, B_WIDTH), lambda i: (i, 0, 0)), wspec, wspec],
        out_specs=[ospec, ospec],
        out_shape=[jax.ShapeDtypeStruct((r, S5_STATE_COLS), F32)] * 2,
        compiler_params=_cparams("parallel"),
        name="s5_state",
    )(u2, bre, bim)


def _s5_scan_body(sre_ref, sim_ref, ar_ref, ai_ref, hre_ref, him_ref, fre_ref, fim_ref, cr_scr, ci_scr, *, rb, nblk):
    i = pl.program_id(1)

    @pl.when(i == 0)
    def _():
        cr_scr[...] = jnp.zeros_like(cr_scr)
        ci_scr[...] = jnp.zeros_like(ci_scr)

    ar, ai = ar_ref[...], ai_ref[...]

    def step(r, carry):
        hr, hi = carry
        hre_ref[pl.ds(r, 1), :] = hr
        him_ref[pl.ds(r, 1), :] = hi
        sr, si = sre_ref[pl.ds(r, 1), :], sim_ref[pl.ds(r, 1), :]
        return ar * hr - ai * hi + sr, ai * hr + ar * hi + si

    hr, hi = lax.fori_loop(0, rb, step, (cr_scr[...], ci_scr[...]))
    cr_scr[...] = hr
    ci_scr[...] = hi

    @pl.when(i == nblk - 1)
    def _():
        fre_ref[0] = hr
        fim_ref[0] = hi


def _s5_scan(sre, sim, ar, ai, nb, rb):
    r = sre.shape[0]
    nblk = r // nb // rb
    blk = pl.BlockSpec((rb, S5_STATE_COLS), lambda b, i: (b * nblk + i, 0))
    vec = _full((1, S5_STATE_COLS))
    fin = pl.BlockSpec((1, 1, S5_STATE_COLS), lambda b, i: (b, 0, 0))
    return pl.pallas_call(
        functools.partial(_s5_scan_body, rb=rb, nblk=nblk),
        grid=(nb, nblk),
        in_specs=[blk, blk, vec, vec],
        out_specs=[blk, blk, fin, fin],
        out_shape=[jax.ShapeDtypeStruct((r, S5_STATE_COLS), F32)] * 2
        + [jax.ShapeDtypeStruct((nb, 1, S5_STATE_COLS), F32)] * 2,
        scratch_shapes=[pltpu.VMEM((1, S5_STATE_COLS), F32)] * 2,
        compiler_params=_cparams("parallel", "arbitrary"),
        name="s5_scan",
    )(sre, sim, ar, ai)


def _s5_sample_body(u_ref, toep_ref, cre_ref, cim_ref, bre_ref, bim_ref, hre_ref, him_ref, ar_ref, ai_ref,
                    y_ref, fre_ref, fim_ref, *, valid):
    w = S5_OCT * B_STATE
    ns = LANE
    for k8 in range(N_OCT):
        blocks = [u_ref[t * ns:(t + 1) * ns, k8 * LANE:(k8 + 1) * LANE] for t in range(valid)]
        uo = jnp.concatenate(blocks, axis=1).astype(BF16)
        uo_t = jnp.concatenate([b.T for b in blocks], axis=0).astype(BF16)
        hr, hi = hre_ref[0, k8 * w:(k8 + 1) * w, :], him_ref[0, k8 * w:(k8 + 1) * w, :]
        yo = (_dot(uo, toep_ref[k8]) + _dot_nt(hr.T.astype(BF16), cre_ref[k8])
              + _dot_nt(hi.T.astype(BF16), cim_ref[k8]))
        for t in range(valid):
            y_ref[t * ns:(t + 1) * ns, k8 * LANE:(k8 + 1) * LANE] = yo[:, t * LANE:(t + 1) * LANE]
        ar, ai = ar_ref[k8 * w:(k8 + 1) * w, :], ai_ref[k8 * w:(k8 + 1) * w, :]
        fre_ref[k8 * w:(k8 + 1) * w, :] = ar * hr - ai * hi + _dot_tn(bre_ref[k8], uo_t)
        fim_ref[k8 * w:(k8 + 1) * w, :] = ai * hr + ar * hi + _dot_tn(bim_ref[k8], uo_t)


def _s5_sample(u, s5w, hre_all, him_all, ar, ai, layer, valid):
    rows = valid * LANE
    vq = valid * LANE
    sub = lambda c: pl.BlockSpec((N_OCT, vq, c), lambda i: (0, 0, 0))
    hblk = pl.BlockSpec((1, S5_STATE_COLS, LANE), lambda i: (layer, 0, 0))
    st = _full((S5_STATE_COLS, LANE))
    return pl.pallas_call(
        functools.partial(_s5_sample_body, valid=valid),
        grid=(1,),
        in_specs=[_full((rows, B_WIDTH)), sub(vq), sub(S5_OCT * B_STATE), sub(S5_OCT * B_STATE),
                  sub(S5_OCT * B_STATE), sub(S5_OCT * B_STATE), hblk, hblk, st, st],
        out_specs=[_full((rows, B_WIDTH)), st, st],
        out_shape=[jax.ShapeDtypeStruct((rows, B_WIDTH), F32)] + [jax.ShapeDtypeStruct((S5_STATE_COLS, LANE), F32)] * 2,
        compiler_params=_cparams("arbitrary"),
        name="s5_sample",
    )(u, s5w["toep"], s5w["cst_re"], s5w["cst_im"], s5w["bsts_re"], s5w["bsts_im"], hre_all, him_all, ar, ai)


def _s5_out_body(u_ref, hre_ref, him_ref, toep_ref, cre_ref, cim_ref, y_ref):
    w = S5_OCT * B_STATE
    for k8 in range(N_OCT):
        uo = _oct_rows(u_ref, k8)
        yo = (_dot(uo, toep_ref[k8])
              + _dot_nt(hre_ref[:, k8 * w:(k8 + 1) * w].astype(BF16), cre_ref[k8])
              + _dot_nt(him_ref[:, k8 * w:(k8 + 1) * w].astype(BF16), cim_ref[k8]))
        for j in range(S5_Q):
            y_ref[:, j, k8 * LANE:(k8 + 1) * LANE] = yo[:, j * LANE:(j + 1) * LANE]


def _s5_out(u2, hre, him, toep, cre, cim, rt):
    r = u2.shape[0]
    ublk = pl.BlockSpec((rt, S5_Q, B_WIDTH), lambda i: (i, 0, 0))
    hblk = pl.BlockSpec((rt, S5_STATE_COLS), lambda i: (i, 0))
    return pl.pallas_call(
        _s5_out_body,
        grid=(r // rt,),
        in_specs=[ublk, hblk, hblk, _full((N_OCT, S5_Q * LANE, S5_Q * LANE)),
                  _full((N_OCT, S5_Q * LANE, S5_OCT * B_STATE)), _full((N_OCT, S5_Q * LANE, S5_OCT * B_STATE))],
        out_specs=ublk,
        out_shape=jax.ShapeDtypeStruct((r, S5_Q, B_WIDTH), F32),
        compiler_params=_cparams("parallel"),
        name="s5_out",
    )(u2, hre, him, toep, cre, cim)


def _band_body(bucket_ref, table_ref, o_ref):
    bucket = bucket_ref[...]
    col = lax.broadcasted_iota(jnp.int32, (WINDOW, 2 * WINDOW), 1)
    for hq in range(C_HEADS):
        base = jnp.full((1, 2 * WINDOW), NEG, F32)
        for b in range(REL_BUCKETS):
            base = jnp.where(bucket == b, table_ref[b, hq], base)
        tile = pltpu.roll(jnp.broadcast_to(base, (WINDOW, 2 * WINDOW)), 0, 1, stride=1, stride_axis=0)
        o_ref[1, hq] = tile.T
        o_ref[0, hq] = jnp.where(col >= WINDOW, tile, NEG).T


def _bias_band(table):
    lane = np.arange(2 * WINDOW)
    n = WINDOW - lane
    exact = REL_BUCKETS // 2
    nf = np.maximum(n, 1).astype(np.float32)
    large = exact + (np.log(nf / np.float32(exact)) / np.float32(math.log(REL_MAX_DIST / exact))
                     * np.float32(REL_BUCKETS - exact)).astype(np.int32)
    bucket = np.where(n < exact, n, np.minimum(large, REL_BUCKETS - 1))
    bucket = np.where((n >= 0) & (n < WINDOW), bucket, -1).astype(np.int32)
    return pl.pallas_call(
        _band_body,
        in_specs=[pl.BlockSpec(memory_space=pltpu.VMEM), pl.BlockSpec(memory_space=pltpu.SMEM)],
        out_specs=pl.BlockSpec(memory_space=pltpu.VMEM),
        out_shape=jax.ShapeDtypeStruct((2, C_HEADS, 2 * WINDOW, WINDOW), F32),
        name="rel_bias_band",
    )(jnp.asarray(bucket).reshape(1, 2 * WINDOW), table)


def _attn_prompt_body(q_ref, kc_ref, kp_ref, vc_ref, vp_ref, b0_ref, br_ref, sink_ref, o_ref, *, nblk):
    k_all = jnp.concatenate([kp_ref[...], kc_ref[...]], axis=0) * (C_HEAD_DIM ** -0.5)
    v_all = jnp.concatenate([vp_ref[...], vc_ref[...]], axis=0)
    swapped = (pltpu.roll(k_all, C_HEAD_DIM, 1), pltpu.roll(v_all, C_HEAD_DIM, 1))
    low = lax.broadcasted_iota(jnp.int32, k_all.shape, 1) < C_HEAD_DIM
    for hk in range(C_KV_HEADS):
        k_lo = k_all if hk == 0 else swapped[0]
        k_hi = swapped[0] if hk == 0 else k_all
        kz = (jnp.where(low, k_lo, 0.0).astype(BF16), jnp.where(low, 0.0, k_hi).astype(BF16))
        vz = (v_all if hk == 0 else swapped[1])[:, :C_HEAD_DIM].astype(BF16)
        for g in range(C_GQA):
            hq = hk * C_GQA + g
            sk = sink_ref[hq]
            for i in range(nblk):
                keys = slice(i * WINDOW, (i + 2) * WINDOW)
                qt = q_ref[i * WINDOW:(i + 1) * WINDOW, (hq // 2) * LANE:(hq // 2 + 1) * LANE]
                bias = b0_ref[0, hq] if i == 0 else br_ref[0, hq]
                s = _dot_nt(kz[hq % 2][keys], qt) + bias
                m = jnp.maximum(jnp.max(s, axis=0, keepdims=True), sk)
                p = jnp.exp(s - m)
                den = jnp.sum(p, axis=0, keepdims=True) + jnp.exp(sk - m)
                o = _dot_tn(vz[keys], p.astype(BF16)) / den
                o_ref[hq * C_HEAD_DIM:(hq + 1) * C_HEAD_DIM, i * WINDOW:(i + 1) * WINDOW] = o.astype(BF16)


def _attn_prompt(q, k, v, band, sinks, nb, nblk_seq, nblk):
    steps = nblk_seq // nblk
    cur = lambda b, n: (b * steps + n, 0)
    prev = lambda b, n: (b * nblk_seq + jnp.maximum(n * nblk - 1, 0), 0)
    bspec = lambda m: pl.BlockSpec((1, C_HEADS, 2 * WINDOW, WINDOW), m)
    return pl.pallas_call(
        functools.partial(_attn_prompt_body, nblk=nblk),
        grid=(nb, steps),
        in_specs=[pl.BlockSpec((nblk * WINDOW, C_WIDTH), cur),
                  pl.BlockSpec((nblk * WINDOW, C_KV_WIDTH), cur), pl.BlockSpec((WINDOW, C_KV_WIDTH), prev),
                  pl.BlockSpec((nblk * WINDOW, C_KV_WIDTH), cur), pl.BlockSpec((WINDOW, C_KV_WIDTH), prev),
                  bspec(lambda b, n: (jnp.minimum(n, 1), 0, 0, 0)), bspec(lambda b, n: (1, 0, 0, 0)),
                  pl.BlockSpec(memory_space=pltpu.SMEM)],
        out_specs=pl.BlockSpec((C_WIDTH, nblk * WINDOW), lambda b, n: (0, b * steps + n)),
        out_shape=jax.ShapeDtypeStruct((C_WIDTH, nb * nblk_seq * WINDOW), BF16),
        compiler_params=_cparams("parallel", "arbitrary"),
        name="attn_prompt",
    )(q, k, k, v, v, band, band, sinks)


def _attn_sample_body(*refs, valid, lb, nseq):
    q_ref, kn_ref, vn_ref, kc_ref, vc_ref, b1_ref, b2_ref, sink_ref = refs[:8]
    o_ref, ko_ref, vo_ref, q_scr = refs[-4:]
    d = C_HEAD_DIM
    q_scr[...] = q_ref[...].astype(F32)
    lane_c = lax.broadcasted_iota(jnp.int32, (C_KV_WIDTH, lb), 1)
    lane_n = lax.broadcasted_iota(jnp.int32, (SUBLANE, C_KV_WIDTH), 1)
    pad = jnp.zeros((SUBLANE - valid, C_KV_WIDTH), F32)
    zero_half = jnp.zeros((d, lb), BF16)

    def per_seq(s, carry):
        qs = q_scr[:, s, :]
        kn = jnp.concatenate([kn_ref[:, s, :], pad], axis=0)
        vn = jnp.concatenate([vn_ref[:, s, :], pad], axis=0)
        kt, vt = kc_ref[0, s], vc_ref[0, s]
        kn_t, vn_t = kn.T, vn.T
        new_k, new_v = pltpu.roll(kt, lb - valid, 1), pltpu.roll(vt, lb - valid, 1)
        for t in range(valid):
            at = lane_c == lb - valid + t
            new_k = jnp.where(at, jnp.broadcast_to(kn_t[:, t:t + 1], (C_KV_WIDTH, lb)), new_k)
            new_v = jnp.where(at, jnp.broadcast_to(vn_t[:, t:t + 1], (C_KV_WIDTH, lb)), new_v)
        ko_ref[0, s] = new_k
        vo_ref[0, s] = new_v

        kn_sw = pltpu.roll(kn, d, 1)
        outs = [None] * C_HEADS
        for hk in range(C_KV_HEADS):
            kth = kt[hk * d:(hk + 1) * d].astype(BF16)
            vth = vt[hk * d:(hk + 1) * d].astype(BF16)
            kz = (jnp.concatenate([kth, zero_half], axis=0), jnp.concatenate([zero_half, kth], axis=0))
            vnh = vn[:, hk * d:(hk + 1) * d].astype(BF16)
            for par in range(2):
                src = kn if hk == par else kn_sw
                knz = jnp.where((lane_n < d) if par == 0 else (lane_n >= d), src, 0.0).astype(BF16)
                qq = jnp.concatenate([qs[:, (hk * 2 + j) * LANE:(hk * 2 + j + 1) * LANE] for j in range(2)],
                                     axis=0).astype(BF16)
                s1 = _dot(qq, kz[par]) * (d ** -0.5) + b1_ref[hk, par]
                s2 = _dot_nt(qq, knz) * (d ** -0.5) + b2_ref[hk, par]
                sk = sink_ref[hk, par]
                m = jnp.maximum(jnp.maximum(jnp.max(s1, axis=-1, keepdims=True),
                                            jnp.max(s2, axis=-1, keepdims=True)), sk)
                p1, p2 = jnp.exp(s1 - m), jnp.exp(s2 - m)
                den = jnp.sum(p1, axis=-1, keepdims=True) + jnp.sum(p2, axis=-1, keepdims=True) + jnp.exp(sk - m)
                o = (_dot_nt(p1.astype(BF16), vth) + _dot(p2.astype(BF16), vnh)) / den
                for j in range(2):
                    outs[hk * C_GQA + 2 * j + par] = o[j * valid:(j + 1) * valid]
        o_ref[:, s, :] = jnp.concatenate(outs, axis=1)
        return carry

    lax.fori_loop(0, nseq, per_seq, 0)


def _attn_sample(q3, kn3, vn3, kc_all, vc_all, b1, b2, sink_col, layer, k_prev, v_prev):
    valid, nsb, _ = q3.shape
    lb = kc_all.shape[-1]
    nseq = 2 * SUBLANE
    rows = lambda w: pl.BlockSpec((valid, nseq, w), lambda i: (0, i, 0))
    cblk = pl.BlockSpec((1, nseq, C_KV_WIDTH, lb), lambda i: (layer, i, 0, 0))
    in_specs = [rows(C_WIDTH), rows(C_KV_WIDTH), rows(C_KV_WIDTH), cblk, cblk,
                _full(b1.shape), _full(b2.shape), _full(sink_col.shape)]
    args = [q3, kn3, vn3, kc_all, vc_all, b1, b2, sink_col]
    aliases = {}
    if k_prev is not None:
        in_specs += [pl.BlockSpec(memory_space=pl.ANY)] * 2
        aliases = {len(args): 1, len(args) + 1: 2}
        args += [k_prev, v_prev]
    return pl.pallas_call(
        functools.partial(_attn_sample_body, valid=valid, lb=lb, nseq=nseq),
        grid=(nsb // nseq,),
        in_specs=in_specs,
        out_specs=[rows(C_WIDTH), cblk, cblk],
        out_shape=[jax.ShapeDtypeStruct((valid, nsb, C_WIDTH), F32), jax.ShapeDtypeStruct(kc_all.shape, F32),
                   jax.ShapeDtypeStruct(vc_all.shape, F32)],
        scratch_shapes=[pltpu.VMEM((valid, nseq, C_WIDTH), F32)],
        input_output_aliases=aliases,
        compiler_params=_cparams("arbitrary"),
        name="attn_sample",
    )(*args)


def _merge_body(x_ref, ya_ref, yb_ref, u_ref, yc_ref, nw_ref, wg_ref, d5_ref, wglu_ref,
                wa_ref, wb_ref, wc_ref, wo_ref, o_ref, *, ya_features_major, yc_features_major):
    x = x_ref[...]
    h = _rms(x, nw_ref[...]).astype(BF16)
    g = jax.nn.gelu(yb_ref[...] + d5_ref[...] * u_ref[...].astype(F32))
    gv = _dot(g.astype(BF16), wglu_ref[...])
    yb = (gv[:, :B_WIDTH] * jax.nn.sigmoid(gv[:, B_WIDTH:])).astype(BF16)

    def gate(i):
        return jax.nn.sigmoid(_dot(h, wg_ref[:, i * D_MODEL:(i + 1) * D_MODEL]))

    def branch(y_ref, w_ref, features_major):
        y = y_ref[...].astype(BF16)
        return _dot_tn(y, w_ref[...]) if features_major else _dot(y, w_ref[...])

    mixed = gate(0) * branch(ya_ref, wa_ref, ya_features_major)
    mixed = mixed + gate(1) * _dot(yb, wb_ref[...])
    mixed = mixed + gate(2) * branch(yc_ref, wc_ref, yc_features_major)
    o_ref[...] = x + _dot(mixed.astype(BF16), wo_ref[...])


def _merge(x, ya, yb, u, yc, lw, tm, ya_features_major, yc_features_major):
    t = x.shape[0]
    rows = lambda w: pl.BlockSpec((tm, w), lambda i: (i, 0))
    branch_spec = lambda fm: pl.BlockSpec((A_WIDTH, tm), lambda i: (0, i)) if fm else rows(A_WIDTH)
    return pl.pallas_call(
        functools.partial(_merge_body, ya_features_major=ya_features_major, yc_features_major=yc_features_major),
        grid=(t // tm,),
        in_specs=[rows(D_MODEL), branch_spec(ya_features_major), rows(B_WIDTH), rows(B_WIDTH),
                  branch_spec(yc_features_major),
                  _full((1, D_MODEL)), _full((D_MODEL, 3 * D_MODEL)), _full((1, B_WIDTH)),
                  _full((B_WIDTH, 2 * B_WIDTH)), _full((A_WIDTH, D_MODEL)), _full((B_WIDTH, D_MODEL)),
                  _full((C_WIDTH, D_MODEL)), _full((D_MODEL, D_MODEL))],
        out_specs=rows(D_MODEL),
        out_shape=jax.ShapeDtypeStruct((t, D_MODEL), F32),
        compiler_params=_cparams("parallel"),
        name="merge",
    )(x, ya, yb, u, yc, lw["norm1_w"], lw["w_gate"], lw["s5_d"], lw["w_glu"],
      lw["w_br_a"], lw["w_br_b"], lw["w_br_c"], lw["w_out"])


def _ffn_body(x_ref, nw_ref, wup_ref, wdn_ref, fnw_ref, o_ref, act_scr, *, final_norm):
    x = x_ref[...]
    h = _rms(x, nw_ref[...]).astype(BF16)
    for c in range(D_FF // FF_CHUNK):
        lo, hi = c * FF_CHUNK, (c + 1) * FF_CHUNK
        a = _dot(h, wup_ref[:, lo:hi])
        b = _dot(h, wup_ref[:, D_FF + lo:D_FF + hi])
        act_scr[:, lo:hi] = (jax.nn.silu(a) * b).astype(BF16)
    y = x + _dot(act_scr[...], wdn_ref[...])
    if final_norm:
        y = _rms(y, fnw_ref[...])
    o_ref[...] = y


def _ffn(x, lw, fnw, tm, final_norm):
    t = x.shape[0]
    rows = pl.BlockSpec((tm, D_MODEL), lambda i: (i, 0))
    return pl.pallas_call(
        functools.partial(_ffn_body, final_norm=final_norm),
        grid=(t // tm,),
        in_specs=[rows, _full((1, D_MODEL)), _full((D_MODEL, 2 * D_FF)), _full((D_FF, D_MODEL)), _full((1, D_MODEL))],
        out_specs=rows,
        out_shape=jax.ShapeDtypeStruct((t, D_MODEL), F32),
        scratch_shapes=[pltpu.VMEM((tm, D_FF), BF16)],
        compiler_params=_cparams("parallel"),
        name="ffn",
    )(x, lw["norm2_w"], lw["w_ffn_up"], lw["w_ffn_down"], fnw)


def _row_tile(t, pref):
    tm = pref
    while t % tm:
        tm //= 2
    return tm


def kernel(x_prompt, x_sample, state_ssd, state_conv, state_s5_re, state_s5_im, cache_k, cache_v, norm1_w, w_in, conv_w, conv_b, ssd_a_log, ssd_dt_bias, ssd_d, ssd_norm_w, s5_lam_re, s5_lam_im, s5_log_dt, s5_b_re, s5_b_im, s5_c_re, s5_c_im, s5_d, s5_w_glu, attn_sinks, w_br_a, w_br_b, w_br_c, w_out, norm2_w, w_ffn_up, w_ffn_down, rel_bias, final_norm_w):
    nb, seq, _ = x_prompt.shape
    nsb, valid, _ = x_sample.shape
    depth = w_in.shape[0]
    lb = cache_k.shape[2]
    assert seq % A_CHUNK == 0 and seq % WINDOW == 0 and seq % (S5_Q * SUBLANE) == 0
    assert nsb == LANE and A_CONV - 1 <= valid <= S5_Q and lb == WINDOW
    nchunk = seq // A_CHUNK
    tp = nb * seq
    ts = valid * nsb

    band = _bias_band(rel_bias)

    def pair_rows(a):
        a = a.reshape(C_KV_HEADS, 2, 2, valid, a.shape[-1])
        return jnp.swapaxes(a, 1, 2).reshape(C_KV_HEADS, 2, 2 * valid, a.shape[-1])

    band_s = pair_rows(jnp.swapaxes(band[1, :, :, :valid], 1, 2))
    bias_s1 = band_s[..., :lb]
    bias_s2 = band_s[..., lb:lb + SUBLANE]
    fnw = final_norm_w.reshape(1, D_MODEL)

    xp = x_prompt.reshape(tp, D_MODEL)
    xs = jnp.swapaxes(x_sample, 0, 1).reshape(ts, D_MODEL)
    conv_all = jnp.swapaxes(state_conv, 1, 2)
    ssd_all = jnp.transpose(state_ssd, (0, 2, 3, 4, 1))
    s5re_all = jnp.transpose(state_s5_re, (0, 2, 3, 1)).reshape(depth, S5_STATE_COLS, nsb)
    s5im_all = jnp.transpose(state_s5_im, (0, 2, 3, 1)).reshape(depth, S5_STATE_COLS, nsb)
    kc_all = jnp.transpose(cache_k, (0, 1, 3, 4, 2)).reshape(depth, nsb, C_KV_WIDTH, lb)
    vc_all = jnp.transpose(cache_v, (0, 1, 3, 4, 2)).reshape(depth, nsb, C_KV_WIDTH, lb)
    tm_p = _row_tile(tp, 512)
    tm_s = _row_tile(ts, 512)
    rt_p = _row_tile(tp // S5_Q, 256)
    rb_p = _row_tile(seq // S5_Q, 128)
    attn_blk = _row_tile(seq // WINDOW, 4)

    new_p, s5_s = [], []
    conv_new = ssd_new = k_new = v_new = None
    for i in range(depth):
        wi = w_in[i]
        off = [0, 512, 1280, 1288, 1800, 2312, 2440, 2568, 5640]
        w_pack = jnp.concatenate(
            [wi[:, off[0]:off[1]], wi[:, off[1]:off[2]], wi[:, off[3]:off[4]], wi[:, off[4]:off[5]],
             wi[:, off[5]:off[6]], wi[:, off[6]:off[7]],
             jnp.pad(wi[:, off[2]:off[3]], ((0, 0), (0, LANE - A_HEADS)))], axis=1).astype(BF16)
        sink_rows = pair_rows(jnp.broadcast_to(attn_sinks[i].reshape(C_HEADS, 1, 1), (C_HEADS, valid, 1)))
        lw = dict(
            norm1_w=norm1_w[i].reshape(1, D_MODEL), w_gate=wi[:, off[7]:off[8]].astype(BF16),
            conv_w=conv_w[i], conv_b=conv_b[i].reshape(1, A_CONV_DIM),
            a_log=jnp.pad(ssd_a_log[i], (0, LANE - A_HEADS)).reshape(1, LANE),
            dt_bias=jnp.pad(ssd_dt_bias[i], (0, LANE - A_HEADS)).reshape(1, LANE),
            d_exp=jnp.repeat(ssd_d[i], A_HEAD_DIM).reshape(1, A_WIDTH),
            ssd_norm_w=ssd_norm_w[i].reshape(1, A_WIDTH),
            ssd_norm_wb=jnp.broadcast_to(ssd_norm_w[i].reshape(A_WIDTH, 1), (A_WIDTH, LANE)),
            s5_d=s5_d[i].reshape(1, B_WIDTH), w_glu=s5_w_glu[i].astype(BF16),
            w_br_a=w_br_a[i].astype(BF16), w_br_b=w_br_b[i].astype(BF16), w_br_c=w_br_c[i].astype(BF16),
            w_out=w_out[i].astype(BF16), norm2_w=norm2_w[i].reshape(1, D_MODEL),
            w_ffn_up=w_ffn_up[i].astype(BF16), w_ffn_down=w_ffn_down[i].astype(BF16),
        )
        s5w = _s5_weights(s5_lam_re[i], s5_lam_im[i], s5_log_dt[i], s5_b_re[i], s5_b_im[i],
                          s5_c_re[i], s5_c_im[i], valid)
        last = i == depth - 1

        z, xbc, u, q, k, v, dt = _inproj(xp, lw["norm1_w"], w_pack, tm_p)
        ya, ssd_h = _ssd(xbc, z, dt, lw, nb, nchunk, A_CHUNK)
        u2 = u.reshape(tp // S5_Q, S5_Q, B_WIDTH)
        sre, sim = _s5_state(u2, s5w["bst_re"], s5w["bst_im"], rt_p)
        hre, him, fre, fim = _s5_scan(sre, sim, s5w["aq_re"], s5w["aq_im"], nb, rb_p)
        yb = _s5_out(u2, hre, him, s5w["toep"], s5w["cst_re"], s5w["cst_im"], rt_p).reshape(tp, B_WIDTH)
        yc = _attn_prompt(q, k, v, band, attn_sinks[i], nb, seq // WINDOW, attn_blk)
        x1 = _merge(xp, ya, yb, u, yc, lw, tm_p, False, True)
        xp = _ffn(x1, lw, fnw, tm_p, last)
        new_p.append((
            ssd_h.reshape(nb, A_HEADS, A_HEAD_DIM, A_STATE),
            xbc.reshape(nb, seq, A_CONV_DIM)[:, seq - (A_CONV - 1):],
            fre.reshape(nb, B_GROUPS, B_STATE), fim.reshape(nb, B_GROUPS, B_STATE),
            k.reshape(nb, seq, C_KV_HEADS, C_HEAD_DIM)[:, seq - WINDOW:],
            v.reshape(nb, seq, C_KV_HEADS, C_HEAD_DIM)[:, seq - WINDOW:]))

        z, xbc, u, q, k, v, dt = _inproj(xs, lw["norm1_w"], w_pack, tm_s)
        ya, conv_new, ssd_new = _ssd_sample(xbc, z, dt, conv_all, ssd_all, lw, i, valid, conv_new, ssd_new)
        av_re = jnp.broadcast_to(s5w["av_re"].reshape(S5_STATE_COLS, 1), (S5_STATE_COLS, nsb))
        av_im = jnp.broadcast_to(s5w["av_im"].reshape(S5_STATE_COLS, 1), (S5_STATE_COLS, nsb))
        yb, fre, fim = _s5_sample(u, s5w, s5re_all, s5im_all, av_re, av_im, i, valid)
        yc, k_new, v_new = _attn_sample(
            q.reshape(valid, nsb, C_WIDTH), k.reshape(valid, nsb, C_KV_WIDTH), v.reshape(valid, nsb, C_KV_WIDTH),
            kc_all, vc_all, bias_s1, bias_s2, sink_rows, i, k_new, v_new)
        x1 = _merge(xs, ya, yb, u, yc.reshape(ts, C_WIDTH), lw, tm_s, True, False)
        xs = _ffn(x1, lw, fnw, tm_s, last)
        s5_s.append((fre, fim))

    def stack(states, j):
        return jnp.stack([s[j] for s in states], axis=0)

    def s5_state(j):
        return jnp.transpose(stack(s5_s, j).reshape(depth, B_GROUPS, B_STATE, nsb), (0, 3, 1, 2))

    def cache(a):
        return jnp.transpose(a.reshape(depth, nsb, C_KV_HEADS, C_HEAD_DIM, lb), (0, 1, 4, 2, 3))

    y_prompt = xp.reshape(nb, seq, D_MODEL)
    y_sample = jnp.swapaxes(xs.reshape(valid, nsb, D_MODEL), 0, 1)
    return (y_prompt, y_sample,
            stack(new_p, 0), stack(new_p, 1), stack(new_p, 2), stack(new_p, 3), stack(new_p, 4), stack(new_p, 5),
            jnp.transpose(ssd_new, (0, 4, 1, 2, 3)), jnp.swapaxes(conv_new, 1, 2), s5_state(0), s5_state(1),
            cache(k_new), cache(v_new))
```

```python
import functools
import math

import numpy as np
import jax
import jax.numpy as jnp
from jax import lax
from jax.experimental import pallas as pl
from jax.experimental.pallas import tpu as pltpu

F32 = jnp.float32
BF16 = jnp.bfloat16
HIGHEST = lax.Precision.HIGHEST

D_MODEL = 1024
A_HEAD_DIM = 64
A_WIDTH = 512
A_HEADS = 8
A_GROUPS = 2
A_STATE = 64
A_CONV = 4
A_CONV_DIM = 768
A_CHUNK = 128
B_CH = 16
B_WIDTH = 512
B_GROUPS = 32
B_STATE = 64
C_HEAD_DIM = 64
C_WIDTH = 512
C_HEADS = 8
C_KV_HEADS = 2
C_GQA = 4
C_KV_WIDTH = 128
WINDOW = 128
REL_BUCKETS = 32
REL_MAX_DIST = 128
D_FF = 2816
EPS = 1e-6

P_Z = 0
P_XBC = P_Z + A_WIDTH
P_U = P_XBC + A_CONV_DIM
P_Q = P_U + B_WIDTH
P_K = P_Q + C_WIDTH
P_V = P_K + C_KV_WIDTH
P_DT = P_V + C_KV_WIDTH
P_COLS = P_DT + 128

LANE = 128
SUBLANE = 8
S5_Q = 8
S5_OCT = 8
N_OCT = B_GROUPS // S5_OCT
S5_STATE_COLS = B_GROUPS * B_STATE
NEG = -1e30
VMEM_LIMIT = 56 * 1024 * 1024
FF_CHUNK = 256


def _cparams(*sem):
    return pltpu.CompilerParams(dimension_semantics=sem, vmem_limit_bytes=VMEM_LIMIT)


def _rms(x, w):
    return x * lax.rsqrt(jnp.mean(x * x, axis=-1, keepdims=True) + EPS) * w


def _dot(a, b):
    return jnp.dot(a, b, preferred_element_type=F32)


def _dot_nt(a, b, precision=None):
    return lax.dot_general(a, b, (((1,), (1,)), ((), ())), preferred_element_type=F32, precision=precision)


def _dot_tn(a, b):
    return lax.dot_general(a, b, (((0,), (0,)), ((), ())), preferred_element_type=F32)


def _full(shape):
    n = len(shape)
    return pl.BlockSpec(shape, lambda *_: (0,) * n)


def _tile_major_spec(rows):
    return pl.BlockSpec((N_OCT, rows, LANE), lambda i: (0, i, 0))


def _lanes(ref):
    return jnp.concatenate([ref[k8] for k8 in range(N_OCT)], axis=1)


def _inproj_body(x_ref, nw_ref, w_ref, z_ref, xbc_ref, u_ref, q_ref, k_ref, v_ref, dt_ref):
    h = _rms(x_ref[...], nw_ref[...]).astype(BF16)

    def proj(lo, hi):
        return _dot(h, w_ref[:, lo:hi])

    z_ref[...] = proj(P_Z, P_XBC).astype(BF16)
    xbc_ref[...] = proj(P_XBC, P_U)
    u = proj(P_U, P_Q)
    for k8 in range(N_OCT):
        u_ref[k8] = u[:, k8 * LANE:(k8 + 1) * LANE]
    q_ref[...] = proj(P_Q, P_K).astype(BF16)
    k_ref[...] = proj(P_K, P_V)
    v_ref[...] = proj(P_V, P_DT)
    dt_ref[...] = proj(P_DT, P_COLS)


def _inproj(x, nw, w, tm):
    t = x.shape[0]
    widths = [(A_WIDTH, BF16), (A_CONV_DIM, F32), None, (C_WIDTH, BF16),
              (C_KV_WIDTH, F32), (C_KV_WIDTH, F32), (LANE, F32)]
    spec = lambda w: _tile_major_spec(tm) if w is None else pl.BlockSpec((tm, w[0]), lambda i: (i, 0))
    shape = lambda w: (jax.ShapeDtypeStruct((N_OCT, t, LANE), F32) if w is None
                       else jax.ShapeDtypeStruct((t, w[0]), w[1]))
    return pl.pallas_call(
        _inproj_body,
        grid=(t // tm,),
        in_specs=[pl.BlockSpec((tm, D_MODEL), lambda i: (i, 0)), _full((1, D_MODEL)), _full((D_MODEL, P_COLS))],
        out_specs=[spec(w) for w in widths],
        out_shape=[shape(w) for w in widths],
        compiler_params=_cparams("parallel"),
        name="inproj",
    )(x, nw, w)


def _ssd_body(xbc_ref, z_ref, dt_ref, cw_ref, cb_ref, alog_ref, dtb_ref, dexp_ref, nw_ref,
              y_ref, hout_ref, h_scr, tail_scr, *, q, nchunks):
    c = pl.program_id(1)

    @pl.when(c == 0)
    def _():
        h_scr[...] = jnp.zeros_like(h_scr)
        tail_scr[...] = jnp.zeros_like(tail_scr)

    xbc = xbc_ref[...]
    ext = jnp.concatenate([tail_scr[...], xbc], axis=0)
    acc = cb_ref[...] + cw_ref[A_CONV - 1:A_CONV, :] * xbc
    for j in range(1, A_CONV):
        acc = acc + cw_ref[A_CONV - 1 - j:A_CONV - j, :] * pltpu.roll(ext, j, axis=0)[SUBLANE:SUBLANE + q]
    tail_scr[...] = xbc[q - SUBLANE:q]
    xc = jax.nn.silu(acc)
    x = xc[:, :A_WIDTH]
    bm = xc[:, A_WIDTH:A_WIDTH + A_GROUPS * A_STATE].astype(BF16)
    cm = xc[:, A_WIDTH + A_GROUPS * A_STATE:].astype(BF16)

    dt = jax.nn.softplus(dt_ref[...] + dtb_ref[...])
    dta = dt * (-jnp.exp(alog_ref[...]))
    row = lax.broadcasted_iota(jnp.int32, (q, q), 0)
    col = lax.broadcasted_iota(jnp.int32, (q, q), 1)
    tri = row >= col
    a_cum = jnp.dot(tri.astype(F32), dta, preferred_element_type=F32, precision=HIGHEST)

    def expand(v):
        return jnp.concatenate([jnp.broadcast_to(v[:, k:k + 1], (q, A_HEAD_DIM)) for k in range(A_HEADS)], axis=1)

    a_cum_e = expand(a_cum)
    xs = x * expand(dt)
    xs_b = xs.astype(BF16)
    a_cum_t = a_cum.T

    h = h_scr[...]
    h_b = h.astype(BF16)
    hpg = A_HEADS // A_GROUPS
    gw = hpg * A_HEAD_DIM
    y_diag, y_off = [], []
    for g in range(A_GROUPS):
        cg = cm[:, g * A_STATE:(g + 1) * A_STATE]
        bg = bm[:, g * A_STATE:(g + 1) * A_STATE]
        cb = _dot_nt(cg, bg)
        for kk in range(hpg):
            k = g * hpg + kk
            seg = a_cum[:, k:k + 1] - a_cum_t[k:k + 1, :]
            decay = jnp.exp(jnp.where(tri, seg, -jnp.inf))
            y_diag.append(_dot((cb * decay).astype(BF16), xs_b[:, k * A_HEAD_DIM:(k + 1) * A_HEAD_DIM]))
        y_off.append(_dot_nt(cg, h_b[g * gw:(g + 1) * gw]))
    y = (jnp.concatenate(y_diag, axis=1) + jnp.concatenate(y_off, axis=1) * jnp.exp(a_cum_e)
         + dexp_ref[...] * x)
    y = y * jax.nn.silu(z_ref[...].astype(F32))
    y_ref[...] = _rms(y, nw_ref[...]).astype(BF16)

    last = a_cum_e[q - 1:q, :]
    xs_dec = (xs * jnp.exp(last - a_cum_e)).astype(BF16)
    dec_col = jnp.exp(a_cum_e.T[:, q - 1:q])
    upd = [_dot_tn(xs_dec[:, g * gw:(g + 1) * gw], bm[:, g * A_STATE:(g + 1) * A_STATE]) for g in range(A_GROUPS)]
    h_new = dec_col * h + jnp.concatenate(upd, axis=0)
    h_scr[...] = h_new

    @pl.when(c == nchunks - 1)
    def _():
        hout_ref[0] = h_new


def _ssd(xbc, z, dt, lw, nb, nchunks, q):
    rows = nb * nchunks * q
    rmap = lambda b, c: (b * nchunks + c, 0)
    in_specs = [pl.BlockSpec((q, A_CONV_DIM), rmap), pl.BlockSpec((q, A_WIDTH), rmap), pl.BlockSpec((q, LANE), rmap),
                _full((A_CONV, A_CONV_DIM)), _full((1, A_CONV_DIM)), _full((1, LANE)), _full((1, LANE)),
                _full((1, A_WIDTH)), _full((1, A_WIDTH))]
    return pl.pallas_call(
        functools.partial(_ssd_body, q=q, nchunks=nchunks),
        grid=(nb, nchunks),
        in_specs=in_specs,
        out_specs=[pl.BlockSpec((q, A_WIDTH), rmap), pl.BlockSpec((1, A_WIDTH, A_STATE), lambda b, c: (b, 0, 0))],
        out_shape=[jax.ShapeDtypeStruct((rows, A_WIDTH), BF16), jax.ShapeDtypeStruct((nb, A_WIDTH, A_STATE), F32)],
        scratch_shapes=[pltpu.VMEM((A_WIDTH, A_STATE), F32), pltpu.VMEM((SUBLANE, A_CONV_DIM), F32)],
        compiler_params=_cparams("parallel", "arbitrary"),
        name="ssd",
    )(xbc, z, dt, lw["conv_w"], lw["conv_b"], lw["a_log"], lw["dt_bias"], lw["d_exp"], lw["ssd_norm_w"])


def _ssd_sample_body(*refs, valid, aliased):
    n_in = 11 + (2 if aliased else 0)
    (xbc_ref, z_ref, dt_ref, conv0_ref, h0_ref, cw_ref, cb_ref, alog_ref, dtb_ref, dexp_ref, nwb_ref) = refs[:11]
    y_ref, conv_out_ref, h_out_ref = refs[n_in:n_in + 3]
    xs_scr, dx_scr, z_scr, y_scr, bt_scr, ct_scr, da_scr = refs[n_in + 3:]
    k = pl.program_id(0)
    ns = LANE

    @pl.when(k == 0)
    def _():
        rows = [conv0_ref[0, r] for r in range(A_CONV - 1)] + [xbc_ref[t * ns:(t + 1) * ns, :] for t in range(valid)]
        for t in range(valid):
            acc = cb_ref[...]
            for j in range(A_CONV):
                acc = acc + cw_ref[A_CONV - 1 - j:A_CONV - j, :] * rows[t + A_CONV - 1 - j]
            xc = jax.nn.silu(acc)
            dt = jax.nn.softplus(dt_ref[t * ns:(t + 1) * ns, :] + dtb_ref[...])
            dt_t = dt.T
            da_scr[t] = jnp.exp((dt * (-jnp.exp(alog_ref[...]))).T)
            x = xc[:, :A_WIDTH]
            x_t = x.T
            for kk in range(A_HEADS):
                hs = slice(kk * A_HEAD_DIM, (kk + 1) * A_HEAD_DIM)
                xs_scr[t, hs, :] = x_t[hs] * dt_t[kk:kk + 1, :]
            dx_scr[t] = (x * dexp_ref[...]).T
            bt_scr[t] = xc[:, A_WIDTH:A_WIDTH + A_GROUPS * A_STATE].T
            ct_scr[t] = xc[:, A_WIDTH + A_GROUPS * A_STATE:].T
            z_scr[t] = jax.nn.silu(z_ref[t * ns:(t + 1) * ns, :].astype(F32)).T
        for r in range(A_CONV - 1):
            conv_out_ref[0, r] = rows[valid + r]

    g0 = pl.multiple_of((k // (A_HEADS // A_GROUPS)) * A_STATE, A_STATE)

    def per_p(p, carry):
        h = h0_ref[0, 0, p]
        row = k * A_HEAD_DIM + p
        for t in range(valid):
            da = da_scr[t, pl.ds(k, 1), :]
            xr = xs_scr[t, pl.ds(row, 1), :]
            h = da * h + xr * bt_scr[t, pl.ds(g0, A_STATE), :]
            y_scr[t, pl.ds(row, 1), :] = jnp.sum(h * ct_scr[t, pl.ds(g0, A_STATE), :], axis=0, keepdims=True)
        h_out_ref[0, 0, p] = h
        return carry

    lax.fori_loop(0, A_HEAD_DIM, per_p, 0, unroll=2)

    @pl.when(k == A_HEADS - 1)
    def _():
        for t in range(valid):
            y = (y_scr[t] + dx_scr[t]) * z_scr[t]
            y = y * lax.rsqrt(jnp.mean(y * y, axis=0, keepdims=True) + EPS) * nwb_ref[...]
            y_ref[:, t * ns:(t + 1) * ns] = y.astype(BF16)


def _ssd_sample(xbc, z, dt, conv_all, h_all, lw, layer, valid, conv_prev, h_prev):
    aliased = conv_prev is not None
    rows = valid * LANE
    hblk = pl.BlockSpec((1, 1, A_HEAD_DIM, A_STATE, LANE), lambda k: (layer, k, 0, 0, 0))
    cblk = pl.BlockSpec((1, A_CONV - 1, LANE, A_CONV_DIM), lambda k: (layer, 0, 0, 0))
    in_specs = [_full((rows, A_CONV_DIM)), _full((rows, A_WIDTH)), _full((rows, LANE)), cblk, hblk,
                _full((A_CONV, A_CONV_DIM)), _full((1, A_CONV_DIM)), _full((1, LANE)), _full((1, LANE)),
                _full((1, A_WIDTH)), _full((A_WIDTH, LANE))]
    args = [xbc, z, dt, conv_all, h_all, lw["conv_w"], lw["conv_b"], lw["a_log"], lw["dt_bias"], lw["d_exp"],
            lw["ssd_norm_wb"]]
    aliases = {}
    if aliased:
        in_specs += [pl.BlockSpec(memory_space=pl.ANY)] * 2
        aliases = {len(args): 1, len(args) + 1: 2}
        args += [conv_prev, h_prev]
    big = lambda n: pltpu.VMEM((valid, n, LANE), F32)
    return pl.pallas_call(
        functools.partial(_ssd_sample_body, valid=valid, aliased=aliased),
        grid=(A_HEADS,),
        in_specs=in_specs,
        out_specs=[_full((A_WIDTH, rows)), cblk, hblk],
        out_shape=[jax.ShapeDtypeStruct((A_WIDTH, rows), BF16), jax.ShapeDtypeStruct(conv_all.shape, F32),
                   jax.ShapeDtypeStruct(h_all.shape, F32)],
        scratch_shapes=[big(A_WIDTH), big(A_WIDTH), big(A_WIDTH), big(A_WIDTH), big(LANE), big(LANE), big(LANE)],
        input_output_aliases=aliases,
        compiler_params=_cparams("arbitrary"),
        name="ssd_sample",
    )(*args)


def _s5w_body(lre_ref, lim_ref, ldt_ref, btr_ref, bti_ref, cr_ref, ci_ref,
              bre_ref, bim_ref, bsre_ref, bsim_ref, cre_ref, cim_ref, toep_ref, ap_ref, cd_scr, bb_scr, *, valid):
    h, q = B_STATE, S5_Q
    for ref in (bre_ref, bim_ref, bsre_ref, bsim_ref, cre_ref, cim_ref, toep_ref):
        ref[...] = jnp.zeros(ref.shape, BF16)
    r_i = lax.broadcasted_iota(jnp.int32, (q * B_CH, q * LANE), 0)
    c_i = lax.broadcasted_iota(jnp.int32, (q * B_CH, q * LANE), 1)
    d = lax.broadcasted_iota(jnp.int32, (2 * SUBLANE, h), 0).astype(F32)
    for g8 in range(S5_OCT):
        lr, li = lre_ref[g8], lim_ref[g8]
        step = jnp.exp(ldt_ref[g8])
        er = jnp.exp(lr * step * d)
        ang = li * step * d
        ar, ai = er * jnp.cos(ang), er * jnp.sin(ang)
        nr, ni = ar[1:2] - 1.0, ai[1:2]
        den = lr * lr + li * li
        wr, wi = (nr * lr + ni * li) / den, (ni * lr - nr * li) / den
        btr, bti = btr_ref[g8], bti_ref[g8]
        bbr, bbi = wr * btr - wi * bti, wr * bti + wi * btr
        cr, ci = cr_ref[g8], ci_ref[g8]
        bb_scr[:, 0:h] = bbr
        bb_scr[:, h:2 * h] = bbi
        cols = slice(g8 * h, (g8 + 1) * h)
        for s in range(q):
            rows = slice(s * LANE + g8 * B_CH, s * LANE + (g8 + 1) * B_CH)
            e = q - 1 - s
            bre_ref[0, rows, cols] = (ar[e:e + 1] * bbr - ai[e:e + 1] * bbi).astype(BF16)
            bim_ref[0, rows, cols] = (ai[e:e + 1] * bbr + ar[e:e + 1] * bbi).astype(BF16)
            if s < valid:
                e = valid - 1 - s
                bsre_ref[0, rows, cols] = (ar[e:e + 1] * bbr - ai[e:e + 1] * bbi).astype(BF16)
                bsim_ref[0, rows, cols] = (ai[e:e + 1] * bbr + ar[e:e + 1] * bbi).astype(BF16)
            cre_ref[0, rows, cols] = (cr * ar[s + 1:s + 2] - ci * ai[s + 1:s + 2]).astype(BF16)
            cim_ref[0, rows, cols] = (-(cr * ai[s + 1:s + 2] + ci * ar[s + 1:s + 2])).astype(BF16)
            cd_scr[s * B_CH:(s + 1) * B_CH, 0:h] = cr * ar[s:s + 1] - ci * ai[s:s + 1]
            cd_scr[s * B_CH:(s + 1) * B_CH, h:2 * h] = -(cr * ai[s:s + 1] + ci * ar[s:s + 1])
        kdt = _dot_nt(bb_scr[...], cd_scr[...], precision=HIGHEST)
        place = (c_i == (r_i >> 4) * LANE + g8 * B_CH + (r_i & (B_CH - 1))).astype(F32)
        slab = jnp.dot(kdt, place, preferred_element_type=F32, precision=HIGHEST).astype(BF16)
        for s in range(q):
            toep_ref[0, s * LANE + g8 * B_CH:s * LANE + (g8 + 1) * B_CH, s * LANE:] = slab[:, 0:(q - s) * LANE]
        ap_ref[g8, 0:1, 0:h] = ar[q:q + 1]
        ap_ref[g8, 0:1, h:2 * h] = ai[q:q + 1]
        ap_ref[g8, 1:2, 0:h] = ar[valid:valid + 1]
        ap_ref[g8, 1:2, h:2 * h] = ai[valid:valid + 1]


def _s5_weights(lam_re, lam_im, log_dt, b_re, b_im, c_re, c_im, valid):
    g, h, q = B_GROUPS, B_STATE, S5_Q
    vec = pl.BlockSpec((S5_OCT, 1, h), lambda i: (i, 0, 0))
    mat = pl.BlockSpec((S5_OCT, B_CH, h), lambda i: (i, 0, 0))
    st = pl.BlockSpec((1, q * LANE, S5_OCT * h), lambda i: (i, 0, 0))
    st_shape = jax.ShapeDtypeStruct((N_OCT, q * LANE, S5_OCT * h), BF16)
    bre, bim, bsre, bsim, cre, cim, toep, ap = pl.pallas_call(
        functools.partial(_s5w_body, valid=valid),
        grid=(N_OCT,),
        in_specs=[vec, vec, vec, mat, mat, mat, mat],
        out_specs=[st] * 6 + [pl.BlockSpec((1, q * LANE, q * LANE), lambda i: (i, 0, 0)),
                              pl.BlockSpec((S5_OCT, 2, 2 * h), lambda i: (i, 0, 0))],
        out_shape=[st_shape] * 6 + [jax.ShapeDtypeStruct((N_OCT, q * LANE, q * LANE), BF16),
                                    jax.ShapeDtypeStruct((g, 2, 2 * h), F32)],
        scratch_shapes=[pltpu.VMEM((q * B_CH, 2 * h), F32), pltpu.VMEM((B_CH, 2 * h), F32)],
        compiler_params=_cparams("parallel"),
        name="s5_weights",
    )(lam_re.reshape(g, 1, h), lam_im.reshape(g, 1, h), jnp.broadcast_to(log_dt.reshape(g, 1, 1), (g, 1, h)),
      jnp.swapaxes(b_re, 1, 2), jnp.swapaxes(b_im, 1, 2), c_re, c_im)
    return dict(
        bst_re=bre, bst_im=bim, bsts_re=bsre, bsts_im=bsim, cst_re=cre, cst_im=cim, toep=toep,
        aq_re=ap[:, 0, :h].reshape(1, S5_STATE_COLS), aq_im=ap[:, 0, h:].reshape(1, S5_STATE_COLS),
        av_re=ap[:, 1, :h].reshape(1, S5_STATE_COLS), av_im=ap[:, 1, h:].reshape(1, S5_STATE_COLS),
    )


def _oct_rows(u_ref, k8, rt):
    return jnp.concatenate([u_ref[k8, pl.ds(j, rt, stride=S5_Q), :] for j in range(S5_Q)], axis=1).astype(BF16)


def _s5_state_body(u_ref, bre_ref, bim_ref, sre_ref, sim_ref):
    w = S5_OCT * B_STATE
    for k8 in range(N_OCT):
        uo = _oct_rows(u_ref, k8, sre_ref.shape[0])
        sre_ref[:, k8 * w:(k8 + 1) * w] = _dot(uo, bre_ref[k8])
        sim_ref[:, k8 * w:(k8 + 1) * w] = _dot(uo, bim_ref[k8])


def _s5_state(u, bre, bim, rt):
    r = u.shape[1] // S5_Q
    wspec = _full((N_OCT, S5_Q * LANE, S5_OCT * B_STATE))
    ospec = pl.BlockSpec((rt, S5_STATE_COLS), lambda i: (i, 0))
    return pl.pallas_call(
        _s5_state_body,
        grid=(r // rt,),
        in_specs=[_tile_major_spec(rt * S5_Q), wspec, wspec],
        out_specs=[ospec, ospec],
        out_shape=[jax.ShapeDtypeStruct((r, S5_STATE_COLS), F32)] * 2,
        compiler_params=_cparams("parallel"),
        name="s5_state",
    )(u, bre, bim)


def _s5_scan_body(sre_ref, sim_ref, ar_ref, ai_ref, hre_ref, him_ref, fre_ref, fim_ref, cr_scr, ci_scr, *, rb, nblk):
    i = pl.program_id(1)

    @pl.when(i == 0)
    def _():
        cr_scr[...] = jnp.zeros_like(cr_scr)
        ci_scr[...] = jnp.zeros_like(ci_scr)

    ar, ai = ar_ref[...], ai_ref[...]

    def step(r, carry):
        hr, hi = carry
        hre_ref[pl.ds(r, 1), :] = hr
        him_ref[pl.ds(r, 1), :] = hi
        sr, si = sre_ref[pl.ds(r, 1), :], sim_ref[pl.ds(r, 1), :]
        return ar * hr - ai * hi + sr, ai * hr + ar * hi + si

    hr, hi = lax.fori_loop(0, rb, step, (cr_scr[...], ci_scr[...]))
    cr_scr[...] = hr
    ci_scr[...] = hi

    @pl.when(i == nblk - 1)
    def _():
        fre_ref[0] = hr
        fim_ref[0] = hi


def _s5_scan(sre, sim, ar, ai, nb, rb):
    r = sre.shape[0]
    nblk = r // nb // rb
    blk = pl.BlockSpec((rb, S5_STATE_COLS), lambda b, i: (b * nblk + i, 0))
    vec = _full((1, S5_STATE_COLS))
    fin = pl.BlockSpec((1, 1, S5_STATE_COLS), lambda b, i: (b, 0, 0))
    return pl.pallas_call(
        functools.partial(_s5_scan_body, rb=rb, nblk=nblk),
        grid=(nb, nblk),
        in_specs=[blk, blk, vec, vec],
        out_specs=[blk, blk, fin, fin],
        out_shape=[jax.ShapeDtypeStruct((r, S5_STATE_COLS), F32)] * 2
        + [jax.ShapeDtypeStruct((nb, 1, S5_STATE_COLS), F32)] * 2,
        scratch_shapes=[pltpu.VMEM((1, S5_STATE_COLS), F32)] * 2,
        compiler_params=_cparams("parallel", "arbitrary"),
        name="s5_scan",
    )(sre, sim, ar, ai)


def _s5_sample_body(u_ref, toep_ref, cre_ref, cim_ref, bre_ref, bim_ref, hre_ref, him_ref, ar_ref, ai_ref,
                    y_ref, fre_ref, fim_ref, *, valid):
    w = S5_OCT * B_STATE
    ns = LANE
    for k8 in range(N_OCT):
        blocks = [u_ref[k8, t * ns:(t + 1) * ns, :] for t in range(valid)]
        uo = jnp.concatenate(blocks, axis=1).astype(BF16)
        uo_t = jnp.concatenate([b.T for b in blocks], axis=0).astype(BF16)
        hr, hi = hre_ref[0, k8 * w:(k8 + 1) * w, :], him_ref[0, k8 * w:(k8 + 1) * w, :]
        yo = (_dot(uo, toep_ref[k8]) + _dot_nt(hr.T.astype(BF16), cre_ref[k8])
              + _dot_nt(hi.T.astype(BF16), cim_ref[k8]))
        for t in range(valid):
            y_ref[k8, t * ns:(t + 1) * ns, :] = yo[:, t * LANE:(t + 1) * LANE]
        ar, ai = ar_ref[k8 * w:(k8 + 1) * w, :], ai_ref[k8 * w:(k8 + 1) * w, :]
        fre_ref[k8 * w:(k8 + 1) * w, :] = ar * hr - ai * hi + _dot_tn(bre_ref[k8], uo_t)
        fim_ref[k8 * w:(k8 + 1) * w, :] = ai * hr + ar * hi + _dot_tn(bim_ref[k8], uo_t)


def _s5_sample(u, s5w, hre_all, him_all, ar, ai, layer, valid):
    rows = valid * LANE
    vq = valid * LANE
    sub = lambda c: pl.BlockSpec((N_OCT, vq, c), lambda i: (0, 0, 0))
    hblk = pl.BlockSpec((1, S5_STATE_COLS, LANE), lambda i: (layer, 0, 0))
    st = _full((S5_STATE_COLS, LANE))
    return pl.pallas_call(
        functools.partial(_s5_sample_body, valid=valid),
        grid=(1,),
        in_specs=[_tile_major_spec(rows), sub(vq), sub(S5_OCT * B_STATE), sub(S5_OCT * B_STATE),
                  sub(S5_OCT * B_STATE), sub(S5_OCT * B_STATE), hblk, hblk, st, st],
        out_specs=[_tile_major_spec(rows), st, st],
        out_shape=[jax.ShapeDtypeStruct((N_OCT, rows, LANE), F32)] + [jax.ShapeDtypeStruct((S5_STATE_COLS, LANE), F32)] * 2,
        compiler_params=_cparams("arbitrary"),
        name="s5_sample",
    )(u, s5w["toep"], s5w["cst_re"], s5w["cst_im"], s5w["bsts_re"], s5w["bsts_im"], hre_all, him_all, ar, ai)


def _s5_out_body(u_ref, hre_ref, him_ref, toep_ref, cre_ref, cim_ref, y_ref):
    w = S5_OCT * B_STATE
    rt = hre_ref.shape[0]
    for k8 in range(N_OCT):
        uo = _oct_rows(u_ref, k8, rt)
        yo = (_dot(uo, toep_ref[k8])
              + _dot_nt(hre_ref[:, k8 * w:(k8 + 1) * w].astype(BF16), cre_ref[k8])
              + _dot_nt(him_ref[:, k8 * w:(k8 + 1) * w].astype(BF16), cim_ref[k8]))
        for j in range(S5_Q):
            y_ref[k8, pl.ds(j, rt, stride=S5_Q), :] = yo[:, j * LANE:(j + 1) * LANE]


def _s5_out(u, hre, him, toep, cre, cim, rt):
    r = u.shape[1] // S5_Q
    hblk = pl.BlockSpec((rt, S5_STATE_COLS), lambda i: (i, 0))
    return pl.pallas_call(
        _s5_out_body,
        grid=(r // rt,),
        in_specs=[_tile_major_spec(rt * S5_Q), hblk, hblk, _full((N_OCT, S5_Q * LANE, S5_Q * LANE)),
                  _full((N_OCT, S5_Q * LANE, S5_OCT * B_STATE)), _full((N_OCT, S5_Q * LANE, S5_OCT * B_STATE))],
        out_specs=_tile_major_spec(rt * S5_Q),
        out_shape=jax.ShapeDtypeStruct(u.shape, F32),
        compiler_params=_cparams("parallel"),
        name="s5_out",
    )(u, hre, him, toep, cre, cim)


def _band_body(bucket_ref, table_ref, o_ref):
    bucket = bucket_ref[...]
    col = lax.broadcasted_iota(jnp.int32, (WINDOW, 2 * WINDOW), 1)
    for hq in range(C_HEADS):
        base = jnp.full((1, 2 * WINDOW), NEG, F32)
        for b in range(REL_BUCKETS):
            base = jnp.where(bucket == b, table_ref[b, hq], base)
        tile = pltpu.roll(jnp.broadcast_to(base, (WINDOW, 2 * WINDOW)), 0, 1, stride=1, stride_axis=0)
        o_ref[1, hq] = tile.T
        o_ref[0, hq] = jnp.where(col >= WINDOW, tile, NEG).T


def _bias_band(table):
    lane = np.arange(2 * WINDOW)
    n = WINDOW - lane
    exact = REL_BUCKETS // 2
    nf = np.maximum(n, 1).astype(np.float32)
    large = exact + (np.log(nf / np.float32(exact)) / np.float32(math.log(REL_MAX_DIST / exact))
                     * np.float32(REL_BUCKETS - exact)).astype(np.int32)
    bucket = np.where(n < exact, n, np.minimum(large, REL_BUCKETS - 1))
    bucket = np.where((n >= 0) & (n < WINDOW), bucket, -1).astype(np.int32)
    return pl.pallas_call(
        _band_body,
        in_specs=[pl.BlockSpec(memory_space=pltpu.VMEM), pl.BlockSpec(memory_space=pltpu.SMEM)],
        out_specs=pl.BlockSpec(memory_space=pltpu.VMEM),
        out_shape=jax.ShapeDtypeStruct((2, C_HEADS, 2 * WINDOW, WINDOW), F32),
        name="rel_bias_band",
    )(jnp.asarray(bucket).reshape(1, 2 * WINDOW), table)


def _attn_prompt_body(q_ref, kc_ref, kp_ref, vc_ref, vp_ref, b0_ref, br_ref, sink_ref, o_ref, *, nblk):
    k_all = jnp.concatenate([kp_ref[...], kc_ref[...]], axis=0) * (C_HEAD_DIM ** -0.5)
    v_all = jnp.concatenate([vp_ref[...], vc_ref[...]], axis=0)
    swapped = (pltpu.roll(k_all, C_HEAD_DIM, 1), pltpu.roll(v_all, C_HEAD_DIM, 1))
    low = lax.broadcasted_iota(jnp.int32, k_all.shape, 1) < C_HEAD_DIM
    for hk in range(C_KV_HEADS):
        k_lo = k_all if hk == 0 else swapped[0]
        k_hi = swapped[0] if hk == 0 else k_all
        kz = (jnp.where(low, k_lo, 0.0).astype(BF16), jnp.where(low, 0.0, k_hi).astype(BF16))
        vz = (v_all if hk == 0 else swapped[1])[:, :C_HEAD_DIM].astype(BF16)
        for g in range(C_GQA):
            hq = hk * C_GQA + g
            sk = sink_ref[hq]
            for i in range(nblk):
                keys = slice(i * WINDOW, (i + 2) * WINDOW)
                qt = q_ref[i * WINDOW:(i + 1) * WINDOW, (hq // 2) * LANE:(hq // 2 + 1) * LANE]
                bias = b0_ref[0, hq] if i == 0 else br_ref[0, hq]
                s = _dot_nt(kz[hq % 2][keys], qt) + bias
                m = jnp.maximum(jnp.max(s, axis=0, keepdims=True), sk)
                p = jnp.exp(s - m)
                den = jnp.sum(p, axis=0, keepdims=True) + jnp.exp(sk - m)
                o = _dot_tn(vz[keys], p.astype(BF16)) / den
                o_ref[hq * C_HEAD_DIM:(hq + 1) * C_HEAD_DIM, i * WINDOW:(i + 1) * WINDOW] = o.astype(BF16)


def _attn_prompt(q, k, v, band, sinks, nb, nblk_seq, nblk):
    steps = nblk_seq // nblk
    cur = lambda b, n: (b * steps + n, 0)
    prev = lambda b, n: (b * nblk_seq + jnp.maximum(n * nblk - 1, 0), 0)
    bspec = lambda m: pl.BlockSpec((1, C_HEADS, 2 * WINDOW, WINDOW), m)
    return pl.pallas_call(
        functools.partial(_attn_prompt_body, nblk=nblk),
        grid=(nb, steps),
        in_specs=[pl.BlockSpec((nblk * WINDOW, C_WIDTH), cur),
                  pl.BlockSpec((nblk * WINDOW, C_KV_WIDTH), cur), pl.BlockSpec((WINDOW, C_KV_WIDTH), prev),
                  pl.BlockSpec((nblk * WINDOW, C_KV_WIDTH), cur), pl.BlockSpec((WINDOW, C_KV_WIDTH), prev),
                  bspec(lambda b, n: (jnp.minimum(n, 1), 0, 0, 0)), bspec(lambda b, n: (1, 0, 0, 0)),
                  pl.BlockSpec(memory_space=pltpu.SMEM)],
        out_specs=pl.BlockSpec((C_WIDTH, nblk * WINDOW), lambda b, n: (0, b * steps + n)),
        out_shape=jax.ShapeDtypeStruct((C_WIDTH, nb * nblk_seq * WINDOW), BF16),
        compiler_params=_cparams("parallel", "arbitrary"),
        name="attn_prompt",
    )(q, k, k, v, v, band, band, sinks)


def _attn_sample_body(*refs, valid, lb, nseq):
    q_ref, kn_ref, vn_ref, kc_ref, vc_ref, b1_ref, b2_ref, sink_ref = refs[:8]
    o_ref, ko_ref, vo_ref, s1_scr, s2_scr, inv_scr, o_scr = refs[-7:]
    d = C_HEAD_DIM
    rs = 2 * valid
    lane_c = lax.broadcasted_iota(jnp.int32, (C_KV_WIDTH, lb), 1)
    lane_n = lax.broadcasted_iota(jnp.int32, (SUBLANE, C_KV_WIDTH), 1)
    pad = jnp.zeros((SUBLANE - valid, C_KV_WIDTH), F32)
    zero_half = jnp.zeros((d, lb), BF16)
    combos = [(hk, par) for hk in range(C_KV_HEADS) for par in range(2)]

    def new_rows(ref, s):
        return jnp.concatenate([ref[:, s, :], pad], axis=0)

    for s in range(nseq):
        qs = q_ref[:, s, :].astype(F32)
        kn, vn = new_rows(kn_ref, s), new_rows(vn_ref, s)
        kt, vt = kc_ref[0, s], vc_ref[0, s]
        kn_t, vn_t = kn.T, vn.T
        new_k, new_v = pltpu.roll(kt, lb - valid, 1), pltpu.roll(vt, lb - valid, 1)
        for t in range(valid):
            at = lane_c == lb - valid + t
            new_k = jnp.where(at, jnp.broadcast_to(kn_t[:, t:t + 1], (C_KV_WIDTH, lb)), new_k)
            new_v = jnp.where(at, jnp.broadcast_to(vn_t[:, t:t + 1], (C_KV_WIDTH, lb)), new_v)
        ko_ref[0, s] = new_k
        vo_ref[0, s] = new_v
        kn_sw = pltpu.roll(kn, d, 1)
        for c, (hk, par) in enumerate(combos):
            kth = kt[hk * d:(hk + 1) * d].astype(BF16)
            kz = jnp.concatenate([kth, zero_half] if par == 0 else [zero_half, kth], axis=0)
            src = kn if hk == par else kn_sw
            knz = jnp.where((lane_n < d) if par == 0 else (lane_n >= d), src, 0.0).astype(BF16)
            qq = jnp.concatenate([qs[:, (hk * 2 + j) * LANE:(hk * 2 + j + 1) * LANE] for j in range(2)],
                                 axis=0).astype(BF16)
            s1_scr[c, s * rs:(s + 1) * rs, :] = _dot(qq, kz)
            s2_scr[c, s * rs:(s + 1) * rs, :] = _dot_nt(qq, knz)

    for c, (hk, par) in enumerate(combos):
        s1 = s1_scr[c] * (d ** -0.5) + b1_ref[hk, par]
        s2 = s2_scr[c] * (d ** -0.5) + b2_ref[hk, par]
        sk = sink_ref[hk, par]
        m = jnp.maximum(jnp.maximum(jnp.max(s1, axis=-1, keepdims=True), jnp.max(s2, axis=-1, keepdims=True)), sk)
        p1, p2 = jnp.exp(s1 - m), jnp.exp(s2 - m)
        den = jnp.sum(p1, axis=-1, keepdims=True) + jnp.sum(p2, axis=-1, keepdims=True) + jnp.exp(sk - m)
        s1_scr[c] = p1
        s2_scr[c] = p2
        inv_scr[c] = jnp.broadcast_to(1.0 / den, (nseq * rs, SUBLANE))

    for s in range(nseq):
        rows = slice(s * rs, (s + 1) * rs)
        vt = vc_ref[0, s]
        vn = new_rows(vn_ref, s)
        for c, (hk, par) in enumerate(combos):
            vth = vt[hk * d:(hk + 1) * d].astype(BF16)
            vnh = vn[:, hk * d:(hk + 1) * d].astype(BF16)
            o = _dot_nt(s1_scr[c, rows, :].astype(BF16), vth) + _dot(s2_scr[c, rows, :].astype(BF16), vnh)
            o_scr[hk, rows, par * d:(par + 1) * d] = o * inv_scr[c, rows, 0:1]

    for hk in range(C_KV_HEADS):
        for j in range(2):
            for t in range(valid):
                tile = hk * 2 + j
                o_ref[t, :, tile * LANE:(tile + 1) * LANE] = o_scr[hk, pl.ds(j * valid + t, nseq, stride=rs), :]


def _attn_sample(q3, kn3, vn3, kc_all, vc_all, b1, b2, sink_col, layer, k_prev, v_prev):
    valid, nsb, _ = q3.shape
    lb = kc_all.shape[-1]
    nseq = 2 * SUBLANE
    assert 2 * valid == SUBLANE
    rows = lambda w: pl.BlockSpec((valid, nseq, w), lambda i: (0, i, 0))
    cblk = pl.BlockSpec((1, nseq, C_KV_WIDTH, lb), lambda i: (layer, i, 0, 0))
    b1, b2, sink_col = (jnp.tile(a, (1, 1, nseq, 1)) for a in (b1, b2, sink_col))
    in_specs = [rows(C_WIDTH), rows(C_KV_WIDTH), rows(C_KV_WIDTH), cblk, cblk,
                _full(b1.shape), _full(b2.shape), _full(sink_col.shape)]
    args = [q3, kn3, vn3, kc_all, vc_all, b1, b2, sink_col]
    aliases = {}
    if k_prev is not None:
        in_specs += [pl.BlockSpec(memory_space=pl.ANY)] * 2
        aliases = {len(args): 1, len(args) + 1: 2}
        args += [k_prev, v_prev]
    return pl.pallas_call(
        functools.partial(_attn_sample_body, valid=valid, lb=lb, nseq=nseq),
        grid=(nsb // nseq,),
        in_specs=in_specs,
        out_specs=[rows(C_WIDTH), cblk, cblk],
        out_shape=[jax.ShapeDtypeStruct((valid, nsb, C_WIDTH), F32), jax.ShapeDtypeStruct(kc_all.shape, F32),
                   jax.ShapeDtypeStruct(vc_all.shape, F32)],
        scratch_shapes=[pltpu.VMEM((2 * C_KV_HEADS, nseq * SUBLANE, lb), F32),
                        pltpu.VMEM((2 * C_KV_HEADS, nseq * SUBLANE, SUBLANE), F32),
                        pltpu.VMEM((2 * C_KV_HEADS, nseq * SUBLANE, SUBLANE), F32),
                        pltpu.VMEM((C_KV_HEADS, nseq * SUBLANE, LANE), F32)],
        input_output_aliases=aliases,
        compiler_params=_cparams("arbitrary"),
        name="attn_sample",
    )(*args)


def _merge_body(x_ref, ya_ref, yb_ref, u_ref, yc_ref, nw_ref, wg_ref, d5_ref, wglu_ref,
                wa_ref, wb_ref, wc_ref, wo_ref, o_ref, *, ya_features_major, yc_features_major):
    x = x_ref[...]
    h = _rms(x, nw_ref[...]).astype(BF16)
    g = jax.nn.gelu(_lanes(yb_ref) + d5_ref[...] * _lanes(u_ref))
    gv = _dot(g.astype(BF16), wglu_ref[...])
    yb = (gv[:, :B_WIDTH] * jax.nn.sigmoid(gv[:, B_WIDTH:])).astype(BF16)

    def gate(i):
        return jax.nn.sigmoid(_dot(h, wg_ref[:, i * D_MODEL:(i + 1) * D_MODEL]))

    def branch(y_ref, w_ref, features_major):
        y = y_ref[...].astype(BF16)
        return _dot_tn(y, w_ref[...]) if features_major else _dot(y, w_ref[...])

    mixed = gate(0) * branch(ya_ref, wa_ref, ya_features_major)
    mixed = mixed + gate(1) * _dot(yb, wb_ref[...])
    mixed = mixed + gate(2) * branch(yc_ref, wc_ref, yc_features_major)
    o_ref[...] = x + _dot(mixed.astype(BF16), wo_ref[...])


def _merge(x, ya, yb, u, yc, lw, tm, ya_features_major, yc_features_major):
    t = x.shape[0]
    rows = lambda w: pl.BlockSpec((tm, w), lambda i: (i, 0))
    branch_spec = lambda fm: pl.BlockSpec((A_WIDTH, tm), lambda i: (0, i)) if fm else rows(A_WIDTH)
    return pl.pallas_call(
        functools.partial(_merge_body, ya_features_major=ya_features_major, yc_features_major=yc_features_major),
        grid=(t // tm,),
        in_specs=[rows(D_MODEL), branch_spec(ya_features_major), _tile_major_spec(tm), _tile_major_spec(tm),
                  branch_spec(yc_features_major),
                  _full((1, D_MODEL)), _full((D_MODEL, 3 * D_MODEL)), _full((1, B_WIDTH)),
                  _full((B_WIDTH, 2 * B_WIDTH)), _full((A_WIDTH, D_MODEL)), _full((B_WIDTH, D_MODEL)),
                  _full((C_WIDTH, D_MODEL)), _full((D_MODEL, D_MODEL))],
        out_specs=rows(D_MODEL),
        out_shape=jax.ShapeDtypeStruct((t, D_MODEL), F32),
        compiler_params=_cparams("parallel"),
        name="merge",
    )(x, ya, yb, u, yc, lw["norm1_w"], lw["w_gate"], lw["s5_d"], lw["w_glu"],
      lw["w_br_a"], lw["w_br_b"], lw["w_br_c"], lw["w_out"])


def _ffn_body(x_ref, nw_ref, wup_ref, wdn_ref, fnw_ref, o_ref, act_scr, *, final_norm):
    x = x_ref[...]
    h = _rms(x, nw_ref[...]).astype(BF16)
    for c in range(D_FF // FF_CHUNK):
        lo, hi = c * FF_CHUNK, (c + 1) * FF_CHUNK
        a = _dot(h, wup_ref[:, lo:hi])
        b = _dot(h, wup_ref[:, D_FF + lo:D_FF + hi])
        act_scr[:, lo:hi] = (jax.nn.silu(a) * b).astype(BF16)
    y = x + _dot(act_scr[...], wdn_ref[...])
    if final_norm:
        y = _rms(y, fnw_ref[...])
    o_ref[...] = y


def _ffn(x, lw, fnw, tm, final_norm):
    t = x.shape[0]
    rows = pl.BlockSpec((tm, D_MODEL), lambda i: (i, 0))
    return pl.pallas_call(
        functools.partial(_ffn_body, final_norm=final_norm),
        grid=(t // tm,),
        in_specs=[rows, _full((1, D_MODEL)), _full((D_MODEL, 2 * D_FF)), _full((D_FF, D_MODEL)), _full((1, D_MODEL))],
        out_specs=rows,
        out_shape=jax.ShapeDtypeStruct((t, D_MODEL), F32),
        scratch_shapes=[pltpu.VMEM((tm, D_FF), BF16)],
        compiler_params=_cparams("parallel"),
        name="ffn",
    )(x, lw["norm2_w"], lw["w_ffn_up"], lw["w_ffn_down"], fnw)


def _row_tile(t, pref):
    tm = pref
    while t % tm:
        tm //= 2
    return tm


def kernel(x_prompt, x_sample, state_ssd, state_conv, state_s5_re, state_s5_im, cache_k, cache_v, norm1_w, w_in, conv_w, conv_b, ssd_a_log, ssd_dt_bias, ssd_d, ssd_norm_w, s5_lam_re, s5_lam_im, s5_log_dt, s5_b_re, s5_b_im, s5_c_re, s5_c_im, s5_d, s5_w_glu, attn_sinks, w_br_a, w_br_b, w_br_c, w_out, norm2_w, w_ffn_up, w_ffn_down, rel_bias, final_norm_w):
    nb, seq, _ = x_prompt.shape
    nsb, valid, _ = x_sample.shape
    depth = w_in.shape[0]
    lb = cache_k.shape[2]
    assert seq % A_CHUNK == 0 and seq % WINDOW == 0 and seq % (S5_Q * SUBLANE) == 0
    assert nsb == LANE and A_CONV - 1 <= valid <= S5_Q and lb == WINDOW
    nchunk = seq // A_CHUNK
    tp = nb * seq
    ts = valid * nsb

    band = _bias_band(rel_bias)

    def pair_rows(a):
        a = a.reshape(C_KV_HEADS, 2, 2, valid, a.shape[-1])
        return jnp.swapaxes(a, 1, 2).reshape(C_KV_HEADS, 2, 2 * valid, a.shape[-1])

    band_s = pair_rows(jnp.swapaxes(band[1, :, :, :valid], 1, 2))
    bias_s1 = band_s[..., :lb]
    bias_s2 = band_s[..., lb:lb + SUBLANE]
    fnw = final_norm_w.reshape(1, D_MODEL)

    xp = x_prompt.reshape(tp, D_MODEL)
    xs = jnp.swapaxes(x_sample, 0, 1).reshape(ts, D_MODEL)
    conv_all = jnp.swapaxes(state_conv, 1, 2)
    ssd_all = jnp.transpose(state_ssd, (0, 2, 3, 4, 1))
    s5re_all = jnp.transpose(state_s5_re, (0, 2, 3, 1)).reshape(depth, S5_STATE_COLS, nsb)
    s5im_all = jnp.transpose(state_s5_im, (0, 2, 3, 1)).reshape(depth, S5_STATE_COLS, nsb)
    kc_all = jnp.transpose(cache_k, (0, 1, 3, 4, 2)).reshape(depth, nsb, C_KV_WIDTH, lb)
    vc_all = jnp.transpose(cache_v, (0, 1, 3, 4, 2)).reshape(depth, nsb, C_KV_WIDTH, lb)
    tm_p = _row_tile(tp, 512)
    tm_s = _row_tile(ts, 512)
    rt_p = _row_tile(tp // S5_Q, 256)
    rb_p = _row_tile(seq // S5_Q, 128)
    attn_blk = _row_tile(seq // WINDOW, 4)

    new_p, s5_s = [], []
    conv_new = ssd_new = k_new = v_new = None
    for i in range(depth):
        wi = w_in[i]
        off = [0, 512, 1280, 1288, 1800, 2312, 2440, 2568, 5640]
        w_pack = jnp.concatenate(
            [wi[:, off[0]:off[1]], wi[:, off[1]:off[2]], wi[:, off[3]:off[4]], wi[:, off[4]:off[5]],
             wi[:, off[5]:off[6]], wi[:, off[6]:off[7]],
             jnp.pad(wi[:, off[2]:off[3]], ((0, 0), (0, LANE - A_HEADS)))], axis=1).astype(BF16)
        sink_rows = pair_rows(jnp.broadcast_to(attn_sinks[i].reshape(C_HEADS, 1, 1), (C_HEADS, valid, 1)))
        lw = dict(
            norm1_w=norm1_w[i].reshape(1, D_MODEL), w_gate=wi[:, off[7]:off[8]].astype(BF16),
            conv_w=conv_w[i], conv_b=conv_b[i].reshape(1, A_CONV_DIM),
            a_log=jnp.pad(ssd_a_log[i], (0, LANE - A_HEADS)).reshape(1, LANE),
            dt_bias=jnp.pad(ssd_dt_bias[i], (0, LANE - A_HEADS)).reshape(1, LANE),
            d_exp=jnp.repeat(ssd_d[i], A_HEAD_DIM).reshape(1, A_WIDTH),
            ssd_norm_w=ssd_norm_w[i].reshape(1, A_WIDTH),
            ssd_norm_wb=jnp.broadcast_to(ssd_norm_w[i].reshape(A_WIDTH, 1), (A_WIDTH, LANE)),
            s5_d=s5_d[i].reshape(1, B_WIDTH), w_glu=s5_w_glu[i].astype(BF16),
            w_br_a=w_br_a[i].astype(BF16), w_br_b=w_br_b[i].astype(BF16), w_br_c=w_br_c[i].astype(BF16),
            w_out=w_out[i].astype(BF16), norm2_w=norm2_w[i].reshape(1, D_MODEL),
            w_ffn_up=w_ffn_up[i].astype(BF16), w_ffn_down=w_ffn_down[i].astype(BF16),
        )
        s5w = _s5_weights(s5_lam_re[i], s5_lam_im[i], s5_log_dt[i], s5_b_re[i], s5_b_im[i],
                          s5_c_re[i], s5_c_im[i], valid)
        last = i == depth - 1

        z, xbc, u, q, k, v, dt = _inproj(xp, lw["norm1_w"], w_pack, tm_p)
        ya, ssd_h = _ssd(xbc, z, dt, lw, nb, nchunk, A_CHUNK)
        sre, sim = _s5_state(u, s5w["bst_re"], s5w["bst_im"], rt_p)
        hre, him, fre, fim = _s5_scan(sre, sim, s5w["aq_re"], s5w["aq_im"], nb, rb_p)
        yb = _s5_out(u, hre, him, s5w["toep"], s5w["cst_re"], s5w["cst_im"], rt_p)
        yc = _attn_prompt(q, k, v, band, attn_sinks[i], nb, seq // WINDOW, attn_blk)
        x1 = _merge(xp, ya, yb, u, yc, lw, tm_p, False, True)
        xp = _ffn(x1, lw, fnw, tm_p, last)
        new_p.append((
            ssd_h.reshape(nb, A_HEADS, A_HEAD_DIM, A_STATE),
            xbc.reshape(nb, seq, A_CONV_DIM)[:, seq - (A_CONV - 1):],
            fre.reshape(nb, B_GROUPS, B_STATE), fim.reshape(nb, B_GROUPS, B_STATE),
            k.reshape(nb, seq, C_KV_HEADS, C_HEAD_DIM)[:, seq - WINDOW:],
            v.reshape(nb, seq, C_KV_HEADS, C_HEAD_DIM)[:, seq - WINDOW:]))

        z, xbc, u, q, k, v, dt = _inproj(xs, lw["norm1_w"], w_pack, tm_s)
        ya, conv_new, ssd_new = _ssd_sample(xbc, z, dt, conv_all, ssd_all, lw, i, valid, conv_new, ssd_new)
        av_re = jnp.broadcast_to(s5w["av_re"].reshape(S5_STATE_COLS, 1), (S5_STATE_COLS, nsb))
        av_im = jnp.broadcast_to(s5w["av_im"].reshape(S5_STATE_COLS, 1), (S5_STATE_COLS, nsb))
        yb, fre, fim = _s5_sample(u, s5w, s5re_all, s5im_all, av_re, av_im, i, valid)
        yc, k_new, v_new = _attn_sample(
            q.reshape(valid, nsb, C_WIDTH), k.reshape(valid, nsb, C_KV_WIDTH), v.reshape(valid, nsb, C_KV_WIDTH),
            kc_all, vc_all, bias_s1, bias_s2, sink_rows, i, k_new, v_new)
        x1 = _merge(xs, ya, yb, u, yc.reshape(ts, C_WIDTH), lw, tm_s, True, False)
        xs = _ffn(x1, lw, fnw, tm_s, last)
        s5_s.append((fre, fim))

    def stack(states, j):
        return jnp.stack([s[j] for s in states], axis=0)

    def s5_state(j):
        return jnp.transpose(stack(s5_s, j).reshape(depth, B_GROUPS, B_STATE, nsb), (0, 3, 1, 2))

    def cache(a):
        return jnp.transpose(a.reshape(depth, nsb, C_KV_HEADS, C_HEAD_DIM, lb), (0, 1, 4, 2, 3))

    y_prompt = xp.reshape(nb, seq, D_MODEL)
    y_sample = jnp.swapaxes(xs.reshape(valid, nsb, D_MODEL), 0, 1)
    return (y_prompt, y_sample,
            stack(new_p, 0), stack(new_p, 1), stack(new_p, 2), stack(new_p, 3), stack(new_p, 4), stack(new_p, 5),
            jnp.transpose(ssd_new, (0, 4, 1, 2, 3)), jnp.swapaxes(conv_new, 1, 2), s5_state(0), s5_state(1),
            cache(k_new), cache(v_new))
```

```python
import functools
import math

import numpy as np
import jax
import jax.numpy as jnp
from jax import lax
from jax.experimental import pallas as pl
from jax.experimental.pallas import tpu as pltpu

F32 = jnp.float32
BF16 = jnp.bfloat16
HIGHEST = lax.Precision.HIGHEST

D_MODEL = 1024
A_HEAD_DIM = 64
A_WIDTH = 512
A_HEADS = 8
A_GROUPS = 2
A_STATE = 64
A_CONV = 4
A_CONV_DIM = 768
A_CHUNK = 128
B_CH = 16
B_WIDTH = 512
B_GROUPS = 32
B_STATE = 64
C_HEAD_DIM = 64
C_WIDTH = 512
C_HEADS = 8
C_KV_HEADS = 2
C_GQA = 4
C_KV_WIDTH = 128
WINDOW = 128
REL_BUCKETS = 32
REL_MAX_DIST = 128
D_FF = 2816
EPS = 1e-6

P_Z = 0
P_XBC = P_Z + A_WIDTH
P_U = P_XBC + A_CONV_DIM
P_Q = P_U + B_WIDTH
P_K = P_Q + C_WIDTH
P_V = P_K + C_KV_WIDTH
P_DT = P_V + C_KV_WIDTH
P_COLS = P_DT + 128

LANE = 128
SUBLANE = 8
S5_Q = 8
S5_OCT = 8
N_OCT = B_GROUPS // S5_OCT
S5_STATE_COLS = B_GROUPS * B_STATE
NEG = -1e30
VMEM_LIMIT = 56 * 1024 * 1024
FF_CHUNK = 256


def _cparams(*sem):
    return pltpu.CompilerParams(dimension_semantics=sem, vmem_limit_bytes=VMEM_LIMIT)


def _rms(x, w):
    return x * lax.rsqrt(jnp.mean(x * x, axis=-1, keepdims=True) + EPS) * w


def _dot(a, b):
    return jnp.dot(a, b, preferred_element_type=F32)


def _dot_nt(a, b, precision=None):
    return lax.dot_general(a, b, (((1,), (1,)), ((), ())), preferred_element_type=F32, precision=precision)


def _dot_tn(a, b):
    return lax.dot_general(a, b, (((0,), (0,)), ((), ())), preferred_element_type=F32)


def _full(shape):
    n = len(shape)
    return pl.BlockSpec(shape, lambda *_: (0,) * n)


def _tile_major_spec(rows):
    return pl.BlockSpec((N_OCT, rows, LANE), lambda i: (0, i, 0))


def _lanes(ref):
    return jnp.concatenate([ref[k8] for k8 in range(N_OCT)], axis=1)


def _inproj_body(x_ref, nw_ref, w_ref, z_ref, xbc_ref, u_ref, q_ref, k_ref, v_ref, dt_ref):
    h = _rms(x_ref[...], nw_ref[...]).astype(BF16)

    def proj(lo, hi):
        return _dot(h, w_ref[:, lo:hi])

    z_ref[...] = proj(P_Z, P_XBC).astype(BF16)
    xbc_ref[...] = proj(P_XBC, P_U)
    u = proj(P_U, P_Q)
    for k8 in range(N_OCT):
        u_ref[k8] = u[:, k8 * LANE:(k8 + 1) * LANE]
    q_ref[...] = proj(P_Q, P_K).astype(BF16)
    kvdt = proj(P_K, P_COLS)
    k_ref[...] = kvdt[:, :C_KV_WIDTH]
    v_ref[...] = kvdt[:, C_KV_WIDTH:2 * C_KV_WIDTH]
    dt_ref[...] = kvdt[:, 2 * C_KV_WIDTH:]


def _inproj(x, nw, w, tm):
    t = x.shape[0]
    widths = [(A_WIDTH, BF16), (A_CONV_DIM, F32), None, (C_WIDTH, BF16),
              (C_KV_WIDTH, F32), (C_KV_WIDTH, F32), (LANE, F32)]
    spec = lambda w: _tile_major_spec(tm) if w is None else pl.BlockSpec((tm, w[0]), lambda i: (i, 0))
    shape = lambda w: (jax.ShapeDtypeStruct((N_OCT, t, LANE), F32) if w is None
                       else jax.ShapeDtypeStruct((t, w[0]), w[1]))
    return pl.pallas_call(
        _inproj_body,
        grid=(t // tm,),
        in_specs=[pl.BlockSpec((tm, D_MODEL), lambda i: (i, 0)), _full((1, D_MODEL)), _full((D_MODEL, P_COLS))],
        out_specs=[spec(w) for w in widths],
        out_shape=[shape(w) for w in widths],
        compiler_params=_cparams("parallel"),
        name="inproj",
    )(x, nw, w)


def _ssd_chunk(xbc, z, dt_raw, cw_ref, cb_ref, alog_ref, dtb_ref, dexp_ref, nw_ref, h_scr, tail_scr):
    q = xbc.shape[0]
    tail_scr[SUBLANE:SUBLANE + q] = xbc
    acc = cb_ref[...] + cw_ref[A_CONV - 1:A_CONV, :] * xbc
    for j in range(1, A_CONV):
        acc = acc + cw_ref[A_CONV - 1 - j:A_CONV - j, :] * tail_scr[SUBLANE - j:SUBLANE - j + q]
    tail_scr[0:SUBLANE] = xbc[q - SUBLANE:q]
    xc = jax.nn.silu(acc)
    x = xc[:, :A_WIDTH]
    bm = xc[:, A_WIDTH:A_WIDTH + A_GROUPS * A_STATE].astype(BF16)
    cm = xc[:, A_WIDTH + A_GROUPS * A_STATE:].astype(BF16)

    dt = jax.nn.softplus(dt_raw + dtb_ref[...])
    dta = dt * (-jnp.exp(alog_ref[...]))
    row = lax.broadcasted_iota(jnp.int32, (q, q), 0)
    col = lax.broadcasted_iota(jnp.int32, (q, q), 1)
    tri = row >= col
    a_cum = jnp.dot(tri.astype(F32), dta, preferred_element_type=F32, precision=HIGHEST)

    def expand(v):
        return jnp.concatenate([jnp.broadcast_to(v[:, k:k + 1], (q, A_HEAD_DIM)) for k in range(A_HEADS)], axis=1)

    a_cum_e = expand(a_cum)
    xs = x * expand(dt)
    xs_b = xs.astype(BF16)
    a_cum_t = a_cum.T

    h = h_scr[...]
    h_b = h.astype(BF16)
    hpg = A_HEADS // A_GROUPS
    gw = hpg * A_HEAD_DIM
    y_diag, y_off = [], []
    for g in range(A_GROUPS):
        cg = cm[:, g * A_STATE:(g + 1) * A_STATE]
        bg = bm[:, g * A_STATE:(g + 1) * A_STATE]
        cb = _dot_nt(cg, bg)
        for kk in range(hpg):
            k = g * hpg + kk
            seg = a_cum[:, k:k + 1] - a_cum_t[k:k + 1, :]
            decay = jnp.exp(jnp.where(tri, seg, -jnp.inf))
            y_diag.append(_dot((cb * decay).astype(BF16), xs_b[:, k * A_HEAD_DIM:(k + 1) * A_HEAD_DIM]))
        y_off.append(_dot_nt(cg, h_b[g * gw:(g + 1) * gw]))
    y = (jnp.concatenate(y_diag, axis=1) + jnp.concatenate(y_off, axis=1) * jnp.exp(a_cum_e)
         + dexp_ref[...] * x)
    y = y * jax.nn.silu(z)

    last = a_cum_e[q - 1:q, :]
    xs_dec = (xs * jnp.exp(last - a_cum_e)).astype(BF16)
    dec_col = jnp.exp(a_cum_e.T[:, q - 1:q])
    upd = [_dot_tn(xs_dec[:, g * gw:(g + 1) * gw], bm[:, g * A_STATE:(g + 1) * A_STATE]) for g in range(A_GROUPS)]
    h_scr[...] = dec_col * h + jnp.concatenate(upd, axis=0)
    return _rms(y, nw_ref[...]).astype(BF16)


def _inproj_ssd_body(x_ref, nw_ref, w_ref, cw_ref, cb_ref, alog_ref, dtb_ref, dexp_ref, snw_ref,
                     u_ref, q_ref, k_ref, v_ref, ya_ref, hout_ref, tail_ref, h_scr, tail_scr, *, nsteps, chunk):
    c = pl.program_id(1)

    @pl.when(c == 0)
    def _():
        h_scr[...] = jnp.zeros_like(h_scr)
        tail_scr[0:SUBLANE] = jnp.zeros((SUBLANE, A_CONV_DIM), F32)

    h = _rms(x_ref[...], nw_ref[...]).astype(BF16)

    def proj(lo, hi):
        return _dot(h, w_ref[:, lo:hi])

    z = proj(P_Z, P_XBC)
    xbc = proj(P_XBC, P_U)
    kvdt = proj(P_K, P_COLS)
    k_ref[...] = kvdt[:, :C_KV_WIDTH]
    v_ref[...] = kvdt[:, C_KV_WIDTH:2 * C_KV_WIDTH]
    dt = kvdt[:, 2 * C_KV_WIDTH:]
    for i in range(x_ref.shape[0] // chunk):
        rows = slice(i * chunk, (i + 1) * chunk)
        ya_ref[rows, :] = _ssd_chunk(xbc[rows], z[rows], dt[rows], cw_ref, cb_ref, alog_ref, dtb_ref, dexp_ref,
                                     snw_ref, h_scr, tail_scr)
    u = proj(P_U, P_Q)
    for k8 in range(N_OCT):
        u_ref[k8] = u[:, k8 * LANE:(k8 + 1) * LANE]
    q_ref[...] = proj(P_Q, P_K).astype(BF16)

    @pl.when(c == nsteps - 1)
    def _():
        hout_ref[0] = h_scr[...]
        tail_ref[0] = tail_scr[0:SUBLANE]


def _inproj_ssd(x, w, lw, nb, tm):
    t = x.shape[0]
    nsteps = t // nb // tm
    rmap = lambda b, c: (b * nsteps + c, 0)
    rows = lambda wd: pl.BlockSpec((tm, wd), rmap)
    per_seq = lambda s: pl.BlockSpec((1,) + s, lambda b, c: (b, 0, 0))
    return pl.pallas_call(
        functools.partial(_inproj_ssd_body, nsteps=nsteps, chunk=A_CHUNK),
        grid=(nb, nsteps),
        in_specs=[rows(D_MODEL), _full((1, D_MODEL)), _full((D_MODEL, P_COLS)),
                  _full((A_CONV, A_CONV_DIM)), _full((1, A_CONV_DIM)), _full((1, LANE)), _full((1, LANE)),
                  _full((1, A_WIDTH)), _full((1, A_WIDTH))],
        out_specs=[pl.BlockSpec((N_OCT, tm, LANE), lambda b, c: (0, b * nsteps + c, 0)), rows(C_WIDTH),
                   rows(C_KV_WIDTH), rows(C_KV_WIDTH), rows(A_WIDTH),
                   per_seq((A_WIDTH, A_STATE)), per_seq((SUBLANE, A_CONV_DIM))],
        out_shape=[jax.ShapeDtypeStruct((N_OCT, t, LANE), F32), jax.ShapeDtypeStruct((t, C_WIDTH), BF16),
                   jax.ShapeDtypeStruct((t, C_KV_WIDTH), F32), jax.ShapeDtypeStruct((t, C_KV_WIDTH), F32),
                   jax.ShapeDtypeStruct((t, A_WIDTH), BF16), jax.ShapeDtypeStruct((nb, A_WIDTH, A_STATE), F32),
                   jax.ShapeDtypeStruct((nb, SUBLANE, A_CONV_DIM), F32)],
        scratch_shapes=[pltpu.VMEM((A_WIDTH, A_STATE), F32), pltpu.VMEM((SUBLANE + A_CHUNK, A_CONV_DIM), F32)],
        compiler_params=_cparams("parallel", "arbitrary"),
        name="inproj_ssd",
    )(x, lw["norm1_w"], w, lw["conv_w"], lw["conv_b"], lw["a_log"], lw["dt_bias"], lw["d_exp"], lw["ssd_norm_w"])


def _ssd_sample_body(*refs, valid):
    n_in = 13
    (xbc_ref, z_ref, dt_ref, conv0_ref, h0_ref, cw_ref, cb_ref, alog_ref, dtb_ref, dexp_ref, nwb_ref) = refs[:11]
    y_ref, conv_out_ref, h_out_ref = refs[n_in:n_in + 3]
    xs_scr, dx_scr, z_scr, y_scr, bt_scr, ct_scr, da_scr = refs[n_in + 3:]
    k = pl.program_id(0)
    ns = LANE

    @pl.when(k == 0)
    def _():
        rows = [conv0_ref[0, r] for r in range(A_CONV - 1)] + [xbc_ref[t * ns:(t + 1) * ns, :] for t in range(valid)]
        for t in range(valid):
            acc = cb_ref[...]
            for j in range(A_CONV):
                acc = acc + cw_ref[A_CONV - 1 - j:A_CONV - j, :] * rows[t + A_CONV - 1 - j]
            xc = jax.nn.silu(acc)
            dt = jax.nn.softplus(dt_ref[t * ns:(t + 1) * ns, :] + dtb_ref[...])
            dt_t = dt.T
            da_scr[t] = jnp.exp((dt * (-jnp.exp(alog_ref[...]))).T)
            x = xc[:, :A_WIDTH]
            x_t = x.T
            for kk in range(A_HEADS):
                hs = slice(kk * A_HEAD_DIM, (kk + 1) * A_HEAD_DIM)
                xs_scr[t, hs, :] = x_t[hs] * dt_t[kk:kk + 1, :]
            dx_scr[t] = (x * dexp_ref[...]).T
            bt_scr[t] = xc[:, A_WIDTH:A_WIDTH + A_GROUPS * A_STATE].T
            ct_scr[t] = xc[:, A_WIDTH + A_GROUPS * A_STATE:].T
            z_scr[t] = jax.nn.silu(z_ref[t * ns:(t + 1) * ns, :].astype(F32)).T
        for r in range(A_CONV - 1):
            conv_out_ref[0, r] = rows[valid + r]

    g0 = pl.multiple_of((k // (A_HEADS // A_GROUPS)) * A_STATE, A_STATE)

    def per_p(p, carry):
        h = h0_ref[0, 0, p]
        row = k * A_HEAD_DIM + p
        for t in range(valid):
            da = da_scr[t, pl.ds(k, 1), :]
            xr = xs_scr[t, pl.ds(row, 1), :]
            h = da * h + xr * bt_scr[t, pl.ds(g0, A_STATE), :]
            y_scr[t, pl.ds(row, 1), :] = jnp.sum(h * ct_scr[t, pl.ds(g0, A_STATE), :], axis=0, keepdims=True)
        h_out_ref[0, 0, p] = h
        return carry

    lax.fori_loop(0, A_HEAD_DIM, per_p, 0, unroll=2)

    @pl.when(k == A_HEADS - 1)
    def _():
        for t in range(valid):
            y = (y_scr[t] + dx_scr[t]) * z_scr[t]
            y = y * lax.rsqrt(jnp.mean(y * y, axis=0, keepdims=True) + EPS) * nwb_ref[...]
            y_ref[:, t * ns:(t + 1) * ns] = y.astype(BF16)


def _ssd_sample(xbc, z, dt, conv_all, h_all, lw, layer, valid, conv_prev, h_prev):
    rows = valid * LANE
    hblk = pl.BlockSpec((1, 1, A_HEAD_DIM, A_STATE, LANE), lambda k: (layer, k, 0, 0, 0))
    cblk = pl.BlockSpec((1, A_CONV - 1, LANE, A_CONV_DIM), lambda k: (layer, 0, 0, 0))
    in_specs = [_full((rows, A_CONV_DIM)), _full((rows, A_WIDTH)), _full((rows, LANE)), cblk, hblk,
                _full((A_CONV, A_CONV_DIM)), _full((1, A_CONV_DIM)), _full((1, LANE)), _full((1, LANE)),
                _full((1, A_WIDTH)), _full((A_WIDTH, LANE))] + [pl.BlockSpec(memory_space=pl.ANY)] * 2
    args = [xbc, z, dt, conv_all, h_all, lw["conv_w"], lw["conv_b"], lw["a_log"], lw["dt_bias"], lw["d_exp"],
            lw["ssd_norm_wb"], conv_prev, h_prev]
    aliases = {len(args) - 2: 1, len(args) - 1: 2}
    big = lambda n: pltpu.VMEM((valid, n, LANE), F32)
    return pl.pallas_call(
        functools.partial(_ssd_sample_body, valid=valid),
        grid=(A_HEADS,),
        in_specs=in_specs,
        out_specs=[_full((A_WIDTH, rows)), cblk, hblk],
        out_shape=[jax.ShapeDtypeStruct((A_WIDTH, rows), BF16), jax.ShapeDtypeStruct(conv_all.shape, F32),
                   jax.ShapeDtypeStruct(h_all.shape, F32)],
        scratch_shapes=[big(A_WIDTH), big(A_WIDTH), big(A_WIDTH), big(A_WIDTH), big(LANE), big(LANE), big(LANE)],
        input_output_aliases=aliases,
        compiler_params=_cparams("arbitrary"),
        name="ssd_sample",
    )(*args)


def _s5w_body(lre_ref, lim_ref, ldt_ref, btr_ref, bti_ref, cr_ref, ci_ref,
              bre_ref, bim_ref, bsre_ref, bsim_ref, cre_ref, cim_ref, toep_ref, ap_ref, cd_scr, bb_scr, *, valid):
    h, q = B_STATE, S5_Q
    for ref in (bre_ref, bim_ref, bsre_ref, bsim_ref, cre_ref, cim_ref, toep_ref):
        ref[...] = jnp.zeros(ref.shape, BF16)
    r_i = lax.broadcasted_iota(jnp.int32, (q * B_CH, q * LANE), 0)
    c_i = lax.broadcasted_iota(jnp.int32, (q * B_CH, q * LANE), 1)
    d = lax.broadcasted_iota(jnp.int32, (2 * SUBLANE, h), 0).astype(F32)
    for g8 in range(S5_OCT):
        lr, li = lre_ref[g8], lim_ref[g8]
        step = jnp.exp(ldt_ref[g8])
        er = jnp.exp(lr * step * d)
        ang = li * step * d
        ar, ai = er * jnp.cos(ang), er * jnp.sin(ang)
        nr, ni = ar[1:2] - 1.0, ai[1:2]
        den = lr * lr + li * li
        wr, wi = (nr * lr + ni * li) / den, (ni * lr - nr * li) / den
        btr, bti = btr_ref[g8], bti_ref[g8]
        bbr, bbi = wr * btr - wi * bti, wr * bti + wi * btr
        cr, ci = cr_ref[g8], ci_ref[g8]
        bb_scr[:, 0:h] = bbr
        bb_scr[:, h:2 * h] = bbi
        cols = slice(g8 * h, (g8 + 1) * h)
        for s in range(q):
            rows = slice(s * LANE + g8 * B_CH, s * LANE + (g8 + 1) * B_CH)
            e = q - 1 - s
            bre_ref[0, rows, cols] = (ar[e:e + 1] * bbr - ai[e:e + 1] * bbi).astype(BF16)
            bim_ref[0, rows, cols] = (ai[e:e + 1] * bbr + ar[e:e + 1] * bbi).astype(BF16)
            if s < valid:
                e = valid - 1 - s
                bsre_ref[0, rows, cols] = (ar[e:e + 1] * bbr - ai[e:e + 1] * bbi).astype(BF16)
                bsim_ref[0, rows, cols] = (ai[e:e + 1] * bbr + ar[e:e + 1] * bbi).astype(BF16)
            cre_ref[0, rows, cols] = (cr * ar[s + 1:s + 2] - ci * ai[s + 1:s + 2]).astype(BF16)
            cim_ref[0, rows, cols] = (-(cr * ai[s + 1:s + 2] + ci * ar[s + 1:s + 2])).astype(BF16)
            cd_scr[s * B_CH:(s + 1) * B_CH, 0:h] = cr * ar[s:s + 1] - ci * ai[s:s + 1]
            cd_scr[s * B_CH:(s + 1) * B_CH, h:2 * h] = -(cr * ai[s:s + 1] + ci * ar[s:s + 1])
        kdt = _dot_nt(bb_scr[...], cd_scr[...], precision=HIGHEST)
        place = (c_i == (r_i >> 4) * LANE + g8 * B_CH + (r_i & (B_CH - 1))).astype(F32)
        slab = jnp.dot(kdt, place, preferred_element_type=F32, precision=HIGHEST).astype(BF16)
        for s in range(q):
            toep_ref[0, s * LANE + g8 * B_CH:s * LANE + (g8 + 1) * B_CH, s * LANE:] = slab[:, 0:(q - s) * LANE]
        ap_ref[g8, 0:1, 0:h] = ar[q:q + 1]
        ap_ref[g8, 0:1, h:2 * h] = ai[q:q + 1]
        ap_ref[g8, 1:2, 0:h] = ar[valid:valid + 1]
        ap_ref[g8, 1:2, h:2 * h] = ai[valid:valid + 1]


def _s5_weights(lam_re, lam_im, log_dt, b_re, b_im, c_re, c_im, valid):
    g, h, q = B_GROUPS, B_STATE, S5_Q
    vec = pl.BlockSpec((S5_OCT, 1, h), lambda i: (i, 0, 0))
    mat = pl.BlockSpec((S5_OCT, B_CH, h), lambda i: (i, 0, 0))
    st = pl.BlockSpec((1, q * LANE, S5_OCT * h), lambda i: (i, 0, 0))
    st_shape = jax.ShapeDtypeStruct((N_OCT, q * LANE, S5_OCT * h), BF16)
    bre, bim, bsre, bsim, cre, cim, toep, ap = pl.pallas_call(
        functools.partial(_s5w_body, valid=valid),
        grid=(N_OCT,),
        in_specs=[vec, vec, vec, mat, mat, mat, mat],
        out_specs=[st] * 6 + [pl.BlockSpec((1, q * LANE, q * LANE), lambda i: (i, 0, 0)),
                              pl.BlockSpec((S5_OCT, 2, 2 * h), lambda i: (i, 0, 0))],
        out_shape=[st_shape] * 6 + [jax.ShapeDtypeStruct((N_OCT, q * LANE, q * LANE), BF16),
                                    jax.ShapeDtypeStruct((g, 2, 2 * h), F32)],
        scratch_shapes=[pltpu.VMEM((q * B_CH, 2 * h), F32), pltpu.VMEM((B_CH, 2 * h), F32)],
        compiler_params=_cparams("parallel"),
        name="s5_weights",
    )(lam_re.reshape(g, 1, h), lam_im.reshape(g, 1, h), jnp.broadcast_to(log_dt.reshape(g, 1, 1), (g, 1, h)),
      jnp.swapaxes(b_re, 1, 2), jnp.swapaxes(b_im, 1, 2), c_re, c_im)
    return dict(
        bst_re=bre, bst_im=bim, bsts_re=bsre, bsts_im=bsim, cst_re=cre, cst_im=cim, toep=toep,
        aq_re=ap[:, 0, :h].reshape(1, S5_STATE_COLS), aq_im=ap[:, 0, h:].reshape(1, S5_STATE_COLS),
        av_re=ap[:, 1, :h].reshape(1, S5_STATE_COLS), av_im=ap[:, 1, h:].reshape(1, S5_STATE_COLS),
    )


def _oct_rows(u_ref, k8, rt):
    return jnp.concatenate([u_ref[k8, pl.ds(j, rt, stride=S5_Q), :] for j in range(S5_Q)], axis=1).astype(BF16)


def _s5_state_body(u_ref, bre_ref, bim_ref, sre_ref, sim_ref):
    w = S5_OCT * B_STATE
    for k8 in range(N_OCT):
        uo = _oct_rows(u_ref, k8, sre_ref.shape[0])
        sre_ref[:, k8 * w:(k8 + 1) * w] = _dot(uo, bre_ref[k8])
        sim_ref[:, k8 * w:(k8 + 1) * w] = _dot(uo, bim_ref[k8])


def _s5_state(u, bre, bim, rt):
    r = u.shape[1] // S5_Q
    wspec = _full((N_OCT, S5_Q * LANE, S5_OCT * B_STATE))
    ospec = pl.BlockSpec((rt, S5_STATE_COLS), lambda i: (i, 0))
    return pl.pallas_call(
        _s5_state_body,
        grid=(r // rt,),
        in_specs=[_tile_major_spec(rt * S5_Q), wspec, wspec],
        out_specs=[ospec, ospec],
        out_shape=[jax.ShapeDtypeStruct((r, S5_STATE_COLS), F32)] * 2,
        compiler_params=_cparams("parallel"),
        name="s5_state",
    )(u, bre, bim)


def _s5_scan_body(sre_ref, sim_ref, ar_ref, ai_ref, hre_ref, him_ref, fre_ref, fim_ref, cr_scr, ci_scr, *, rb, nblk):
    i = pl.program_id(1)

    @pl.when(i == 0)
    def _():
        cr_scr[...] = jnp.zeros_like(cr_scr)
        ci_scr[...] = jnp.zeros_like(ci_scr)

    ar, ai = ar_ref[...], ai_ref[...]

    def step(r, carry):
        hr, hi = carry
        hre_ref[pl.ds(r, 1), :] = hr
        him_ref[pl.ds(r, 1), :] = hi
        sr, si = sre_ref[pl.ds(r, 1), :], sim_ref[pl.ds(r, 1), :]
        return ar * hr - ai * hi + sr, ai * hr + ar * hi + si

    hr, hi = lax.fori_loop(0, rb, step, (cr_scr[...], ci_scr[...]))
    cr_scr[...] = hr
    ci_scr[...] = hi

    @pl.when(i == nblk - 1)
    def _():
        fre_ref[0] = hr
        fim_ref[0] = hi


def _s5_scan(sre, sim, ar, ai, nb, rb):
    r = sre.shape[0]
    nblk = r // nb // rb
    blk = pl.BlockSpec((rb, S5_STATE_COLS), lambda b, i: (b * nblk + i, 0))
    vec = _full((1, S5_STATE_COLS))
    fin = pl.BlockSpec((1, 1, S5_STATE_COLS), lambda b, i: (b, 0, 0))
    return pl.pallas_call(
        functools.partial(_s5_scan_body, rb=rb, nblk=nblk),
        grid=(nb, nblk),
        in_specs=[blk, blk, vec, vec],
        out_specs=[blk, blk, fin, fin],
        out_shape=[jax.ShapeDtypeStruct((r, S5_STATE_COLS), F32)] * 2
        + [jax.ShapeDtypeStruct((nb, 1, S5_STATE_COLS), F32)] * 2,
        scratch_shapes=[pltpu.VMEM((1, S5_STATE_COLS), F32)] * 2,
        compiler_params=_cparams("parallel", "arbitrary"),
        name="s5_scan",
    )(sre, sim, ar, ai)


def _s5_sample_body(u_ref, toep_ref, cre_ref, cim_ref, bre_ref, bim_ref, hre_ref, him_ref, ar_ref, ai_ref,
                    y_ref, fre_ref, fim_ref, *, valid):
    w = S5_OCT * B_STATE
    ns = LANE
    for k8 in range(N_OCT):
        blocks = [u_ref[k8, t * ns:(t + 1) * ns, :] for t in range(valid)]
        uo = jnp.concatenate(blocks, axis=1).astype(BF16)
        uo_t = jnp.concatenate([b.T for b in blocks], axis=0).astype(BF16)
        hr, hi = hre_ref[0, k8 * w:(k8 + 1) * w, :], him_ref[0, k8 * w:(k8 + 1) * w, :]
        yo = (_dot(uo, toep_ref[k8]) + _dot_nt(hr.T.astype(BF16), cre_ref[k8])
              + _dot_nt(hi.T.astype(BF16), cim_ref[k8]))
        for t in range(valid):
            y_ref[k8, t * ns:(t + 1) * ns, :] = yo[:, t * LANE:(t + 1) * LANE]
        ar, ai = ar_ref[k8 * w:(k8 + 1) * w, :], ai_ref[k8 * w:(k8 + 1) * w, :]
        fre_ref[k8 * w:(k8 + 1) * w, :] = ar * hr - ai * hi + _dot_tn(bre_ref[k8], uo_t)
        fim_ref[k8 * w:(k8 + 1) * w, :] = ai * hr + ar * hi + _dot_tn(bim_ref[k8], uo_t)


def _s5_sample(u, s5w, hre_all, him_all, ar, ai, layer, valid):
    rows = valid * LANE
    vq = valid * LANE
    sub = lambda c: pl.BlockSpec((N_OCT, vq, c), lambda i: (0, 0, 0))
    hblk = pl.BlockSpec((1, S5_STATE_COLS, LANE), lambda i: (layer, 0, 0))
    st = _full((S5_STATE_COLS, LANE))
    return pl.pallas_call(
        functools.partial(_s5_sample_body, valid=valid),
        grid=(1,),
        in_specs=[_tile_major_spec(rows), sub(vq), sub(S5_OCT * B_STATE), sub(S5_OCT * B_STATE),
                  sub(S5_OCT * B_STATE), sub(S5_OCT * B_STATE), hblk, hblk, st, st],
        out_specs=[_tile_major_spec(rows), st, st],
        out_shape=[jax.ShapeDtypeStruct((N_OCT, rows, LANE), F32)] + [jax.ShapeDtypeStruct((S5_STATE_COLS, LANE), F32)] * 2,
        compiler_params=_cparams("arbitrary"),
        name="s5_sample",
    )(u, s5w["toep"], s5w["cst_re"], s5w["cst_im"], s5w["bsts_re"], s5w["bsts_im"], hre_all, him_all, ar, ai)


def _s5_out_body(u_ref, hre_ref, him_ref, toep_ref, cre_ref, cim_ref, y_ref):
    w = S5_OCT * B_STATE
    rt = hre_ref.shape[0]
    for k8 in range(N_OCT):
        uo = _oct_rows(u_ref, k8, rt)
        yo = (_dot(uo, toep_ref[k8])
              + _dot_nt(hre_ref[:, k8 * w:(k8 + 1) * w].astype(BF16), cre_ref[k8])
              + _dot_nt(him_ref[:, k8 * w:(k8 + 1) * w].astype(BF16), cim_ref[k8]))
        for j in range(S5_Q):
            y_ref[k8, pl.ds(j, rt, stride=S5_Q), :] = yo[:, j * LANE:(j + 1) * LANE]


def _s5_out(u, hre, him, toep, cre, cim, rt):
    r = u.shape[1] // S5_Q
    hblk = pl.BlockSpec((rt, S5_STATE_COLS), lambda i: (i, 0))
    return pl.pallas_call(
        _s5_out_body,
        grid=(r // rt,),
        in_specs=[_tile_major_spec(rt * S5_Q), hblk, hblk, _full((N_OCT, S5_Q * LANE, S5_Q * LANE)),
                  _full((N_OCT, S5_Q * LANE, S5_OCT * B_STATE)), _full((N_OCT, S5_Q * LANE, S5_OCT * B_STATE))],
        out_specs=_tile_major_spec(rt * S5_Q),
        out_shape=jax.ShapeDtypeStruct(u.shape, F32),
        compiler_params=_cparams("parallel"),
        name="s5_out",
    )(u, hre, him, toep, cre, cim)


def _band_body(bucket_ref, table_ref, o_ref):
    bucket = bucket_ref[...]
    col = lax.broadcasted_iota(jnp.int32, (WINDOW, 2 * WINDOW), 1)
    for hq in range(C_HEADS):
        base = jnp.full((1, 2 * WINDOW), NEG, F32)
        for b in range(REL_BUCKETS):
            base = jnp.where(bucket == b, table_ref[b, hq], base)
        tile = pltpu.roll(jnp.broadcast_to(base, (WINDOW, 2 * WINDOW)), 0, 1, stride=1, stride_axis=0)
        o_ref[1, hq] = tile.T
        o_ref[0, hq] = jnp.where(col >= WINDOW, tile, NEG).T


def _bias_band(table):
    lane = np.arange(2 * WINDOW)
    n = WINDOW - lane
    exact = REL_BUCKETS // 2
    nf = np.maximum(n, 1).astype(np.float32)
    large = exact + (np.log(nf / np.float32(exact)) / np.float32(math.log(REL_MAX_DIST / exact))
                     * np.float32(REL_BUCKETS - exact)).astype(np.int32)
    bucket = np.where(n < exact, n, np.minimum(large, REL_BUCKETS - 1))
    bucket = np.where((n >= 0) & (n < WINDOW), bucket, -1).astype(np.int32)
    return pl.pallas_call(
        _band_body,
        in_specs=[pl.BlockSpec(memory_space=pltpu.VMEM), pl.BlockSpec(memory_space=pltpu.SMEM)],
        out_specs=pl.BlockSpec(memory_space=pltpu.VMEM),
        out_shape=jax.ShapeDtypeStruct((2, C_HEADS, 2 * WINDOW, WINDOW), F32),
        name="rel_bias_band",
    )(jnp.asarray(bucket).reshape(1, 2 * WINDOW), table)


def _attn_prompt_body(q_ref, kc_ref, kp_ref, vc_ref, vp_ref, b0_ref, br_ref, sink_ref, o_ref, *, nblk):
    k_all = jnp.concatenate([kp_ref[...], kc_ref[...]], axis=0) * (C_HEAD_DIM ** -0.5)
    v_all = jnp.concatenate([vp_ref[...], vc_ref[...]], axis=0)
    swapped = (pltpu.roll(k_all, C_HEAD_DIM, 1), pltpu.roll(v_all, C_HEAD_DIM, 1))
    low = lax.broadcasted_iota(jnp.int32, k_all.shape, 1) < C_HEAD_DIM
    for hk in range(C_KV_HEADS):
        k_lo = k_all if hk == 0 else swapped[0]
        k_hi = swapped[0] if hk == 0 else k_all
        kz = (jnp.where(low, k_lo, 0.0).astype(BF16), jnp.where(low, 0.0, k_hi).astype(BF16))
        vz = (v_all if hk == 0 else swapped[1])[:, :C_HEAD_DIM].astype(BF16)
        for g in range(C_GQA):
            hq = hk * C_GQA + g
            sk = sink_ref[hq]
            for i in range(nblk):
                keys = slice(i * WINDOW, (i + 2) * WINDOW)
                qt = q_ref[i * WINDOW:(i + 1) * WINDOW, (hq // 2) * LANE:(hq // 2 + 1) * LANE]
                bias = b0_ref[0, hq] if i == 0 else br_ref[0, hq]
                s = _dot_nt(kz[hq % 2][keys], qt) + bias
                m = jnp.maximum(jnp.max(s, axis=0, keepdims=True), sk)
                p = jnp.exp(s - m)
                den = jnp.sum(p, axis=0, keepdims=True) + jnp.exp(sk - m)
                o = _dot_tn(vz[keys], p.astype(BF16)) / den
                o_ref[hq * C_HEAD_DIM:(hq + 1) * C_HEAD_DIM, i * WINDOW:(i + 1) * WINDOW] = o.astype(BF16)


def _attn_prompt(q, k, v, band, sinks, nb, nblk_seq, nblk):
    steps = nblk_seq // nblk
    cur = lambda b, n: (b * steps + n, 0)
    prev = lambda b, n: (b * nblk_seq + jnp.maximum(n * nblk - 1, 0), 0)
    bspec = lambda m: pl.BlockSpec((1, C_HEADS, 2 * WINDOW, WINDOW), m)
    return pl.pallas_call(
        functools.partial(_attn_prompt_body, nblk=nblk),
        grid=(nb, steps),
        in_specs=[pl.BlockSpec((nblk * WINDOW, C_WIDTH), cur),
                  pl.BlockSpec((nblk * WINDOW, C_KV_WIDTH), cur), pl.BlockSpec((WINDOW, C_KV_WIDTH), prev),
                  pl.BlockSpec((nblk * WINDOW, C_KV_WIDTH), cur), pl.BlockSpec((WINDOW, C_KV_WIDTH), prev),
                  bspec(lambda b, n: (jnp.minimum(n, 1), 0, 0, 0)), bspec(lambda b, n: (1, 0, 0, 0)),
                  pl.BlockSpec(memory_space=pltpu.SMEM)],
        out_specs=pl.BlockSpec((C_WIDTH, nblk * WINDOW), lambda b, n: (0, b * steps + n)),
        out_shape=jax.ShapeDtypeStruct((C_WIDTH, nb * nblk_seq * WINDOW), BF16),
        compiler_params=_cparams("parallel", "arbitrary"),
        name="attn_prompt",
    )(q, k, k, v, v, band, band, sinks)


def _attn_sample_body(*refs, valid, lb, nseq):
    q_ref, kn_ref, vn_ref, kc_ref, vc_ref, b1_ref, b2_ref, sink_ref = refs[:8]
    o_ref, ko_ref, vo_ref, s1_scr, s2_scr, inv_scr, o_scr = refs[-7:]
    d = C_HEAD_DIM
    rs = 2 * valid
    lane_c = lax.broadcasted_iota(jnp.int32, (C_KV_WIDTH, lb), 1)
    lane_n = lax.broadcasted_iota(jnp.int32, (SUBLANE, C_KV_WIDTH), 1)
    pad = jnp.zeros((SUBLANE - valid, C_KV_WIDTH), F32)
    zero_half = jnp.zeros((d, lb), BF16)
    combos = [(hk, par) for hk in range(C_KV_HEADS) for par in range(2)]

    def new_rows(ref, s):
        return jnp.concatenate([ref[:, s, :], pad], axis=0)

    for s in range(nseq):
        qs = q_ref[:, s, :].astype(F32)
        kn, vn = new_rows(kn_ref, s), new_rows(vn_ref, s)
        kt, vt = kc_ref[0, s], vc_ref[0, s]
        kn_t, vn_t = kn.T, vn.T
        new_k, new_v = pltpu.roll(kt, lb - valid, 1), pltpu.roll(vt, lb - valid, 1)
        for t in range(valid):
            at = lane_c == lb - valid + t
            new_k = jnp.where(at, jnp.broadcast_to(kn_t[:, t:t + 1], (C_KV_WIDTH, lb)), new_k)
            new_v = jnp.where(at, jnp.broadcast_to(vn_t[:, t:t + 1], (C_KV_WIDTH, lb)), new_v)
        ko_ref[0, s] = new_k
        vo_ref[0, s] = new_v
        kn_sw = pltpu.roll(kn, d, 1)
        for c, (hk, par) in enumerate(combos):
            kth = kt[hk * d:(hk + 1) * d].astype(BF16)
            kz = jnp.concatenate([kth, zero_half] if par == 0 else [zero_half, kth], axis=0)
            src = kn if hk == par else kn_sw
            knz = jnp.where((lane_n < d) if par == 0 else (lane_n >= d), src, 0.0).astype(BF16)
            qq = jnp.concatenate([qs[:, (hk * 2 + j) * LANE:(hk * 2 + j + 1) * LANE] for j in range(2)],
                                 axis=0).astype(BF16)
            s1_scr[c, s * rs:(s + 1) * rs, :] = _dot(qq, kz)
            s2_scr[c, s * rs:(s + 1) * rs, :] = _dot_nt(qq, knz)

    for c, (hk, par) in enumerate(combos):
        s1 = s1_scr[c] * (d ** -0.5) + b1_ref[hk, par]
        s2 = s2_scr[c] * (d ** -0.5) + b2_ref[hk, par]
        sk = sink_ref[hk, par]
        m = jnp.maximum(jnp.maximum(jnp.max(s1, axis=-1, keepdims=True), jnp.max(s2, axis=-1, keepdims=True)), sk)
        p1, p2 = jnp.exp(s1 - m), jnp.exp(s2 - m)
        den = jnp.sum(p1, axis=-1, keepdims=True) + jnp.sum(p2, axis=-1, keepdims=True) + jnp.exp(sk - m)
        s1_scr[c] = p1
        s2_scr[c] = p2
        inv_scr[c] = jnp.broadcast_to(1.0 / den, (nseq * rs, SUBLANE))

    for s in range(nseq):
        rows = slice(s * rs, (s + 1) * rs)
        vt = vc_ref[0, s]
        vn = new_rows(vn_ref, s)
        for c, (hk, par) in enumerate(combos):
            vth = vt[hk * d:(hk + 1) * d].astype(BF16)
            vnh = vn[:, hk * d:(hk + 1) * d].astype(BF16)
            o = _dot_nt(s1_scr[c, rows, :].astype(BF16), vth) + _dot(s2_scr[c, rows, :].astype(BF16), vnh)
            o_scr[hk, rows, par * d:(par + 1) * d] = o * inv_scr[c, rows, 0:1]

    for hk in range(C_KV_HEADS):
        for j in range(2):
            for t in range(valid):
                tile = hk * 2 + j
                o_ref[t, :, tile * LANE:(tile + 1) * LANE] = o_scr[hk, pl.ds(j * valid + t, nseq, stride=rs), :]


def _attn_sample(q3, kn3, vn3, kc_all, vc_all, b1, b2, sink_col, layer, k_prev, v_prev):
    valid, nsb, _ = q3.shape
    lb = kc_all.shape[-1]
    nseq = 2 * SUBLANE
    assert 2 * valid == SUBLANE
    rows = lambda w: pl.BlockSpec((valid, nseq, w), lambda i: (0, i, 0))
    cblk = pl.BlockSpec((1, nseq, C_KV_WIDTH, lb), lambda i: (layer, i, 0, 0))
    b1, b2, sink_col = (jnp.tile(a, (1, 1, nseq, 1)) for a in (b1, b2, sink_col))
    in_specs = [rows(C_WIDTH), rows(C_KV_WIDTH), rows(C_KV_WIDTH), cblk, cblk,
                _full(b1.shape), _full(b2.shape), _full(sink_col.shape)] + [pl.BlockSpec(memory_space=pl.ANY)] * 2
    args = [q3, kn3, vn3, kc_all, vc_all, b1, b2, sink_col, k_prev, v_prev]
    aliases = {len(args) - 2: 1, len(args) - 1: 2}
    return pl.pallas_call(
        functools.partial(_attn_sample_body, valid=valid, lb=lb, nseq=nseq),
        grid=(nsb // nseq,),
        in_specs=in_specs,
        out_specs=[rows(C_WIDTH), cblk, cblk],
        out_shape=[jax.ShapeDtypeStruct((valid, nsb, C_WIDTH), F32), jax.ShapeDtypeStruct(kc_all.shape, F32),
                   jax.ShapeDtypeStruct(vc_all.shape, F32)],
        scratch_shapes=[pltpu.VMEM((2 * C_KV_HEADS, nseq * SUBLANE, lb), F32),
                        pltpu.VMEM((2 * C_KV_HEADS, nseq * SUBLANE, SUBLANE), F32),
                        pltpu.VMEM((2 * C_KV_HEADS, nseq * SUBLANE, SUBLANE), F32),
                        pltpu.VMEM((C_KV_HEADS, nseq * SUBLANE, LANE), F32)],
        input_output_aliases=aliases,
        compiler_params=_cparams("arbitrary"),
        name="attn_sample",
    )(*args)


def _merge_body(x_ref, ya_ref, yb_ref, u_ref, yc_ref, nw_ref, wg_ref, d5_ref, wglu_ref,
                wa_ref, wb_ref, wc_ref, wo_ref, o_ref, *, ya_features_major, yc_features_major):
    x = x_ref[...]
    h = _rms(x, nw_ref[...]).astype(BF16)
    g = jax.nn.gelu(_lanes(yb_ref) + d5_ref[...] * _lanes(u_ref))
    gv = _dot(g.astype(BF16), wglu_ref[...])
    yb = (gv[:, :B_WIDTH] * jax.nn.sigmoid(gv[:, B_WIDTH:])).astype(BF16)

    def gate(i):
        return jax.nn.sigmoid(_dot(h, wg_ref[:, i * D_MODEL:(i + 1) * D_MODEL]))

    def branch(y_ref, w_ref, features_major):
        y = y_ref[...].astype(BF16)
        return _dot_tn(y, w_ref[...]) if features_major else _dot(y, w_ref[...])

    mixed = gate(0) * branch(ya_ref, wa_ref, ya_features_major)
    mixed = mixed + gate(1) * _dot(yb, wb_ref[...])
    mixed = mixed + gate(2) * branch(yc_ref, wc_ref, yc_features_major)
    o_ref[...] = x + _dot(mixed.astype(BF16), wo_ref[...])


def _merge(x, ya, yb, u, yc, lw, tm, ya_features_major, yc_features_major):
    t = x.shape[0]
    rows = lambda w: pl.BlockSpec((tm, w), lambda i: (i, 0))
    branch_spec = lambda fm: pl.BlockSpec((A_WIDTH, tm), lambda i: (0, i)) if fm else rows(A_WIDTH)
    return pl.pallas_call(
        functools.partial(_merge_body, ya_features_major=ya_features_major, yc_features_major=yc_features_major),
        grid=(t // tm,),
        in_specs=[rows(D_MODEL), branch_spec(ya_features_major), _tile_major_spec(tm), _tile_major_spec(tm),
                  branch_spec(yc_features_major),
                  _full((1, D_MODEL)), _full((D_MODEL, 3 * D_MODEL)), _full((1, B_WIDTH)),
                  _full((B_WIDTH, 2 * B_WIDTH)), _full((A_WIDTH, D_MODEL)), _full((B_WIDTH, D_MODEL)),
                  _full((C_WIDTH, D_MODEL)), _full((D_MODEL, D_MODEL))],
        out_specs=rows(D_MODEL),
        out_shape=jax.ShapeDtypeStruct((t, D_MODEL), F32),
        compiler_params=_cparams("parallel"),
        name="merge",
    )(x, ya, yb, u, yc, lw["norm1_w"], lw["w_gate"], lw["s5_d"], lw["w_glu"],
      lw["w_br_a"], lw["w_br_b"], lw["w_br_c"], lw["w_out"])


def _ffn_body(x_ref, nw_ref, wup_ref, wdn_ref, fnw_ref, o_ref, act_scr, *, final_norm):
    x = x_ref[...]
    h = _rms(x, nw_ref[...]).astype(BF16)
    for c in range(D_FF // FF_CHUNK):
        lo, hi = c * FF_CHUNK, (c + 1) * FF_CHUNK
        a = _dot(h, wup_ref[:, lo:hi])
        b = _dot(h, wup_ref[:, D_FF + lo:D_FF + hi])
        act_scr[:, lo:hi] = (jax.nn.silu(a) * b).astype(BF16)
    y = x + _dot(act_scr[...], wdn_ref[...])
    if final_norm:
        y = _rms(y, fnw_ref[...])
    o_ref[...] = y


def _ffn(x, lw, fnw, tm, final_norm):
    t = x.shape[0]
    rows = pl.BlockSpec((tm, D_MODEL), lambda i: (i, 0))
    return pl.pallas_call(
        functools.partial(_ffn_body, final_norm=final_norm),
        grid=(t // tm,),
        in_specs=[rows, _full((1, D_MODEL)), _full((D_MODEL, 2 * D_FF)), _full((D_FF, D_MODEL)), _full((1, D_MODEL))],
        out_specs=rows,
        out_shape=jax.ShapeDtypeStruct((t, D_MODEL), F32),
        scratch_shapes=[pltpu.VMEM((tm, D_FF), BF16)],
        compiler_params=_cparams("parallel"),
        name="ffn",
    )(x, lw["norm2_w"], lw["w_ffn_up"], lw["w_ffn_down"], fnw)


def _row_tile(t, pref):
    tm = pref
    while t % tm:
        tm //= 2
    return tm


def kernel(x_prompt, x_sample, state_ssd, state_conv, state_s5_re, state_s5_im, cache_k, cache_v, norm1_w, w_in, conv_w, conv_b, ssd_a_log, ssd_dt_bias, ssd_d, ssd_norm_w, s5_lam_re, s5_lam_im, s5_log_dt, s5_b_re, s5_b_im, s5_c_re, s5_c_im, s5_d, s5_w_glu, attn_sinks, w_br_a, w_br_b, w_br_c, w_out, norm2_w, w_ffn_up, w_ffn_down, rel_bias, final_norm_w):
    nb, seq, _ = x_prompt.shape
    nsb, valid, _ = x_sample.shape
    depth = w_in.shape[0]
    lb = cache_k.shape[2]
    assert seq % A_CHUNK == 0 and seq % WINDOW == 0 and seq % (S5_Q * SUBLANE) == 0
    assert nsb == LANE and A_CONV - 1 <= valid <= S5_Q and lb == WINDOW
    nchunk = seq // A_CHUNK
    tp = nb * seq
    ts = valid * nsb

    band = _bias_band(rel_bias)

    def pair_rows(a):
        a = a.reshape(C_KV_HEADS, 2, 2, valid, a.shape[-1])
        return jnp.swapaxes(a, 1, 2).reshape(C_KV_HEADS, 2, 2 * valid, a.shape[-1])

    band_s = pair_rows(jnp.swapaxes(band[1, :, :, :valid], 1, 2))
    bias_s1 = band_s[..., :lb]
    bias_s2 = band_s[..., lb:lb + SUBLANE]
    fnw = final_norm_w.reshape(1, D_MODEL)

    xp = x_prompt.reshape(tp, D_MODEL)
    xs = jnp.swapaxes(x_sample, 0, 1).reshape(ts, D_MODEL)
    conv_all = jnp.swapaxes(state_conv, 1, 2)
    ssd_all = jnp.transpose(state_ssd, (0, 2, 3, 4, 1))
    s5re_all = jnp.transpose(state_s5_re, (0, 2, 3, 1)).reshape(depth, S5_STATE_COLS, nsb)
    s5im_all = jnp.transpose(state_s5_im, (0, 2, 3, 1)).reshape(depth, S5_STATE_COLS, nsb)
    kc_all = jnp.transpose(cache_k, (0, 1, 3, 4, 2)).reshape(depth, nsb, C_KV_WIDTH, lb)
    vc_all = jnp.transpose(cache_v, (0, 1, 3, 4, 2)).reshape(depth, nsb, C_KV_WIDTH, lb)
    tm_p = _row_tile(tp, 512)
    tm_s = _row_tile(ts, 512)
    rt_p = _row_tile(tp // S5_Q, 256)
    rb_p = _row_tile(seq // S5_Q, 128)
    attn_blk = _row_tile(seq // WINDOW, 8)

    new_p, s5_s = [], []
    conv_new, ssd_new = jnp.zeros(conv_all.shape, F32), jnp.zeros(ssd_all.shape, F32)
    k_new, v_new = jnp.zeros(kc_all.shape, F32), jnp.zeros(vc_all.shape, F32)
    for i in range(depth):
        wi = w_in[i]
        off = [0, 512, 1280, 1288, 1800, 2312, 2440, 2568, 5640]
        w_pack = jnp.concatenate(
            [wi[:, off[0]:off[1]], wi[:, off[1]:off[2]], wi[:, off[3]:off[4]], wi[:, off[4]:off[5]],
             wi[:, off[5]:off[6]], wi[:, off[6]:off[7]],
             jnp.pad(wi[:, off[2]:off[3]], ((0, 0), (0, LANE - A_HEADS)))], axis=1).astype(BF16)
        sink_rows = pair_rows(jnp.broadcast_to(attn_sinks[i].reshape(C_HEADS, 1, 1), (C_HEADS, valid, 1)))
        lw = dict(
            norm1_w=norm1_w[i].reshape(1, D_MODEL), w_gate=wi[:, off[7]:off[8]].astype(BF16),
            conv_w=conv_w[i], conv_b=conv_b[i].reshape(1, A_CONV_DIM),
            a_log=jnp.pad(ssd_a_log[i], (0, LANE - A_HEADS)).reshape(1, LANE),
            dt_bias=jnp.pad(ssd_dt_bias[i], (0, LANE - A_HEADS)).reshape(1, LANE),
            d_exp=jnp.repeat(ssd_d[i], A_HEAD_DIM).reshape(1, A_WIDTH),
            ssd_norm_w=ssd_norm_w[i].reshape(1, A_WIDTH),
            ssd_norm_wb=jnp.broadcast_to(ssd_norm_w[i].reshape(A_WIDTH, 1), (A_WIDTH, LANE)),
            s5_d=s5_d[i].reshape(1, B_WIDTH), w_glu=s5_w_glu[i].astype(BF16),
            w_br_a=w_br_a[i].astype(BF16), w_br_b=w_br_b[i].astype(BF16), w_br_c=w_br_c[i].astype(BF16),
            w_out=w_out[i].astype(BF16), norm2_w=norm2_w[i].reshape(1, D_MODEL),
            w_ffn_up=w_ffn_up[i].astype(BF16), w_ffn_down=w_ffn_down[i].astype(BF16),
        )
        s5w = _s5_weights(s5_lam_re[i], s5_lam_im[i], s5_log_dt[i], s5_b_re[i], s5_b_im[i],
                          s5_c_re[i], s5_c_im[i], valid)
        last = i == depth - 1

        u, q, k, v, ya, ssd_h, conv_tail = _inproj_ssd(xp, w_pack, lw, nb, tm_p)
        sre, sim = _s5_state(u, s5w["bst_re"], s5w["bst_im"], rt_p)
        hre, him, fre, fim = _s5_scan(sre, sim, s5w["aq_re"], s5w["aq_im"], nb, rb_p)
        yb = _s5_out(u, hre, him, s5w["toep"], s5w["cst_re"], s5w["cst_im"], rt_p)
        yc = _attn_prompt(q, k, v, band, attn_sinks[i], nb, seq // WINDOW, attn_blk)
        x1 = _merge(xp, ya, yb, u, yc, lw, tm_p, False, True)
        xp = _ffn(x1, lw, fnw, tm_p, last)
        new_p.append((
            ssd_h.reshape(nb, A_HEADS, A_HEAD_DIM, A_STATE),
            conv_tail[:, SUBLANE - (A_CONV - 1):],
            fre.reshape(nb, B_GROUPS, B_STATE), fim.reshape(nb, B_GROUPS, B_STATE),
            k.reshape(nb, seq, C_KV_HEADS, C_HEAD_DIM)[:, seq - WINDOW:],
            v.reshape(nb, seq, C_KV_HEADS, C_HEAD_DIM)[:, seq - WINDOW:]))

        z, xbc, u, q, k, v, dt = _inproj(xs, lw["norm1_w"], w_pack, tm_s)
        ya, conv_new, ssd_new = _ssd_sample(xbc, z, dt, conv_all, ssd_all, lw, i, valid, conv_new, ssd_new)
        av_re = jnp.broadcast_to(s5w["av_re"].reshape(S5_STATE_COLS, 1), (S5_STATE_COLS, nsb))
        av_im = jnp.broadcast_to(s5w["av_im"].reshape(S5_STATE_COLS, 1), (S5_STATE_COLS, nsb))
        yb, fre, fim = _s5_sample(u, s5w, s5re_all, s5im_all, av_re, av_im, i, valid)
        yc, k_new, v_new = _attn_sample(
            q.reshape(valid, nsb, C_WIDTH), k.reshape(valid, nsb, C_KV_WIDTH), v.reshape(valid, nsb, C_KV_WIDTH),
            kc_all, vc_all, bias_s1, bias_s2, sink_rows, i, k_new, v_new)
        x1 = _merge(xs, ya, yb, u, yc.reshape(ts, C_WIDTH), lw, tm_s, True, False)
        xs = _ffn(x1, lw, fnw, tm_s, last)
        s5_s.append((fre, fim))

    def stack(states, j):
        return jnp.stack([s[j] for s in states], axis=0)

    def s5_state(j):
        return jnp.transpose(stack(s5_s, j).reshape(depth, B_GROUPS, B_STATE, nsb), (0, 3, 1, 2))

    def cache(a):
        return jnp.transpose(a.reshape(depth, nsb, C_KV_HEADS, C_HEAD_DIM, lb), (0, 1, 4, 2, 3))

    y_prompt = xp.reshape(nb, seq, D_MODEL)
    y_sample = jnp.swapaxes(xs.reshape(valid, nsb, D_MODEL), 0, 1)
    return (y_prompt, y_sample,
            stack(new_p, 0), stack(new_p, 1), stack(new_p, 2), stack(new_p, 3), stack(new_p, 4), stack(new_p, 5),
            jnp.transpose(ssd_new, (0, 4, 1, 2, 3)), jnp.swapaxes(conv_new, 1, 2), s5_state(0), s5_state(1),
            cache(k_new), cache(v_new))
```

```python
import functools
import math

import numpy as np
import jax
import jax.numpy as jnp
from jax import lax
from jax.experimental import pallas as pl
from jax.experimental.pallas import tpu as pltpu

F32 = jnp.float32
BF16 = jnp.bfloat16
HIGHEST = lax.Precision.HIGHEST

D_MODEL = 1024
A_HEAD_DIM = 64
A_WIDTH = 512
A_HEADS = 8
A_GROUPS = 2
A_STATE = 64
A_CONV = 4
A_CONV_DIM = 768
A_CHUNK = 128
B_CH = 16
B_WIDTH = 512
B_GROUPS = 32
B_STATE = 64
C_HEAD_DIM = 64
C_WIDTH = 512
C_HEADS = 8
C_KV_HEADS = 2
C_GQA = 4
C_KV_WIDTH = 128
WINDOW = 128
REL_BUCKETS = 32
REL_MAX_DIST = 128
D_FF = 2816
EPS = 1e-6

P_Z = 0
P_XBC = P_Z + A_WIDTH
P_U = P_XBC + A_CONV_DIM
P_Q = P_U + B_WIDTH
P_K = P_Q + C_WIDTH
P_V = P_K + C_KV_WIDTH
P_DT = P_V + C_KV_WIDTH
P_COLS = P_DT + 128

LANE = 128
SUBLANE = 8
S5_Q = 8
S5_OCT = 8
N_OCT = B_GROUPS // S5_OCT
S5_STATE_COLS = B_GROUPS * B_STATE
NEG = -1e30
VMEM_LIMIT = 56 * 1024 * 1024
FF_CHUNK = 256


def _cparams(*sem):
    return pltpu.CompilerParams(dimension_semantics=sem, vmem_limit_bytes=VMEM_LIMIT)


def _rms(x, w):
    return x * lax.rsqrt(jnp.mean(x * x, axis=-1, keepdims=True) + EPS) * w


def _dot(a, b):
    return jnp.dot(a, b, preferred_element_type=F32)


def _dot_nt(a, b, precision=None):
    return lax.dot_general(a, b, (((1,), (1,)), ((), ())), preferred_element_type=F32, precision=precision)


def _dot_tn(a, b):
    return lax.dot_general(a, b, (((0,), (0,)), ((), ())), preferred_element_type=F32)


def _full(shape):
    n = len(shape)
    return pl.BlockSpec(shape, lambda *_: (0,) * n)


def _tile_major_spec(rows):
    return pl.BlockSpec((N_OCT, rows, LANE), lambda i: (0, i, 0))


def _lanes(ref):
    return jnp.concatenate([ref[k8] for k8 in range(N_OCT)], axis=1)


def _inproj_body(x_ref, nw_ref, w_ref, z_ref, xbc_ref, u_ref, q_ref, k_ref, v_ref, dt_ref):
    h = _rms(x_ref[...], nw_ref[...]).astype(BF16)

    def proj(lo, hi):
        return _dot(h, w_ref[:, lo:hi])

    z_ref[...] = proj(P_Z, P_XBC).astype(BF16)
    xbc_ref[...] = proj(P_XBC, P_U)
    u = proj(P_U, P_Q)
    for k8 in range(N_OCT):
        u_ref[k8] = u[:, k8 * LANE:(k8 + 1) * LANE]
    q_ref[...] = proj(P_Q, P_K).astype(BF16)
    kvdt = proj(P_K, P_COLS)
    k_ref[...] = kvdt[:, :C_KV_WIDTH]
    v_ref[...] = kvdt[:, C_KV_WIDTH:2 * C_KV_WIDTH]
    dt_ref[...] = kvdt[:, 2 * C_KV_WIDTH:]


def _inproj(x, nw, w, tm):
    t = x.shape[0]
    widths = [(A_WIDTH, BF16), (A_CONV_DIM, F32), None, (C_WIDTH, BF16),
              (C_KV_WIDTH, F32), (C_KV_WIDTH, F32), (LANE, F32)]
    spec = lambda w: _tile_major_spec(tm) if w is None else pl.BlockSpec((tm, w[0]), lambda i: (i, 0))
    shape = lambda w: (jax.ShapeDtypeStruct((N_OCT, t, LANE), F32) if w is None
                       else jax.ShapeDtypeStruct((t, w[0]), w[1]))
    return pl.pallas_call(
        _inproj_body,
        grid=(t // tm,),
        in_specs=[pl.BlockSpec((tm, D_MODEL), lambda i: (i, 0)), _full((1, D_MODEL)), _full((D_MODEL, P_COLS))],
        out_specs=[spec(w) for w in widths],
        out_shape=[shape(w) for w in widths],
        compiler_params=_cparams("parallel"),
        name="inproj",
    )(x, nw, w)


def _ssd_chunk(xbc, z, dt_raw, cw_ref, cb_ref, alog_ref, dtb_ref, dexp_ref, nw_ref, h_scr, tail_scr):
    q = xbc.shape[0]
    tail_scr[SUBLANE:SUBLANE + q] = xbc
    acc = cb_ref[...] + cw_ref[A_CONV - 1:A_CONV, :] * xbc
    for j in range(1, A_CONV):
        acc = acc + cw_ref[A_CONV - 1 - j:A_CONV - j, :] * tail_scr[SUBLANE - j:SUBLANE - j + q]
    tail_scr[0:SUBLANE] = xbc[q - SUBLANE:q]
    xc = jax.nn.silu(acc)
    x = xc[:, :A_WIDTH]
    bm = xc[:, A_WIDTH:A_WIDTH + A_GROUPS * A_STATE].astype(BF16)
    cm = xc[:, A_WIDTH + A_GROUPS * A_STATE:].astype(BF16)

    dt = jax.nn.softplus(dt_raw + dtb_ref[...])
    dta = dt * (-jnp.exp(alog_ref[...]))
    row = lax.broadcasted_iota(jnp.int32, (q, q), 0)
    col = lax.broadcasted_iota(jnp.int32, (q, q), 1)
    tri = row >= col
    a_cum = jnp.dot(tri.astype(F32), dta, preferred_element_type=F32, precision=HIGHEST)

    def expand(v):
        return jnp.concatenate([jnp.broadcast_to(v[:, k:k + 1], (q, A_HEAD_DIM)) for k in range(A_HEADS)], axis=1)

    a_cum_e = expand(a_cum)
    xs = x * expand(dt)
    xs_b = xs.astype(BF16)
    a_cum_t = a_cum.T

    h = h_scr[...]
    h_b = h.astype(BF16)
    hpg = A_HEADS // A_GROUPS
    gw = hpg * A_HEAD_DIM
    y_diag, y_off = [], []
    for g in range(A_GROUPS):
        cg = cm[:, g * A_STATE:(g + 1) * A_STATE]
        bg = bm[:, g * A_STATE:(g + 1) * A_STATE]
        cb = _dot_nt(cg, bg)
        for kk in range(hpg):
            k = g * hpg + kk
            seg = a_cum[:, k:k + 1] - a_cum_t[k:k + 1, :]
            decay = jnp.exp(jnp.where(tri, seg, -jnp.inf))
            y_diag.append(_dot((cb * decay).astype(BF16), xs_b[:, k * A_HEAD_DIM:(k + 1) * A_HEAD_DIM]))
        y_off.append(_dot_nt(cg, h_b[g * gw:(g + 1) * gw]))
    y = (jnp.concatenate(y_diag, axis=1) + jnp.concatenate(y_off, axis=1) * jnp.exp(a_cum_e)
         + dexp_ref[...] * x)
    y = y * jax.nn.silu(z)

    last = a_cum_e[q - 1:q, :]
    xs_dec = (xs * jnp.exp(last - a_cum_e)).astype(BF16)
    dec_col = jnp.exp(a_cum_e.T[:, q - 1:q])
    upd = [_dot_tn(xs_dec[:, g * gw:(g + 1) * gw], bm[:, g * A_STATE:(g + 1) * A_STATE]) for g in range(A_GROUPS)]
    h_scr[...] = dec_col * h + jnp.concatenate(upd, axis=0)
    return _rms(y, nw_ref[...]).astype(BF16)


def _inproj_ssd_body(x_ref, nw_ref, w_ref, cw_ref, cb_ref, alog_ref, dtb_ref, dexp_ref, snw_ref,
                     u_ref, q_ref, k_ref, v_ref, ya_ref, hout_ref, tail_ref, h_scr, tail_scr, *, nsteps, chunk):
    c = pl.program_id(1)

    @pl.when(c == 0)
    def _():
        h_scr[...] = jnp.zeros_like(h_scr)
        tail_scr[0:SUBLANE] = jnp.zeros((SUBLANE, A_CONV_DIM), F32)

    h = _rms(x_ref[...], nw_ref[...]).astype(BF16)

    def proj(lo, hi):
        return _dot(h, w_ref[:, lo:hi])

    z = proj(P_Z, P_XBC)
    xbc = proj(P_XBC, P_U)
    kvdt = proj(P_K, P_COLS)
    k_ref[...] = kvdt[:, :C_KV_WIDTH]
    v_ref[...] = kvdt[:, C_KV_WIDTH:2 * C_KV_WIDTH]
    dt = kvdt[:, 2 * C_KV_WIDTH:]
    for i in range(x_ref.shape[0] // chunk):
        rows = slice(i * chunk, (i + 1) * chunk)
        ya_ref[rows, :] = _ssd_chunk(xbc[rows], z[rows], dt[rows], cw_ref, cb_ref, alog_ref, dtb_ref, dexp_ref,
                                     snw_ref, h_scr, tail_scr)
    u = proj(P_U, P_Q)
    for k8 in range(N_OCT):
        u_ref[k8] = u[:, k8 * LANE:(k8 + 1) * LANE]
    q_ref[...] = proj(P_Q, P_K).astype(BF16)

    @pl.when(c == nsteps - 1)
    def _():
        hout_ref[0] = h_scr[...]
        tail_ref[0] = tail_scr[0:SUBLANE]


def _inproj_ssd(x, w, lw, nb, tm):
    t = x.shape[0]
    nsteps = t // nb // tm
    rmap = lambda b, c: (b * nsteps + c, 0)
    rows = lambda wd: pl.BlockSpec((tm, wd), rmap)
    per_seq = lambda s: pl.BlockSpec((1,) + s, lambda b, c: (b, 0, 0))
    return pl.pallas_call(
        functools.partial(_inproj_ssd_body, nsteps=nsteps, chunk=A_CHUNK),
        grid=(nb, nsteps),
        in_specs=[rows(D_MODEL), _full((1, D_MODEL)), _full((D_MODEL, P_COLS)),
                  _full((A_CONV, A_CONV_DIM)), _full((1, A_CONV_DIM)), _full((1, LANE)), _full((1, LANE)),
                  _full((1, A_WIDTH)), _full((1, A_WIDTH))],
        out_specs=[pl.BlockSpec((N_OCT, tm, LANE), lambda b, c: (0, b * nsteps + c, 0)), rows(C_WIDTH),
                   rows(C_KV_WIDTH), rows(C_KV_WIDTH), rows(A_WIDTH),
                   per_seq((A_WIDTH, A_STATE)), per_seq((SUBLANE, A_CONV_DIM))],
        out_shape=[jax.ShapeDtypeStruct((N_OCT, t, LANE), F32), jax.ShapeDtypeStruct((t, C_WIDTH), BF16),
                   jax.ShapeDtypeStruct((t, C_KV_WIDTH), F32), jax.ShapeDtypeStruct((t, C_KV_WIDTH), F32),
                   jax.ShapeDtypeStruct((t, A_WIDTH), BF16), jax.ShapeDtypeStruct((nb, A_WIDTH, A_STATE), F32),
                   jax.ShapeDtypeStruct((nb, SUBLANE, A_CONV_DIM), F32)],
        scratch_shapes=[pltpu.VMEM((A_WIDTH, A_STATE), F32), pltpu.VMEM((SUBLANE + A_CHUNK, A_CONV_DIM), F32)],
        compiler_params=_cparams("parallel", "arbitrary"),
        name="inproj_ssd",
    )(x, lw["norm1_w"], w, lw["conv_w"], lw["conv_b"], lw["a_log"], lw["dt_bias"], lw["d_exp"], lw["ssd_norm_w"])


def _ssd_sample_body(*refs, valid):
    n_in = 13
    (xbc_ref, z_ref, dt_ref, conv0_ref, h0_ref, cw_ref, cb_ref, alog_ref, dtb_ref, dexp_ref, nwb_ref) = refs[:11]
    y_ref, conv_out_ref, h_out_ref = refs[n_in:n_in + 3]
    xs_scr, dx_scr, z_scr, y_scr, bt_scr, ct_scr, da_scr = refs[n_in + 3:]
    k = pl.program_id(0)
    ns = LANE

    @pl.when(k == 0)
    def _():
        rows = [conv0_ref[0, r] for r in range(A_CONV - 1)] + [xbc_ref[t * ns:(t + 1) * ns, :] for t in range(valid)]
        for t in range(valid):
            acc = cb_ref[...]
            for j in range(A_CONV):
                acc = acc + cw_ref[A_CONV - 1 - j:A_CONV - j, :] * rows[t + A_CONV - 1 - j]
            xc = jax.nn.silu(acc)
            dt = jax.nn.softplus(dt_ref[t * ns:(t + 1) * ns, :] + dtb_ref[...])
            dt_t = dt.T
            da_scr[t] = jnp.exp((dt * (-jnp.exp(alog_ref[...]))).T)
            x = xc[:, :A_WIDTH]
            x_t = x.T
            for kk in range(A_HEADS):
                hs = slice(kk * A_HEAD_DIM, (kk + 1) * A_HEAD_DIM)
                xs_scr[t, hs, :] = x_t[hs] * dt_t[kk:kk + 1, :]
            dx_scr[t] = (x * dexp_ref[...]).T
            bt_scr[t] = xc[:, A_WIDTH:A_WIDTH + A_GROUPS * A_STATE].T
            ct_scr[t] = xc[:, A_WIDTH + A_GROUPS * A_STATE:].T
            z_scr[t] = jax.nn.silu(z_ref[t * ns:(t + 1) * ns, :].astype(F32)).T
        for r in range(A_CONV - 1):
            conv_out_ref[0, r] = rows[valid + r]

    g0 = pl.multiple_of((k // (A_HEADS // A_GROUPS)) * A_STATE, A_STATE)

    def per_p(p, carry):
        h = h0_ref[0, 0, p]
        row = k * A_HEAD_DIM + p
        for t in range(valid):
            da = da_scr[t, pl.ds(k, 1), :]
            xr = xs_scr[t, pl.ds(row, 1), :]
            h = da * h + xr * bt_scr[t, pl.ds(g0, A_STATE), :]
            y_scr[t, pl.ds(row, 1), :] = jnp.sum(h * ct_scr[t, pl.ds(g0, A_STATE), :], axis=0, keepdims=True)
        h_out_ref[0, 0, p] = h
        return carry

    lax.fori_loop(0, A_HEAD_DIM, per_p, 0, unroll=2)

    @pl.when(k == A_HEADS - 1)
    def _():
        for t in range(valid):
            y = (y_scr[t] + dx_scr[t]) * z_scr[t]
            y = y * lax.rsqrt(jnp.mean(y * y, axis=0, keepdims=True) + EPS) * nwb_ref[...]
            y_ref[:, t * ns:(t + 1) * ns] = y.astype(BF16)


def _ssd_sample(xbc, z, dt, conv_all, h_all, lw, layer, valid, conv_prev, h_prev):
    rows = valid * LANE
    hblk = pl.BlockSpec((1, 1, A_HEAD_DIM, A_STATE, LANE), lambda k: (layer, k, 0, 0, 0))
    cblk = pl.BlockSpec((1, A_CONV - 1, LANE, A_CONV_DIM), lambda k: (layer, 0, 0, 0))
    in_specs = [_full((rows, A_CONV_DIM)), _full((rows, A_WIDTH)), _full((rows, LANE)), cblk, hblk,
                _full((A_CONV, A_CONV_DIM)), _full((1, A_CONV_DIM)), _full((1, LANE)), _full((1, LANE)),
                _full((1, A_WIDTH)), _full((A_WIDTH, LANE))] + [pl.BlockSpec(memory_space=pl.ANY)] * 2
    args = [xbc, z, dt, conv_all, h_all, lw["conv_w"], lw["conv_b"], lw["a_log"], lw["dt_bias"], lw["d_exp"],
            lw["ssd_norm_wb"], conv_prev, h_prev]
    aliases = {len(args) - 2: 1, len(args) - 1: 2}
    big = lambda n: pltpu.VMEM((valid, n, LANE), F32)
    return pl.pallas_call(
        functools.partial(_ssd_sample_body, valid=valid),
        grid=(A_HEADS,),
        in_specs=in_specs,
        out_specs=[_full((A_WIDTH, rows)), cblk, hblk],
        out_shape=[jax.ShapeDtypeStruct((A_WIDTH, rows), BF16), jax.ShapeDtypeStruct(conv_all.shape, F32),
                   jax.ShapeDtypeStruct(h_all.shape, F32)],
        scratch_shapes=[big(A_WIDTH), big(A_WIDTH), big(A_WIDTH), big(A_WIDTH), big(LANE), big(LANE), big(LANE)],
        input_output_aliases=aliases,
        compiler_params=_cparams("arbitrary"),
        name="ssd_sample",
    )(*args)


def _s5w_body(lre_ref, lim_ref, ldt_ref, btr_ref, bti_ref, cr_ref, ci_ref,
              bre_ref, bim_ref, bsre_ref, bsim_ref, cre_ref, cim_ref, toep_ref, ap_ref, cd_scr, bb_scr, *, valid):
    h, q = B_STATE, S5_Q
    for ref in (bre_ref, bim_ref, bsre_ref, bsim_ref, cre_ref, cim_ref, toep_ref):
        ref[...] = jnp.zeros(ref.shape, BF16)
    r_i = lax.broadcasted_iota(jnp.int32, (q * B_CH, q * LANE), 0)
    c_i = lax.broadcasted_iota(jnp.int32, (q * B_CH, q * LANE), 1)
    d = lax.broadcasted_iota(jnp.int32, (2 * SUBLANE, h), 0).astype(F32)
    for g8 in range(S5_OCT):
        lr, li = lre_ref[g8], lim_ref[g8]
        step = jnp.exp(ldt_ref[g8])
        er = jnp.exp(lr * step * d)
        ang = li * step * d
        ar, ai = er * jnp.cos(ang), er * jnp.sin(ang)
        nr, ni = ar[1:2] - 1.0, ai[1:2]
        den = lr * lr + li * li
        wr, wi = (nr * lr + ni * li) / den, (ni * lr - nr * li) / den
        btr, bti = btr_ref[g8], bti_ref[g8]
        bbr, bbi = wr * btr - wi * bti, wr * bti + wi * btr
        cr, ci = cr_ref[g8], ci_ref[g8]
        bb_scr[:, 0:h] = bbr
        bb_scr[:, h:2 * h] = bbi
        cols = slice(g8 * h, (g8 + 1) * h)
        for s in range(q):
            rows = slice(s * LANE + g8 * B_CH, s * LANE + (g8 + 1) * B_CH)
            e = q - 1 - s
            bre_ref[0, rows, cols] = (ar[e:e + 1] * bbr - ai[e:e + 1] * bbi).astype(BF16)
            bim_ref[0, rows, cols] = (ai[e:e + 1] * bbr + ar[e:e + 1] * bbi).astype(BF16)
            if s < valid:
                e = valid - 1 - s
                bsre_ref[0, rows, cols] = (ar[e:e + 1] * bbr - ai[e:e + 1] * bbi).astype(BF16)
                bsim_ref[0, rows, cols] = (ai[e:e + 1] * bbr + ar[e:e + 1] * bbi).astype(BF16)
            cre_ref[0, rows, cols] = (cr * ar[s + 1:s + 2] - ci * ai[s + 1:s + 2]).astype(BF16)
            cim_ref[0, rows, cols] = (-(cr * ai[s + 1:s + 2] + ci * ar[s + 1:s + 2])).astype(BF16)
            cd_scr[s * B_CH:(s + 1) * B_CH, 0:h] = cr * ar[s:s + 1] - ci * ai[s:s + 1]
            cd_scr[s * B_CH:(s + 1) * B_CH, h:2 * h] = -(cr * ai[s:s + 1] + ci * ar[s:s + 1])
        kdt = _dot_nt(bb_scr[...], cd_scr[...], precision=HIGHEST)
        place = (c_i == (r_i >> 4) * LANE + g8 * B_CH + (r_i & (B_CH - 1))).astype(F32)
        slab = jnp.dot(kdt, place, preferred_element_type=F32, precision=HIGHEST).astype(BF16)
        for s in range(q):
            toep_ref[0, s * LANE + g8 * B_CH:s * LANE + (g8 + 1) * B_CH, s * LANE:] = slab[:, 0:(q - s) * LANE]
        ap_ref[g8, 0:1, 0:h] = ar[q:q + 1]
        ap_ref[g8, 0:1, h:2 * h] = ai[q:q + 1]
        ap_ref[g8, 1:2, 0:h] = ar[valid:valid + 1]
        ap_ref[g8, 1:2, h:2 * h] = ai[valid:valid + 1]


def _s5_weights(lam_re, lam_im, log_dt, b_re, b_im, c_re, c_im, valid):
    g, h, q = B_GROUPS, B_STATE, S5_Q
    vec = pl.BlockSpec((S5_OCT, 1, h), lambda i: (i, 0, 0))
    mat = pl.BlockSpec((S5_OCT, B_CH, h), lambda i: (i, 0, 0))
    st = pl.BlockSpec((1, q * LANE, S5_OCT * h), lambda i: (i, 0, 0))
    st_shape = jax.ShapeDtypeStruct((N_OCT, q * LANE, S5_OCT * h), BF16)
    bre, bim, bsre, bsim, cre, cim, toep, ap = pl.pallas_call(
        functools.partial(_s5w_body, valid=valid),
        grid=(N_OCT,),
        in_specs=[vec, vec, vec, mat, mat, mat, mat],
        out_specs=[st] * 6 + [pl.BlockSpec((1, q * LANE, q * LANE), lambda i: (i, 0, 0)),
                              pl.BlockSpec((S5_OCT, 2, 2 * h), lambda i: (i, 0, 0))],
        out_shape=[st_shape] * 6 + [jax.ShapeDtypeStruct((N_OCT, q * LANE, q * LANE), BF16),
                                    jax.ShapeDtypeStruct((g, 2, 2 * h), F32)],
        scratch_shapes=[pltpu.VMEM((q * B_CH, 2 * h), F32), pltpu.VMEM((B_CH, 2 * h), F32)],
        compiler_params=_cparams("parallel"),
        name="s5_weights",
    )(lam_re.reshape(g, 1, h), lam_im.reshape(g, 1, h), jnp.broadcast_to(log_dt.reshape(g, 1, 1), (g, 1, h)),
      jnp.swapaxes(b_re, 1, 2), jnp.swapaxes(b_im, 1, 2), c_re, c_im)
    return dict(
        bst_re=bre, bst_im=bim, bsts_re=bsre, bsts_im=bsim, cst_re=cre, cst_im=cim, toep=toep,
        aq_re=ap[:, 0, :h].reshape(1, S5_STATE_COLS), aq_im=ap[:, 0, h:].reshape(1, S5_STATE_COLS),
        av_re=ap[:, 1, :h].reshape(1, S5_STATE_COLS), av_im=ap[:, 1, h:].reshape(1, S5_STATE_COLS),
    )


def _oct_rows(u_ref, k8, rt):
    return jnp.concatenate([u_ref[k8, pl.ds(j, rt, stride=S5_Q), :] for j in range(S5_Q)], axis=1).astype(BF16)


def _toep_dot(uo, toep_ref, k8):
    two = 2 * LANE
    cols = [_dot(uo[:, :(t + 1) * two], toep_ref[k8, :(t + 1) * two, t * two:(t + 1) * two])
            for t in range(uo.shape[1] // two)]
    return jnp.concatenate(cols, axis=1)


def _s5_state_body(u_ref, bre_ref, bim_ref, sre_ref, sim_ref):
    w = S5_OCT * B_STATE
    for k8 in range(N_OCT):
        uo = _oct_rows(u_ref, k8, sre_ref.shape[0])
        sre_ref[:, k8 * w:(k8 + 1) * w] = _dot(uo, bre_ref[k8])
        sim_ref[:, k8 * w:(k8 + 1) * w] = _dot(uo, bim_ref[k8])


def _s5_state(u, bre, bim, rt):
    r = u.shape[1] // S5_Q
    wspec = _full((N_OCT, S5_Q * LANE, S5_OCT * B_STATE))
    ospec = pl.BlockSpec((rt, S5_STATE_COLS), lambda i: (i, 0))
    return pl.pallas_call(
        _s5_state_body,
        grid=(r // rt,),
        in_specs=[_tile_major_spec(rt * S5_Q), wspec, wspec],
        out_specs=[ospec, ospec],
        out_shape=[jax.ShapeDtypeStruct((r, S5_STATE_COLS), F32)] * 2,
        compiler_params=_cparams("parallel"),
        name="s5_state",
    )(u, bre, bim)


def _s5_scan_body(sre_ref, sim_ref, ar_ref, ai_ref, hre_ref, him_ref, fre_ref, fim_ref, cr_scr, ci_scr, *, rb, nblk):
    i = pl.program_id(0)
    nb = sre_ref.shape[0]

    @pl.when(i == 0)
    def _():
        cr_scr[...] = jnp.zeros_like(cr_scr)
        ci_scr[...] = jnp.zeros_like(ci_scr)

    ar, ai = ar_ref[...], ai_ref[...]

    def step(r, carry):
        out = []
        for b in range(nb):
            hr, hi = carry[2 * b], carry[2 * b + 1]
            hre_ref[b, pl.ds(r, 1), :] = hr
            him_ref[b, pl.ds(r, 1), :] = hi
            sr, si = sre_ref[b, pl.ds(r, 1), :], sim_ref[b, pl.ds(r, 1), :]
            out += [ar * hr - ai * hi + sr, ai * hr + ar * hi + si]
        return tuple(out)

    init = tuple(s[b] for b in range(nb) for s in (cr_scr, ci_scr))
    fin = lax.fori_loop(0, rb, step, init)
    for b in range(nb):
        cr_scr[b] = fin[2 * b]
        ci_scr[b] = fin[2 * b + 1]

    @pl.when(i == nblk - 1)
    def _():
        for b in range(nb):
            fre_ref[b] = fin[2 * b]
            fim_ref[b] = fin[2 * b + 1]


def _s5_scan(sre, sim, ar, ai, nb, rb):
    r = sre.shape[0] // nb
    nblk = r // rb
    blk = pl.BlockSpec((nb, rb, S5_STATE_COLS), lambda i: (0, i, 0))
    vec = _full((1, S5_STATE_COLS))
    fin = _full((nb, 1, S5_STATE_COLS))
    per_batch = lambda a: a.reshape(nb, r, S5_STATE_COLS)
    hre, him, fre, fim = pl.pallas_call(
        functools.partial(_s5_scan_body, rb=rb, nblk=nblk),
        grid=(nblk,),
        in_specs=[blk, blk, vec, vec],
        out_specs=[blk, blk, fin, fin],
        out_shape=[jax.ShapeDtypeStruct((nb, r, S5_STATE_COLS), F32)] * 2
        + [jax.ShapeDtypeStruct((nb, 1, S5_STATE_COLS), F32)] * 2,
        scratch_shapes=[pltpu.VMEM((nb, 1, S5_STATE_COLS), F32)] * 2,
        compiler_params=_cparams("arbitrary"),
        name="s5_scan",
    )(per_batch(sre), per_batch(sim), ar, ai)
    return hre.reshape(nb * r, S5_STATE_COLS), him.reshape(nb * r, S5_STATE_COLS), fre, fim


def _s5_sample_body(u_ref, toep_ref, cre_ref, cim_ref, bre_ref, bim_ref, hre_ref, him_ref, ar_ref, ai_ref,
                    y_ref, fre_ref, fim_ref, *, valid):
    w = S5_OCT * B_STATE
    ns = LANE
    for k8 in range(N_OCT):
        blocks = [u_ref[k8, t * ns:(t + 1) * ns, :] for t in range(valid)]
        uo = jnp.concatenate(blocks, axis=1).astype(BF16)
        uo_t = jnp.concatenate([b.T for b in blocks], axis=0).astype(BF16)
        hr, hi = hre_ref[0, k8 * w:(k8 + 1) * w, :], him_ref[0, k8 * w:(k8 + 1) * w, :]
        yo = (_toep_dot(uo, toep_ref, k8) + _dot_nt(hr.T.astype(BF16), cre_ref[k8])
              + _dot_nt(hi.T.astype(BF16), cim_ref[k8]))
        for t in range(valid):
            y_ref[k8, t * ns:(t + 1) * ns, :] = yo[:, t * LANE:(t + 1) * LANE]
        ar, ai = ar_ref[k8 * w:(k8 + 1) * w, :], ai_ref[k8 * w:(k8 + 1) * w, :]
        fre_ref[k8 * w:(k8 + 1) * w, :] = ar * hr - ai * hi + _dot_tn(bre_ref[k8], uo_t)
        fim_ref[k8 * w:(k8 + 1) * w, :] = ai * hr + ar * hi + _dot_tn(bim_ref[k8], uo_t)


def _s5_sample(u, s5w, hre_all, him_all, ar, ai, layer, valid):
    rows = valid * LANE
    vq = valid * LANE
    sub = lambda c: pl.BlockSpec((N_OCT, vq, c), lambda i: (0, 0, 0))
    hblk = pl.BlockSpec((1, S5_STATE_COLS, LANE), lambda i: (layer, 0, 0))
    st = _full((S5_STATE_COLS, LANE))
    return pl.pallas_call(
        functools.partial(_s5_sample_body, valid=valid),
        grid=(1,),
        in_specs=[_tile_major_spec(rows), sub(vq), sub(S5_OCT * B_STATE), sub(S5_OCT * B_STATE),
                  sub(S5_OCT * B_STATE), sub(S5_OCT * B_STATE), hblk, hblk, st, st],
        out_specs=[_tile_major_spec(rows), st, st],
        out_shape=[jax.ShapeDtypeStruct((N_OCT, rows, LANE), F32)] + [jax.ShapeDtypeStruct((S5_STATE_COLS, LANE), F32)] * 2,
        compiler_params=_cparams("arbitrary"),
        name="s5_sample",
    )(u, s5w["toep"], s5w["cst_re"], s5w["cst_im"], s5w["bsts_re"], s5w["bsts_im"], hre_all, him_all, ar, ai)


def _s5_out_body(u_ref, hre_ref, him_ref, toep_ref, cre_ref, cim_ref, y_ref):
    w = S5_OCT * B_STATE
    rt = hre_ref.shape[0]
    for k8 in range(N_OCT):
        uo = _oct_rows(u_ref, k8, rt)
        yo = (_toep_dot(uo, toep_ref, k8)
              + _dot_nt(hre_ref[:, k8 * w:(k8 + 1) * w].astype(BF16), cre_ref[k8])
              + _dot_nt(him_ref[:, k8 * w:(k8 + 1) * w].astype(BF16), cim_ref[k8]))
        for j in range(S5_Q):
            y_ref[k8, pl.ds(j, rt, stride=S5_Q), :] = yo[:, j * LANE:(j + 1) * LANE]


def _s5_out(u, hre, him, toep, cre, cim, rt):
    r = u.shape[1] // S5_Q
    hblk = pl.BlockSpec((rt, S5_STATE_COLS), lambda i: (i, 0))
    return pl.pallas_call(
        _s5_out_body,
        grid=(r // rt,),
        in_specs=[_tile_major_spec(rt * S5_Q), hblk, hblk, _full((N_OCT, S5_Q * LANE, S5_Q * LANE)),
                  _full((N_OCT, S5_Q * LANE, S5_OCT * B_STATE)), _full((N_OCT, S5_Q * LANE, S5_OCT * B_STATE))],
        out_specs=_tile_major_spec(rt * S5_Q),
        out_shape=jax.ShapeDtypeStruct(u.shape, F32),
        compiler_params=_cparams("parallel"),
        name="s5_out",
    )(u, hre, him, toep, cre, cim)


def _band_body(bucket_ref, table_ref, o_ref):
    bucket = bucket_ref[...]
    col = lax.broadcasted_iota(jnp.int32, (WINDOW, 2 * WINDOW), 1)
    for hq in range(C_HEADS):
        base = jnp.full((1, 2 * WINDOW), NEG, F32)
        for b in range(REL_BUCKETS):
            base = jnp.where(bucket == b, table_ref[b, hq], base)
        tile = pltpu.roll(jnp.broadcast_to(base, (WINDOW, 2 * WINDOW)), 0, 1, stride=1, stride_axis=0)
        o_ref[1, hq] = tile.T
        o_ref[0, hq] = jnp.where(col >= WINDOW, tile, NEG).T


def _bias_band(table):
    lane = np.arange(2 * WINDOW)
    n = WINDOW - lane
    exact = REL_BUCKETS // 2
    nf = np.maximum(n, 1).astype(np.float32)
    large = exact + (np.log(nf / np.float32(exact)) / np.float32(math.log(REL_MAX_DIST / exact))
                     * np.float32(REL_BUCKETS - exact)).astype(np.int32)
    bucket = np.where(n < exact, n, np.minimum(large, REL_BUCKETS - 1))
    bucket = np.where((n >= 0) & (n < WINDOW), bucket, -1).astype(np.int32)
    return pl.pallas_call(
        _band_body,
        in_specs=[pl.BlockSpec(memory_space=pltpu.VMEM), pl.BlockSpec(memory_space=pltpu.SMEM)],
        out_specs=pl.BlockSpec(memory_space=pltpu.VMEM),
        out_shape=jax.ShapeDtypeStruct((2, C_HEADS, 2 * WINDOW, WINDOW), F32),
        name="rel_bias_band",
    )(jnp.asarray(bucket).reshape(1, 2 * WINDOW), table)


def _attn_prompt_body(q_ref, kc_ref, kp_ref, vc_ref, vp_ref, b0_ref, br_ref, sink_ref, o_ref, *, nblk):
    k_all = jnp.concatenate([kp_ref[...], kc_ref[...]], axis=0) * (C_HEAD_DIM ** -0.5)
    v_all = jnp.concatenate([vp_ref[...], vc_ref[...]], axis=0)
    swapped = (pltpu.roll(k_all, C_HEAD_DIM, 1), pltpu.roll(v_all, C_HEAD_DIM, 1))
    low = lax.broadcasted_iota(jnp.int32, k_all.shape, 1) < C_HEAD_DIM
    for hk in range(C_KV_HEADS):
        k_lo = k_all if hk == 0 else swapped[0]
        k_hi = swapped[0] if hk == 0 else k_all
        kz = (jnp.where(low, k_lo, 0.0).astype(BF16), jnp.where(low, 0.0, k_hi).astype(BF16))
        vz = (v_all if hk == 0 else swapped[1])[:, :C_HEAD_DIM].astype(BF16)
        for g in range(C_GQA):
            hq = hk * C_GQA + g
            sk = sink_ref[hq]
            for i in range(nblk):
                keys = slice(i * WINDOW, (i + 2) * WINDOW)
                qt = q_ref[i * WINDOW:(i + 1) * WINDOW, (hq // 2) * LANE:(hq // 2 + 1) * LANE]
                bias = b0_ref[0, hq] if i == 0 else br_ref[0, hq]
                s = _dot_nt(kz[hq % 2][keys], qt) + bias
                m = jnp.maximum(jnp.max(s, axis=0, keepdims=True), sk)
                p = jnp.exp(s - m)
                den = jnp.sum(p, axis=0, keepdims=True) + jnp.exp(sk - m)
                o = _dot_tn(vz[keys], p.astype(BF16)) / den
                o_ref[hq * C_HEAD_DIM:(hq + 1) * C_HEAD_DIM, i * WINDOW:(i + 1) * WINDOW] = o.astype(BF16)


def _attn_prompt(q, k, v, band, sinks, nb, nblk_seq, nblk):
    steps = nblk_seq // nblk
    cur = lambda b, n: (b * steps + n, 0)
    prev = lambda b, n: (b * nblk_seq + jnp.maximum(n * nblk - 1, 0), 0)
    bspec = lambda m: pl.BlockSpec((1, C_HEADS, 2 * WINDOW, WINDOW), m)
    return pl.pallas_call(
        functools.partial(_attn_prompt_body, nblk=nblk),
        grid=(nb, steps),
        in_specs=[pl.BlockSpec((nblk * WINDOW, C_WIDTH), cur),
                  pl.BlockSpec((nblk * WINDOW, C_KV_WIDTH), cur), pl.BlockSpec((WINDOW, C_KV_WIDTH), prev),
                  pl.BlockSpec((nblk * WINDOW, C_KV_WIDTH), cur), pl.BlockSpec((WINDOW, C_KV_WIDTH), prev),
                  bspec(lambda b, n: (jnp.minimum(n, 1), 0, 0, 0)), bspec(lambda b, n: (1, 0, 0, 0)),
                  pl.BlockSpec(memory_space=pltpu.SMEM)],
        out_specs=pl.BlockSpec((C_WIDTH, nblk * WINDOW), lambda b, n: (0, b * steps + n)),
        out_shape=jax.ShapeDtypeStruct((C_WIDTH, nb * nblk_seq * WINDOW), BF16),
        compiler_params=_cparams("parallel", "arbitrary"),
        name="attn_prompt",
    )(q, k, k, v, v, band, band, sinks)


def _attn_sample_body(*refs, valid, lb, nseq):
    q_ref, kn_ref, vn_ref, kc_ref, vc_ref, b1_ref, b2_ref, sink_ref = refs[:8]
    o_ref, ko_ref, vo_ref, s1_scr, s2_scr, inv_scr, o_scr = refs[-7:]
    d = C_HEAD_DIM
    rs = 2 * valid
    lane_c = lax.broadcasted_iota(jnp.int32, (C_KV_WIDTH, lb), 1)
    lane_n = lax.broadcasted_iota(jnp.int32, (SUBLANE, C_KV_WIDTH), 1)
    pad = jnp.zeros((SUBLANE - valid, C_KV_WIDTH), F32)
    zero_half = jnp.zeros((d, lb), BF16)
    combos = [(hk, par) for hk in range(C_KV_HEADS) for par in range(2)]

    def new_rows(ref, s):
        return jnp.concatenate([ref[:, s, :], pad], axis=0)

    for s in range(nseq):
        qs = q_ref[:, s, :].astype(F32)
        kn, vn = new_rows(kn_ref, s), new_rows(vn_ref, s)
        kt, vt = kc_ref[0, s], vc_ref[0, s]
        kn_t, vn_t = kn.T, vn.T
        new_k, new_v = pltpu.roll(kt, lb - valid, 1), pltpu.roll(vt, lb - valid, 1)
        for t in range(valid):
            at = lane_c == lb - valid + t
            new_k = jnp.where(at, jnp.broadcast_to(kn_t[:, t:t + 1], (C_KV_WIDTH, lb)), new_k)
            new_v = jnp.where(at, jnp.broadcast_to(vn_t[:, t:t + 1], (C_KV_WIDTH, lb)), new_v)
        ko_ref[0, s] = new_k
        vo_ref[0, s] = new_v
        kn_sw = pltpu.roll(kn, d, 1)
        for c, (hk, par) in enumerate(combos):
            kth = kt[hk * d:(hk + 1) * d].astype(BF16)
            kz = jnp.concatenate([kth, zero_half] if par == 0 else [zero_half, kth], axis=0)
            src = kn if hk == par else kn_sw
            knz = jnp.where((lane_n < d) if par == 0 else (lane_n >= d), src, 0.0).astype(BF16)
            qq = jnp.concatenate([qs[:, (hk * 2 + j) * LANE:(hk * 2 + j + 1) * LANE] for j in range(2)],
                                 axis=0).astype(BF16)
            s1_scr[c, s * rs:(s + 1) * rs, :] = _dot(qq, kz)
            s2_scr[c, s * rs:(s + 1) * rs, :] = _dot_nt(qq, knz)

    for c, (hk, par) in enumerate(combos):
        s1 = s1_scr[c] * (d ** -0.5) + b1_ref[hk, par]
        s2 = s2_scr[c] * (d ** -0.5) + b2_ref[hk, par]
        sk = sink_ref[hk, par]
        m = jnp.maximum(jnp.maximum(jnp.max(s1, axis=-1, keepdims=True), jnp.max(s2, axis=-1, keepdims=True)), sk)
        p1, p2 = jnp.exp(s1 - m), jnp.exp(s2 - m)
        den = jnp.sum(p1, axis=-1, keepdims=True) + jnp.sum(p2, axis=-1, keepdims=True) + jnp.exp(sk - m)
        s1_scr[c] = p1
        s2_scr[c] = p2
        inv_scr[c] = jnp.broadcast_to(1.0 / den, (nseq * rs, SUBLANE))

    for s in range(nseq):
        rows = slice(s * rs, (s + 1) * rs)
        vt = vc_ref[0, s]
        vn = new_rows(vn_ref, s)
        for c, (hk, par) in enumerate(combos):
            vth = vt[hk * d:(hk + 1) * d].astype(BF16)
            vnh = vn[:, hk * d:(hk + 1) * d].astype(BF16)
            o = _dot_nt(s1_scr[c, rows, :].astype(BF16), vth) + _dot(s2_scr[c, rows, :].astype(BF16), vnh)
            o_scr[hk, rows, par * d:(par + 1) * d] = o * inv_scr[c, rows, 0:1]

    for hk in range(C_KV_HEADS):
        for j in range(2):
            for t in range(valid):
                tile = hk * 2 + j
                o_ref[t, :, tile * LANE:(tile + 1) * LANE] = o_scr[hk, pl.ds(j * valid + t, nseq, stride=rs), :]


def _attn_sample(q3, kn3, vn3, kc_all, vc_all, b1, b2, sink_col, layer, k_prev, v_prev):
    valid, nsb, _ = q3.shape
    lb = kc_all.shape[-1]
    nseq = 2 * SUBLANE
    assert 2 * valid == SUBLANE
    rows = lambda w: pl.BlockSpec((valid, nseq, w), lambda i: (0, i, 0))
    cblk = pl.BlockSpec((1, nseq, C_KV_WIDTH, lb), lambda i: (layer, i, 0, 0))
    b1, b2, sink_col = (jnp.tile(a, (1, 1, nseq, 1)) for a in (b1, b2, sink_col))
    in_specs = [rows(C_WIDTH), rows(C_KV_WIDTH), rows(C_KV_WIDTH), cblk, cblk,
                _full(b1.shape), _full(b2.shape), _full(sink_col.shape)] + [pl.BlockSpec(memory_space=pl.ANY)] * 2
    args = [q3, kn3, vn3, kc_all, vc_all, b1, b2, sink_col, k_prev, v_prev]
    aliases = {len(args) - 2: 1, len(args) - 1: 2}
    return pl.pallas_call(
        functools.partial(_attn_sample_body, valid=valid, lb=lb, nseq=nseq),
        grid=(nsb // nseq,),
        in_specs=in_specs,
        out_specs=[rows(C_WIDTH), cblk, cblk],
        out_shape=[jax.ShapeDtypeStruct((valid, nsb, C_WIDTH), F32), jax.ShapeDtypeStruct(kc_all.shape, F32),
                   jax.ShapeDtypeStruct(vc_all.shape, F32)],
        scratch_shapes=[pltpu.VMEM((2 * C_KV_HEADS, nseq * SUBLANE, lb), F32),
                        pltpu.VMEM((2 * C_KV_HEADS, nseq * SUBLANE, SUBLANE), F32),
                        pltpu.VMEM((2 * C_KV_HEADS, nseq * SUBLANE, SUBLANE), F32),
                        pltpu.VMEM((C_KV_HEADS, nseq * SUBLANE, LANE), F32)],
        input_output_aliases=aliases,
        compiler_params=_cparams("arbitrary"),
        name="attn_sample",
    )(*args)


def _merge_body(x_ref, ya_ref, yb_ref, u_ref, yc_ref, nw_ref, wg_ref, d5_ref, wglu_ref,
                wa_ref, wb_ref, wc_ref, wo_ref, o_ref, *, ya_features_major, yc_features_major):
    x = x_ref[...]
    h = _rms(x, nw_ref[...]).astype(BF16)
    g = jax.nn.gelu(_lanes(yb_ref) + d5_ref[...] * _lanes(u_ref))
    gv = _dot(g.astype(BF16), wglu_ref[...])
    yb = (gv[:, :B_WIDTH] * jax.nn.sigmoid(gv[:, B_WIDTH:])).astype(BF16)

    def gate(i):
        return jax.nn.sigmoid(_dot(h, wg_ref[:, i * D_MODEL:(i + 1) * D_MODEL]))

    def branch(y_ref, w_ref, features_major):
        y = y_ref[...].astype(BF16)
        return _dot_tn(y, w_ref[...]) if features_major else _dot(y, w_ref[...])

    mixed = gate(0) * branch(ya_ref, wa_ref, ya_features_major)
    mixed = mixed + gate(1) * _dot(yb, wb_ref[...])
    mixed = mixed + gate(2) * branch(yc_ref, wc_ref, yc_features_major)
    o_ref[...] = x + _dot(mixed.astype(BF16), wo_ref[...])


def _merge(x, ya, yb, u, yc, lw, tm, ya_features_major, yc_features_major):
    t = x.shape[0]
    rows = lambda w: pl.BlockSpec((tm, w), lambda i: (i, 0))
    branch_spec = lambda fm: pl.BlockSpec((A_WIDTH, tm), lambda i: (0, i)) if fm else rows(A_WIDTH)
    return pl.pallas_call(
        functools.partial(_merge_body, ya_features_major=ya_features_major, yc_features_major=yc_features_major),
        grid=(t // tm,),
        in_specs=[rows(D_MODEL), branch_spec(ya_features_major), _tile_major_spec(tm), _tile_major_spec(tm),
                  branch_spec(yc_features_major),
                  _full((1, D_MODEL)), _full((D_MODEL, 3 * D_MODEL)), _full((1, B_WIDTH)),
                  _full((B_WIDTH, 2 * B_WIDTH)), _full((A_WIDTH, D_MODEL)), _full((B_WIDTH, D_MODEL)),
                  _full((C_WIDTH, D_MODEL)), _full((D_MODEL, D_MODEL))],
        out_specs=rows(D_MODEL),
        out_shape=jax.ShapeDtypeStruct((t, D_MODEL), F32),
        compiler_params=_cparams("parallel"),
        name="merge",
    )(x, ya, yb, u, yc, lw["norm1_w"], lw["w_gate"], lw["s5_d"], lw["w_glu"],
      lw["w_br_a"], lw["w_br_b"], lw["w_br_c"], lw["w_out"])


def _ffn_body(x_ref, nw_ref, wup_ref, wdn_ref, fnw_ref, o_ref, act_scr, *, final_norm):
    x = x_ref[...]
    h = _rms(x, nw_ref[...]).astype(BF16)
    for c in range(D_FF // FF_CHUNK):
        lo, hi = c * FF_CHUNK, (c + 1) * FF_CHUNK
        a = _dot(h, wup_ref[:, lo:hi])
        b = _dot(h, wup_ref[:, D_FF + lo:D_FF + hi])
        act_scr[:, lo:hi] = (jax.nn.silu(a) * b).astype(BF16)
    y = x + _dot(act_scr[...], wdn_ref[...])
    if final_norm:
        y = _rms(y, fnw_ref[...])
    o_ref[...] = y


def _ffn(x, lw, fnw, tm, final_norm):
    t = x.shape[0]
    rows = pl.BlockSpec((tm, D_MODEL), lambda i: (i, 0))
    return pl.pallas_call(
        functools.partial(_ffn_body, final_norm=final_norm),
        grid=(t // tm,),
        in_specs=[rows, _full((1, D_MODEL)), _full((D_MODEL, 2 * D_FF)), _full((D_FF, D_MODEL)), _full((1, D_MODEL))],
        out_specs=rows,
        out_shape=jax.ShapeDtypeStruct((t, D_MODEL), F32),
        scratch_shapes=[pltpu.VMEM((tm, D_FF), BF16)],
        compiler_params=_cparams("parallel"),
        name="ffn",
    )(x, lw["norm2_w"], lw["w_ffn_up"], lw["w_ffn_down"], fnw)


def _row_tile(t, pref):
    tm = pref
    while t % tm:
        tm //= 2
    return tm


def kernel(x_prompt, x_sample, state_ssd, state_conv, state_s5_re, state_s5_im, cache_k, cache_v, norm1_w, w_in, conv_w, conv_b, ssd_a_log, ssd_dt_bias, ssd_d, ssd_norm_w, s5_lam_re, s5_lam_im, s5_log_dt, s5_b_re, s5_b_im, s5_c_re, s5_c_im, s5_d, s5_w_glu, attn_sinks, w_br_a, w_br_b, w_br_c, w_out, norm2_w, w_ffn_up, w_ffn_down, rel_bias, final_norm_w):
    nb, seq, _ = x_prompt.shape
    nsb, valid, _ = x_sample.shape
    depth = w_in.shape[0]
    lb = cache_k.shape[2]
    assert seq % A_CHUNK == 0 and seq % WINDOW == 0 and seq % (S5_Q * SUBLANE) == 0
    assert nsb == LANE and A_CONV - 1 <= valid <= S5_Q and lb == WINDOW
    nchunk = seq // A_CHUNK
    tp = nb * seq
    ts = valid * nsb

    band = _bias_band(rel_bias)

    def pair_rows(a):
        a = a.reshape(C_KV_HEADS, 2, 2, valid, a.shape[-1])
        return jnp.swapaxes(a, 1, 2).reshape(C_KV_HEADS, 2, 2 * valid, a.shape[-1])

    band_s = pair_rows(jnp.swapaxes(band[1, :, :, :valid], 1, 2))
    bias_s1 = band_s[..., :lb]
    bias_s2 = band_s[..., lb:lb + SUBLANE]
    fnw = final_norm_w.reshape(1, D_MODEL)

    xp = x_prompt.reshape(tp, D_MODEL)
    xs = jnp.swapaxes(x_sample, 0, 1).reshape(ts, D_MODEL)
    conv_all = jnp.swapaxes(state_conv, 1, 2)
    ssd_all = jnp.transpose(state_ssd, (0, 2, 3, 4, 1))
    s5re_all = jnp.transpose(state_s5_re, (0, 2, 3, 1)).reshape(depth, S5_STATE_COLS, nsb)
    s5im_all = jnp.transpose(state_s5_im, (0, 2, 3, 1)).reshape(depth, S5_STATE_COLS, nsb)
    kc_all = jnp.transpose(cache_k, (0, 1, 3, 4, 2)).reshape(depth, nsb, C_KV_WIDTH, lb)
    vc_all = jnp.transpose(cache_v, (0, 1, 3, 4, 2)).reshape(depth, nsb, C_KV_WIDTH, lb)
    tm_p = _row_tile(tp, 512)
    tm_s = _row_tile(ts, 512)
    rt_p = _row_tile(tp // S5_Q, 256)
    rb_p = _row_tile(seq // S5_Q, 128)
    attn_blk = _row_tile(seq // WINDOW, 8)

    new_p, s5_s = [], []
    conv_new, ssd_new = jnp.zeros(conv_all.shape, F32), jnp.zeros(ssd_all.shape, F32)
    k_new, v_new = jnp.zeros(kc_all.shape, F32), jnp.zeros(vc_all.shape, F32)
    for i in range(depth):
        wi = w_in[i]
        off = [0, 512, 1280, 1288, 1800, 2312, 2440, 2568, 5640]
        w_pack = jnp.concatenate(
            [wi[:, off[0]:off[1]], wi[:, off[1]:off[2]], wi[:, off[3]:off[4]], wi[:, off[4]:off[5]],
             wi[:, off[5]:off[6]], wi[:, off[6]:off[7]],
             jnp.pad(wi[:, off[2]:off[3]], ((0, 0), (0, LANE - A_HEADS)))], axis=1).astype(BF16)
        sink_rows = pair_rows(jnp.broadcast_to(attn_sinks[i].reshape(C_HEADS, 1, 1), (C_HEADS, valid, 1)))
        lw = dict(
            norm1_w=norm1_w[i].reshape(1, D_MODEL), w_gate=wi[:, off[7]:off[8]].astype(BF16),
            conv_w=conv_w[i], conv_b=conv_b[i].reshape(1, A_CONV_DIM),
            a_log=jnp.pad(ssd_a_log[i], (0, LANE - A_HEADS)).reshape(1, LANE),
            dt_bias=jnp.pad(ssd_dt_bias[i], (0, LANE - A_HEADS)).reshape(1, LANE),
            d_exp=jnp.repeat(ssd_d[i], A_HEAD_DIM).reshape(1, A_WIDTH),
            ssd_norm_w=ssd_norm_w[i].reshape(1, A_WIDTH),
            ssd_norm_wb=jnp.broadcast_to(ssd_norm_w[i].reshape(A_WIDTH, 1), (A_WIDTH, LANE)),
            s5_d=s5_d[i].reshape(1, B_WIDTH), w_glu=s5_w_glu[i].astype(BF16),
            w_br_a=w_br_a[i].astype(BF16), w_br_b=w_br_b[i].astype(BF16), w_br_c=w_br_c[i].astype(BF16),
            w_out=w_out[i].astype(BF16), norm2_w=norm2_w[i].reshape(1, D_MODEL),
            w_ffn_up=w_ffn_up[i].astype(BF16), w_ffn_down=w_ffn_down[i].astype(BF16),
        )
        s5w = _s5_weights(s5_lam_re[i], s5_lam_im[i], s5_log_dt[i], s5_b_re[i], s5_b_im[i],
                          s5_c_re[i], s5_c_im[i], valid)
        last = i == depth - 1

        u, q, k, v, ya, ssd_h, conv_tail = _inproj_ssd(xp, w_pack, lw, nb, tm_p)
        sre, sim = _s5_state(u, s5w["bst_re"], s5w["bst_im"], rt_p)
        hre, him, fre, fim = _s5_scan(sre, sim, s5w["aq_re"], s5w["aq_im"], nb, rb_p)
        yb = _s5_out(u, hre, him, s5w["toep"], s5w["cst_re"], s5w["cst_im"], rt_p)
        yc = _attn_prompt(q, k, v, band, attn_sinks[i], nb, seq // WINDOW, attn_blk)
        x1 = _merge(xp, ya, yb, u, yc, lw, tm_p, False, True)
        xp = _ffn(x1, lw, fnw, tm_p, last)
        new_p.append((
            ssd_h.reshape(nb, A_HEADS, A_HEAD_DIM, A_STATE),
            conv_tail[:, SUBLANE - (A_CONV - 1):],
            fre.reshape(nb, B_GROUPS, B_STATE), fim.reshape(nb, B_GROUPS, B_STATE),
            k.reshape(nb, seq, C_KV_WIDTH)[:, seq - WINDOW:].reshape(nb, WINDOW, C_KV_HEADS, C_HEAD_DIM),
            v.reshape(nb, seq, C_KV_WIDTH)[:, seq - WINDOW:].reshape(nb, WINDOW, C_KV_HEADS, C_HEAD_DIM)))

        z, xbc, u, q, k, v, dt = _inproj(xs, lw["norm1_w"], w_pack, tm_s)
        ya, conv_new, ssd_new = _ssd_sample(xbc, z, dt, conv_all, ssd_all, lw, i, valid, conv_new, ssd_new)
        av_re = jnp.broadcast_to(s5w["av_re"].reshape(S5_STATE_COLS, 1), (S5_STATE_COLS, nsb))
        av_im = jnp.broadcast_to(s5w["av_im"].reshape(S5_STATE_COLS, 1), (S5_STATE_COLS, nsb))
        yb, fre, fim = _s5_sample(u, s5w, s5re_all, s5im_all, av_re, av_im, i, valid)
        yc, k_new, v_new = _attn_sample(
            q.reshape(valid, nsb, C_WIDTH), k.reshape(valid, nsb, C_KV_WIDTH), v.reshape(valid, nsb, C_KV_WIDTH),
            kc_all, vc_all, bias_s1, bias_s2, sink_rows, i, k_new, v_new)
        x1 = _merge(xs, ya, yb, u, yc.reshape(ts, C_WIDTH), lw, tm_s, True, False)
        xs = _ffn(x1, lw, fnw, tm_s, last)
        s5_s.append((fre, fim))

    def stack(states, j):
        return jnp.stack([s[j] for s in states], axis=0)

    def s5_state(j):
        return jnp.transpose(stack(s5_s, j).reshape(depth, B_GROUPS, B_STATE, nsb), (0, 3, 1, 2))

    def cache(a):
        return jnp.transpose(a.reshape(depth, nsb, C_KV_HEADS, C_HEAD_DIM, lb), (0, 1, 4, 2, 3))

    y_prompt = xp.reshape(nb, seq, D_MODEL)
    y_sample = jnp.swapaxes(xs.reshape(valid, nsb, D_MODEL), 0, 1)
    return (y_prompt, y_sample,
            stack(new_p, 0), stack(new_p, 1), stack(new_p, 2), stack(new_p, 3), stack(new_p, 4), stack(new_p, 5),
            jnp.transpose(ssd_new, (0, 4, 1, 2, 3)), jnp.swapaxes(conv_new, 1, 2), s5_state(0), s5_state(1),
            cache(k_new), cache(v_new))
```

```python
import functools
import math

import numpy as np
import jax
import jax.numpy as jnp
from jax import lax
from jax.experimental import pallas as pl
from jax.experimental.pallas import tpu as pltpu

F32 = jnp.float32
BF16 = jnp.bfloat16
HIGHEST = lax.Precision.HIGHEST

D_MODEL = 1024
A_HEAD_DIM = 64
A_WIDTH = 512
A_HEADS = 8
A_GROUPS = 2
A_STATE = 64
A_CONV = 4
A_CONV_DIM = 768
A_CHUNK = 128
B_CH = 16
B_WIDTH = 512
B_GROUPS = 32
B_STATE = 64
C_HEAD_DIM = 64
C_WIDTH = 512
C_HEADS = 8
C_KV_HEADS = 2
C_GQA = 4
C_KV_WIDTH = 128
WINDOW = 128
REL_BUCKETS = 32
REL_MAX_DIST = 128
D_FF = 2816
EPS = 1e-6

P_Z = 0
P_XBC = P_Z + A_WIDTH
P_U = P_XBC + A_CONV_DIM
P_Q = P_U + B_WIDTH
P_K = P_Q + C_WIDTH
P_V = P_K + C_KV_WIDTH
P_DT = P_V + C_KV_WIDTH
P_COLS = P_DT + 128

LANE = 128
SUBLANE = 8
S5_Q = 8
S5_OCT = 8
N_OCT = B_GROUPS // S5_OCT
S5_STATE_COLS = B_GROUPS * B_STATE
NEG = -1e30
VMEM_LIMIT = 56 * 1024 * 1024
FF_CHUNK = 256


def _cparams(*sem):
    return pltpu.CompilerParams(dimension_semantics=sem, vmem_limit_bytes=VMEM_LIMIT)


def _rms(x, w):
    return x * lax.rsqrt(jnp.mean(x * x, axis=-1, keepdims=True) + EPS) * w


def _dot(a, b):
    return jnp.dot(a, b, preferred_element_type=F32)


def _dot_nt(a, b, precision=None):
    return lax.dot_general(a, b, (((1,), (1,)), ((), ())), preferred_element_type=F32, precision=precision)


def _dot_tn(a, b):
    return lax.dot_general(a, b, (((0,), (0,)), ((), ())), preferred_element_type=F32)


def _full(shape):
    n = len(shape)
    return pl.BlockSpec(shape, lambda *_: (0,) * n)


def _layer_weight(shape, layer):
    return pl.BlockSpec((None,) + shape, lambda *_: (layer,) + (0,) * len(shape))


def _tile_major_spec(rows):
    return pl.BlockSpec((N_OCT, rows, LANE), lambda i: (0, i, 0))


def _lanes(ref):
    return jnp.concatenate([ref[k8] for k8 in range(N_OCT)], axis=1)


def _inproj_body(x_ref, nw_ref, w_ref, z_ref, xbc_ref, u_ref, q_ref, k_ref, v_ref, dt_ref):
    h = _rms(x_ref[...], nw_ref[...]).astype(BF16)

    def proj(lo, hi):
        return _dot(h, w_ref[:, lo:hi])

    z_ref[...] = proj(P_Z, P_XBC).astype(BF16)
    xbc_ref[...] = proj(P_XBC, P_U)
    u = proj(P_U, P_Q)
    for k8 in range(N_OCT):
        u_ref[k8] = u[:, k8 * LANE:(k8 + 1) * LANE]
    q_ref[...] = proj(P_Q, P_K).astype(BF16)
    kvdt = proj(P_K, P_COLS)
    k_ref[...] = kvdt[:, :C_KV_WIDTH]
    v_ref[...] = kvdt[:, C_KV_WIDTH:2 * C_KV_WIDTH]
    dt_ref[...] = kvdt[:, 2 * C_KV_WIDTH:]


def _inproj(x, nw, w, layer, tm):
    t = x.shape[0]
    widths = [(A_WIDTH, BF16), (A_CONV_DIM, F32), None, (C_WIDTH, BF16),
              (C_KV_WIDTH, F32), (C_KV_WIDTH, F32), (LANE, F32)]
    spec = lambda w: _tile_major_spec(tm) if w is None else pl.BlockSpec((tm, w[0]), lambda i: (i, 0))
    shape = lambda w: (jax.ShapeDtypeStruct((N_OCT, t, LANE), F32) if w is None
                       else jax.ShapeDtypeStruct((t, w[0]), w[1]))
    return pl.pallas_call(
        _inproj_body,
        grid=(t // tm,),
        in_specs=[pl.BlockSpec((tm, D_MODEL), lambda i: (i, 0)), _full((1, D_MODEL)),
                  _layer_weight((D_MODEL, P_COLS), layer)],
        out_specs=[spec(w) for w in widths],
        out_shape=[shape(w) for w in widths],
        compiler_params=_cparams("parallel"),
        name="inproj",
    )(x, nw, w)


def _ssd_chunk(xbc, z, dt_raw, cw_ref, cb_ref, alog_ref, dtb_ref, dexp_ref, nw_ref, h_scr, tail_scr):
    q = xbc.shape[0]
    tail_scr[SUBLANE:SUBLANE + q] = xbc
    acc = cb_ref[...] + cw_ref[A_CONV - 1:A_CONV, :] * xbc
    for j in range(1, A_CONV):
        acc = acc + cw_ref[A_CONV - 1 - j:A_CONV - j, :] * tail_scr[SUBLANE - j:SUBLANE - j + q]
    tail_scr[0:SUBLANE] = xbc[q - SUBLANE:q]
    xc = jax.nn.silu(acc)
    x = xc[:, :A_WIDTH]
    bm = xc[:, A_WIDTH:A_WIDTH + A_GROUPS * A_STATE].astype(BF16)
    cm = xc[:, A_WIDTH + A_GROUPS * A_STATE:].astype(BF16)

    dt = jax.nn.softplus(dt_raw + dtb_ref[...])
    dta = dt * (-jnp.exp(alog_ref[...]))
    row = lax.broadcasted_iota(jnp.int32, (q, q), 0)
    col = lax.broadcasted_iota(jnp.int32, (q, q), 1)
    tri = row >= col
    a_cum = jnp.dot(tri.astype(F32), dta, preferred_element_type=F32, precision=HIGHEST)

    def expand(v):
        return jnp.concatenate([jnp.broadcast_to(v[:, k:k + 1], (q, A_HEAD_DIM)) for k in range(A_HEADS)], axis=1)

    a_cum_e = expand(a_cum)
    xs = x * expand(dt)
    xs_b = xs.astype(BF16)
    a_cum_t = a_cum.T

    h = h_scr[...]
    h_b = h.astype(BF16)
    hpg = A_HEADS // A_GROUPS
    gw = hpg * A_HEAD_DIM
    y_diag, y_off = [], []
    for g in range(A_GROUPS):
        cg = cm[:, g * A_STATE:(g + 1) * A_STATE]
        bg = bm[:, g * A_STATE:(g + 1) * A_STATE]
        cb = _dot_nt(cg, bg)
        for kk in range(hpg):
            k = g * hpg + kk
            seg = a_cum[:, k:k + 1] - a_cum_t[k:k + 1, :]
            decay = jnp.exp(jnp.where(tri, seg, -jnp.inf))
            y_diag.append(_dot((cb * decay).astype(BF16), xs_b[:, k * A_HEAD_DIM:(k + 1) * A_HEAD_DIM]))
        y_off.append(_dot_nt(cg, h_b[g * gw:(g + 1) * gw]))
    y = (jnp.concatenate(y_diag, axis=1) + jnp.concatenate(y_off, axis=1) * jnp.exp(a_cum_e)
         + dexp_ref[...] * x)
    y = y * jax.nn.silu(z)

    last = a_cum_e[q - 1:q, :]
    xs_dec = (xs * jnp.exp(last - a_cum_e)).astype(BF16)
    dec_col = jnp.exp(a_cum_e.T[:, q - 1:q])
    upd = [_dot_tn(xs_dec[:, g * gw:(g + 1) * gw], bm[:, g * A_STATE:(g + 1) * A_STATE]) for g in range(A_GROUPS)]
    h_scr[...] = dec_col * h + jnp.concatenate(upd, axis=0)
    return _rms(y, nw_ref[...]).astype(BF16)


def _inproj_ssd_body(x_ref, nw_ref, w_ref, cw_ref, cb_ref, alog_ref, dtb_ref, dexp_ref, snw_ref,
                     u_ref, q_ref, k_ref, v_ref, ya_ref, hout_ref, tail_ref, h_scr, tail_scr, *, nsteps, chunk):
    c = pl.program_id(1)

    @pl.when(c == 0)
    def _():
        h_scr[...] = jnp.zeros_like(h_scr)
        tail_scr[0:SUBLANE] = jnp.zeros((SUBLANE, A_CONV_DIM), F32)

    h = _rms(x_ref[...], nw_ref[...]).astype(BF16)

    def proj(lo, hi):
        return _dot(h, w_ref[:, lo:hi])

    z = proj(P_Z, P_XBC)
    xbc = proj(P_XBC, P_U)
    kvdt = proj(P_K, P_COLS)
    k_ref[...] = kvdt[:, :C_KV_WIDTH]
    v_ref[...] = kvdt[:, C_KV_WIDTH:2 * C_KV_WIDTH]
    dt = kvdt[:, 2 * C_KV_WIDTH:]
    for i in range(x_ref.shape[0] // chunk):
        rows = slice(i * chunk, (i + 1) * chunk)
        ya_ref[rows, :] = _ssd_chunk(xbc[rows], z[rows], dt[rows], cw_ref, cb_ref, alog_ref, dtb_ref, dexp_ref,
                                     snw_ref, h_scr, tail_scr)
    u = proj(P_U, P_Q)
    for k8 in range(N_OCT):
        u_ref[k8] = u[:, k8 * LANE:(k8 + 1) * LANE]
    q_ref[...] = proj(P_Q, P_K).astype(BF16)

    @pl.when(c == nsteps - 1)
    def _():
        hout_ref[0] = h_scr[...]
        tail_ref[0] = tail_scr[0:SUBLANE]


def _inproj_ssd(x, w, lw, nb, tm):
    t = x.shape[0]
    nsteps = t // nb // tm
    rmap = lambda b, c: (b * nsteps + c, 0)
    rows = lambda wd: pl.BlockSpec((tm, wd), rmap)
    per_seq = lambda s: pl.BlockSpec((1,) + s, lambda b, c: (b, 0, 0))
    return pl.pallas_call(
        functools.partial(_inproj_ssd_body, nsteps=nsteps, chunk=A_CHUNK),
        grid=(nb, nsteps),
        in_specs=[rows(D_MODEL), _full((1, D_MODEL)), _layer_weight((D_MODEL, P_COLS), lw["layer"]),
                  _full((A_CONV, A_CONV_DIM)), _full((1, A_CONV_DIM)), _full((1, LANE)), _full((1, LANE)),
                  _full((1, A_WIDTH)), _full((1, A_WIDTH))],
        out_specs=[pl.BlockSpec((N_OCT, tm, LANE), lambda b, c: (0, b * nsteps + c, 0)), rows(C_WIDTH),
                   rows(C_KV_WIDTH), rows(C_KV_WIDTH), rows(A_WIDTH),
                   per_seq((A_WIDTH, A_STATE)), per_seq((SUBLANE, A_CONV_DIM))],
        out_shape=[jax.ShapeDtypeStruct((N_OCT, t, LANE), F32), jax.ShapeDtypeStruct((t, C_WIDTH), BF16),
                   jax.ShapeDtypeStruct((t, C_KV_WIDTH), F32), jax.ShapeDtypeStruct((t, C_KV_WIDTH), F32),
                   jax.ShapeDtypeStruct((t, A_WIDTH), BF16), jax.ShapeDtypeStruct((nb, A_WIDTH, A_STATE), F32),
                   jax.ShapeDtypeStruct((nb, SUBLANE, A_CONV_DIM), F32)],
        scratch_shapes=[pltpu.VMEM((A_WIDTH, A_STATE), F32), pltpu.VMEM((SUBLANE + A_CHUNK, A_CONV_DIM), F32)],
        compiler_params=_cparams("parallel", "arbitrary"),
        name="inproj_ssd",
    )(x, lw["norm1_w"], w, lw["conv_w"], lw["conv_b"], lw["a_log"], lw["dt_bias"], lw["d_exp"], lw["ssd_norm_w"])


def _ssd_sample_body(*refs, valid):
    n_in = 13
    (xbc_ref, z_ref, dt_ref, conv0_ref, h0_ref, cw_ref, cb_ref, alog_ref, dtb_ref, dexp_ref, nwb_ref) = refs[:11]
    y_ref, conv_out_ref, h_out_ref = refs[n_in:n_in + 3]
    xs_scr, dx_scr, z_scr, y_scr, bt_scr, ct_scr, da_scr = refs[n_in + 3:]
    k = pl.program_id(0)
    ns = LANE

    @pl.when(k == 0)
    def _():
        rows = [conv0_ref[0, r] for r in range(A_CONV - 1)] + [xbc_ref[t * ns:(t + 1) * ns, :] for t in range(valid)]
        for t in range(valid):
            acc = cb_ref[...]
            for j in range(A_CONV):
                acc = acc + cw_ref[A_CONV - 1 - j:A_CONV - j, :] * rows[t + A_CONV - 1 - j]
            xc = jax.nn.silu(acc)
            dt = jax.nn.softplus(dt_ref[t * ns:(t + 1) * ns, :] + dtb_ref[...])
            dt_t = dt.T
            da_scr[t] = jnp.exp((dt * (-jnp.exp(alog_ref[...]))).T)
            x = xc[:, :A_WIDTH]
            x_t = x.T
            for kk in range(A_HEADS):
                hs = slice(kk * A_HEAD_DIM, (kk + 1) * A_HEAD_DIM)
                xs_scr[t, hs, :] = x_t[hs] * dt_t[kk:kk + 1, :]
            dx_scr[t] = (x * dexp_ref[...]).T
            bt_scr[t] = xc[:, A_WIDTH:A_WIDTH + A_GROUPS * A_STATE].T
            ct_scr[t] = xc[:, A_WIDTH + A_GROUPS * A_STATE:].T
            z_scr[t] = jax.nn.silu(z_ref[t * ns:(t + 1) * ns, :].astype(F32)).T
        for r in range(A_CONV - 1):
            conv_out_ref[0, r] = rows[valid + r]

    g0 = pl.multiple_of((k // (A_HEADS // A_GROUPS)) * A_STATE, A_STATE)

    def per_p(p, carry):
        h = h0_ref[0, 0, p]
        row = k * A_HEAD_DIM + p
        for t in range(valid):
            da = da_scr[t, pl.ds(k, 1), :]
            xr = xs_scr[t, pl.ds(row, 1), :]
            h = da * h + xr * bt_scr[t, pl.ds(g0, A_STATE), :]
            y_scr[t, pl.ds(row, 1), :] = jnp.sum(h * ct_scr[t, pl.ds(g0, A_STATE), :], axis=0, keepdims=True)
        h_out_ref[0, 0, p] = h
        return carry

    lax.fori_loop(0, A_HEAD_DIM, per_p, 0, unroll=2)

    @pl.when(k == A_HEADS - 1)
    def _():
        for t in range(valid):
            y = (y_scr[t] + dx_scr[t]) * z_scr[t]
            y = y * lax.rsqrt(jnp.mean(y * y, axis=0, keepdims=True) + EPS) * nwb_ref[...]
            y_ref[:, t * ns:(t + 1) * ns] = y.astype(BF16)


def _ssd_sample(xbc, z, dt, conv_all, h_all, lw, layer, valid, conv_prev, h_prev):
    rows = valid * LANE
    hblk = pl.BlockSpec((1, 1, A_HEAD_DIM, A_STATE, LANE), lambda k: (layer, k, 0, 0, 0))
    cblk = pl.BlockSpec((1, A_CONV - 1, LANE, A_CONV_DIM), lambda k: (layer, 0, 0, 0))
    in_specs = [_full((rows, A_CONV_DIM)), _full((rows, A_WIDTH)), _full((rows, LANE)), cblk, hblk,
                _full((A_CONV, A_CONV_DIM)), _full((1, A_CONV_DIM)), _full((1, LANE)), _full((1, LANE)),
                _full((1, A_WIDTH)), _full((A_WIDTH, LANE))] + [pl.BlockSpec(memory_space=pl.ANY)] * 2
    args = [xbc, z, dt, conv_all, h_all, lw["conv_w"], lw["conv_b"], lw["a_log"], lw["dt_bias"], lw["d_exp"],
            lw["ssd_norm_wb"], conv_prev, h_prev]
    aliases = {len(args) - 2: 1, len(args) - 1: 2}
    big = lambda n: pltpu.VMEM((valid, n, LANE), F32)
    return pl.pallas_call(
        functools.partial(_ssd_sample_body, valid=valid),
        grid=(A_HEADS,),
        in_specs=in_specs,
        out_specs=[_full((A_WIDTH, rows)), cblk, hblk],
        out_shape=[jax.ShapeDtypeStruct((A_WIDTH, rows), BF16), jax.ShapeDtypeStruct(conv_all.shape, F32),
                   jax.ShapeDtypeStruct(h_all.shape, F32)],
        scratch_shapes=[big(A_WIDTH), big(A_WIDTH), big(A_WIDTH), big(A_WIDTH), big(LANE), big(LANE), big(LANE)],
        input_output_aliases=aliases,
        compiler_params=_cparams("arbitrary"),
        name="ssd_sample",
    )(*args)


def _s5w_body(lre_ref, lim_ref, ldt_ref, btr_ref, bti_ref, cr_ref, ci_ref,
              bre_ref, bim_ref, bsre_ref, bsim_ref, cre_ref, cim_ref, toep_ref, ap_ref, cd_scr, bb_scr, *, valid):
    h, q = B_STATE, S5_Q
    for ref in (bre_ref, bim_ref, bsre_ref, bsim_ref, cre_ref, cim_ref, toep_ref):
        ref[...] = jnp.zeros(ref.shape, BF16)
    r_i = lax.broadcasted_iota(jnp.int32, (q * B_CH, q * LANE), 0)
    c_i = lax.broadcasted_iota(jnp.int32, (q * B_CH, q * LANE), 1)
    d = lax.broadcasted_iota(jnp.int32, (2 * SUBLANE, h), 0).astype(F32)
    for g8 in range(S5_OCT):
        lr, li = lre_ref[g8], lim_ref[g8]
        step = jnp.exp(ldt_ref[g8])
        er = jnp.exp(lr * step * d)
        ang = li * step * d
        ar, ai = er * jnp.cos(ang), er * jnp.sin(ang)
        nr, ni = ar[1:2] - 1.0, ai[1:2]
        den = lr * lr + li * li
        wr, wi = (nr * lr + ni * li) / den, (ni * lr - nr * li) / den
        btr, bti = btr_ref[g8], bti_ref[g8]
        bbr, bbi = wr * btr - wi * bti, wr * bti + wi * btr
        cr, ci = cr_ref[g8], ci_ref[g8]
        bb_scr[:, 0:h] = bbr
        bb_scr[:, h:2 * h] = bbi
        cols = slice(g8 * h, (g8 + 1) * h)
        for s in range(q):
            rows = slice(s * LANE + g8 * B_CH, s * LANE + (g8 + 1) * B_CH)
            e = q - 1 - s
            bre_ref[0, rows, cols] = (ar[e:e + 1] * bbr - ai[e:e + 1] * bbi).astype(BF16)
            bim_ref[0, rows, cols] = (ai[e:e + 1] * bbr + ar[e:e + 1] * bbi).astype(BF16)
            if s < valid:
                e = valid - 1 - s
                bsre_ref[0, rows, cols] = (ar[e:e + 1] * bbr - ai[e:e + 1] * bbi).astype(BF16)
                bsim_ref[0, rows, cols] = (ai[e:e + 1] * bbr + ar[e:e + 1] * bbi).astype(BF16)
            cre_ref[0, rows, cols] = (cr * ar[s + 1:s + 2] - ci * ai[s + 1:s + 2]).astype(BF16)
            cim_ref[0, rows, cols] = (-(cr * ai[s + 1:s + 2] + ci * ar[s + 1:s + 2])).astype(BF16)
            cd_scr[s * B_CH:(s + 1) * B_CH, 0:h] = cr * ar[s:s + 1] - ci * ai[s:s + 1]
            cd_scr[s * B_CH:(s + 1) * B_CH, h:2 * h] = -(cr * ai[s:s + 1] + ci * ar[s:s + 1])
        kdt = _dot_nt(bb_scr[...], cd_scr[...], precision=HIGHEST)
        place = (c_i == (r_i >> 4) * LANE + g8 * B_CH + (r_i & (B_CH - 1))).astype(F32)
        slab = jnp.dot(kdt, place, preferred_element_type=F32, precision=HIGHEST).astype(BF16)
        for s in range(q):
            toep_ref[0, s * LANE + g8 * B_CH:s * LANE + (g8 + 1) * B_CH, s * LANE:] = slab[:, 0:(q - s) * LANE]
        ap_ref[g8, 0:1, 0:h] = ar[q:q + 1]
        ap_ref[g8, 0:1, h:2 * h] = ai[q:q + 1]
        ap_ref[g8, 1:2, 0:h] = ar[valid:valid + 1]
        ap_ref[g8, 1:2, h:2 * h] = ai[valid:valid + 1]


def _s5_weights(lam_re, lam_im, log_dt, b_re, b_im, c_re, c_im, valid):
    g, h, q = B_GROUPS, B_STATE, S5_Q
    vec = pl.BlockSpec((S5_OCT, 1, h), lambda i: (i, 0, 0))
    mat = pl.BlockSpec((S5_OCT, B_CH, h), lambda i: (i, 0, 0))
    st = pl.BlockSpec((1, q * LANE, S5_OCT * h), lambda i: (i, 0, 0))
    st_shape = jax.ShapeDtypeStruct((N_OCT, q * LANE, S5_OCT * h), BF16)
    bre, bim, bsre, bsim, cre, cim, toep, ap = pl.pallas_call(
        functools.partial(_s5w_body, valid=valid),
        grid=(N_OCT,),
        in_specs=[vec, vec, vec, mat, mat, mat, mat],
        out_specs=[st] * 6 + [pl.BlockSpec((1, q * LANE, q * LANE), lambda i: (i, 0, 0)),
                              pl.BlockSpec((S5_OCT, 2, 2 * h), lambda i: (i, 0, 0))],
        out_shape=[st_shape] * 6 + [jax.ShapeDtypeStruct((N_OCT, q * LANE, q * LANE), BF16),
                                    jax.ShapeDtypeStruct((g, 2, 2 * h), F32)],
        scratch_shapes=[pltpu.VMEM((q * B_CH, 2 * h), F32), pltpu.VMEM((B_CH, 2 * h), F32)],
        compiler_params=_cparams("parallel"),
        name="s5_weights",
    )(lam_re.reshape(g, 1, h), lam_im.reshape(g, 1, h), jnp.broadcast_to(log_dt.reshape(g, 1, 1), (g, 1, h)),
      jnp.swapaxes(b_re, 1, 2), jnp.swapaxes(b_im, 1, 2), c_re, c_im)
    return dict(
        bst_re=bre, bst_im=bim, bsts_re=bsre, bsts_im=bsim, cst_re=cre, cst_im=cim, toep=toep,
        aq_re=ap[:, 0, :h].reshape(1, S5_STATE_COLS), aq_im=ap[:, 0, h:].reshape(1, S5_STATE_COLS),
        av_re=ap[:, 1, :h].reshape(1, S5_STATE_COLS), av_im=ap[:, 1, h:].reshape(1, S5_STATE_COLS),
    )


def _oct_rows(u_ref, k8, rt):
    return jnp.concatenate([u_ref[k8, pl.ds(j, rt, stride=S5_Q), :] for j in range(S5_Q)], axis=1).astype(BF16)


def _toep_dot(uo, toep_ref, k8):
    two = 2 * LANE
    cols = [_dot(uo[:, :(t + 1) * two], toep_ref[k8, :(t + 1) * two, t * two:(t + 1) * two])
            for t in range(uo.shape[1] // two)]
    return jnp.concatenate(cols, axis=1)


def _s5_state_body(u_ref, bre_ref, bim_ref, sre_ref, sim_ref):
    w = S5_OCT * B_STATE
    for k8 in range(N_OCT):
        uo = _oct_rows(u_ref, k8, sre_ref.shape[0])
        sre_ref[:, k8 * w:(k8 + 1) * w] = _dot(uo, bre_ref[k8])
        sim_ref[:, k8 * w:(k8 + 1) * w] = _dot(uo, bim_ref[k8])


def _s5_state(u, bre, bim, rt):
    r = u.shape[1] // S5_Q
    wspec = _full((N_OCT, S5_Q * LANE, S5_OCT * B_STATE))
    ospec = pl.BlockSpec((rt, S5_STATE_COLS), lambda i: (i, 0))
    return pl.pallas_call(
        _s5_state_body,
        grid=(r // rt,),
        in_specs=[_tile_major_spec(rt * S5_Q), wspec, wspec],
        out_specs=[ospec, ospec],
        out_shape=[jax.ShapeDtypeStruct((r, S5_STATE_COLS), F32)] * 2,
        compiler_params=_cparams("parallel"),
        name="s5_state",
    )(u, bre, bim)


def _s5_scan_body(sre_ref, sim_ref, ar_ref, ai_ref, hre_ref, him_ref, fre_ref, fim_ref, cr_scr, ci_scr, *, rb, nblk):
    i = pl.program_id(0)
    nb = sre_ref.shape[0]

    @pl.when(i == 0)
    def _():
        cr_scr[...] = jnp.zeros_like(cr_scr)
        ci_scr[...] = jnp.zeros_like(ci_scr)

    ar, ai = ar_ref[...], ai_ref[...]

    def step(r, carry):
        out = []
        for b in range(nb):
            hr, hi = carry[2 * b], carry[2 * b + 1]
            hre_ref[b, pl.ds(r, 1), :] = hr
            him_ref[b, pl.ds(r, 1), :] = hi
            sr, si = sre_ref[b, pl.ds(r, 1), :], sim_ref[b, pl.ds(r, 1), :]
            out += [ar * hr - ai * hi + sr, ai * hr + ar * hi + si]
        return tuple(out)

    init = tuple(s[b] for b in range(nb) for s in (cr_scr, ci_scr))
    fin = lax.fori_loop(0, rb, step, init)
    for b in range(nb):
        cr_scr[b] = fin[2 * b]
        ci_scr[b] = fin[2 * b + 1]

    @pl.when(i == nblk - 1)
    def _():
        for b in range(nb):
            fre_ref[b] = fin[2 * b]
            fim_ref[b] = fin[2 * b + 1]


def _s5_scan(sre, sim, ar, ai, nb, rb):
    r = sre.shape[0] // nb
    nblk = r // rb
    blk = pl.BlockSpec((nb, rb, S5_STATE_COLS), lambda i: (0, i, 0))
    vec = _full((1, S5_STATE_COLS))
    fin = _full((nb, 1, S5_STATE_COLS))
    per_batch = lambda a: a.reshape(nb, r, S5_STATE_COLS)
    hre, him, fre, fim = pl.pallas_call(
        functools.partial(_s5_scan_body, rb=rb, nblk=nblk),
        grid=(nblk,),
        in_specs=[blk, blk, vec, vec],
        out_specs=[blk, blk, fin, fin],
        out_shape=[jax.ShapeDtypeStruct((nb, r, S5_STATE_COLS), F32)] * 2
        + [jax.ShapeDtypeStruct((nb, 1, S5_STATE_COLS), F32)] * 2,
        scratch_shapes=[pltpu.VMEM((nb, 1, S5_STATE_COLS), F32)] * 2,
        compiler_params=_cparams("arbitrary"),
        name="s5_scan",
    )(per_batch(sre), per_batch(sim), ar, ai)
    return hre.reshape(nb * r, S5_STATE_COLS), him.reshape(nb * r, S5_STATE_COLS), fre, fim


def _s5_sample_body(u_ref, toep_ref, cre_ref, cim_ref, bre_ref, bim_ref, hre_ref, him_ref, ar_ref, ai_ref,
                    y_ref, fre_ref, fim_ref, *, valid):
    w = S5_OCT * B_STATE
    ns = LANE
    for k8 in range(N_OCT):
        blocks = [u_ref[k8, t * ns:(t + 1) * ns, :] for t in range(valid)]
        uo = jnp.concatenate(blocks, axis=1).astype(BF16)
        uo_t = jnp.concatenate([b.T for b in blocks], axis=0).astype(BF16)
        hr, hi = hre_ref[0, k8 * w:(k8 + 1) * w, :], him_ref[0, k8 * w:(k8 + 1) * w, :]
        yo = (_toep_dot(uo, toep_ref, k8) + _dot_nt(hr.T.astype(BF16), cre_ref[k8])
              + _dot_nt(hi.T.astype(BF16), cim_ref[k8]))
        for t in range(valid):
            y_ref[k8, t * ns:(t + 1) * ns, :] = yo[:, t * LANE:(t + 1) * LANE]
        ar, ai = ar_ref[k8 * w:(k8 + 1) * w, :], ai_ref[k8 * w:(k8 + 1) * w, :]
        fre_ref[k8 * w:(k8 + 1) * w, :] = ar * hr - ai * hi + _dot_tn(bre_ref[k8], uo_t)
        fim_ref[k8 * w:(k8 + 1) * w, :] = ai * hr + ar * hi + _dot_tn(bim_ref[k8], uo_t)


def _s5_sample(u, s5w, hre_all, him_all, ar, ai, layer, valid):
    rows = valid * LANE
    vq = valid * LANE
    sub = lambda c: pl.BlockSpec((N_OCT, vq, c), lambda i: (0, 0, 0))
    hblk = pl.BlockSpec((1, S5_STATE_COLS, LANE), lambda i: (layer, 0, 0))
    st = _full((S5_STATE_COLS, LANE))
    return pl.pallas_call(
        functools.partial(_s5_sample_body, valid=valid),
        grid=(1,),
        in_specs=[_tile_major_spec(rows), sub(vq), sub(S5_OCT * B_STATE), sub(S5_OCT * B_STATE),
                  sub(S5_OCT * B_STATE), sub(S5_OCT * B_STATE), hblk, hblk, st, st],
        out_specs=[_tile_major_spec(rows), st, st],
        out_shape=[jax.ShapeDtypeStruct((N_OCT, rows, LANE), F32)] + [jax.ShapeDtypeStruct((S5_STATE_COLS, LANE), F32)] * 2,
        compiler_params=_cparams("arbitrary"),
        name="s5_sample",
    )(u, s5w["toep"], s5w["cst_re"], s5w["cst_im"], s5w["bsts_re"], s5w["bsts_im"], hre_all, him_all, ar, ai)


def _s5_out_body(u_ref, hre_ref, him_ref, toep_ref, cre_ref, cim_ref, y_ref):
    w = S5_OCT * B_STATE
    rt = hre_ref.shape[0]
    for k8 in range(N_OCT):
        uo = _oct_rows(u_ref, k8, rt)
        yo = (_toep_dot(uo, toep_ref, k8)
              + _dot_nt(hre_ref[:, k8 * w:(k8 + 1) * w].astype(BF16), cre_ref[k8])
              + _dot_nt(him_ref[:, k8 * w:(k8 + 1) * w].astype(BF16), cim_ref[k8]))
        for j in range(S5_Q):
            y_ref[k8, pl.ds(j, rt, stride=S5_Q), :] = yo[:, j * LANE:(j + 1) * LANE]


def _s5_out(u, hre, him, toep, cre, cim, rt):
    r = u.shape[1] // S5_Q
    hblk = pl.BlockSpec((rt, S5_STATE_COLS), lambda i: (i, 0))
    return pl.pallas_call(
        _s5_out_body,
        grid=(r // rt,),
        in_specs=[_tile_major_spec(rt * S5_Q), hblk, hblk, _full((N_OCT, S5_Q * LANE, S5_Q * LANE)),
                  _full((N_OCT, S5_Q * LANE, S5_OCT * B_STATE)), _full((N_OCT, S5_Q * LANE, S5_OCT * B_STATE))],
        out_specs=_tile_major_spec(rt * S5_Q),
        out_shape=jax.ShapeDtypeStruct(u.shape, F32),
        compiler_params=_cparams("parallel"),
        name="s5_out",
    )(u, hre, him, toep, cre, cim)


def _band_body(bucket_ref, table_ref, o_ref):
    bucket = bucket_ref[...]
    col = lax.broadcasted_iota(jnp.int32, (WINDOW, 2 * WINDOW), 1)
    for hq in range(C_HEADS):
        base = jnp.full((1, 2 * WINDOW), NEG, F32)
        for b in range(REL_BUCKETS):
            base = jnp.where(bucket == b, table_ref[b, hq], base)
        tile = pltpu.roll(jnp.broadcast_to(base, (WINDOW, 2 * WINDOW)), 0, 1, stride=1, stride_axis=0)
        o_ref[1, hq] = tile.T
        o_ref[0, hq] = jnp.where(col >= WINDOW, tile, NEG).T


def _bias_band(table):
    lane = np.arange(2 * WINDOW)
    n = WINDOW - lane
    exact = REL_BUCKETS // 2
    nf = np.maximum(n, 1).astype(np.float32)
    large = exact + (np.log(nf / np.float32(exact)) / np.float32(math.log(REL_MAX_DIST / exact))
                     * np.float32(REL_BUCKETS - exact)).astype(np.int32)
    bucket = np.where(n < exact, n, np.minimum(large, REL_BUCKETS - 1))
    bucket = np.where((n >= 0) & (n < WINDOW), bucket, -1).astype(np.int32)
    return pl.pallas_call(
        _band_body,
        in_specs=[pl.BlockSpec(memory_space=pltpu.VMEM), pl.BlockSpec(memory_space=pltpu.SMEM)],
        out_specs=pl.BlockSpec(memory_space=pltpu.VMEM),
        out_shape=jax.ShapeDtypeStruct((2, C_HEADS, 2 * WINDOW, WINDOW), F32),
        name="rel_bias_band",
    )(jnp.asarray(bucket).reshape(1, 2 * WINDOW), table)


def _attn_prompt_body(q_ref, kc_ref, kp_ref, vc_ref, vp_ref, b0_ref, br_ref, sink_ref, o_ref, *, nblk):
    k_all = jnp.concatenate([kp_ref[...], kc_ref[...]], axis=0) * (C_HEAD_DIM ** -0.5)
    v_all = jnp.concatenate([vp_ref[...], vc_ref[...]], axis=0)
    swapped = (pltpu.roll(k_all, C_HEAD_DIM, 1), pltpu.roll(v_all, C_HEAD_DIM, 1))
    low = lax.broadcasted_iota(jnp.int32, k_all.shape, 1) < C_HEAD_DIM
    for hk in range(C_KV_HEADS):
        k_lo = k_all if hk == 0 else swapped[0]
        k_hi = swapped[0] if hk == 0 else k_all
        kz = (jnp.where(low, k_lo, 0.0).astype(BF16), jnp.where(low, 0.0, k_hi).astype(BF16))
        vz = (v_all if hk == 0 else swapped[1])[:, :C_HEAD_DIM].astype(BF16)
        for g in range(C_GQA):
            hq = hk * C_GQA + g
            sk = sink_ref[hq]
            for i in range(nblk):
                keys = slice(i * WINDOW, (i + 2) * WINDOW)
                qt = q_ref[i * WINDOW:(i + 1) * WINDOW, (hq // 2) * LANE:(hq // 2 + 1) * LANE]
                bias = b0_ref[0, hq] if i == 0 else br_ref[0, hq]
                s = _dot_nt(kz[hq % 2][keys], qt) + bias
                m = jnp.maximum(jnp.max(s, axis=0, keepdims=True), sk)
                p = jnp.exp(s - m)
                den = jnp.sum(p, axis=0, keepdims=True) + jnp.exp(sk - m)
                o = _dot_tn(vz[keys], p.astype(BF16)) / den
                o_ref[hq * C_HEAD_DIM:(hq + 1) * C_HEAD_DIM, i * WINDOW:(i + 1) * WINDOW] = o.astype(BF16)


def _attn_prompt(q, k, v, band, sinks, nb, nblk_seq, nblk):
    steps = nblk_seq // nblk
    cur = lambda b, n: (b * steps + n, 0)
    prev = lambda b, n: (b * nblk_seq + jnp.maximum(n * nblk - 1, 0), 0)
    bspec = lambda m: pl.BlockSpec((1, C_HEADS, 2 * WINDOW, WINDOW), m)
    return pl.pallas_call(
        functools.partial(_attn_prompt_body, nblk=nblk),
        grid=(nb, steps),
        in_specs=[pl.BlockSpec((nblk * WINDOW, C_WIDTH), cur),
                  pl.BlockSpec((nblk * WINDOW, C_KV_WIDTH), cur), pl.BlockSpec((WINDOW, C_KV_WIDTH), prev),
                  pl.BlockSpec((nblk * WINDOW, C_KV_WIDTH), cur), pl.BlockSpec((WINDOW, C_KV_WIDTH), prev),
                  bspec(lambda b, n: (jnp.minimum(n, 1), 0, 0, 0)), bspec(lambda b, n: (1, 0, 0, 0)),
                  pl.BlockSpec(memory_space=pltpu.SMEM)],
        out_specs=pl.BlockSpec((C_WIDTH, nblk * WINDOW), lambda b, n: (0, b * steps + n)),
        out_shape=jax.ShapeDtypeStruct((C_WIDTH, nb * nblk_seq * WINDOW), BF16),
        compiler_params=_cparams("parallel", "arbitrary"),
        name="attn_prompt",
    )(q, k, k, v, v, band, band, sinks)


def _attn_sample_body(*refs, valid, lb, nseq):
    q_ref, kn_ref, vn_ref, kc_ref, vc_ref, b1_ref, b2_ref, sink_ref = refs[:8]
    o_ref, ko_ref, vo_ref, s1_scr, s2_scr, inv_scr, o_scr = refs[-7:]
    d = C_HEAD_DIM
    rs = 2 * valid
    lane_c = lax.broadcasted_iota(jnp.int32, (C_KV_WIDTH, lb), 1)
    lane_n = lax.broadcasted_iota(jnp.int32, (SUBLANE, C_KV_WIDTH), 1)
    pad = jnp.zeros((SUBLANE - valid, C_KV_WIDTH), F32)
    zero_half = jnp.zeros((d, lb), BF16)
    combos = [(hk, par) for hk in range(C_KV_HEADS) for par in range(2)]

    def new_rows(ref, s):
        return jnp.concatenate([ref[:, s, :], pad], axis=0)

    for s in range(nseq):
        qs = q_ref[:, s, :].astype(F32)
        kn, vn = new_rows(kn_ref, s), new_rows(vn_ref, s)
        kt, vt = kc_ref[0, s], vc_ref[0, s]
        kn_t, vn_t = kn.T, vn.T
        new_k, new_v = pltpu.roll(kt, lb - valid, 1), pltpu.roll(vt, lb - valid, 1)
        for t in range(valid):
            at = lane_c == lb - valid + t
            new_k = jnp.where(at, jnp.broadcast_to(kn_t[:, t:t + 1], (C_KV_WIDTH, lb)), new_k)
            new_v = jnp.where(at, jnp.broadcast_to(vn_t[:, t:t + 1], (C_KV_WIDTH, lb)), new_v)
        ko_ref[0, s] = new_k
        vo_ref[0, s] = new_v
        kn_sw = pltpu.roll(kn, d, 1)
        for c, (hk, par) in enumerate(combos):
            kth = kt[hk * d:(hk + 1) * d].astype(BF16)
            kz = jnp.concatenate([kth, zero_half] if par == 0 else [zero_half, kth], axis=0)
            src = kn if hk == par else kn_sw
            knz = jnp.where((lane_n < d) if par == 0 else (lane_n >= d), src, 0.0).astype(BF16)
            qq = jnp.concatenate([qs[:, (hk * 2 + j) * LANE:(hk * 2 + j + 1) * LANE] for j in range(2)],
                                 axis=0).astype(BF16)
            s1_scr[c, s * rs:(s + 1) * rs, :] = _dot(qq, kz)
            s2_scr[c, s * rs:(s + 1) * rs, :] = _dot_nt(qq, knz)

    for c, (hk, par) in enumerate(combos):
        s1 = s1_scr[c] * (d ** -0.5) + b1_ref[hk, par]
        s2 = s2_scr[c] * (d ** -0.5) + b2_ref[hk, par]
        sk = sink_ref[hk, par]
        m = jnp.maximum(jnp.maximum(jnp.max(s1, axis=-1, keepdims=True), jnp.max(s2, axis=-1, keepdims=True)), sk)
        p1, p2 = jnp.exp(s1 - m), jnp.exp(s2 - m)
        den = jnp.sum(p1, axis=-1, keepdims=True) + jnp.sum(p2, axis=-1, keepdims=True) + jnp.exp(sk - m)
        s1_scr[c] = p1
        s2_scr[c] = p2
        inv_scr[c] = jnp.broadcast_to(1.0 / den, (nseq * rs, SUBLANE))

    for s in range(nseq):
        rows = slice(s * rs, (s + 1) * rs)
        vt = vc_ref[0, s]
        vn = new_rows(vn_ref, s)
        for c, (hk, par) in enumerate(combos):
            vth = vt[hk * d:(hk + 1) * d].astype(BF16)
            vnh = vn[:, hk * d:(hk + 1) * d].astype(BF16)
            o = _dot_nt(s1_scr[c, rows, :].astype(BF16), vth) + _dot(s2_scr[c, rows, :].astype(BF16), vnh)
            o_scr[hk, rows, par * d:(par + 1) * d] = o * inv_scr[c, rows, 0:1]

    for hk in range(C_KV_HEADS):
        for j in range(2):
            for t in range(valid):
                tile = hk * 2 + j
                o_ref[t, :, tile * LANE:(tile + 1) * LANE] = o_scr[hk, pl.ds(j * valid + t, nseq, stride=rs), :]


def _attn_sample(q3, kn3, vn3, kc_all, vc_all, b1, b2, sink_col, layer, k_prev, v_prev):
    valid, nsb, _ = q3.shape
    lb = kc_all.shape[-1]
    nseq = 2 * SUBLANE
    assert 2 * valid == SUBLANE
    rows = lambda w: pl.BlockSpec((valid, nseq, w), lambda i: (0, i, 0))
    cblk = pl.BlockSpec((1, nseq, C_KV_WIDTH, lb), lambda i: (layer, i, 0, 0))
    b1, b2, sink_col = (jnp.tile(a, (1, 1, nseq, 1)) for a in (b1, b2, sink_col))
    in_specs = [rows(C_WIDTH), rows(C_KV_WIDTH), rows(C_KV_WIDTH), cblk, cblk,
                _full(b1.shape), _full(b2.shape), _full(sink_col.shape)] + [pl.BlockSpec(memory_space=pl.ANY)] * 2
    args = [q3, kn3, vn3, kc_all, vc_all, b1, b2, sink_col, k_prev, v_prev]
    aliases = {len(args) - 2: 1, len(args) - 1: 2}
    return pl.pallas_call(
        functools.partial(_attn_sample_body, valid=valid, lb=lb, nseq=nseq),
        grid=(nsb // nseq,),
        in_specs=in_specs,
        out_specs=[rows(C_WIDTH), cblk, cblk],
        out_shape=[jax.ShapeDtypeStruct((valid, nsb, C_WIDTH), F32), jax.ShapeDtypeStruct(kc_all.shape, F32),
                   jax.ShapeDtypeStruct(vc_all.shape, F32)],
        scratch_shapes=[pltpu.VMEM((2 * C_KV_HEADS, nseq * SUBLANE, lb), F32),
                        pltpu.VMEM((2 * C_KV_HEADS, nseq * SUBLANE, SUBLANE), F32),
                        pltpu.VMEM((2 * C_KV_HEADS, nseq * SUBLANE, SUBLANE), F32),
                        pltpu.VMEM((C_KV_HEADS, nseq * SUBLANE, LANE), F32)],
        input_output_aliases=aliases,
        compiler_params=_cparams("arbitrary"),
        name="attn_sample",
    )(*args)


def _merge_body(x_ref, ya_ref, yb_ref, u_ref, yc_ref, nw_ref, wg_ref, d5_ref, wglu_ref,
                wa_ref, wb_ref, wc_ref, wo_ref, o_ref, *, ya_features_major, yc_features_major):
    x = x_ref[...]
    h = _rms(x, nw_ref[...]).astype(BF16)
    g = jax.nn.gelu(_lanes(yb_ref) + d5_ref[...] * _lanes(u_ref))
    gv = _dot(g.astype(BF16), wglu_ref[...])
    yb = (gv[:, :B_WIDTH] * jax.nn.sigmoid(gv[:, B_WIDTH:])).astype(BF16)

    def gate(i):
        return jax.nn.sigmoid(_dot(h, wg_ref[:, i * D_MODEL:(i + 1) * D_MODEL]))

    def branch(y_ref, w_ref, features_major):
        y = y_ref[...].astype(BF16)
        return _dot_tn(y, w_ref[...]) if features_major else _dot(y, w_ref[...])

    mixed = gate(0) * branch(ya_ref, wa_ref, ya_features_major)
    mixed = mixed + gate(1) * _dot(yb, wb_ref[...])
    mixed = mixed + gate(2) * branch(yc_ref, wc_ref, yc_features_major)
    o_ref[...] = x + _dot(mixed.astype(BF16), wo_ref[...])


def _merge(x, ya, yb, u, yc, lw, tm, ya_features_major, yc_features_major):
    t = x.shape[0]
    rows = lambda w: pl.BlockSpec((tm, w), lambda i: (i, 0))
    branch_spec = lambda fm: pl.BlockSpec((A_WIDTH, tm), lambda i: (0, i)) if fm else rows(A_WIDTH)
    lwt = lambda shape: _layer_weight(shape, lw["layer"])
    return pl.pallas_call(
        functools.partial(_merge_body, ya_features_major=ya_features_major, yc_features_major=yc_features_major),
        grid=(t // tm,),
        in_specs=[rows(D_MODEL), branch_spec(ya_features_major), _tile_major_spec(tm), _tile_major_spec(tm),
                  branch_spec(yc_features_major),
                  _full((1, D_MODEL)), lwt((D_MODEL, 3 * D_MODEL)), _full((1, B_WIDTH)),
                  lwt((B_WIDTH, 2 * B_WIDTH)), lwt((A_WIDTH, D_MODEL)), lwt((B_WIDTH, D_MODEL)),
                  lwt((C_WIDTH, D_MODEL)), lwt((D_MODEL, D_MODEL))],
        out_specs=rows(D_MODEL),
        out_shape=jax.ShapeDtypeStruct((t, D_MODEL), F32),
        compiler_params=_cparams("parallel"),
        name="merge",
    )(x, ya, yb, u, yc, lw["norm1_w"], lw["w_gate"], lw["s5_d"], lw["w_glu"],
      lw["w_br_a"], lw["w_br_b"], lw["w_br_c"], lw["w_out"])


def _ffn_body(x_ref, nw_ref, wup_ref, wdn_ref, fnw_ref, o_ref, act_scr, *, final_norm):
    x = x_ref[...]
    h = _rms(x, nw_ref[...]).astype(BF16)
    for c in range(D_FF // FF_CHUNK):
        lo, hi = c * FF_CHUNK, (c + 1) * FF_CHUNK
        a = _dot(h, wup_ref[:, lo:hi])
        b = _dot(h, wup_ref[:, D_FF + lo:D_FF + hi])
        act_scr[:, lo:hi] = (jax.nn.silu(a) * b).astype(BF16)
    y = x + _dot(act_scr[...], wdn_ref[...])
    if final_norm:
        y = _rms(y, fnw_ref[...])
    o_ref[...] = y


def _ffn(x, lw, fnw, tm, final_norm):
    t = x.shape[0]
    rows = pl.BlockSpec((tm, D_MODEL), lambda i: (i, 0))
    return pl.pallas_call(
        functools.partial(_ffn_body, final_norm=final_norm),
        grid=(t // tm,),
        in_specs=[rows, _full((1, D_MODEL)), _layer_weight((D_MODEL, 2 * D_FF), lw["layer"]),
                  _layer_weight((D_FF, D_MODEL), lw["layer"]), _full((1, D_MODEL))],
        out_specs=rows,
        out_shape=jax.ShapeDtypeStruct((t, D_MODEL), F32),
        scratch_shapes=[pltpu.VMEM((tm, D_FF), BF16)],
        compiler_params=_cparams("parallel"),
        name="ffn",
    )(x, lw["norm2_w"], lw["w_ffn_up"], lw["w_ffn_down"], fnw)


def _row_tile(t, pref):
    tm = pref
    while t % tm:
        tm //= 2
    return tm


def kernel(x_prompt, x_sample, state_ssd, state_conv, state_s5_re, state_s5_im, cache_k, cache_v, norm1_w, w_in, conv_w, conv_b, ssd_a_log, ssd_dt_bias, ssd_d, ssd_norm_w, s5_lam_re, s5_lam_im, s5_log_dt, s5_b_re, s5_b_im, s5_c_re, s5_c_im, s5_d, s5_w_glu, attn_sinks, w_br_a, w_br_b, w_br_c, w_out, norm2_w, w_ffn_up, w_ffn_down, rel_bias, final_norm_w):
    nb, seq, _ = x_prompt.shape
    nsb, valid, _ = x_sample.shape
    depth = w_in.shape[0]
    lb = cache_k.shape[2]
    assert seq % A_CHUNK == 0 and seq % WINDOW == 0 and seq % (S5_Q * SUBLANE) == 0
    assert nsb == LANE and A_CONV - 1 <= valid <= S5_Q and lb == WINDOW
    nchunk = seq // A_CHUNK
    tp = nb * seq
    ts = valid * nsb

    band = _bias_band(rel_bias)

    def pair_rows(a):
        a = a.reshape(C_KV_HEADS, 2, 2, valid, a.shape[-1])
        return jnp.swapaxes(a, 1, 2).reshape(C_KV_HEADS, 2, 2 * valid, a.shape[-1])

    band_s = pair_rows(jnp.swapaxes(band[1, :, :, :valid], 1, 2))
    bias_s1 = band_s[..., :lb]
    bias_s2 = band_s[..., lb:lb + SUBLANE]
    fnw = final_norm_w.reshape(1, D_MODEL)

    xp = x_prompt.reshape(tp, D_MODEL)
    xs = jnp.swapaxes(x_sample, 0, 1).reshape(ts, D_MODEL)
    conv_all = jnp.swapaxes(state_conv, 1, 2)
    ssd_all = jnp.transpose(state_ssd, (0, 2, 3, 4, 1))
    s5re_all = jnp.transpose(state_s5_re, (0, 2, 3, 1)).reshape(depth, S5_STATE_COLS, nsb)
    s5im_all = jnp.transpose(state_s5_im, (0, 2, 3, 1)).reshape(depth, S5_STATE_COLS, nsb)
    kc_all = jnp.transpose(cache_k, (0, 1, 3, 4, 2)).reshape(depth, nsb, C_KV_WIDTH, lb)
    vc_all = jnp.transpose(cache_v, (0, 1, 3, 4, 2)).reshape(depth, nsb, C_KV_WIDTH, lb)
    tm_p = _row_tile(tp, 512)
    tm_s = _row_tile(ts, 512)
    rt_p = _row_tile(tp // S5_Q, 256)
    rb_p = _row_tile(seq // S5_Q, 128)
    attn_blk = _row_tile(seq // WINDOW, 8)

    new_p, s5_s = [], []
    conv_new, ssd_new = jnp.zeros(conv_all.shape, F32), jnp.zeros(ssd_all.shape, F32)
    k_new, v_new = jnp.zeros(kc_all.shape, F32), jnp.zeros(vc_all.shape, F32)
    off = [0, 512, 1280, 1288, 1800, 2312, 2440, 2568, 5640]
    w_pack = jnp.concatenate(
        [w_in[:, :, off[0]:off[1]], w_in[:, :, off[1]:off[2]], w_in[:, :, off[3]:off[4]], w_in[:, :, off[4]:off[5]],
         w_in[:, :, off[5]:off[6]], w_in[:, :, off[6]:off[7]],
         jnp.pad(w_in[:, :, off[2]:off[3]], ((0, 0), (0, 0), (0, LANE - A_HEADS)))], axis=2).astype(BF16)
    stacked = dict(
        w_gate=w_in[:, :, off[7]:off[8]].astype(BF16), w_glu=s5_w_glu.astype(BF16),
        w_br_a=w_br_a.astype(BF16), w_br_b=w_br_b.astype(BF16), w_br_c=w_br_c.astype(BF16),
        w_out=w_out.astype(BF16), w_ffn_up=w_ffn_up.astype(BF16), w_ffn_down=w_ffn_down.astype(BF16))
    for i in range(depth):
        sink_rows = pair_rows(jnp.broadcast_to(attn_sinks[i].reshape(C_HEADS, 1, 1), (C_HEADS, valid, 1)))
        lw = dict(
            stacked, layer=i, norm1_w=norm1_w[i].reshape(1, D_MODEL),
            conv_w=conv_w[i], conv_b=conv_b[i].reshape(1, A_CONV_DIM),
            a_log=jnp.pad(ssd_a_log[i], (0, LANE - A_HEADS)).reshape(1, LANE),
            dt_bias=jnp.pad(ssd_dt_bias[i], (0, LANE - A_HEADS)).reshape(1, LANE),
            d_exp=jnp.repeat(ssd_d[i], A_HEAD_DIM).reshape(1, A_WIDTH),
            ssd_norm_w=ssd_norm_w[i].reshape(1, A_WIDTH),
            ssd_norm_wb=jnp.broadcast_to(ssd_norm_w[i].reshape(A_WIDTH, 1), (A_WIDTH, LANE)),
            s5_d=s5_d[i].reshape(1, B_WIDTH), norm2_w=norm2_w[i].reshape(1, D_MODEL),
        )
        s5w = _s5_weights(s5_lam_re[i], s5_lam_im[i], s5_log_dt[i], s5_b_re[i], s5_b_im[i],
                          s5_c_re[i], s5_c_im[i], valid)
        last = i == depth - 1

        u, q, k, v, ya, ssd_h, conv_tail = _inproj_ssd(xp, w_pack, lw, nb, tm_p)
        sre, sim = _s5_state(u, s5w["bst_re"], s5w["bst_im"], rt_p)
        hre, him, fre, fim = _s5_scan(sre, sim, s5w["aq_re"], s5w["aq_im"], nb, rb_p)
        yb = _s5_out(u, hre, him, s5w["toep"], s5w["cst_re"], s5w["cst_im"], rt_p)
        yc = _attn_prompt(q, k, v, band, attn_sinks[i], nb, seq // WINDOW, attn_blk)
        x1 = _merge(xp, ya, yb, u, yc, lw, tm_p, False, True)
        xp = _ffn(x1, lw, fnw, tm_p, last)
        new_p.append((
            ssd_h.reshape(nb, A_HEADS, A_HEAD_DIM, A_STATE),
            conv_tail[:, SUBLANE - (A_CONV - 1):],
            fre.reshape(nb, B_GROUPS, B_STATE), fim.reshape(nb, B_GROUPS, B_STATE),
            k.reshape(nb, seq, C_KV_WIDTH)[:, seq - WINDOW:].reshape(nb, WINDOW, C_KV_HEADS, C_HEAD_DIM),
            v.reshape(nb, seq, C_KV_WIDTH)[:, seq - WINDOW:].reshape(nb, WINDOW, C_KV_HEADS, C_HEAD_DIM)))

        z, xbc, u, q, k, v, dt = _inproj(xs, lw["norm1_w"], w_pack, i, tm_s)
        ya, conv_new, ssd_new = _ssd_sample(xbc, z, dt, conv_all, ssd_all, lw, i, valid, conv_new, ssd_new)
        av_re = jnp.broadcast_to(s5w["av_re"].reshape(S5_STATE_COLS, 1), (S5_STATE_COLS, nsb))
        av_im = jnp.broadcast_to(s5w["av_im"].reshape(S5_STATE_COLS, 1), (S5_STATE_COLS, nsb))
        yb, fre, fim = _s5_sample(u, s5w, s5re_all, s5im_all, av_re, av_im, i, valid)
        yc, k_new, v_new = _attn_sample(
            q.reshape(valid, nsb, C_WIDTH), k.reshape(valid, nsb, C_KV_WIDTH), v.reshape(valid, nsb, C_KV_WIDTH),
            kc_all, vc_all, bias_s1, bias_s2, sink_rows, i, k_new, v_new)
        x1 = _merge(xs, ya, yb, u, yc.reshape(ts, C_WIDTH), lw, tm_s, True, False)
        xs = _ffn(x1, lw, fnw, tm_s, last)
        s5_s.append((fre, fim))

    def stack(states, j):
        return jnp.stack([s[j] for s in states], axis=0)

    def s5_state(j):
        return jnp.transpose(stack(s5_s, j).reshape(depth, B_GROUPS, B_STATE, nsb), (0, 3, 1, 2))

    def cache(a):
        return jnp.transpose(a.reshape(depth, nsb, C_KV_HEADS, C_HEAD_DIM, lb), (0, 1, 4, 2, 3))

    y_prompt = xp.reshape(nb, seq, D_MODEL)
    y_sample = jnp.swapaxes(xs.reshape(valid, nsb, D_MODEL), 0, 1)
    return (y_prompt, y_sample,
            stack(new_p, 0), stack(new_p, 1), stack(new_p, 2), stack(new_p, 3), stack(new_p, 4), stack(new_p, 5),
            jnp.transpose(ssd_new, (0, 4, 1, 2, 3)), jnp.swapaxes(conv_new, 1, 2), s5_state(0), s5_state(1),
            cache(k_new), cache(v_new))
```

```python
import functools
import math

import numpy as np
import jax
import jax.numpy as jnp
from jax import lax
from jax.experimental import pallas as pl
from jax.experimental.pallas import tpu as pltpu

F32 = jnp.float32
BF16 = jnp.bfloat16
HIGHEST = lax.Precision.HIGHEST

D_MODEL = 1024
A_HEAD_DIM = 64
A_WIDTH = 512
A_HEADS = 8
A_GROUPS = 2
A_STATE = 64
A_CONV = 4
A_CONV_DIM = 768
A_CHUNK = 128
B_CH = 16
B_WIDTH = 512
B_GROUPS = 32
B_STATE = 64
C_HEAD_DIM = 64
C_WIDTH = 512
C_HEADS = 8
C_KV_HEADS = 2
C_GQA = 4
C_KV_WIDTH = 128
WINDOW = 128
REL_BUCKETS = 32
REL_MAX_DIST = 128
D_FF = 2816
EPS = 1e-6

P_Z = 0
P_XBC = P_Z + A_WIDTH
P_U = P_XBC + A_CONV_DIM
P_Q = P_U + B_WIDTH
P_K = P_Q + C_WIDTH
P_V = P_K + C_KV_WIDTH
P_DT = P_V + C_KV_WIDTH
P_COLS = P_DT + 128

LANE = 128
SUBLANE = 8
S5_Q = 8
S5_OCT = 8
N_OCT = B_GROUPS // S5_OCT
S5_STATE_COLS = B_GROUPS * B_STATE
NEG = -1e30
VMEM_LIMIT = 56 * 1024 * 1024
FF_CHUNK = 256


def _cparams(*sem):
    return pltpu.CompilerParams(dimension_semantics=sem, vmem_limit_bytes=VMEM_LIMIT)


def _rms(x, w):
    return x * lax.rsqrt(jnp.mean(x * x, axis=-1, keepdims=True) + EPS) * w


def _dot(a, b):
    return jnp.dot(a, b, preferred_element_type=F32)


def _dot_nt(a, b, precision=None):
    return lax.dot_general(a, b, (((1,), (1,)), ((), ())), preferred_element_type=F32, precision=precision)


def _dot_tn(a, b):
    return lax.dot_general(a, b, (((0,), (0,)), ((), ())), preferred_element_type=F32)


def _full(shape):
    n = len(shape)
    return pl.BlockSpec(shape, lambda *_: (0,) * n)


def _layer_weight(shape, layer):
    return pl.BlockSpec((None,) + shape, lambda *_: (layer,) + (0,) * len(shape))


def _tile_major_spec(rows):
    return pl.BlockSpec((N_OCT, rows, LANE), lambda i: (0, i, 0))


def _lanes(ref):
    return jnp.concatenate([ref[k8] for k8 in range(N_OCT)], axis=1)


def _inproj_body(x_ref, nw_ref, w_ref, z_ref, xbc_ref, u_ref, q_ref, k_ref, v_ref, dt_ref):
    h = _rms(x_ref[...], nw_ref[...]).astype(BF16)

    def proj(lo, hi):
        return _dot_nt(h, w_ref[lo:hi, :])

    z_ref[...] = proj(P_Z, P_XBC).astype(BF16)
    xbc_ref[...] = proj(P_XBC, P_U)
    u = proj(P_U, P_Q)
    for k8 in range(N_OCT):
        u_ref[k8] = u[:, k8 * LANE:(k8 + 1) * LANE]
    q_ref[...] = proj(P_Q, P_K).astype(BF16)
    kvdt = proj(P_K, P_COLS)
    k_ref[...] = kvdt[:, :C_KV_WIDTH]
    v_ref[...] = kvdt[:, C_KV_WIDTH:2 * C_KV_WIDTH]
    dt_ref[...] = kvdt[:, 2 * C_KV_WIDTH:]


def _inproj(x, nw, w, layer, tm):
    t = x.shape[0]
    widths = [(A_WIDTH, BF16), (A_CONV_DIM, F32), None, (C_WIDTH, BF16),
              (C_KV_WIDTH, F32), (C_KV_WIDTH, F32), (LANE, F32)]
    spec = lambda w: _tile_major_spec(tm) if w is None else pl.BlockSpec((tm, w[0]), lambda i: (i, 0))
    shape = lambda w: (jax.ShapeDtypeStruct((N_OCT, t, LANE), F32) if w is None
                       else jax.ShapeDtypeStruct((t, w[0]), w[1]))
    return pl.pallas_call(
        _inproj_body,
        grid=(t // tm,),
        in_specs=[pl.BlockSpec((tm, D_MODEL), lambda i: (i, 0)), _full((1, D_MODEL)),
                  _layer_weight((P_COLS, D_MODEL), layer)],
        out_specs=[spec(w) for w in widths],
        out_shape=[shape(w) for w in widths],
        compiler_params=_cparams("parallel"),
        name="inproj",
    )(x, nw, w)


def _ssd_chunk(xbc, z, dt_raw, cw_ref, cb_ref, alog_ref, dtb_ref, dexp_ref, nw_ref, h_scr, tail_scr):
    q = xbc.shape[0]
    tail_scr[SUBLANE:SUBLANE + q] = xbc
    acc = cb_ref[...] + cw_ref[A_CONV - 1:A_CONV, :] * xbc
    for j in range(1, A_CONV):
        acc = acc + cw_ref[A_CONV - 1 - j:A_CONV - j, :] * tail_scr[SUBLANE - j:SUBLANE - j + q]
    tail_scr[0:SUBLANE] = xbc[q - SUBLANE:q]
    xc = jax.nn.silu(acc)
    x = xc[:, :A_WIDTH]
    bm = xc[:, A_WIDTH:A_WIDTH + A_GROUPS * A_STATE].astype(BF16)
    cm = xc[:, A_WIDTH + A_GROUPS * A_STATE:].astype(BF16)

    dt = jax.nn.softplus(dt_raw + dtb_ref[...])
    dta = dt * (-jnp.exp(alog_ref[...]))
    row = lax.broadcasted_iota(jnp.int32, (q, q), 0)
    col = lax.broadcasted_iota(jnp.int32, (q, q), 1)
    tri = row >= col
    a_cum = jnp.dot(tri.astype(F32), dta, preferred_element_type=F32, precision=HIGHEST)

    def expand(v):
        return jnp.concatenate([jnp.broadcast_to(v[:, k:k + 1], (q, A_HEAD_DIM)) for k in range(A_HEADS)], axis=1)

    a_cum_e = expand(a_cum)
    xs = x * expand(dt)
    xs_b = xs.astype(BF16)
    a_cum_t = a_cum.T

    h = h_scr[...]
    h_b = h.astype(BF16)
    hpg = A_HEADS // A_GROUPS
    gw = hpg * A_HEAD_DIM
    y_diag, y_off = [], []
    for g in range(A_GROUPS):
        cg = cm[:, g * A_STATE:(g + 1) * A_STATE]
        bg = bm[:, g * A_STATE:(g + 1) * A_STATE]
        cb = _dot_nt(cg, bg)
        for kk in range(hpg):
            k = g * hpg + kk
            seg = a_cum[:, k:k + 1] - a_cum_t[k:k + 1, :]
            decay = jnp.exp(jnp.where(tri, seg, -jnp.inf))
            y_diag.append(_dot((cb * decay).astype(BF16), xs_b[:, k * A_HEAD_DIM:(k + 1) * A_HEAD_DIM]))
        y_off.append(_dot_nt(cg, h_b[g * gw:(g + 1) * gw]))
    y = (jnp.concatenate(y_diag, axis=1) + jnp.concatenate(y_off, axis=1) * jnp.exp(a_cum_e)
         + dexp_ref[...] * x)
    y = y * jax.nn.silu(z)

    last = a_cum_e[q - 1:q, :]
    xs_dec = (xs * jnp.exp(last - a_cum_e)).astype(BF16)
    dec_col = jnp.exp(a_cum_e.T[:, q - 1:q])
    upd = [_dot_tn(xs_dec[:, g * gw:(g + 1) * gw], bm[:, g * A_STATE:(g + 1) * A_STATE]) for g in range(A_GROUPS)]
    h_scr[...] = dec_col * h + jnp.concatenate(upd, axis=0)
    return _rms(y, nw_ref[...]).astype(BF16)


def _inproj_ssd_body(x_ref, nw_ref, w_ref, cw_ref, cb_ref, alog_ref, dtb_ref, dexp_ref, snw_ref,
                     u_ref, q_ref, k_ref, v_ref, ya_ref, hout_ref, tail_ref, h_scr, tail_scr, *, nsteps, chunk):
    c = pl.program_id(1)

    @pl.when(c == 0)
    def _():
        h_scr[...] = jnp.zeros_like(h_scr)
        tail_scr[0:SUBLANE] = jnp.zeros((SUBLANE, A_CONV_DIM), F32)

    h = _rms(x_ref[...], nw_ref[...]).astype(BF16)

    def proj(lo, hi):
        return _dot_nt(h, w_ref[lo:hi, :])

    z = proj(P_Z, P_XBC)
    xbc = proj(P_XBC, P_U)
    kvdt = proj(P_K, P_COLS)
    k_ref[...] = kvdt[:, :C_KV_WIDTH]
    v_ref[...] = kvdt[:, C_KV_WIDTH:2 * C_KV_WIDTH]
    dt = kvdt[:, 2 * C_KV_WIDTH:]
    for i in range(x_ref.shape[0] // chunk):
        rows = slice(i * chunk, (i + 1) * chunk)
        ya_ref[rows, :] = _ssd_chunk(xbc[rows], z[rows], dt[rows], cw_ref, cb_ref, alog_ref, dtb_ref, dexp_ref,
                                     snw_ref, h_scr, tail_scr)
    u = proj(P_U, P_Q)
    for k8 in range(N_OCT):
        u_ref[k8] = u[:, k8 * LANE:(k8 + 1) * LANE]
    q_ref[...] = proj(P_Q, P_K).astype(BF16)

    @pl.when(c == nsteps - 1)
    def _():
        hout_ref[0] = h_scr[...]
        tail_ref[0] = tail_scr[0:SUBLANE]


def _inproj_ssd(x, w, lw, nb, tm):
    t = x.shape[0]
    nsteps = t // nb // tm
    rmap = lambda b, c: (b * nsteps + c, 0)
    rows = lambda wd: pl.BlockSpec((tm, wd), rmap)
    per_seq = lambda s: pl.BlockSpec((1,) + s, lambda b, c: (b, 0, 0))
    return pl.pallas_call(
        functools.partial(_inproj_ssd_body, nsteps=nsteps, chunk=A_CHUNK),
        grid=(nb, nsteps),
        in_specs=[rows(D_MODEL), _full((1, D_MODEL)), _layer_weight((P_COLS, D_MODEL), lw["layer"]),
                  _full((A_CONV, A_CONV_DIM)), _full((1, A_CONV_DIM)), _full((1, LANE)), _full((1, LANE)),
                  _full((1, A_WIDTH)), _full((1, A_WIDTH))],
        out_specs=[pl.BlockSpec((N_OCT, tm, LANE), lambda b, c: (0, b * nsteps + c, 0)), rows(C_WIDTH),
                   rows(C_KV_WIDTH), rows(C_KV_WIDTH), rows(A_WIDTH),
                   per_seq((A_WIDTH, A_STATE)), per_seq((SUBLANE, A_CONV_DIM))],
        out_shape=[jax.ShapeDtypeStruct((N_OCT, t, LANE), F32), jax.ShapeDtypeStruct((t, C_WIDTH), BF16),
                   jax.ShapeDtypeStruct((t, C_KV_WIDTH), F32), jax.ShapeDtypeStruct((t, C_KV_WIDTH), F32),
                   jax.ShapeDtypeStruct((t, A_WIDTH), BF16), jax.ShapeDtypeStruct((nb, A_WIDTH, A_STATE), F32),
                   jax.ShapeDtypeStruct((nb, SUBLANE, A_CONV_DIM), F32)],
        scratch_shapes=[pltpu.VMEM((A_WIDTH, A_STATE), F32), pltpu.VMEM((SUBLANE + A_CHUNK, A_CONV_DIM), F32)],
        compiler_params=_cparams("parallel", "arbitrary"),
        name="inproj_ssd",
    )(x, lw["norm1_w"], w, lw["conv_w"], lw["conv_b"], lw["a_log"], lw["dt_bias"], lw["d_exp"], lw["ssd_norm_w"])


def _ssd_sample_body(*refs, valid):
    n_in = 13
    (xbc_ref, z_ref, dt_ref, conv0_ref, h0_ref, cw_ref, cb_ref, alog_ref, dtb_ref, dexp_ref, nwb_ref) = refs[:11]
    y_ref, conv_out_ref, h_out_ref = refs[n_in:n_in + 3]
    xs_scr, dx_scr, z_scr, y_scr, bt_scr, ct_scr, da_scr = refs[n_in + 3:]
    k = pl.program_id(0)
    ns = LANE

    @pl.when(k == 0)
    def _():
        rows = [conv0_ref[0, r] for r in range(A_CONV - 1)] + [xbc_ref[t * ns:(t + 1) * ns, :] for t in range(valid)]
        for t in range(valid):
            acc = cb_ref[...]
            for j in range(A_CONV):
                acc = acc + cw_ref[A_CONV - 1 - j:A_CONV - j, :] * rows[t + A_CONV - 1 - j]
            xc = jax.nn.silu(acc)
            dt = jax.nn.softplus(dt_ref[t * ns:(t + 1) * ns, :] + dtb_ref[...])
            dt_t = dt.T
            da_scr[t] = jnp.exp((dt * (-jnp.exp(alog_ref[...]))).T)
            x = xc[:, :A_WIDTH]
            x_t = x.T
            for kk in range(A_HEADS):
                hs = slice(kk * A_HEAD_DIM, (kk + 1) * A_HEAD_DIM)
                xs_scr[t, hs, :] = x_t[hs] * dt_t[kk:kk + 1, :]
            dx_scr[t] = (x * dexp_ref[...]).T
            bt_scr[t] = xc[:, A_WIDTH:A_WIDTH + A_GROUPS * A_STATE].T
            ct_scr[t] = xc[:, A_WIDTH + A_GROUPS * A_STATE:].T
            z_scr[t] = jax.nn.silu(z_ref[t * ns:(t + 1) * ns, :].astype(F32)).T
        for r in range(A_CONV - 1):
            conv_out_ref[0, r] = rows[valid + r]

    g0 = pl.multiple_of((k // (A_HEADS // A_GROUPS)) * A_STATE, A_STATE)

    def per_p(p, carry):
        h = h0_ref[0, 0, p]
        row = k * A_HEAD_DIM + p
        for t in range(valid):
            da = da_scr[t, pl.ds(k, 1), :]
            xr = xs_scr[t, pl.ds(row, 1), :]
            h = da * h + xr * bt_scr[t, pl.ds(g0, A_STATE), :]
            y_scr[t, pl.ds(row, 1), :] = jnp.sum(h * ct_scr[t, pl.ds(g0, A_STATE), :], axis=0, keepdims=True)
        h_out_ref[0, 0, p] = h
        return carry

    lax.fori_loop(0, A_HEAD_DIM, per_p, 0, unroll=2)

    @pl.when(k == A_HEADS - 1)
    def _():
        for t in range(valid):
            y = (y_scr[t] + dx_scr[t]) * z_scr[t]
            y = y * lax.rsqrt(jnp.mean(y * y, axis=0, keepdims=True) + EPS) * nwb_ref[...]
            y_ref[:, t * ns:(t + 1) * ns] = y.astype(BF16)


def _ssd_sample(xbc, z, dt, conv_all, h_all, lw, layer, valid, conv_prev, h_prev):
    rows = valid * LANE
    hblk = pl.BlockSpec((1, 1, A_HEAD_DIM, A_STATE, LANE), lambda k: (layer, k, 0, 0, 0))
    cblk = pl.BlockSpec((1, A_CONV - 1, LANE, A_CONV_DIM), lambda k: (layer, 0, 0, 0))
    in_specs = [_full((rows, A_CONV_DIM)), _full((rows, A_WIDTH)), _full((rows, LANE)), cblk, hblk,
                _full((A_CONV, A_CONV_DIM)), _full((1, A_CONV_DIM)), _full((1, LANE)), _full((1, LANE)),
                _full((1, A_WIDTH)), _full((A_WIDTH, LANE))] + [pl.BlockSpec(memory_space=pl.ANY)] * 2
    args = [xbc, z, dt, conv_all, h_all, lw["conv_w"], lw["conv_b"], lw["a_log"], lw["dt_bias"], lw["d_exp"],
            lw["ssd_norm_wb"], conv_prev, h_prev]
    aliases = {len(args) - 2: 1, len(args) - 1: 2}
    big = lambda n: pltpu.VMEM((valid, n, LANE), F32)
    return pl.pallas_call(
        functools.partial(_ssd_sample_body, valid=valid),
        grid=(A_HEADS,),
        in_specs=in_specs,
        out_specs=[_full((A_WIDTH, rows)), cblk, hblk],
        out_shape=[jax.ShapeDtypeStruct((A_WIDTH, rows), BF16), jax.ShapeDtypeStruct(conv_all.shape, F32),
                   jax.ShapeDtypeStruct(h_all.shape, F32)],
        scratch_shapes=[big(A_WIDTH), big(A_WIDTH), big(A_WIDTH), big(A_WIDTH), big(LANE), big(LANE), big(LANE)],
        input_output_aliases=aliases,
        compiler_params=_cparams("arbitrary"),
        name="ssd_sample",
    )(*args)


def _s5w_body(lre_ref, lim_ref, ldt_ref, btr_ref, bti_ref, cr_ref, ci_ref,
              bre_ref, bim_ref, bsre_ref, bsim_ref, cre_ref, cim_ref, toep_ref, ap_ref, cd_scr, bb_scr, *, valid):
    h, q = B_STATE, S5_Q
    for ref in (bre_ref, bim_ref, bsre_ref, bsim_ref, cre_ref, cim_ref, toep_ref):
        ref[...] = jnp.zeros(ref.shape, BF16)
    r_i = lax.broadcasted_iota(jnp.int32, (q * B_CH, q * LANE), 0)
    c_i = lax.broadcasted_iota(jnp.int32, (q * B_CH, q * LANE), 1)
    d = lax.broadcasted_iota(jnp.int32, (2 * SUBLANE, h), 0).astype(F32)
    for g8 in range(S5_OCT):
        lr, li = lre_ref[g8], lim_ref[g8]
        step = jnp.exp(ldt_ref[g8])
        er = jnp.exp(lr * step * d)
        ang = li * step * d
        ar, ai = er * jnp.cos(ang), er * jnp.sin(ang)
        nr, ni = ar[1:2] - 1.0, ai[1:2]
        den = lr * lr + li * li
        wr, wi = (nr * lr + ni * li) / den, (ni * lr - nr * li) / den
        btr, bti = btr_ref[g8], bti_ref[g8]
        bbr, bbi = wr * btr - wi * bti, wr * bti + wi * btr
        cr, ci = cr_ref[g8], ci_ref[g8]
        bb_scr[:, 0:h] = bbr
        bb_scr[:, h:2 * h] = bbi
        cols = slice(g8 * h, (g8 + 1) * h)
        for s in range(q):
            rows = slice(s * LANE + g8 * B_CH, s * LANE + (g8 + 1) * B_CH)
            e = q - 1 - s
            bre_ref[0, rows, cols] = (ar[e:e + 1] * bbr - ai[e:e + 1] * bbi).astype(BF16)
            bim_ref[0, rows, cols] = (ai[e:e + 1] * bbr + ar[e:e + 1] * bbi).astype(BF16)
            if s < valid:
                e = valid - 1 - s
                bsre_ref[0, rows, cols] = (ar[e:e + 1] * bbr - ai[e:e + 1] * bbi).astype(BF16)
                bsim_ref[0, rows, cols] = (ai[e:e + 1] * bbr + ar[e:e + 1] * bbi).astype(BF16)
            cre_ref[0, rows, cols] = (cr * ar[s + 1:s + 2] - ci * ai[s + 1:s + 2]).astype(BF16)
            cim_ref[0, rows, cols] = (-(cr * ai[s + 1:s + 2] + ci * ar[s + 1:s + 2])).astype(BF16)
            cd_scr[s * B_CH:(s + 1) * B_CH, 0:h] = cr * ar[s:s + 1] - ci * ai[s:s + 1]
            cd_scr[s * B_CH:(s + 1) * B_CH, h:2 * h] = -(cr * ai[s:s + 1] + ci * ar[s:s + 1])
        kdt = _dot_nt(bb_scr[...], cd_scr[...], precision=HIGHEST)
        place = (c_i == (r_i >> 4) * LANE + g8 * B_CH + (r_i & (B_CH - 1))).astype(F32)
        slab = jnp.dot(kdt, place, preferred_element_type=F32, precision=HIGHEST).astype(BF16)
        for s in range(q):
            toep_ref[0, s * LANE + g8 * B_CH:s * LANE + (g8 + 1) * B_CH, s * LANE:] = slab[:, 0:(q - s) * LANE]
        ap_ref[g8, 0:1, 0:h] = ar[q:q + 1]
        ap_ref[g8, 0:1, h:2 * h] = ai[q:q + 1]
        ap_ref[g8, 1:2, 0:h] = ar[valid:valid + 1]
        ap_ref[g8, 1:2, h:2 * h] = ai[valid:valid + 1]


def _s5_weights(lam_re, lam_im, log_dt, b_re, b_im, c_re, c_im, valid):
    g, h, q = B_GROUPS, B_STATE, S5_Q
    vec = pl.BlockSpec((S5_OCT, 1, h), lambda i: (i, 0, 0))
    mat = pl.BlockSpec((S5_OCT, B_CH, h), lambda i: (i, 0, 0))
    st = pl.BlockSpec((1, q * LANE, S5_OCT * h), lambda i: (i, 0, 0))
    st_shape = jax.ShapeDtypeStruct((N_OCT, q * LANE, S5_OCT * h), BF16)
    bre, bim, bsre, bsim, cre, cim, toep, ap = pl.pallas_call(
        functools.partial(_s5w_body, valid=valid),
        grid=(N_OCT,),
        in_specs=[vec, vec, vec, mat, mat, mat, mat],
        out_specs=[st] * 6 + [pl.BlockSpec((1, q * LANE, q * LANE), lambda i: (i, 0, 0)),
                              pl.BlockSpec((S5_OCT, 2, 2 * h), lambda i: (i, 0, 0))],
        out_shape=[st_shape] * 6 + [jax.ShapeDtypeStruct((N_OCT, q * LANE, q * LANE), BF16),
                                    jax.ShapeDtypeStruct((g, 2, 2 * h), F32)],
        scratch_shapes=[pltpu.VMEM((q * B_CH, 2 * h), F32), pltpu.VMEM((B_CH, 2 * h), F32)],
        compiler_params=_cparams("parallel"),
        name="s5_weights",
    )(lam_re.reshape(g, 1, h), lam_im.reshape(g, 1, h), jnp.broadcast_to(log_dt.reshape(g, 1, 1), (g, 1, h)),
      jnp.swapaxes(b_re, 1, 2), jnp.swapaxes(b_im, 1, 2), c_re, c_im)
    return dict(
        bst_re=bre, bst_im=bim, bsts_re=bsre, bsts_im=bsim, cst_re=cre, cst_im=cim, toep=toep,
        aq_re=ap[:, 0, :h].reshape(1, S5_STATE_COLS), aq_im=ap[:, 0, h:].reshape(1, S5_STATE_COLS),
        av_re=ap[:, 1, :h].reshape(1, S5_STATE_COLS), av_im=ap[:, 1, h:].reshape(1, S5_STATE_COLS),
    )


def _oct_rows(u_ref, k8, rt):
    return jnp.concatenate([u_ref[k8, pl.ds(j, rt, stride=S5_Q), :] for j in range(S5_Q)], axis=1).astype(BF16)


def _toep_dot(uo, toep_ref, k8):
    two = 2 * LANE
    cols = [_dot(uo[:, :(t + 1) * two], toep_ref[k8, :(t + 1) * two, t * two:(t + 1) * two])
            for t in range(uo.shape[1] // two)]
    return jnp.concatenate(cols, axis=1)


def _s5_state_body(u_ref, bre_ref, bim_ref, sre_ref, sim_ref):
    w = S5_OCT * B_STATE
    for k8 in range(N_OCT):
        uo = _oct_rows(u_ref, k8, sre_ref.shape[0])
        sre_ref[:, k8 * w:(k8 + 1) * w] = _dot(uo, bre_ref[k8])
        sim_ref[:, k8 * w:(k8 + 1) * w] = _dot(uo, bim_ref[k8])


def _s5_state(u, bre, bim, rt):
    r = u.shape[1] // S5_Q
    wspec = _full((N_OCT, S5_Q * LANE, S5_OCT * B_STATE))
    ospec = pl.BlockSpec((rt, S5_STATE_COLS), lambda i: (i, 0))
    return pl.pallas_call(
        _s5_state_body,
        grid=(r // rt,),
        in_specs=[_tile_major_spec(rt * S5_Q), wspec, wspec],
        out_specs=[ospec, ospec],
        out_shape=[jax.ShapeDtypeStruct((r, S5_STATE_COLS), F32)] * 2,
        compiler_params=_cparams("parallel"),
        name="s5_state",
    )(u, bre, bim)


def _s5_scan_body(sre_ref, sim_ref, ar_ref, ai_ref, hre_ref, him_ref, fre_ref, fim_ref, cr_scr, ci_scr, *, rb, nblk):
    i = pl.program_id(0)
    nb = sre_ref.shape[0]

    @pl.when(i == 0)
    def _():
        cr_scr[...] = jnp.zeros_like(cr_scr)
        ci_scr[...] = jnp.zeros_like(ci_scr)

    ar, ai = ar_ref[...], ai_ref[...]

    def step(r, carry):
        out = []
        for b in range(nb):
            hr, hi = carry[2 * b], carry[2 * b + 1]
            hre_ref[b, pl.ds(r, 1), :] = hr
            him_ref[b, pl.ds(r, 1), :] = hi
            sr, si = sre_ref[b, pl.ds(r, 1), :], sim_ref[b, pl.ds(r, 1), :]
            out += [ar * hr - ai * hi + sr, ai * hr + ar * hi + si]
        return tuple(out)

    init = tuple(s[b] for b in range(nb) for s in (cr_scr, ci_scr))
    fin = lax.fori_loop(0, rb, step, init)
    for b in range(nb):
        cr_scr[b] = fin[2 * b]
        ci_scr[b] = fin[2 * b + 1]

    @pl.when(i == nblk - 1)
    def _():
        for b in range(nb):
            fre_ref[b] = fin[2 * b]
            fim_ref[b] = fin[2 * b + 1]


def _s5_scan(sre, sim, ar, ai, nb, rb):
    r = sre.shape[0] // nb
    nblk = r // rb
    blk = pl.BlockSpec((nb, rb, S5_STATE_COLS), lambda i: (0, i, 0))
    vec = _full((1, S5_STATE_COLS))
    fin = _full((nb, 1, S5_STATE_COLS))
    per_batch = lambda a: a.reshape(nb, r, S5_STATE_COLS)
    hre, him, fre, fim = pl.pallas_call(
        functools.partial(_s5_scan_body, rb=rb, nblk=nblk),
        grid=(nblk,),
        in_specs=[blk, blk, vec, vec],
        out_specs=[blk, blk, fin, fin],
        out_shape=[jax.ShapeDtypeStruct((nb, r, S5_STATE_COLS), F32)] * 2
        + [jax.ShapeDtypeStruct((nb, 1, S5_STATE_COLS), F32)] * 2,
        scratch_shapes=[pltpu.VMEM((nb, 1, S5_STATE_COLS), F32)] * 2,
        compiler_params=_cparams("arbitrary"),
        name="s5_scan",
    )(per_batch(sre), per_batch(sim), ar, ai)
    return hre.reshape(nb * r, S5_STATE_COLS), him.reshape(nb * r, S5_STATE_COLS), fre, fim


def _s5_sample_body(u_ref, toep_ref, cre_ref, cim_ref, bre_ref, bim_ref, hre_ref, him_ref, ar_ref, ai_ref,
                    y_ref, fre_ref, fim_ref, *, valid):
    w = S5_OCT * B_STATE
    ns = LANE
    for k8 in range(N_OCT):
        blocks = [u_ref[k8, t * ns:(t + 1) * ns, :] for t in range(valid)]
        uo = jnp.concatenate(blocks, axis=1).astype(BF16)
        uo_t = jnp.concatenate([b.T for b in blocks], axis=0).astype(BF16)
        hr, hi = hre_ref[0, k8 * w:(k8 + 1) * w, :], him_ref[0, k8 * w:(k8 + 1) * w, :]
        yo = (_toep_dot(uo, toep_ref, k8) + _dot_nt(hr.T.astype(BF16), cre_ref[k8])
              + _dot_nt(hi.T.astype(BF16), cim_ref[k8]))
        for t in range(valid):
            y_ref[k8, t * ns:(t + 1) * ns, :] = yo[:, t * LANE:(t + 1) * LANE]
        ar, ai = ar_ref[k8 * w:(k8 + 1) * w, :], ai_ref[k8 * w:(k8 + 1) * w, :]
        fre_ref[k8 * w:(k8 + 1) * w, :] = ar * hr - ai * hi + _dot_tn(bre_ref[k8], uo_t)
        fim_ref[k8 * w:(k8 + 1) * w, :] = ai * hr + ar * hi + _dot_tn(bim_ref[k8], uo_t)


def _s5_sample(u, s5w, hre_all, him_all, ar, ai, layer, valid):
    rows = valid * LANE
    vq = valid * LANE
    sub = lambda c: pl.BlockSpec((N_OCT, vq, c), lambda i: (0, 0, 0))
    hblk = pl.BlockSpec((1, S5_STATE_COLS, LANE), lambda i: (layer, 0, 0))
    st = _full((S5_STATE_COLS, LANE))
    return pl.pallas_call(
        functools.partial(_s5_sample_body, valid=valid),
        grid=(1,),
        in_specs=[_tile_major_spec(rows), sub(vq), sub(S5_OCT * B_STATE), sub(S5_OCT * B_STATE),
                  sub(S5_OCT * B_STATE), sub(S5_OCT * B_STATE), hblk, hblk, st, st],
        out_specs=[_tile_major_spec(rows), st, st],
        out_shape=[jax.ShapeDtypeStruct((N_OCT, rows, LANE), F32)] + [jax.ShapeDtypeStruct((S5_STATE_COLS, LANE), F32)] * 2,
        compiler_params=_cparams("arbitrary"),
        name="s5_sample",
    )(u, s5w["toep"], s5w["cst_re"], s5w["cst_im"], s5w["bsts_re"], s5w["bsts_im"], hre_all, him_all, ar, ai)


def _s5_out_body(u_ref, hre_ref, him_ref, toep_ref, cre_ref, cim_ref, y_ref):
    w = S5_OCT * B_STATE
    rt = hre_ref.shape[0]
    for k8 in range(N_OCT):
        uo = _oct_rows(u_ref, k8, rt)
        yo = (_toep_dot(uo, toep_ref, k8)
              + _dot_nt(hre_ref[:, k8 * w:(k8 + 1) * w].astype(BF16), cre_ref[k8])
              + _dot_nt(him_ref[:, k8 * w:(k8 + 1) * w].astype(BF16), cim_ref[k8]))
        for j in range(S5_Q):
            y_ref[k8, pl.ds(j, rt, stride=S5_Q), :] = yo[:, j * LANE:(j + 1) * LANE]


def _s5_out(u, hre, him, toep, cre, cim, rt):
    r = u.shape[1] // S5_Q
    hblk = pl.BlockSpec((rt, S5_STATE_COLS), lambda i: (i, 0))
    return pl.pallas_call(
        _s5_out_body,
        grid=(r // rt,),
        in_specs=[_tile_major_spec(rt * S5_Q), hblk, hblk, _full((N_OCT, S5_Q * LANE, S5_Q * LANE)),
                  _full((N_OCT, S5_Q * LANE, S5_OCT * B_STATE)), _full((N_OCT, S5_Q * LANE, S5_OCT * B_STATE))],
        out_specs=_tile_major_spec(rt * S5_Q),
        out_shape=jax.ShapeDtypeStruct(u.shape, F32),
        compiler_params=_cparams("parallel"),
        name="s5_out",
    )(u, hre, him, toep, cre, cim)


def _band_body(bucket_ref, table_ref, o_ref):
    bucket = bucket_ref[...]
    col = lax.broadcasted_iota(jnp.int32, (WINDOW, 2 * WINDOW), 1)
    for hq in range(C_HEADS):
        base = jnp.full((1, 2 * WINDOW), NEG, F32)
        for b in range(REL_BUCKETS):
            base = jnp.where(bucket == b, table_ref[b, hq], base)
        tile = pltpu.roll(jnp.broadcast_to(base, (WINDOW, 2 * WINDOW)), 0, 1, stride=1, stride_axis=0)
        o_ref[1, hq] = tile.T
        o_ref[0, hq] = jnp.where(col >= WINDOW, tile, NEG).T


def _bias_band(table):
    lane = np.arange(2 * WINDOW)
    n = WINDOW - lane
    exact = REL_BUCKETS // 2
    nf = np.maximum(n, 1).astype(np.float32)
    large = exact + (np.log(nf / np.float32(exact)) / np.float32(math.log(REL_MAX_DIST / exact))
                     * np.float32(REL_BUCKETS - exact)).astype(np.int32)
    bucket = np.where(n < exact, n, np.minimum(large, REL_BUCKETS - 1))
    bucket = np.where((n >= 0) & (n < WINDOW), bucket, -1).astype(np.int32)
    return pl.pallas_call(
        _band_body,
        in_specs=[pl.BlockSpec(memory_space=pltpu.VMEM), pl.BlockSpec(memory_space=pltpu.SMEM)],
        out_specs=pl.BlockSpec(memory_space=pltpu.VMEM),
        out_shape=jax.ShapeDtypeStruct((2, C_HEADS, 2 * WINDOW, WINDOW), F32),
        name="rel_bias_band",
    )(jnp.asarray(bucket).reshape(1, 2 * WINDOW), table)


def _attn_prompt_body(q_ref, kc_ref, kp_ref, vc_ref, vp_ref, b0_ref, br_ref, sink_ref, o_ref, *, nblk):
    k_all = jnp.concatenate([kp_ref[...], kc_ref[...]], axis=0) * (C_HEAD_DIM ** -0.5)
    v_all = jnp.concatenate([vp_ref[...], vc_ref[...]], axis=0)
    swapped = (pltpu.roll(k_all, C_HEAD_DIM, 1), pltpu.roll(v_all, C_HEAD_DIM, 1))
    low = lax.broadcasted_iota(jnp.int32, k_all.shape, 1) < C_HEAD_DIM
    for hk in range(C_KV_HEADS):
        k_lo = k_all if hk == 0 else swapped[0]
        k_hi = swapped[0] if hk == 0 else k_all
        kz = (jnp.where(low, k_lo, 0.0).astype(BF16), jnp.where(low, 0.0, k_hi).astype(BF16))
        vz = (v_all if hk == 0 else swapped[1])[:, :C_HEAD_DIM].astype(BF16)
        for g in range(C_GQA):
            hq = hk * C_GQA + g
            sk = sink_ref[hq]
            for i in range(nblk):
                keys = slice(i * WINDOW, (i + 2) * WINDOW)
                qt = q_ref[i * WINDOW:(i + 1) * WINDOW, (hq // 2) * LANE:(hq // 2 + 1) * LANE]
                bias = b0_ref[0, hq] if i == 0 else br_ref[0, hq]
                s = _dot_nt(kz[hq % 2][keys], qt) + bias
                m = jnp.maximum(jnp.max(s, axis=0, keepdims=True), sk)
                p = jnp.exp(s - m)
                den = jnp.sum(p, axis=0, keepdims=True) + jnp.exp(sk - m)
                o = _dot_tn(vz[keys], p.astype(BF16)) / den
                o_ref[hq * C_HEAD_DIM:(hq + 1) * C_HEAD_DIM, i * WINDOW:(i + 1) * WINDOW] = o.astype(BF16)


def _attn_prompt(q, k, v, band, sinks, nb, nblk_seq, nblk):
    steps = nblk_seq // nblk
    cur = lambda b, n: (b * steps + n, 0)
    prev = lambda b, n: (b * nblk_seq + jnp.maximum(n * nblk - 1, 0), 0)
    bspec = lambda m: pl.BlockSpec((1, C_HEADS, 2 * WINDOW, WINDOW), m)
    return pl.pallas_call(
        functools.partial(_attn_prompt_body, nblk=nblk),
        grid=(nb, steps),
        in_specs=[pl.BlockSpec((nblk * WINDOW, C_WIDTH), cur),
                  pl.BlockSpec((nblk * WINDOW, C_KV_WIDTH), cur), pl.BlockSpec((WINDOW, C_KV_WIDTH), prev),
                  pl.BlockSpec((nblk * WINDOW, C_KV_WIDTH), cur), pl.BlockSpec((WINDOW, C_KV_WIDTH), prev),
                  bspec(lambda b, n: (jnp.minimum(n, 1), 0, 0, 0)), bspec(lambda b, n: (1, 0, 0, 0)),
                  pl.BlockSpec(memory_space=pltpu.SMEM)],
        out_specs=pl.BlockSpec((C_WIDTH, nblk * WINDOW), lambda b, n: (0, b * steps + n)),
        out_shape=jax.ShapeDtypeStruct((C_WIDTH, nb * nblk_seq * WINDOW), BF16),
        compiler_params=_cparams("parallel", "arbitrary"),
        name="attn_prompt",
    )(q, k, k, v, v, band, band, sinks)


def _attn_sample_body(*refs, valid, lb, nseq):
    q_ref, kn_ref, vn_ref, kc_ref, vc_ref, b1_ref, b2_ref, sink_ref = refs[:8]
    o_ref, ko_ref, vo_ref, s1_scr, s2_scr, inv_scr, o_scr = refs[-7:]
    d = C_HEAD_DIM
    rs = 2 * valid
    lane_c = lax.broadcasted_iota(jnp.int32, (C_KV_WIDTH, lb), 1)
    lane_n = lax.broadcasted_iota(jnp.int32, (SUBLANE, C_KV_WIDTH), 1)
    pad = jnp.zeros((SUBLANE - valid, C_KV_WIDTH), F32)
    zero_half = jnp.zeros((d, lb), BF16)
    combos = [(hk, par) for hk in range(C_KV_HEADS) for par in range(2)]

    def new_rows(ref, s):
        return jnp.concatenate([ref[:, s, :], pad], axis=0)

    for s in range(nseq):
        qs = q_ref[:, s, :].astype(F32)
        kn, vn = new_rows(kn_ref, s), new_rows(vn_ref, s)
        kt, vt = kc_ref[0, s], vc_ref[0, s]
        kn_t, vn_t = kn.T, vn.T
        new_k, new_v = pltpu.roll(kt, lb - valid, 1), pltpu.roll(vt, lb - valid, 1)
        for t in range(valid):
            at = lane_c == lb - valid + t
            new_k = jnp.where(at, jnp.broadcast_to(kn_t[:, t:t + 1], (C_KV_WIDTH, lb)), new_k)
            new_v = jnp.where(at, jnp.broadcast_to(vn_t[:, t:t + 1], (C_KV_WIDTH, lb)), new_v)
        ko_ref[0, s] = new_k
        vo_ref[0, s] = new_v
        kn_sw = pltpu.roll(kn, d, 1)
        for c, (hk, par) in enumerate(combos):
            kth = kt[hk * d:(hk + 1) * d].astype(BF16)
            kz = jnp.concatenate([kth, zero_half] if par == 0 else [zero_half, kth], axis=0)
            src = kn if hk == par else kn_sw
            knz = jnp.where((lane_n < d) if par == 0 else (lane_n >= d), src, 0.0).astype(BF16)
            qq = jnp.concatenate([qs[:, (hk * 2 + j) * LANE:(hk * 2 + j + 1) * LANE] for j in range(2)],
                                 axis=0).astype(BF16)
            s1_scr[c, s * rs:(s + 1) * rs, :] = _dot(qq, kz)
            s2_scr[c, s * rs:(s + 1) * rs, :] = _dot_nt(qq, knz)

    for c, (hk, par) in enumerate(combos):
        s1 = s1_scr[c] * (d ** -0.5) + b1_ref[hk, par]
        s2 = s2_scr[c] * (d ** -0.5) + b2_ref[hk, par]
        sk = sink_ref[hk, par]
        m = jnp.maximum(jnp.maximum(jnp.max(s1, axis=-1, keepdims=True), jnp.max(s2, axis=-1, keepdims=True)), sk)
        p1, p2 = jnp.exp(s1 - m), jnp.exp(s2 - m)
        den = jnp.sum(p1, axis=-1, keepdims=True) + jnp.sum(p2, axis=-1, keepdims=True) + jnp.exp(sk - m)
        s1_scr[c] = p1
        s2_scr[c] = p2
        inv_scr[c] = jnp.broadcast_to(1.0 / den, (nseq * rs, SUBLANE))

    for s in range(nseq):
        rows = slice(s * rs, (s + 1) * rs)
        vt = vc_ref[0, s]
        vn = new_rows(vn_ref, s)
        for c, (hk, par) in enumerate(combos):
            vth = vt[hk * d:(hk + 1) * d].astype(BF16)
            vnh = vn[:, hk * d:(hk + 1) * d].astype(BF16)
            o = _dot_nt(s1_scr[c, rows, :].astype(BF16), vth) + _dot(s2_scr[c, rows, :].astype(BF16), vnh)
            o_scr[hk, rows, par * d:(par + 1) * d] = o * inv_scr[c, rows, 0:1]

    for hk in range(C_KV_HEADS):
        for j in range(2):
            for t in range(valid):
                tile = hk * 2 + j
                o_ref[t, :, tile * LANE:(tile + 1) * LANE] = o_scr[hk, pl.ds(j * valid + t, nseq, stride=rs), :]


def _attn_sample(q3, kn3, vn3, kc_all, vc_all, b1, b2, sink_col, layer, k_prev, v_prev):
    valid, nsb, _ = q3.shape
    lb = kc_all.shape[-1]
    nseq = 2 * SUBLANE
    assert 2 * valid == SUBLANE
    rows = lambda w: pl.BlockSpec((valid, nseq, w), lambda i: (0, i, 0))
    cblk = pl.BlockSpec((1, nseq, C_KV_WIDTH, lb), lambda i: (layer, i, 0, 0))
    b1, b2, sink_col = (jnp.tile(a, (1, 1, nseq, 1)) for a in (b1, b2, sink_col))
    in_specs = [rows(C_WIDTH), rows(C_KV_WIDTH), rows(C_KV_WIDTH), cblk, cblk,
                _full(b1.shape), _full(b2.shape), _full(sink_col.shape)] + [pl.BlockSpec(memory_space=pl.ANY)] * 2
    args = [q3, kn3, vn3, kc_all, vc_all, b1, b2, sink_col, k_prev, v_prev]
    aliases = {len(args) - 2: 1, len(args) - 1: 2}
    return pl.pallas_call(
        functools.partial(_attn_sample_body, valid=valid, lb=lb, nseq=nseq),
        grid=(nsb // nseq,),
        in_specs=in_specs,
        out_specs=[rows(C_WIDTH), cblk, cblk],
        out_shape=[jax.ShapeDtypeStruct((valid, nsb, C_WIDTH), F32), jax.ShapeDtypeStruct(kc_all.shape, F32),
                   jax.ShapeDtypeStruct(vc_all.shape, F32)],
        scratch_shapes=[pltpu.VMEM((2 * C_KV_HEADS, nseq * SUBLANE, lb), F32),
                        pltpu.VMEM((2 * C_KV_HEADS, nseq * SUBLANE, SUBLANE), F32),
                        pltpu.VMEM((2 * C_KV_HEADS, nseq * SUBLANE, SUBLANE), F32),
                        pltpu.VMEM((C_KV_HEADS, nseq * SUBLANE, LANE), F32)],
        input_output_aliases=aliases,
        compiler_params=_cparams("arbitrary"),
        name="attn_sample",
    )(*args)


def _merge_body(x_ref, ya_ref, yb_ref, u_ref, yc_ref, nw_ref, wg_ref, d5_ref, wglu_ref,
                wa_ref, wb_ref, wc_ref, wo_ref, o_ref, *, ya_features_major, yc_features_major):
    x = x_ref[...]
    h = _rms(x, nw_ref[...]).astype(BF16)
    g = jax.nn.gelu(_lanes(yb_ref) + d5_ref[...] * _lanes(u_ref))
    gv = _dot(g.astype(BF16), wglu_ref[...])
    yb = (gv[:, :B_WIDTH] * jax.nn.sigmoid(gv[:, B_WIDTH:])).astype(BF16)

    def gate(i):
        return jax.nn.sigmoid(_dot_nt(h, wg_ref[i * D_MODEL:(i + 1) * D_MODEL, :]))

    def branch(y_ref, w_ref, features_major):
        y = y_ref[...].astype(BF16)
        return _dot_tn(y, w_ref[...]) if features_major else _dot(y, w_ref[...])

    mixed = gate(0) * branch(ya_ref, wa_ref, ya_features_major)
    mixed = mixed + gate(1) * _dot(yb, wb_ref[...])
    mixed = mixed + gate(2) * branch(yc_ref, wc_ref, yc_features_major)
    o_ref[...] = x + _dot(mixed.astype(BF16), wo_ref[...])


def _merge(x, ya, yb, u, yc, lw, tm, ya_features_major, yc_features_major):
    t = x.shape[0]
    rows = lambda w: pl.BlockSpec((tm, w), lambda i: (i, 0))
    branch_spec = lambda fm: pl.BlockSpec((A_WIDTH, tm), lambda i: (0, i)) if fm else rows(A_WIDTH)
    lwt = lambda shape: _layer_weight(shape, lw["layer"])
    return pl.pallas_call(
        functools.partial(_merge_body, ya_features_major=ya_features_major, yc_features_major=yc_features_major),
        grid=(t // tm,),
        in_specs=[rows(D_MODEL), branch_spec(ya_features_major), _tile_major_spec(tm), _tile_major_spec(tm),
                  branch_spec(yc_features_major),
                  _full((1, D_MODEL)), lwt((3 * D_MODEL, D_MODEL)), _full((1, B_WIDTH)),
                  lwt((B_WIDTH, 2 * B_WIDTH)), lwt((A_WIDTH, D_MODEL)), lwt((B_WIDTH, D_MODEL)),
                  lwt((C_WIDTH, D_MODEL)), lwt((D_MODEL, D_MODEL))],
        out_specs=rows(D_MODEL),
        out_shape=jax.ShapeDtypeStruct((t, D_MODEL), F32),
        compiler_params=_cparams("parallel"),
        name="merge",
    )(x, ya, yb, u, yc, lw["norm1_w"], lw["w_gate"], lw["s5_d"], lw["w_glu"],
      lw["w_br_a"], lw["w_br_b"], lw["w_br_c"], lw["w_out"])


def _ffn_body(x_ref, nw_ref, wup_ref, wdn_ref, fnw_ref, o_ref, act_scr, *, final_norm):
    x = x_ref[...]
    h = _rms(x, nw_ref[...]).astype(BF16)
    for c in range(D_FF // FF_CHUNK):
        lo, hi = c * FF_CHUNK, (c + 1) * FF_CHUNK
        a = _dot(h, wup_ref[:, lo:hi])
        b = _dot(h, wup_ref[:, D_FF + lo:D_FF + hi])
        act_scr[:, lo:hi] = (jax.nn.silu(a) * b).astype(BF16)
    y = x + _dot(act_scr[...], wdn_ref[...])
    if final_norm:
        y = _rms(y, fnw_ref[...])
    o_ref[...] = y


def _ffn(x, lw, fnw, tm, final_norm):
    t = x.shape[0]
    rows = pl.BlockSpec((tm, D_MODEL), lambda i: (i, 0))
    return pl.pallas_call(
        functools.partial(_ffn_body, final_norm=final_norm),
        grid=(t // tm,),
        in_specs=[rows, _full((1, D_MODEL)), _layer_weight((D_MODEL, 2 * D_FF), lw["layer"]),
                  _layer_weight((D_FF, D_MODEL), lw["layer"]), _full((1, D_MODEL))],
        out_specs=rows,
        out_shape=jax.ShapeDtypeStruct((t, D_MODEL), F32),
        scratch_shapes=[pltpu.VMEM((tm, D_FF), BF16)],
        compiler_params=_cparams("parallel"),
        name="ffn",
    )(x, lw["norm2_w"], lw["w_ffn_up"], lw["w_ffn_down"], fnw)


def _row_tile(t, pref):
    tm = pref
    while t % tm:
        tm //= 2
    return tm


def kernel(x_prompt, x_sample, state_ssd, state_conv, state_s5_re, state_s5_im, cache_k, cache_v, norm1_w, w_in, conv_w, conv_b, ssd_a_log, ssd_dt_bias, ssd_d, ssd_norm_w, s5_lam_re, s5_lam_im, s5_log_dt, s5_b_re, s5_b_im, s5_c_re, s5_c_im, s5_d, s5_w_glu, attn_sinks, w_br_a, w_br_b, w_br_c, w_out, norm2_w, w_ffn_up, w_ffn_down, rel_bias, final_norm_w):
    nb, seq, _ = x_prompt.shape
    nsb, valid, _ = x_sample.shape
    depth = w_in.shape[0]
    lb = cache_k.shape[2]
    assert seq % A_CHUNK == 0 and seq % WINDOW == 0 and seq % (S5_Q * SUBLANE) == 0
    assert nsb == LANE and A_CONV - 1 <= valid <= S5_Q and lb == WINDOW
    nchunk = seq // A_CHUNK
    tp = nb * seq
    ts = valid * nsb

    band = _bias_band(rel_bias)

    def pair_rows(a):
        a = a.reshape(C_KV_HEADS, 2, 2, valid, a.shape[-1])
        return jnp.swapaxes(a, 1, 2).reshape(C_KV_HEADS, 2, 2 * valid, a.shape[-1])

    band_s = pair_rows(jnp.swapaxes(band[1, :, :, :valid], 1, 2))
    bias_s1 = band_s[..., :lb]
    bias_s2 = band_s[..., lb:lb + SUBLANE]
    fnw = final_norm_w.reshape(1, D_MODEL)

    xp = x_prompt.reshape(tp, D_MODEL)
    xs = jnp.swapaxes(x_sample, 0, 1).reshape(ts, D_MODEL)
    conv_all = jnp.swapaxes(state_conv, 1, 2)
    ssd_all = jnp.transpose(state_ssd, (0, 2, 3, 4, 1))
    s5re_all = jnp.transpose(state_s5_re, (0, 2, 3, 1)).reshape(depth, S5_STATE_COLS, nsb)
    s5im_all = jnp.transpose(state_s5_im, (0, 2, 3, 1)).reshape(depth, S5_STATE_COLS, nsb)
    kc_all = jnp.transpose(cache_k, (0, 1, 3, 4, 2)).reshape(depth, nsb, C_KV_WIDTH, lb)
    vc_all = jnp.transpose(cache_v, (0, 1, 3, 4, 2)).reshape(depth, nsb, C_KV_WIDTH, lb)
    tm_p = _row_tile(seq, 512)
    tm_s = _row_tile(ts, 512)
    rt_p = _row_tile(tp // S5_Q, 256)
    rb_p = _row_tile(seq // S5_Q, 128)
    attn_blk = _row_tile(seq // WINDOW, 8)

    new_p, s5_s = [], []
    conv_new, ssd_new = jnp.zeros(conv_all.shape, F32), jnp.zeros(ssd_all.shape, F32)
    k_new, v_new = jnp.zeros(kc_all.shape, F32), jnp.zeros(vc_all.shape, F32)
    off = [0, 512, 1280, 1288, 1800, 2312, 2440, 2568, 5640]
    w_in_t = jnp.swapaxes(w_in, 1, 2)
    w_pack = jnp.concatenate(
        [w_in_t[:, off[0]:off[1]], w_in_t[:, off[1]:off[2]], w_in_t[:, off[3]:off[4]], w_in_t[:, off[4]:off[5]],
         w_in_t[:, off[5]:off[6]], w_in_t[:, off[6]:off[7]],
         jnp.pad(w_in_t[:, off[2]:off[3]], ((0, 0), (0, LANE - A_HEADS), (0, 0)))], axis=1).astype(BF16)
    stacked = dict(
        w_gate=w_in_t[:, off[7]:off[8]].astype(BF16), w_glu=s5_w_glu.astype(BF16),
        w_br_a=w_br_a.astype(BF16), w_br_b=w_br_b.astype(BF16), w_br_c=w_br_c.astype(BF16),
        w_out=w_out.astype(BF16), w_ffn_up=w_ffn_up.astype(BF16), w_ffn_down=w_ffn_down.astype(BF16))
    for i in range(depth):
        sink_rows = pair_rows(jnp.broadcast_to(attn_sinks[i].reshape(C_HEADS, 1, 1), (C_HEADS, valid, 1)))
        lw = dict(
            stacked, layer=i, norm1_w=norm1_w[i].reshape(1, D_MODEL),
            conv_w=conv_w[i], conv_b=conv_b[i].reshape(1, A_CONV_DIM),
            a_log=jnp.pad(ssd_a_log[i], (0, LANE - A_HEADS)).reshape(1, LANE),
            dt_bias=jnp.pad(ssd_dt_bias[i], (0, LANE - A_HEADS)).reshape(1, LANE),
            d_exp=jnp.repeat(ssd_d[i], A_HEAD_DIM).reshape(1, A_WIDTH),
            ssd_norm_w=ssd_norm_w[i].reshape(1, A_WIDTH),
            ssd_norm_wb=jnp.broadcast_to(ssd_norm_w[i].reshape(A_WIDTH, 1), (A_WIDTH, LANE)),
            s5_d=s5_d[i].reshape(1, B_WIDTH), norm2_w=norm2_w[i].reshape(1, D_MODEL),
        )
        s5w = _s5_weights(s5_lam_re[i], s5_lam_im[i], s5_log_dt[i], s5_b_re[i], s5_b_im[i],
                          s5_c_re[i], s5_c_im[i], valid)
        last = i == depth - 1

        u, q, k, v, ya, ssd_h, conv_tail = _inproj_ssd(xp, w_pack, lw, nb, tm_p)
        sre, sim = _s5_state(u, s5w["bst_re"], s5w["bst_im"], rt_p)
        hre, him, fre, fim = _s5_scan(sre, sim, s5w["aq_re"], s5w["aq_im"], nb, rb_p)
        yb = _s5_out(u, hre, him, s5w["toep"], s5w["cst_re"], s5w["cst_im"], rt_p)
        yc = _attn_prompt(q, k, v, band, attn_sinks[i], nb, seq // WINDOW, attn_blk)
        x1 = _merge(xp, ya, yb, u, yc, lw, tm_p, False, True)
        xp = _ffn(x1, lw, fnw, tm_p, last)
        new_p.append((
            ssd_h.reshape(nb, A_HEADS, A_HEAD_DIM, A_STATE),
            conv_tail[:, SUBLANE - (A_CONV - 1):],
            fre.reshape(nb, B_GROUPS, B_STATE), fim.reshape(nb, B_GROUPS, B_STATE),
            k.reshape(nb, seq, C_KV_WIDTH)[:, seq - WINDOW:].reshape(nb, WINDOW, C_KV_HEADS, C_HEAD_DIM),
            v.reshape(nb, seq, C_KV_WIDTH)[:, seq - WINDOW:].reshape(nb, WINDOW, C_KV_HEADS, C_HEAD_DIM)))

        z, xbc, u, q, k, v, dt = _inproj(xs, lw["norm1_w"], w_pack, i, tm_s)
        ya, conv_new, ssd_new = _ssd_sample(xbc, z, dt, conv_all, ssd_all, lw, i, valid, conv_new, ssd_new)
        av_re = jnp.broadcast_to(s5w["av_re"].reshape(S5_STATE_COLS, 1), (S5_STATE_COLS, nsb))
        av_im = jnp.broadcast_to(s5w["av_im"].reshape(S5_STATE_COLS, 1), (S5_STATE_COLS, nsb))
        yb, fre, fim = _s5_sample(u, s5w, s5re_all, s5im_all, av_re, av_im, i, valid)
        yc, k_new, v_new = _attn_sample(
            q.reshape(valid, nsb, C_WIDTH), k.reshape(valid, nsb, C_KV_WIDTH), v.reshape(valid, nsb, C_KV_WIDTH),
            kc_all, vc_all, bias_s1, bias_s2, sink_rows, i, k_new, v_new)
        x1 = _merge(xs, ya, yb, u, yc.reshape(ts, C_WIDTH), lw, tm_s, True, False)
        xs = _ffn(x1, lw, fnw, tm_s, last)
        s5_s.append((fre, fim))

    def stack(states, j):
        return jnp.stack([s[j] for s in states], axis=0)

    def s5_state(j):
        return jnp.transpose(stack(s5_s, j).reshape(depth, B_GROUPS, B_STATE, nsb), (0, 3, 1, 2))

    def cache(a):
        return jnp.transpose(a.reshape(depth, nsb, C_KV_HEADS, C_HEAD_DIM, lb), (0, 1, 4, 2, 3))

    y_prompt = xp.reshape(nb, seq, D_MODEL)
    y_sample = jnp.swapaxes(xs.reshape(valid, nsb, D_MODEL), 0, 1)
    return (y_prompt, y_sample,
            stack(new_p, 0), stack(new_p, 1), stack(new_p, 2), stack(new_p, 3), stack(new_p, 4), stack(new_p, 5),
            jnp.transpose(ssd_new, (0, 4, 1, 2, 3)), jnp.swapaxes(conv_new, 1, 2), s5_state(0), s5_state(1),
            cache(k_new), cache(v_new))
```

```python
import functools
import math

import numpy as np
import jax
import jax.numpy as jnp
from jax import lax
from jax.experimental import pallas as pl
from jax.experimental.pallas import tpu as pltpu

F32 = jnp.float32
BF16 = jnp.bfloat16
HIGHEST = lax.Precision.HIGHEST

D_MODEL = 1024
A_HEAD_DIM = 64
A_WIDTH = 512
A_HEADS = 8
A_GROUPS = 2
A_STATE = 64
A_CONV = 4
A_CONV_DIM = 768
A_CHUNK = 128
B_CH = 16
B_WIDTH = 512
B_GROUPS = 32
B_STATE = 64
C_HEAD_DIM = 64
C_WIDTH = 512
C_HEADS = 8
C_KV_HEADS = 2
C_GQA = 4
C_KV_WIDTH = 128
WINDOW = 128
REL_BUCKETS = 32
REL_MAX_DIST = 128
D_FF = 2816
EPS = 1e-6

P_Z = 0
P_XBC = P_Z + A_WIDTH
P_U = P_XBC + A_CONV_DIM
P_Q = P_U + B_WIDTH
P_K = P_Q + C_WIDTH
P_V = P_K + C_KV_WIDTH
P_DT = P_V + C_KV_WIDTH
P_COLS = P_DT + 128

LANE = 128
SUBLANE = 8
S5_Q = 8
S5_OCT = 8
N_OCT = B_GROUPS // S5_OCT
S5_STATE_COLS = B_GROUPS * B_STATE
NEG = -1e30
VMEM_LIMIT = 56 * 1024 * 1024
FF_CHUNK = 256


def _cparams(*sem):
    return pltpu.CompilerParams(dimension_semantics=sem, vmem_limit_bytes=VMEM_LIMIT)


def _rms(x, w):
    return x * lax.rsqrt(jnp.mean(x * x, axis=-1, keepdims=True) + EPS) * w


def _dot(a, b):
    return jnp.dot(a, b, preferred_element_type=F32)


def _dot_nt(a, b, precision=None):
    return lax.dot_general(a, b, (((1,), (1,)), ((), ())), preferred_element_type=F32, precision=precision)


def _dot_tn(a, b):
    return lax.dot_general(a, b, (((0,), (0,)), ((), ())), preferred_element_type=F32)


def _full(shape):
    n = len(shape)
    return pl.BlockSpec(shape, lambda *_: (0,) * n)


def _layer_weight(shape, layer):
    return pl.BlockSpec((None,) + shape, lambda *_: (layer,) + (0,) * len(shape))


def _tile_major_spec(rows):
    return pl.BlockSpec((N_OCT, rows, LANE), lambda i: (0, i, 0))


def _lanes(ref):
    return jnp.concatenate([ref[k8] for k8 in range(N_OCT)], axis=1)


def _inproj_body(x_ref, nw_ref, w_ref, z_ref, xbc_ref, u_ref, q_ref, k_ref, v_ref, dt_ref):
    h = _rms(x_ref[...], nw_ref[...]).astype(BF16)

    def proj(lo, hi):
        return _dot_nt(h, w_ref[lo:hi, :])

    z_ref[...] = proj(P_Z, P_XBC).astype(BF16)
    xbc_ref[...] = proj(P_XBC, P_U)
    u = proj(P_U, P_Q)
    for k8 in range(N_OCT):
        u_ref[k8] = u[:, k8 * LANE:(k8 + 1) * LANE]
    q_ref[...] = proj(P_Q, P_K).astype(BF16)
    kvdt = proj(P_K, P_COLS)
    k_ref[...] = kvdt[:, :C_KV_WIDTH]
    v_ref[...] = kvdt[:, C_KV_WIDTH:2 * C_KV_WIDTH]
    dt_ref[...] = kvdt[:, 2 * C_KV_WIDTH:]


def _inproj(x, nw, w, layer, tm):
    t = x.shape[0]
    widths = [(A_WIDTH, BF16), (A_CONV_DIM, F32), None, (C_WIDTH, BF16),
              (C_KV_WIDTH, F32), (C_KV_WIDTH, F32), (LANE, F32)]
    spec = lambda w: _tile_major_spec(tm) if w is None else pl.BlockSpec((tm, w[0]), lambda i: (i, 0))
    shape = lambda w: (jax.ShapeDtypeStruct((N_OCT, t, LANE), F32) if w is None
                       else jax.ShapeDtypeStruct((t, w[0]), w[1]))
    return pl.pallas_call(
        _inproj_body,
        grid=(t // tm,),
        in_specs=[pl.BlockSpec((tm, D_MODEL), lambda i: (i, 0)), _full((1, D_MODEL)),
                  _layer_weight((P_COLS, D_MODEL), layer)],
        out_specs=[spec(w) for w in widths],
        out_shape=[shape(w) for w in widths],
        compiler_params=_cparams("parallel"),
        name="inproj",
    )(x, nw, w)


def _ssd_chunk(xbc, z, dt_raw, cw_ref, cb_ref, alog_ref, dtb_ref, dexp_ref, nw_ref, h_scr, tail_scr):
    q = xbc.shape[0]
    tail_scr[SUBLANE:SUBLANE + q] = xbc
    acc = cb_ref[...] + cw_ref[A_CONV - 1:A_CONV, :] * xbc
    for j in range(1, A_CONV):
        acc = acc + cw_ref[A_CONV - 1 - j:A_CONV - j, :] * tail_scr[SUBLANE - j:SUBLANE - j + q]
    tail_scr[0:SUBLANE] = xbc[q - SUBLANE:q]
    xc = jax.nn.silu(acc)
    x = xc[:, :A_WIDTH]
    bm = xc[:, A_WIDTH:A_WIDTH + A_GROUPS * A_STATE].astype(BF16)
    cm = xc[:, A_WIDTH + A_GROUPS * A_STATE:].astype(BF16)

    dt = jax.nn.softplus(dt_raw + dtb_ref[...])
    dta = dt * (-jnp.exp(alog_ref[...]))
    row = lax.broadcasted_iota(jnp.int32, (q, q), 0)
    col = lax.broadcasted_iota(jnp.int32, (q, q), 1)
    tri = row >= col
    a_cum = jnp.dot(tri.astype(F32), dta, preferred_element_type=F32, precision=HIGHEST)

    def expand(v):
        return jnp.concatenate([jnp.broadcast_to(v[:, k:k + 1], (q, A_HEAD_DIM)) for k in range(A_HEADS)], axis=1)

    a_cum_e = expand(a_cum)
    xs = x * expand(dt)
    xs_b = xs.astype(BF16)
    a_cum_t = a_cum.T

    h = h_scr[...]
    h_b = h.astype(BF16)
    hpg = A_HEADS // A_GROUPS
    gw = hpg * A_HEAD_DIM
    y_diag, y_off = [], []
    for g in range(A_GROUPS):
        cg = cm[:, g * A_STATE:(g + 1) * A_STATE]
        bg = bm[:, g * A_STATE:(g + 1) * A_STATE]
        cb = _dot_nt(cg, bg)
        for kk in range(hpg):
            k = g * hpg + kk
            seg = a_cum[:, k:k + 1] - a_cum_t[k:k + 1, :]
            decay = jnp.exp(jnp.where(tri, seg, -jnp.inf))
            y_diag.append(_dot((cb * decay).astype(BF16), xs_b[:, k * A_HEAD_DIM:(k + 1) * A_HEAD_DIM]))
        y_off.append(_dot_nt(cg, h_b[g * gw:(g + 1) * gw]))
    y = (jnp.concatenate(y_diag, axis=1) + jnp.concatenate(y_off, axis=1) * jnp.exp(a_cum_e)
         + dexp_ref[...] * x)
    y = y * jax.nn.silu(z)

    last = a_cum_e[q - 1:q, :]
    xs_dec = (xs * jnp.exp(last - a_cum_e)).astype(BF16)
    dec_col = jnp.exp(a_cum_e.T[:, q - 1:q])
    upd = [_dot_tn(xs_dec[:, g * gw:(g + 1) * gw], bm[:, g * A_STATE:(g + 1) * A_STATE]) for g in range(A_GROUPS)]
    h_scr[...] = dec_col * h + jnp.concatenate(upd, axis=0)
    return _rms(y, nw_ref[...]).astype(BF16)


def _inproj_ssd_body(x_ref, nw_ref, w_ref, cw_ref, cb_ref, alog_ref, dtb_ref, dexp_ref, snw_ref,
                     u_ref, q_ref, k_ref, v_ref, ya_ref, hout_ref, tail_ref, h_scr, tail_scr, *, nsteps, chunk):
    c = pl.program_id(1)

    @pl.when(c == 0)
    def _():
        h_scr[...] = jnp.zeros_like(h_scr)
        tail_scr[0:SUBLANE] = jnp.zeros((SUBLANE, A_CONV_DIM), F32)

    h = _rms(x_ref[...], nw_ref[...]).astype(BF16)

    def proj(lo, hi):
        return _dot_nt(h, w_ref[lo:hi, :])

    z = proj(P_Z, P_XBC)
    xbc = proj(P_XBC, P_U)
    kvdt = proj(P_K, P_COLS)
    k_ref[...] = kvdt[:, :C_KV_WIDTH]
    v_ref[...] = kvdt[:, C_KV_WIDTH:2 * C_KV_WIDTH]
    dt = kvdt[:, 2 * C_KV_WIDTH:]
    for i in range(x_ref.shape[0] // chunk):
        rows = slice(i * chunk, (i + 1) * chunk)
        ya_ref[rows, :] = _ssd_chunk(xbc[rows], z[rows], dt[rows], cw_ref, cb_ref, alog_ref, dtb_ref, dexp_ref,
                                     snw_ref, h_scr, tail_scr)
    u = proj(P_U, P_Q)
    for k8 in range(N_OCT):
        u_ref[k8] = u[:, k8 * LANE:(k8 + 1) * LANE]
    q_ref[...] = proj(P_Q, P_K).astype(BF16)

    @pl.when(c == nsteps - 1)
    def _():
        hout_ref[0] = h_scr[...]
        tail_ref[0] = tail_scr[0:SUBLANE]


def _inproj_ssd(x, w, lw, nb, tm):
    t = x.shape[0]
    nsteps = t // nb // tm
    rmap = lambda b, c: (b * nsteps + c, 0)
    rows = lambda wd: pl.BlockSpec((tm, wd), rmap)
    per_seq = lambda s: pl.BlockSpec((1,) + s, lambda b, c: (b, 0, 0))
    return pl.pallas_call(
        functools.partial(_inproj_ssd_body, nsteps=nsteps, chunk=A_CHUNK),
        grid=(nb, nsteps),
        in_specs=[rows(D_MODEL), _full((1, D_MODEL)), _layer_weight((P_COLS, D_MODEL), lw["layer"]),
                  _full((A_CONV, A_CONV_DIM)), _full((1, A_CONV_DIM)), _full((1, LANE)), _full((1, LANE)),
                  _full((1, A_WIDTH)), _full((1, A_WIDTH))],
        out_specs=[pl.BlockSpec((N_OCT, tm, LANE), lambda b, c: (0, b * nsteps + c, 0)), rows(C_WIDTH),
                   rows(C_KV_WIDTH), rows(C_KV_WIDTH), rows(A_WIDTH),
                   per_seq((A_WIDTH, A_STATE)), per_seq((SUBLANE, A_CONV_DIM))],
        out_shape=[jax.ShapeDtypeStruct((N_OCT, t, LANE), F32), jax.ShapeDtypeStruct((t, C_WIDTH), BF16),
                   jax.ShapeDtypeStruct((t, C_KV_WIDTH), F32), jax.ShapeDtypeStruct((t, C_KV_WIDTH), F32),
                   jax.ShapeDtypeStruct((t, A_WIDTH), BF16), jax.ShapeDtypeStruct((nb, A_WIDTH, A_STATE), F32),
                   jax.ShapeDtypeStruct((nb, SUBLANE, A_CONV_DIM), F32)],
        scratch_shapes=[pltpu.VMEM((A_WIDTH, A_STATE), F32), pltpu.VMEM((SUBLANE + A_CHUNK, A_CONV_DIM), F32)],
        compiler_params=_cparams("parallel", "arbitrary"),
        name="inproj_ssd",
    )(x, lw["norm1_w"], w, lw["conv_w"], lw["conv_b"], lw["a_log"], lw["dt_bias"], lw["d_exp"], lw["ssd_norm_w"])


def _ssd_sample_body(*refs, valid):
    n_in = 13
    (xbc_ref, z_ref, dt_ref, conv0_ref, h0_ref, cw_ref, cb_ref, alog_ref, dtb_ref, dexp_ref, nwb_ref) = refs[:11]
    y_ref, conv_out_ref, h_out_ref = refs[n_in:n_in + 3]
    xs_scr, dx_scr, z_scr, y_scr, bt_scr, ct_scr, da_scr = refs[n_in + 3:]
    k = pl.program_id(0)
    ns = LANE

    @pl.when(k == 0)
    def _():
        rows = [conv0_ref[0, r] for r in range(A_CONV - 1)] + [xbc_ref[t * ns:(t + 1) * ns, :] for t in range(valid)]
        for t in range(valid):
            acc = cb_ref[...]
            for j in range(A_CONV):
                acc = acc + cw_ref[A_CONV - 1 - j:A_CONV - j, :] * rows[t + A_CONV - 1 - j]
            xc = jax.nn.silu(acc)
            dt = jax.nn.softplus(dt_ref[t * ns:(t + 1) * ns, :] + dtb_ref[...])
            dt_t = dt.T
            da_scr[t] = jnp.exp((dt * (-jnp.exp(alog_ref[...]))).T)
            x = xc[:, :A_WIDTH]
            x_t = x.T
            for kk in range(A_HEADS):
                hs = slice(kk * A_HEAD_DIM, (kk + 1) * A_HEAD_DIM)
                xs_scr[t, hs, :] = x_t[hs] * dt_t[kk:kk + 1, :]
            dx_scr[t] = (x * dexp_ref[...]).T
            bt_scr[t] = xc[:, A_WIDTH:A_WIDTH + A_GROUPS * A_STATE].T
            ct_scr[t] = xc[:, A_WIDTH + A_GROUPS * A_STATE:].T
            z_scr[t] = jax.nn.silu(z_ref[t * ns:(t + 1) * ns, :].astype(F32)).T
        for r in range(A_CONV - 1):
            conv_out_ref[0, r] = rows[valid + r]

    g0 = pl.multiple_of((k // (A_HEADS // A_GROUPS)) * A_STATE, A_STATE)

    def per_p(p, carry):
        h = h0_ref[0, 0, p]
        row = k * A_HEAD_DIM + p
        for t in range(valid):
            da = da_scr[t, pl.ds(k, 1), :]
            xr = xs_scr[t, pl.ds(row, 1), :]
            h = da * h + xr * bt_scr[t, pl.ds(g0, A_STATE), :]
            y_scr[t, pl.ds(row, 1), :] = jnp.sum(h * ct_scr[t, pl.ds(g0, A_STATE), :], axis=0, keepdims=True)
        h_out_ref[0, 0, p] = h
        return carry

    lax.fori_loop(0, A_HEAD_DIM, per_p, 0, unroll=2)

    @pl.when(k == A_HEADS - 1)
    def _():
        for t in range(valid):
            y = (y_scr[t] + dx_scr[t]) * z_scr[t]
            y = y * lax.rsqrt(jnp.mean(y * y, axis=0, keepdims=True) + EPS) * nwb_ref[...]
            y_ref[:, t * ns:(t + 1) * ns] = y.astype(BF16)


def _ssd_sample(xbc, z, dt, conv_all, h_all, lw, layer, valid, conv_prev, h_prev):
    rows = valid * LANE
    hblk = pl.BlockSpec((1, 1, A_HEAD_DIM, A_STATE, LANE), lambda k: (layer, k, 0, 0, 0))
    cblk = pl.BlockSpec((1, A_CONV - 1, LANE, A_CONV_DIM), lambda k: (layer, 0, 0, 0))
    in_specs = [_full((rows, A_CONV_DIM)), _full((rows, A_WIDTH)), _full((rows, LANE)), cblk, hblk,
                _full((A_CONV, A_CONV_DIM)), _full((1, A_CONV_DIM)), _full((1, LANE)), _full((1, LANE)),
                _full((1, A_WIDTH)), _full((A_WIDTH, LANE))] + [pl.BlockSpec(memory_space=pl.ANY)] * 2
    args = [xbc, z, dt, conv_all, h_all, lw["conv_w"], lw["conv_b"], lw["a_log"], lw["dt_bias"], lw["d_exp"],
            lw["ssd_norm_wb"], conv_prev, h_prev]
    aliases = {len(args) - 2: 1, len(args) - 1: 2}
    big = lambda n: pltpu.VMEM((valid, n, LANE), F32)
    return pl.pallas_call(
        functools.partial(_ssd_sample_body, valid=valid),
        grid=(A_HEADS,),
        in_specs=in_specs,
        out_specs=[_full((A_WIDTH, rows)), cblk, hblk],
        out_shape=[jax.ShapeDtypeStruct((A_WIDTH, rows), BF16), jax.ShapeDtypeStruct(conv_all.shape, F32),
                   jax.ShapeDtypeStruct(h_all.shape, F32)],
        scratch_shapes=[big(A_WIDTH), big(A_WIDTH), big(A_WIDTH), big(A_WIDTH), big(LANE), big(LANE), big(LANE)],
        input_output_aliases=aliases,
        compiler_params=_cparams("arbitrary"),
        name="ssd_sample",
    )(*args)


def _s5w_body(lre_ref, lim_ref, ldt_ref, btr_ref, bti_ref, cr_ref, ci_ref,
              bre_ref, bim_ref, bsre_ref, bsim_ref, cre_ref, cim_ref, toep_ref, ap_ref, cd_scr, bb_scr, *, valid):
    h, q = B_STATE, S5_Q
    for ref in (bre_ref, bim_ref, bsre_ref, bsim_ref, cre_ref, cim_ref, toep_ref):
        ref[...] = jnp.zeros(ref.shape, BF16)
    r_i = lax.broadcasted_iota(jnp.int32, (q * B_CH, q * LANE), 0)
    c_i = lax.broadcasted_iota(jnp.int32, (q * B_CH, q * LANE), 1)
    d = lax.broadcasted_iota(jnp.int32, (2 * SUBLANE, h), 0).astype(F32)
    for g8 in range(S5_OCT):
        lr, li = lre_ref[g8], lim_ref[g8]
        step = jnp.exp(ldt_ref[g8])
        er = jnp.exp(lr * step * d)
        ang = li * step * d
        ar, ai = er * jnp.cos(ang), er * jnp.sin(ang)
        nr, ni = ar[1:2] - 1.0, ai[1:2]
        den = lr * lr + li * li
        wr, wi = (nr * lr + ni * li) / den, (ni * lr - nr * li) / den
        btr, bti = btr_ref[g8], bti_ref[g8]
        bbr, bbi = wr * btr - wi * bti, wr * bti + wi * btr
        cr, ci = cr_ref[g8], ci_ref[g8]
        bb_scr[:, 0:h] = bbr
        bb_scr[:, h:2 * h] = bbi
        cols = slice(g8 * h, (g8 + 1) * h)
        for s in range(q):
            rows = slice(s * LANE + g8 * B_CH, s * LANE + (g8 + 1) * B_CH)
            e = q - 1 - s
            bre_ref[0, rows, cols] = (ar[e:e + 1] * bbr - ai[e:e + 1] * bbi).astype(BF16)
            bim_ref[0, rows, cols] = (ai[e:e + 1] * bbr + ar[e:e + 1] * bbi).astype(BF16)
            if s < valid:
                e = valid - 1 - s
                bsre_ref[0, rows, cols] = (ar[e:e + 1] * bbr - ai[e:e + 1] * bbi).astype(BF16)
                bsim_ref[0, rows, cols] = (ai[e:e + 1] * bbr + ar[e:e + 1] * bbi).astype(BF16)
            cre_ref[0, rows, cols] = (cr * ar[s + 1:s + 2] - ci * ai[s + 1:s + 2]).astype(BF16)
            cim_ref[0, rows, cols] = (-(cr * ai[s + 1:s + 2] + ci * ar[s + 1:s + 2])).astype(BF16)
            cd_scr[s * B_CH:(s + 1) * B_CH, 0:h] = cr * ar[s:s + 1] - ci * ai[s:s + 1]
            cd_scr[s * B_CH:(s + 1) * B_CH, h:2 * h] = -(cr * ai[s:s + 1] + ci * ar[s:s + 1])
        kdt = _dot_nt(bb_scr[...], cd_scr[...], precision=HIGHEST)
        place = (c_i == (r_i >> 4) * LANE + g8 * B_CH + (r_i & (B_CH - 1))).astype(F32)
        slab = jnp.dot(kdt, place, preferred_element_type=F32, precision=HIGHEST).astype(BF16)
        for s in range(q):
            toep_ref[0, s * LANE + g8 * B_CH:s * LANE + (g8 + 1) * B_CH, s * LANE:] = slab[:, 0:(q - s) * LANE]
        ap_ref[g8, 0:1, 0:h] = ar[q:q + 1]
        ap_ref[g8, 0:1, h:2 * h] = ai[q:q + 1]
        ap_ref[g8, 1:2, 0:h] = ar[valid:valid + 1]
        ap_ref[g8, 1:2, h:2 * h] = ai[valid:valid + 1]


def _s5_weights(lam_re, lam_im, log_dt, b_re, b_im, c_re, c_im, valid):
    g, h, q = B_GROUPS, B_STATE, S5_Q
    vec = pl.BlockSpec((S5_OCT, 1, h), lambda i: (i, 0, 0))
    mat = pl.BlockSpec((S5_OCT, B_CH, h), lambda i: (i, 0, 0))
    st = pl.BlockSpec((1, q * LANE, S5_OCT * h), lambda i: (i, 0, 0))
    st_shape = jax.ShapeDtypeStruct((N_OCT, q * LANE, S5_OCT * h), BF16)
    bre, bim, bsre, bsim, cre, cim, toep, ap = pl.pallas_call(
        functools.partial(_s5w_body, valid=valid),
        grid=(N_OCT,),
        in_specs=[vec, vec, vec, mat, mat, mat, mat],
        out_specs=[st] * 6 + [pl.BlockSpec((1, q * LANE, q * LANE), lambda i: (i, 0, 0)),
                              pl.BlockSpec((S5_OCT, 2, 2 * h), lambda i: (i, 0, 0))],
        out_shape=[st_shape] * 6 + [jax.ShapeDtypeStruct((N_OCT, q * LANE, q * LANE), BF16),
                                    jax.ShapeDtypeStruct((g, 2, 2 * h), F32)],
        scratch_shapes=[pltpu.VMEM((q * B_CH, 2 * h), F32), pltpu.VMEM((B_CH, 2 * h), F32)],
        compiler_params=_cparams("parallel"),
        name="s5_weights",
    )(lam_re.reshape(g, 1, h), lam_im.reshape(g, 1, h), jnp.broadcast_to(log_dt.reshape(g, 1, 1), (g, 1, h)),
      jnp.swapaxes(b_re, 1, 2), jnp.swapaxes(b_im, 1, 2), c_re, c_im)
    return dict(
        bst_re=bre, bst_im=bim, bsts_re=bsre, bsts_im=bsim, cst_re=cre, cst_im=cim, toep=toep,
        aq_re=ap[:, 0, :h].reshape(1, S5_STATE_COLS), aq_im=ap[:, 0, h:].reshape(1, S5_STATE_COLS),
        av_re=ap[:, 1, :h].reshape(1, S5_STATE_COLS), av_im=ap[:, 1, h:].reshape(1, S5_STATE_COLS),
    )


def _oct_rows(u_ref, k8, rt):
    return jnp.concatenate([u_ref[k8, pl.ds(j, rt, stride=S5_Q), :] for j in range(S5_Q)], axis=1).astype(BF16)


def _toep_dot(uo, toep_ref, k8):
    two = 2 * LANE
    cols = [_dot(uo[:, :(t + 1) * two], toep_ref[k8, :(t + 1) * two, t * two:(t + 1) * two])
            for t in range(uo.shape[1] // two)]
    return jnp.concatenate(cols, axis=1)


def _s5_state_body(u_ref, bre_ref, bim_ref, sre_ref, sim_ref):
    w = S5_OCT * B_STATE
    for k8 in range(N_OCT):
        uo = _oct_rows(u_ref, k8, sre_ref.shape[0])
        sre_ref[:, k8 * w:(k8 + 1) * w] = _dot(uo, bre_ref[k8])
        sim_ref[:, k8 * w:(k8 + 1) * w] = _dot(uo, bim_ref[k8])


def _s5_state(u, bre, bim, rt):
    r = u.shape[1] // S5_Q
    wspec = _full((N_OCT, S5_Q * LANE, S5_OCT * B_STATE))
    ospec = pl.BlockSpec((rt, S5_STATE_COLS), lambda i: (i, 0))
    return pl.pallas_call(
        _s5_state_body,
        grid=(r // rt,),
        in_specs=[_tile_major_spec(rt * S5_Q), wspec, wspec],
        out_specs=[ospec, ospec],
        out_shape=[jax.ShapeDtypeStruct((r, S5_STATE_COLS), F32)] * 2,
        compiler_params=_cparams("parallel"),
        name="s5_state",
    )(u, bre, bim)


def _s5_scan_body(sre_ref, sim_ref, ar_ref, ai_ref, hre_ref, him_ref, fre_ref, fim_ref, cr_scr, ci_scr, *, rb, nblk):
    i = pl.program_id(0)
    nb = sre_ref.shape[0]

    @pl.when(i == 0)
    def _():
        cr_scr[...] = jnp.zeros_like(cr_scr)
        ci_scr[...] = jnp.zeros_like(ci_scr)

    ar, ai = ar_ref[...], ai_ref[...]

    def step(r, carry):
        out = []
        for b in range(nb):
            hr, hi = carry[2 * b], carry[2 * b + 1]
            hre_ref[b, pl.ds(r, 1), :] = hr
            him_ref[b, pl.ds(r, 1), :] = hi
            sr, si = sre_ref[b, pl.ds(r, 1), :], sim_ref[b, pl.ds(r, 1), :]
            out += [ar * hr - ai * hi + sr, ai * hr + ar * hi + si]
        return tuple(out)

    init = tuple(s[b] for b in range(nb) for s in (cr_scr, ci_scr))
    fin = lax.fori_loop(0, rb, step, init)
    for b in range(nb):
        cr_scr[b] = fin[2 * b]
        ci_scr[b] = fin[2 * b + 1]

    @pl.when(i == nblk - 1)
    def _():
        for b in range(nb):
            fre_ref[b] = fin[2 * b]
            fim_ref[b] = fin[2 * b + 1]


def _s5_scan(sre, sim, ar, ai, nb, rb):
    r = sre.shape[0] // nb
    nblk = r // rb
    blk = pl.BlockSpec((nb, rb, S5_STATE_COLS), lambda i: (0, i, 0))
    vec = _full((1, S5_STATE_COLS))
    fin = _full((nb, 1, S5_STATE_COLS))
    per_batch = lambda a: a.reshape(nb, r, S5_STATE_COLS)
    hre, him, fre, fim = pl.pallas_call(
        functools.partial(_s5_scan_body, rb=rb, nblk=nblk),
        grid=(nblk,),
        in_specs=[blk, blk, vec, vec],
        out_specs=[blk, blk, fin, fin],
        out_shape=[jax.ShapeDtypeStruct((nb, r, S5_STATE_COLS), F32)] * 2
        + [jax.ShapeDtypeStruct((nb, 1, S5_STATE_COLS), F32)] * 2,
        scratch_shapes=[pltpu.VMEM((nb, 1, S5_STATE_COLS), F32)] * 2,
        compiler_params=_cparams("arbitrary"),
        name="s5_scan",
    )(per_batch(sre), per_batch(sim), ar, ai)
    return hre.reshape(nb * r, S5_STATE_COLS), him.reshape(nb * r, S5_STATE_COLS), fre, fim


def _s5_sample_body(u_ref, toep_ref, cre_ref, cim_ref, bre_ref, bim_ref, hre_ref, him_ref, ar_ref, ai_ref,
                    y_ref, fre_ref, fim_ref, *, valid):
    w = S5_OCT * B_STATE
    ns = LANE
    for k8 in range(N_OCT):
        blocks = [u_ref[k8, t * ns:(t + 1) * ns, :] for t in range(valid)]
        uo = jnp.concatenate(blocks, axis=1).astype(BF16)
        uo_t = jnp.concatenate([b.T for b in blocks], axis=0).astype(BF16)
        hr, hi = hre_ref[0, k8 * w:(k8 + 1) * w, :], him_ref[0, k8 * w:(k8 + 1) * w, :]
        yo = (_toep_dot(uo, toep_ref, k8) + _dot_nt(hr.T.astype(BF16), cre_ref[k8])
              + _dot_nt(hi.T.astype(BF16), cim_ref[k8]))
        for t in range(valid):
            y_ref[k8, t * ns:(t + 1) * ns, :] = yo[:, t * LANE:(t + 1) * LANE]
        ar, ai = ar_ref[k8 * w:(k8 + 1) * w, :], ai_ref[k8 * w:(k8 + 1) * w, :]
        fre_ref[k8 * w:(k8 + 1) * w, :] = ar * hr - ai * hi + _dot_tn(bre_ref[k8], uo_t)
        fim_ref[k8 * w:(k8 + 1) * w, :] = ai * hr + ar * hi + _dot_tn(bim_ref[k8], uo_t)


def _s5_sample(u, s5w, hre_all, him_all, ar, ai, layer, valid):
    rows = valid * LANE
    vq = valid * LANE
    sub = lambda c: pl.BlockSpec((N_OCT, vq, c), lambda i: (0, 0, 0))
    hblk = pl.BlockSpec((1, S5_STATE_COLS, LANE), lambda i: (layer, 0, 0))
    st = _full((S5_STATE_COLS, LANE))
    return pl.pallas_call(
        functools.partial(_s5_sample_body, valid=valid),
        grid=(1,),
        in_specs=[_tile_major_spec(rows), sub(vq), sub(S5_OCT * B_STATE), sub(S5_OCT * B_STATE),
                  sub(S5_OCT * B_STATE), sub(S5_OCT * B_STATE), hblk, hblk, st, st],
        out_specs=[_tile_major_spec(rows), st, st],
        out_shape=[jax.ShapeDtypeStruct((N_OCT, rows, LANE), F32)] + [jax.ShapeDtypeStruct((S5_STATE_COLS, LANE), F32)] * 2,
        compiler_params=_cparams("arbitrary"),
        name="s5_sample",
    )(u, s5w["toep"], s5w["cst_re"], s5w["cst_im"], s5w["bsts_re"], s5w["bsts_im"], hre_all, him_all, ar, ai)


def _s5_out_body(u_ref, hre_ref, him_ref, toep_ref, cre_ref, cim_ref, y_ref):
    w = S5_OCT * B_STATE
    rt = hre_ref.shape[0]
    for k8 in range(N_OCT):
        uo = _oct_rows(u_ref, k8, rt)
        yo = (_toep_dot(uo, toep_ref, k8)
              + _dot_nt(hre_ref[:, k8 * w:(k8 + 1) * w].astype(BF16), cre_ref[k8])
              + _dot_nt(him_ref[:, k8 * w:(k8 + 1) * w].astype(BF16), cim_ref[k8]))
        for j in range(S5_Q):
            y_ref[k8, pl.ds(j, rt, stride=S5_Q), :] = yo[:, j * LANE:(j + 1) * LANE]


def _s5_out(u, hre, him, toep, cre, cim, rt):
    r = u.shape[1] // S5_Q
    hblk = pl.BlockSpec((rt, S5_STATE_COLS), lambda i: (i, 0))
    return pl.pallas_call(
        _s5_out_body,
        grid=(r // rt,),
        in_specs=[_tile_major_spec(rt * S5_Q), hblk, hblk, _full((N_OCT, S5_Q * LANE, S5_Q * LANE)),
                  _full((N_OCT, S5_Q * LANE, S5_OCT * B_STATE)), _full((N_OCT, S5_Q * LANE, S5_OCT * B_STATE))],
        out_specs=_tile_major_spec(rt * S5_Q),
        out_shape=jax.ShapeDtypeStruct(u.shape, F32),
        compiler_params=_cparams("parallel"),
        name="s5_out",
    )(u, hre, him, toep, cre, cim)


def _band_body(bucket_ref, table_ref, o_ref):
    bucket = bucket_ref[...]
    col = lax.broadcasted_iota(jnp.int32, (WINDOW, 2 * WINDOW), 1)
    for hq in range(C_HEADS):
        base = jnp.full((1, 2 * WINDOW), NEG, F32)
        for b in range(REL_BUCKETS):
            base = jnp.where(bucket == b, table_ref[b, hq], base)
        tile = pltpu.roll(jnp.broadcast_to(base, (WINDOW, 2 * WINDOW)), 0, 1, stride=1, stride_axis=0)
        o_ref[1, hq] = tile.T
        o_ref[0, hq] = jnp.where(col >= WINDOW, tile, NEG).T


def _bias_band(table):
    lane = np.arange(2 * WINDOW)
    n = WINDOW - lane
    exact = REL_BUCKETS // 2
    nf = np.maximum(n, 1).astype(np.float32)
    large = exact + (np.log(nf / np.float32(exact)) / np.float32(math.log(REL_MAX_DIST / exact))
                     * np.float32(REL_BUCKETS - exact)).astype(np.int32)
    bucket = np.where(n < exact, n, np.minimum(large, REL_BUCKETS - 1))
    bucket = np.where((n >= 0) & (n < WINDOW), bucket, -1).astype(np.int32)
    return pl.pallas_call(
        _band_body,
        in_specs=[pl.BlockSpec(memory_space=pltpu.VMEM), pl.BlockSpec(memory_space=pltpu.SMEM)],
        out_specs=pl.BlockSpec(memory_space=pltpu.VMEM),
        out_shape=jax.ShapeDtypeStruct((2, C_HEADS, 2 * WINDOW, WINDOW), F32),
        name="rel_bias_band",
    )(jnp.asarray(bucket).reshape(1, 2 * WINDOW), table)


def _attn_prompt_body(q_ref, kc_ref, kp_ref, vc_ref, vp_ref, b0_ref, br_ref, sink_ref, o_ref, s_scr, p_scr, *, nblk):
    k_all = jnp.concatenate([kp_ref[...], kc_ref[...]], axis=0) * (C_HEAD_DIM ** -0.5)
    v_all = jnp.concatenate([vp_ref[...], vc_ref[...]], axis=0)
    swapped = (pltpu.roll(k_all, C_HEAD_DIM, 1), pltpu.roll(v_all, C_HEAD_DIM, 1))
    low = lax.broadcasted_iota(jnp.int32, k_all.shape, 1) < C_HEAD_DIM
    kz, vz = [], []
    for hk in range(C_KV_HEADS):
        k_lo = k_all if hk == 0 else swapped[0]
        k_hi = swapped[0] if hk == 0 else k_all
        kz.append((jnp.where(low, k_lo, 0.0).astype(BF16), jnp.where(low, 0.0, k_hi).astype(BF16)))
        vz.append((v_all if hk == 0 else swapped[1])[:, :C_HEAD_DIM].astype(BF16))
    for i in range(nblk):
        keys = slice(i * WINDOW, (i + 2) * WINDOW)
        slot = i % 2
        for hq in range(C_HEADS):
            qt = q_ref[i * WINDOW:(i + 1) * WINDOW, (hq // 2) * LANE:(hq // 2 + 1) * LANE]
            s_scr[slot, hq] = _dot_nt(kz[hq // C_GQA][hq % 2][keys], qt)
        inv = []
        for hq in range(C_HEADS):
            sk = sink_ref[hq]
            s = s_scr[slot, hq] + (b0_ref[0, hq] if i == 0 else br_ref[0, hq])
            m = jnp.maximum(jnp.max(s, axis=0, keepdims=True), sk)
            p = jnp.exp(s - m)
            inv.append(1.0 / (jnp.sum(p, axis=0, keepdims=True) + jnp.exp(sk - m)))
            p_scr[slot, hq] = p.astype(BF16)
        for hq in range(C_HEADS):
            o = _dot_tn(vz[hq // C_GQA][keys], p_scr[slot, hq]) * inv[hq]
            o_ref[hq * C_HEAD_DIM:(hq + 1) * C_HEAD_DIM, i * WINDOW:(i + 1) * WINDOW] = o.astype(BF16)


def _attn_prompt(q, k, v, band, sinks, nb, nblk_seq, nblk):
    steps = nblk_seq // nblk
    cur = lambda b, n: (b * steps + n, 0)
    prev = lambda b, n: (b * nblk_seq + jnp.maximum(n * nblk - 1, 0), 0)
    bspec = lambda m: pl.BlockSpec((1, C_HEADS, 2 * WINDOW, WINDOW), m)
    return pl.pallas_call(
        functools.partial(_attn_prompt_body, nblk=nblk),
        grid=(nb, steps),
        in_specs=[pl.BlockSpec((nblk * WINDOW, C_WIDTH), cur),
                  pl.BlockSpec((nblk * WINDOW, C_KV_WIDTH), cur), pl.BlockSpec((WINDOW, C_KV_WIDTH), prev),
                  pl.BlockSpec((nblk * WINDOW, C_KV_WIDTH), cur), pl.BlockSpec((WINDOW, C_KV_WIDTH), prev),
                  bspec(lambda b, n: (jnp.minimum(n, 1), 0, 0, 0)), bspec(lambda b, n: (1, 0, 0, 0)),
                  pl.BlockSpec(memory_space=pltpu.SMEM)],
        out_specs=pl.BlockSpec((C_WIDTH, nblk * WINDOW), lambda b, n: (0, b * steps + n)),
        out_shape=jax.ShapeDtypeStruct((C_WIDTH, nb * nblk_seq * WINDOW), BF16),
        scratch_shapes=[pltpu.VMEM((2, C_HEADS, 2 * WINDOW, WINDOW), F32),
                        pltpu.VMEM((2, C_HEADS, 2 * WINDOW, WINDOW), BF16)],
        compiler_params=_cparams("parallel", "arbitrary"),
        name="attn_prompt",
    )(q, k, k, v, v, band, band, sinks)


def _attn_sample_body(*refs, valid, lb, nseq):
    q_ref, kn_ref, vn_ref, kc_ref, vc_ref, b1_ref, b2_ref, sink_ref = refs[:8]
    o_ref, ko_ref, vo_ref, s1_scr, s2_scr, inv_scr, o_scr = refs[-7:]
    d = C_HEAD_DIM
    rs = 2 * valid
    lane_c = lax.broadcasted_iota(jnp.int32, (C_KV_WIDTH, lb), 1)
    lane_n = lax.broadcasted_iota(jnp.int32, (SUBLANE, C_KV_WIDTH), 1)
    pad = jnp.zeros((SUBLANE - valid, C_KV_WIDTH), F32)
    zero_half = jnp.zeros((d, lb), BF16)
    combos = [(hk, par) for hk in range(C_KV_HEADS) for par in range(2)]

    def new_rows(ref, s):
        return jnp.concatenate([ref[:, s, :], pad], axis=0)

    for s in range(nseq):
        qs = q_ref[:, s, :].astype(F32)
        kn, vn = new_rows(kn_ref, s), new_rows(vn_ref, s)
        kt, vt = kc_ref[0, s], vc_ref[0, s]
        kn_t, vn_t = kn.T, vn.T
        new_k, new_v = pltpu.roll(kt, lb - valid, 1), pltpu.roll(vt, lb - valid, 1)
        for t in range(valid):
            at = lane_c == lb - valid + t
            new_k = jnp.where(at, jnp.broadcast_to(kn_t[:, t:t + 1], (C_KV_WIDTH, lb)), new_k)
            new_v = jnp.where(at, jnp.broadcast_to(vn_t[:, t:t + 1], (C_KV_WIDTH, lb)), new_v)
        ko_ref[0, s] = new_k
        vo_ref[0, s] = new_v
        kn_sw = pltpu.roll(kn, d, 1)
        for c, (hk, par) in enumerate(combos):
            kth = kt[hk * d:(hk + 1) * d].astype(BF16)
            kz = jnp.concatenate([kth, zero_half] if par == 0 else [zero_half, kth], axis=0)
            src = kn if hk == par else kn_sw
            knz = jnp.where((lane_n < d) if par == 0 else (lane_n >= d), src, 0.0).astype(BF16)
            qq = jnp.concatenate([qs[:, (hk * 2 + j) * LANE:(hk * 2 + j + 1) * LANE] for j in range(2)],
                                 axis=0).astype(BF16)
            s1_scr[c, s * rs:(s + 1) * rs, :] = _dot(qq, kz)
            s2_scr[c, s * rs:(s + 1) * rs, :] = _dot_nt(qq, knz)

    for c, (hk, par) in enumerate(combos):
        s1 = s1_scr[c] * (d ** -0.5) + b1_ref[hk, par]
        s2 = s2_scr[c] * (d ** -0.5) + b2_ref[hk, par]
        sk = sink_ref[hk, par]
        m = jnp.maximum(jnp.maximum(jnp.max(s1, axis=-1, keepdims=True), jnp.max(s2, axis=-1, keepdims=True)), sk)
        p1, p2 = jnp.exp(s1 - m), jnp.exp(s2 - m)
        den = jnp.sum(p1, axis=-1, keepdims=True) + jnp.sum(p2, axis=-1, keepdims=True) + jnp.exp(sk - m)
        s1_scr[c] = p1
        s2_scr[c] = p2
        inv_scr[c] = jnp.broadcast_to(1.0 / den, (nseq * rs, SUBLANE))

    for s in range(nseq):
        rows = slice(s * rs, (s + 1) * rs)
        vt = vc_ref[0, s]
        vn = new_rows(vn_ref, s)
        for c, (hk, par) in enumerate(combos):
            vth = vt[hk * d:(hk + 1) * d].astype(BF16)
            vnh = vn[:, hk * d:(hk + 1) * d].astype(BF16)
            o = _dot_nt(s1_scr[c, rows, :].astype(BF16), vth) + _dot(s2_scr[c, rows, :].astype(BF16), vnh)
            o_scr[hk, rows, par * d:(par + 1) * d] = o * inv_scr[c, rows, 0:1]

    for hk in range(C_KV_HEADS):
        for j in range(2):
            for t in range(valid):
                tile = hk * 2 + j
                o_ref[t, :, tile * LANE:(tile + 1) * LANE] = o_scr[hk, pl.ds(j * valid + t, nseq, stride=rs), :]


def _attn_sample(q3, kn3, vn3, kc_all, vc_all, b1, b2, sink_col, layer, k_prev, v_prev):
    valid, nsb, _ = q3.shape
    lb = kc_all.shape[-1]
    nseq = 2 * SUBLANE
    assert 2 * valid == SUBLANE
    rows = lambda w: pl.BlockSpec((valid, nseq, w), lambda i: (0, i, 0))
    cblk = pl.BlockSpec((1, nseq, C_KV_WIDTH, lb), lambda i: (layer, i, 0, 0))
    b1, b2, sink_col = (jnp.tile(a, (1, 1, nseq, 1)) for a in (b1, b2, sink_col))
    in_specs = [rows(C_WIDTH), rows(C_KV_WIDTH), rows(C_KV_WIDTH), cblk, cblk,
                _full(b1.shape), _full(b2.shape), _full(sink_col.shape)] + [pl.BlockSpec(memory_space=pl.ANY)] * 2
    args = [q3, kn3, vn3, kc_all, vc_all, b1, b2, sink_col, k_prev, v_prev]
    aliases = {len(args) - 2: 1, len(args) - 1: 2}
    return pl.pallas_call(
        functools.partial(_attn_sample_body, valid=valid, lb=lb, nseq=nseq),
        grid=(nsb // nseq,),
        in_specs=in_specs,
        out_specs=[rows(C_WIDTH), cblk, cblk],
        out_shape=[jax.ShapeDtypeStruct((valid, nsb, C_WIDTH), F32), jax.ShapeDtypeStruct(kc_all.shape, F32),
                   jax.ShapeDtypeStruct(vc_all.shape, F32)],
        scratch_shapes=[pltpu.VMEM((2 * C_KV_HEADS, nseq * SUBLANE, lb), F32),
                        pltpu.VMEM((2 * C_KV_HEADS, nseq * SUBLANE, SUBLANE), F32),
                        pltpu.VMEM((2 * C_KV_HEADS, nseq * SUBLANE, SUBLANE), F32),
                        pltpu.VMEM((C_KV_HEADS, nseq * SUBLANE, LANE), F32)],
        input_output_aliases=aliases,
        compiler_params=_cparams("arbitrary"),
        name="attn_sample",
    )(*args)


def _merge_body(x_ref, ya_ref, yb_ref, u_ref, yc_ref, nw_ref, wg_ref, d5_ref, wglu_ref,
                wa_ref, wb_ref, wc_ref, wo_ref, o_ref, *, ya_features_major, yc_features_major):
    x = x_ref[...]
    h = _rms(x, nw_ref[...]).astype(BF16)
    g = jax.nn.gelu(_lanes(yb_ref) + d5_ref[...] * _lanes(u_ref))
    gv = _dot(g.astype(BF16), wglu_ref[...])
    yb = (gv[:, :B_WIDTH] * jax.nn.sigmoid(gv[:, B_WIDTH:])).astype(BF16)

    def gate(i):
        return jax.nn.sigmoid(_dot_nt(h, wg_ref[i * D_MODEL:(i + 1) * D_MODEL, :]))

    def branch(y_ref, w_ref, features_major):
        y = y_ref[...].astype(BF16)
        return _dot_tn(y, w_ref[...]) if features_major else _dot(y, w_ref[...])

    mixed = gate(0) * branch(ya_ref, wa_ref, ya_features_major)
    mixed = mixed + gate(1) * _dot(yb, wb_ref[...])
    mixed = mixed + gate(2) * branch(yc_ref, wc_ref, yc_features_major)
    o_ref[...] = x + _dot(mixed.astype(BF16), wo_ref[...])


def _merge(x, ya, yb, u, yc, lw, tm, ya_features_major, yc_features_major):
    t = x.shape[0]
    rows = lambda w: pl.BlockSpec((tm, w), lambda i: (i, 0))
    branch_spec = lambda fm: pl.BlockSpec((A_WIDTH, tm), lambda i: (0, i)) if fm else rows(A_WIDTH)
    lwt = lambda shape: _layer_weight(shape, lw["layer"])
    return pl.pallas_call(
        functools.partial(_merge_body, ya_features_major=ya_features_major, yc_features_major=yc_features_major),
        grid=(t // tm,),
        in_specs=[rows(D_MODEL), branch_spec(ya_features_major), _tile_major_spec(tm), _tile_major_spec(tm),
                  branch_spec(yc_features_major),
                  _full((1, D_MODEL)), lwt((3 * D_MODEL, D_MODEL)), _full((1, B_WIDTH)),
                  lwt((B_WIDTH, 2 * B_WIDTH)), lwt((A_WIDTH, D_MODEL)), lwt((B_WIDTH, D_MODEL)),
                  lwt((C_WIDTH, D_MODEL)), lwt((D_MODEL, D_MODEL))],
        out_specs=rows(D_MODEL),
        out_shape=jax.ShapeDtypeStruct((t, D_MODEL), F32),
        compiler_params=_cparams("parallel"),
        name="merge",
    )(x, ya, yb, u, yc, lw["norm1_w"], lw["w_gate"], lw["s5_d"], lw["w_glu"],
      lw["w_br_a"], lw["w_br_b"], lw["w_br_c"], lw["w_out"])


def _ffn_body(x_ref, nw_ref, wup_ref, wdn_ref, fnw_ref, o_ref, act_scr, *, final_norm):
    x = x_ref[...]
    h = _rms(x, nw_ref[...]).astype(BF16)
    for c in range(D_FF // FF_CHUNK):
        lo, hi = c * FF_CHUNK, (c + 1) * FF_CHUNK
        a = _dot(h, wup_ref[:, lo:hi])
        b = _dot(h, wup_ref[:, D_FF + lo:D_FF + hi])
        act_scr[:, lo:hi] = (jax.nn.silu(a) * b).astype(BF16)
    y = x + _dot(act_scr[...], wdn_ref[...])
    if final_norm:
        y = _rms(y, fnw_ref[...])
    o_ref[...] = y


def _ffn(x, lw, fnw, tm, final_norm):
    t = x.shape[0]
    rows = pl.BlockSpec((tm, D_MODEL), lambda i: (i, 0))
    return pl.pallas_call(
        functools.partial(_ffn_body, final_norm=final_norm),
        grid=(t // tm,),
        in_specs=[rows, _full((1, D_MODEL)), _layer_weight((D_MODEL, 2 * D_FF), lw["layer"]),
                  _layer_weight((D_FF, D_MODEL), lw["layer"]), _full((1, D_MODEL))],
        out_specs=rows,
        out_shape=jax.ShapeDtypeStruct((t, D_MODEL), F32),
        scratch_shapes=[pltpu.VMEM((tm, D_FF), BF16)],
        compiler_params=_cparams("parallel"),
        name="ffn",
    )(x, lw["norm2_w"], lw["w_ffn_up"], lw["w_ffn_down"], fnw)


def _row_tile(t, pref):
    tm = pref
    while t % tm:
        tm //= 2
    return tm


def kernel(x_prompt, x_sample, state_ssd, state_conv, state_s5_re, state_s5_im, cache_k, cache_v, norm1_w, w_in, conv_w, conv_b, ssd_a_log, ssd_dt_bias, ssd_d, ssd_norm_w, s5_lam_re, s5_lam_im, s5_log_dt, s5_b_re, s5_b_im, s5_c_re, s5_c_im, s5_d, s5_w_glu, attn_sinks, w_br_a, w_br_b, w_br_c, w_out, norm2_w, w_ffn_up, w_ffn_down, rel_bias, final_norm_w):
    nb, seq, _ = x_prompt.shape
    nsb, valid, _ = x_sample.shape
    depth = w_in.shape[0]
    lb = cache_k.shape[2]
    assert seq % A_CHUNK == 0 and seq % WINDOW == 0 and seq % (S5_Q * SUBLANE) == 0
    assert nsb == LANE and A_CONV - 1 <= valid <= S5_Q and lb == WINDOW
    nchunk = seq // A_CHUNK
    tp = nb * seq
    ts = valid * nsb

    band = _bias_band(rel_bias)

    def pair_rows(a):
        a = a.reshape(C_KV_HEADS, 2, 2, valid, a.shape[-1])
        return jnp.swapaxes(a, 1, 2).reshape(C_KV_HEADS, 2, 2 * valid, a.shape[-1])

    band_s = pair_rows(jnp.swapaxes(band[1, :, :, :valid], 1, 2))
    bias_s1 = band_s[..., :lb]
    bias_s2 = band_s[..., lb:lb + SUBLANE]
    fnw = final_norm_w.reshape(1, D_MODEL)

    xp = x_prompt.reshape(tp, D_MODEL)
    xs = jnp.swapaxes(x_sample, 0, 1).reshape(ts, D_MODEL)
    conv_all = jnp.swapaxes(state_conv, 1, 2)
    ssd_all = jnp.transpose(state_ssd, (0, 2, 3, 4, 1))
    s5re_all = jnp.transpose(state_s5_re, (0, 2, 3, 1)).reshape(depth, S5_STATE_COLS, nsb)
    s5im_all = jnp.transpose(state_s5_im, (0, 2, 3, 1)).reshape(depth, S5_STATE_COLS, nsb)
    kc_all = jnp.transpose(cache_k, (0, 1, 3, 4, 2)).reshape(depth, nsb, C_KV_WIDTH, lb)
    vc_all = jnp.transpose(cache_v, (0, 1, 3, 4, 2)).reshape(depth, nsb, C_KV_WIDTH, lb)
    tm_p = _row_tile(seq, 512)
    tm_s = _row_tile(ts, 512)
    rt_p = _row_tile(tp // S5_Q, 256)
    rb_p = _row_tile(seq // S5_Q, 128)
    attn_blk = _row_tile(seq // WINDOW, 8)

    new_p, s5_s = [], []
    conv_new, ssd_new = jnp.zeros(conv_all.shape, F32), jnp.zeros(ssd_all.shape, F32)
    k_new, v_new = jnp.zeros(kc_all.shape, F32), jnp.zeros(vc_all.shape, F32)
    off = [0, 512, 1280, 1288, 1800, 2312, 2440, 2568, 5640]
    w_in_t = jnp.swapaxes(w_in, 1, 2)
    w_pack = jnp.concatenate(
        [w_in_t[:, off[0]:off[1]], w_in_t[:, off[1]:off[2]], w_in_t[:, off[3]:off[4]], w_in_t[:, off[4]:off[5]],
         w_in_t[:, off[5]:off[6]], w_in_t[:, off[6]:off[7]],
         jnp.pad(w_in_t[:, off[2]:off[3]], ((0, 0), (0, LANE - A_HEADS), (0, 0)))], axis=1).astype(BF16)
    stacked = dict(
        w_gate=w_in_t[:, off[7]:off[8]].astype(BF16), w_glu=s5_w_glu.astype(BF16),
        w_br_a=w_br_a.astype(BF16), w_br_b=w_br_b.astype(BF16), w_br_c=w_br_c.astype(BF16),
        w_out=w_out.astype(BF16), w_ffn_up=w_ffn_up.astype(BF16), w_ffn_down=w_ffn_down.astype(BF16))
    for i in range(depth):
        sink_rows = pair_rows(jnp.broadcast_to(attn_sinks[i].reshape(C_HEADS, 1, 1), (C_HEADS, valid, 1)))
        lw = dict(
            stacked, layer=i, norm1_w=norm1_w[i].reshape(1, D_MODEL),
            conv_w=conv_w[i], conv_b=conv_b[i].reshape(1, A_CONV_DIM),
            a_log=jnp.pad(ssd_a_log[i], (0, LANE - A_HEADS)).reshape(1, LANE),
            dt_bias=jnp.pad(ssd_dt_bias[i], (0, LANE - A_HEADS)).reshape(1, LANE),
            d_exp=jnp.repeat(ssd_d[i], A_HEAD_DIM).reshape(1, A_WIDTH),
            ssd_norm_w=ssd_norm_w[i].reshape(1, A_WIDTH),
            ssd_norm_wb=jnp.broadcast_to(ssd_norm_w[i].reshape(A_WIDTH, 1), (A_WIDTH, LANE)),
            s5_d=s5_d[i].reshape(1, B_WIDTH), norm2_w=norm2_w[i].reshape(1, D_MODEL),
        )
        s5w = _s5_weights(s5_lam_re[i], s5_lam_im[i], s5_log_dt[i], s5_b_re[i], s5_b_im[i],
                          s5_c_re[i], s5_c_im[i], valid)
        last = i == depth - 1

        u, q, k, v, ya, ssd_h, conv_tail = _inproj_ssd(xp, w_pack, lw, nb, tm_p)
        sre, sim = _s5_state(u, s5w["bst_re"], s5w["bst_im"], rt_p)
        hre, him, fre, fim = _s5_scan(sre, sim, s5w["aq_re"], s5w["aq_im"], nb, rb_p)
        yb = _s5_out(u, hre, him, s5w["toep"], s5w["cst_re"], s5w["cst_im"], rt_p)
        yc = _attn_prompt(q, k, v, band, attn_sinks[i], nb, seq // WINDOW, attn_blk)
        x1 = _merge(xp, ya, yb, u, yc, lw, tm_p, False, True)
        xp = _ffn(x1, lw, fnw, tm_p, last)
        new_p.append((
            ssd_h.reshape(nb, A_HEADS, A_HEAD_DIM, A_STATE),
            conv_tail[:, SUBLANE - (A_CONV - 1):],
            fre.reshape(nb, B_GROUPS, B_STATE), fim.reshape(nb, B_GROUPS, B_STATE),
            k.reshape(nb, seq, C_KV_WIDTH)[:, seq - WINDOW:].reshape(nb, WINDOW, C_KV_HEADS, C_HEAD_DIM),
            v.reshape(nb, seq, C_KV_WIDTH)[:, seq - WINDOW:].reshape(nb, WINDOW, C_KV_HEADS, C_HEAD_DIM)))

        z, xbc, u, q, k, v, dt = _inproj(xs, lw["norm1_w"], w_pack, i, tm_s)
        ya, conv_new, ssd_new = _ssd_sample(xbc, z, dt, conv_all, ssd_all, lw, i, valid, conv_new, ssd_new)
        av_re = jnp.broadcast_to(s5w["av_re"].reshape(S5_STATE_COLS, 1), (S5_STATE_COLS, nsb))
        av_im = jnp.broadcast_to(s5w["av_im"].reshape(S5_STATE_COLS, 1), (S5_STATE_COLS, nsb))
        yb, fre, fim = _s5_sample(u, s5w, s5re_all, s5im_all, av_re, av_im, i, valid)
        yc, k_new, v_new = _attn_sample(
            q.reshape(valid, nsb, C_WIDTH), k.reshape(valid, nsb, C_KV_WIDTH), v.reshape(valid, nsb, C_KV_WIDTH),
            kc_all, vc_all, bias_s1, bias_s2, sink_rows, i, k_new, v_new)
        x1 = _merge(xs, ya, yb, u, yc.reshape(ts, C_WIDTH), lw, tm_s, True, False)
        xs = _ffn(x1, lw, fnw, tm_s, last)
        s5_s.append((fre, fim))

    def stack(states, j):
        return jnp.stack([s[j] for s in states], axis=0)

    def s5_state(j):
        return jnp.transpose(stack(s5_s, j).reshape(depth, B_GROUPS, B_STATE, nsb), (0, 3, 1, 2))

    def cache(a):
        return jnp.transpose(a.reshape(depth, nsb, C_KV_HEADS, C_HEAD_DIM, lb), (0, 1, 4, 2, 3))

    y_prompt = xp.reshape(nb, seq, D_MODEL)
    y_sample = jnp.swapaxes(xs.reshape(valid, nsb, D_MODEL), 0, 1)
    return (y_prompt, y_sample,
            stack(new_p, 0), stack(new_p, 1), stack(new_p, 2), stack(new_p, 3), stack(new_p, 4), stack(new_p, 5),
            jnp.transpose(ssd_new, (0, 4, 1, 2, 3)), jnp.swapaxes(conv_new, 1, 2), s5_state(0), s5_state(1),
            cache(k_new), cache(v_new))
```

```python
import functools
import math

import numpy as np
import jax
import jax.numpy as jnp
from jax import lax
from jax.experimental import pallas as pl
from jax.experimental.pallas import tpu as pltpu

F32 = jnp.float32
BF16 = jnp.bfloat16
HIGHEST = lax.Precision.HIGHEST

D_MODEL = 1024
A_HEAD_DIM = 64
A_WIDTH = 512
A_HEADS = 8
A_GROUPS = 2
A_STATE = 64
A_CONV = 4
A_CONV_DIM = 768
A_CHUNK = 128
B_CH = 16
B_WIDTH = 512
B_GROUPS = 32
B_STATE = 64
C_HEAD_DIM = 64
C_WIDTH = 512
C_HEADS = 8
C_KV_HEADS = 2
C_GQA = 4
C_KV_WIDTH = 128
WINDOW = 128
REL_BUCKETS = 32
REL_MAX_DIST = 128
D_FF = 2816
EPS = 1e-6

P_Z = 0
P_XBC = P_Z + A_WIDTH
P_U = P_XBC + A_CONV_DIM
P_Q = P_U + B_WIDTH
P_K = P_Q + C_WIDTH
P_V = P_K + C_KV_WIDTH
P_DT = P_V + C_KV_WIDTH
P_COLS = P_DT + 128

LANE = 128
SUBLANE = 8
S5_Q = 8
S5_OCT = 8
N_OCT = B_GROUPS // S5_OCT
S5_STATE_COLS = B_GROUPS * B_STATE
NEG = -1e30
VMEM_LIMIT = 56 * 1024 * 1024
FF_CHUNK = 256


def _cparams(*sem):
    return pltpu.CompilerParams(dimension_semantics=sem, vmem_limit_bytes=VMEM_LIMIT)


def _rms(x, w):
    return x * lax.rsqrt(jnp.mean(x * x, axis=-1, keepdims=True) + EPS) * w


def _dot(a, b):
    return jnp.dot(a, b, preferred_element_type=F32)


def _dot_nt(a, b, precision=None):
    return lax.dot_general(a, b, (((1,), (1,)), ((), ())), preferred_element_type=F32, precision=precision)


def _dot_tn(a, b):
    return lax.dot_general(a, b, (((0,), (0,)), ((), ())), preferred_element_type=F32)


def _full(shape):
    n = len(shape)
    return pl.BlockSpec(shape, lambda *_: (0,) * n)


def _layer_weight(shape, layer):
    return pl.BlockSpec((None,) + shape, lambda *_: (layer,) + (0,) * len(shape))


def _tile_major_spec(rows):
    return pl.BlockSpec((N_OCT, rows, LANE), lambda i: (0, i, 0))


def _lanes(ref):
    return jnp.concatenate([ref[k8] for k8 in range(N_OCT)], axis=1)


def _inproj_body(x_ref, nw_ref, w_ref, z_ref, xbc_ref, u_ref, q_ref, k_ref, v_ref, dt_ref):
    h = _rms(x_ref[...], nw_ref[...]).astype(BF16)

    def proj(lo, hi):
        return _dot_nt(h, w_ref[lo:hi, :])

    z_ref[...] = proj(P_Z, P_XBC).astype(BF16)
    xbc_ref[...] = proj(P_XBC, P_U)
    u = proj(P_U, P_Q)
    for k8 in range(N_OCT):
        u_ref[k8] = u[:, k8 * LANE:(k8 + 1) * LANE]
    q_ref[...] = proj(P_Q, P_K).astype(BF16)
    kvdt = proj(P_K, P_COLS)
    k_ref[...] = kvdt[:, :C_KV_WIDTH]
    v_ref[...] = kvdt[:, C_KV_WIDTH:2 * C_KV_WIDTH]
    dt_ref[...] = kvdt[:, 2 * C_KV_WIDTH:]


def _inproj(x, nw, w, layer, tm):
    t = x.shape[0]
    widths = [(A_WIDTH, BF16), (A_CONV_DIM, F32), None, (C_WIDTH, BF16),
              (C_KV_WIDTH, F32), (C_KV_WIDTH, F32), (LANE, F32)]
    spec = lambda w: _tile_major_spec(tm) if w is None else pl.BlockSpec((tm, w[0]), lambda i: (i, 0))
    shape = lambda w: (jax.ShapeDtypeStruct((N_OCT, t, LANE), F32) if w is None
                       else jax.ShapeDtypeStruct((t, w[0]), w[1]))
    return pl.pallas_call(
        _inproj_body,
        grid=(t // tm,),
        in_specs=[pl.BlockSpec((tm, D_MODEL), lambda i: (i, 0)), _full((1, D_MODEL)),
                  _layer_weight((P_COLS, D_MODEL), layer)],
        out_specs=[spec(w) for w in widths],
        out_shape=[shape(w) for w in widths],
        compiler_params=_cparams("parallel"),
        name="inproj",
    )(x, nw, w)


def _ssd_chunk(xbc, z, dt_raw, cw_ref, cb_ref, alog_ref, dtb_ref, dexp_ref, nw_ref, h_scr, tail_scr):
    q = xbc.shape[0]
    tail_scr[SUBLANE:SUBLANE + q] = xbc
    acc = cb_ref[...] + cw_ref[A_CONV - 1:A_CONV, :] * xbc
    for j in range(1, A_CONV):
        acc = acc + cw_ref[A_CONV - 1 - j:A_CONV - j, :] * tail_scr[SUBLANE - j:SUBLANE - j + q]
    tail_scr[0:SUBLANE] = xbc[q - SUBLANE:q]
    xc = jax.nn.silu(acc)
    x = xc[:, :A_WIDTH]
    bm = xc[:, A_WIDTH:A_WIDTH + A_GROUPS * A_STATE].astype(BF16)
    cm = xc[:, A_WIDTH + A_GROUPS * A_STATE:].astype(BF16)

    dt = jax.nn.softplus(dt_raw + dtb_ref[...])
    dta = dt * (-jnp.exp(alog_ref[...]))
    row = lax.broadcasted_iota(jnp.int32, (q, q), 0)
    col = lax.broadcasted_iota(jnp.int32, (q, q), 1)
    tri = row >= col
    a_cum = jnp.dot(tri.astype(F32), dta, preferred_element_type=F32, precision=HIGHEST)

    def expand(v):
        return jnp.concatenate([jnp.broadcast_to(v[:, k:k + 1], (q, A_HEAD_DIM)) for k in range(A_HEADS)], axis=1)

    a_cum_e = expand(a_cum)
    xs = x * expand(dt)
    xs_b = xs.astype(BF16)
    a_cum_t = a_cum.T

    h = h_scr[...]
    h_b = h.astype(BF16)
    hpg = A_HEADS // A_GROUPS
    gw = hpg * A_HEAD_DIM
    y_diag, y_off = [], []
    for g in range(A_GROUPS):
        cg = cm[:, g * A_STATE:(g + 1) * A_STATE]
        bg = bm[:, g * A_STATE:(g + 1) * A_STATE]
        cb = _dot_nt(cg, bg)
        for kk in range(hpg):
            k = g * hpg + kk
            seg = a_cum[:, k:k + 1] - a_cum_t[k:k + 1, :]
            decay = jnp.exp(jnp.where(tri, seg, -jnp.inf))
            y_diag.append(_dot((cb * decay).astype(BF16), xs_b[:, k * A_HEAD_DIM:(k + 1) * A_HEAD_DIM]))
        y_off.append(_dot_nt(cg, h_b[g * gw:(g + 1) * gw]))
    y = (jnp.concatenate(y_diag, axis=1) + jnp.concatenate(y_off, axis=1) * jnp.exp(a_cum_e)
         + dexp_ref[...] * x)
    y = y * jax.nn.silu(z)

    last = a_cum_e[q - 1:q, :]
    xs_dec = (xs * jnp.exp(last - a_cum_e)).astype(BF16)
    dec_col = jnp.exp(a_cum_e.T[:, q - 1:q])
    upd = [_dot_tn(xs_dec[:, g * gw:(g + 1) * gw], bm[:, g * A_STATE:(g + 1) * A_STATE]) for g in range(A_GROUPS)]
    h_scr[...] = dec_col * h + jnp.concatenate(upd, axis=0)
    return _rms(y, nw_ref[...]).astype(BF16)


def _inproj_ssd_body(x_ref, nw_ref, w_ref, cw_ref, cb_ref, alog_ref, dtb_ref, dexp_ref, snw_ref,
                     u_ref, q_ref, k_ref, v_ref, ya_ref, hout_ref, tail_ref, h_scr, tail_scr, *, nsteps, chunk):
    c = pl.program_id(1)

    @pl.when(c == 0)
    def _():
        h_scr[...] = jnp.zeros_like(h_scr)
        tail_scr[0:SUBLANE] = jnp.zeros((SUBLANE, A_CONV_DIM), F32)

    h = _rms(x_ref[...], nw_ref[...]).astype(BF16)

    def proj(lo, hi):
        return _dot_nt(h, w_ref[lo:hi, :])

    z = proj(P_Z, P_XBC)
    xbc = proj(P_XBC, P_U)
    kvdt = proj(P_K, P_COLS)
    k_ref[...] = kvdt[:, :C_KV_WIDTH]
    v_ref[...] = kvdt[:, C_KV_WIDTH:2 * C_KV_WIDTH]
    dt = kvdt[:, 2 * C_KV_WIDTH:]
    for i in range(x_ref.shape[0] // chunk):
        rows = slice(i * chunk, (i + 1) * chunk)
        ya_ref[rows, :] = _ssd_chunk(xbc[rows], z[rows], dt[rows], cw_ref, cb_ref, alog_ref, dtb_ref, dexp_ref,
                                     snw_ref, h_scr, tail_scr)
    u = proj(P_U, P_Q)
    for k8 in range(N_OCT):
        u_ref[k8] = u[:, k8 * LANE:(k8 + 1) * LANE]
    q_ref[...] = proj(P_Q, P_K).astype(BF16)

    @pl.when(c == nsteps - 1)
    def _():
        hout_ref[0] = h_scr[...]
        tail_ref[0] = tail_scr[0:SUBLANE]


def _inproj_ssd(x, w, lw, nb, tm):
    t = x.shape[0]
    nsteps = t // nb // tm
    rmap = lambda b, c: (b * nsteps + c, 0)
    rows = lambda wd: pl.BlockSpec((tm, wd), rmap)
    per_seq = lambda s: pl.BlockSpec((1,) + s, lambda b, c: (b, 0, 0))
    return pl.pallas_call(
        functools.partial(_inproj_ssd_body, nsteps=nsteps, chunk=A_CHUNK),
        grid=(nb, nsteps),
        in_specs=[rows(D_MODEL), _full((1, D_MODEL)), _layer_weight((P_COLS, D_MODEL), lw["layer"]),
                  _full((A_CONV, A_CONV_DIM)), _full((1, A_CONV_DIM)), _full((1, LANE)), _full((1, LANE)),
                  _full((1, A_WIDTH)), _full((1, A_WIDTH))],
        out_specs=[pl.BlockSpec((N_OCT, tm, LANE), lambda b, c: (0, b * nsteps + c, 0)), rows(C_WIDTH),
                   rows(C_KV_WIDTH), rows(C_KV_WIDTH), rows(A_WIDTH),
                   per_seq((A_WIDTH, A_STATE)), per_seq((SUBLANE, A_CONV_DIM))],
        out_shape=[jax.ShapeDtypeStruct((N_OCT, t, LANE), F32), jax.ShapeDtypeStruct((t, C_WIDTH), BF16),
                   jax.ShapeDtypeStruct((t, C_KV_WIDTH), F32), jax.ShapeDtypeStruct((t, C_KV_WIDTH), F32),
                   jax.ShapeDtypeStruct((t, A_WIDTH), BF16), jax.ShapeDtypeStruct((nb, A_WIDTH, A_STATE), F32),
                   jax.ShapeDtypeStruct((nb, SUBLANE, A_CONV_DIM), F32)],
        scratch_shapes=[pltpu.VMEM((A_WIDTH, A_STATE), F32), pltpu.VMEM((SUBLANE + A_CHUNK, A_CONV_DIM), F32)],
        compiler_params=_cparams("parallel", "arbitrary"),
        name="inproj_ssd",
    )(x, lw["norm1_w"], w, lw["conv_w"], lw["conv_b"], lw["a_log"], lw["dt_bias"], lw["d_exp"], lw["ssd_norm_w"])


def _ssd_sample_body(*refs, valid):
    n_in = 13
    (xbc_ref, z_ref, dt_ref, conv0_ref, h0_ref, cw_ref, cb_ref, alog_ref, dtb_ref, dexp_ref, nwb_ref) = refs[:11]
    y_ref, conv_out_ref, h_out_ref = refs[n_in:n_in + 3]
    xs_scr, dx_scr, z_scr, y_scr, bt_scr, ct_scr, da_scr = refs[n_in + 3:]
    k = pl.program_id(0)
    ns = LANE

    @pl.when(k == 0)
    def _():
        rows = [conv0_ref[0, r] for r in range(A_CONV - 1)] + [xbc_ref[t * ns:(t + 1) * ns, :] for t in range(valid)]
        for t in range(valid):
            acc = cb_ref[...]
            for j in range(A_CONV):
                acc = acc + cw_ref[A_CONV - 1 - j:A_CONV - j, :] * rows[t + A_CONV - 1 - j]
            xc = jax.nn.silu(acc)
            dt = jax.nn.softplus(dt_ref[t * ns:(t + 1) * ns, :] + dtb_ref[...])
            dt_t = dt.T
            da_scr[t] = jnp.exp((dt * (-jnp.exp(alog_ref[...]))).T)
            x = xc[:, :A_WIDTH]
            x_t = x.T
            for kk in range(A_HEADS):
                hs = slice(kk * A_HEAD_DIM, (kk + 1) * A_HEAD_DIM)
                xs_scr[t, hs, :] = x_t[hs] * dt_t[kk:kk + 1, :]
            dx_scr[t] = (x * dexp_ref[...]).T
            bt_scr[t] = xc[:, A_WIDTH:A_WIDTH + A_GROUPS * A_STATE].T
            ct_scr[t] = xc[:, A_WIDTH + A_GROUPS * A_STATE:].T
            z_scr[t] = jax.nn.silu(z_ref[t * ns:(t + 1) * ns, :].astype(F32)).T
        for r in range(A_CONV - 1):
            conv_out_ref[0, r] = rows[valid + r]

    g0 = pl.multiple_of((k // (A_HEADS // A_GROUPS)) * A_STATE, A_STATE)

    def per_p(p, carry):
        h = h0_ref[0, 0, p]
        row = k * A_HEAD_DIM + p
        for t in range(valid):
            da = da_scr[t, pl.ds(k, 1), :]
            xr = xs_scr[t, pl.ds(row, 1), :]
            h = da * h + xr * bt_scr[t, pl.ds(g0, A_STATE), :]
            y_scr[t, pl.ds(row, 1), :] = jnp.sum(h * ct_scr[t, pl.ds(g0, A_STATE), :], axis=0, keepdims=True)
        h_out_ref[0, 0, p] = h
        return carry

    lax.fori_loop(0, A_HEAD_DIM, per_p, 0, unroll=2)

    @pl.when(k == A_HEADS - 1)
    def _():
        for t in range(valid):
            y = (y_scr[t] + dx_scr[t]) * z_scr[t]
            y = y * lax.rsqrt(jnp.mean(y * y, axis=0, keepdims=True) + EPS) * nwb_ref[...]
            y_ref[:, t * ns:(t + 1) * ns] = y.astype(BF16)


def _ssd_sample(xbc, z, dt, conv_all, h_all, lw, layer, valid, conv_prev, h_prev):
    rows = valid * LANE
    hblk = pl.BlockSpec((1, 1, A_HEAD_DIM, A_STATE, LANE), lambda k: (layer, k, 0, 0, 0))
    cblk = pl.BlockSpec((1, A_CONV - 1, LANE, A_CONV_DIM), lambda k: (layer, 0, 0, 0))
    in_specs = [_full((rows, A_CONV_DIM)), _full((rows, A_WIDTH)), _full((rows, LANE)), cblk, hblk,
                _full((A_CONV, A_CONV_DIM)), _full((1, A_CONV_DIM)), _full((1, LANE)), _full((1, LANE)),
                _full((1, A_WIDTH)), _full((A_WIDTH, LANE))] + [pl.BlockSpec(memory_space=pl.ANY)] * 2
    args = [xbc, z, dt, conv_all, h_all, lw["conv_w"], lw["conv_b"], lw["a_log"], lw["dt_bias"], lw["d_exp"],
            lw["ssd_norm_wb"], conv_prev, h_prev]
    aliases = {len(args) - 2: 1, len(args) - 1: 2}
    big = lambda n: pltpu.VMEM((valid, n, LANE), F32)
    return pl.pallas_call(
        functools.partial(_ssd_sample_body, valid=valid),
        grid=(A_HEADS,),
        in_specs=in_specs,
        out_specs=[_full((A_WIDTH, rows)), cblk, hblk],
        out_shape=[jax.ShapeDtypeStruct((A_WIDTH, rows), BF16), jax.ShapeDtypeStruct(conv_all.shape, F32),
                   jax.ShapeDtypeStruct(h_all.shape, F32)],
        scratch_shapes=[big(A_WIDTH), big(A_WIDTH), big(A_WIDTH), big(A_WIDTH), big(LANE), big(LANE), big(LANE)],
        input_output_aliases=aliases,
        compiler_params=_cparams("arbitrary"),
        name="ssd_sample",
    )(*args)


def _s5w_body(lre_ref, lim_ref, ldt_ref, btr_ref, bti_ref, cr_ref, ci_ref,
              bre_ref, bim_ref, bsre_ref, bsim_ref, cre_ref, cim_ref, toep_ref, ap_ref, cd_scr, bb_scr, *, valid):
    h, q = B_STATE, S5_Q
    for ref in (bre_ref, bim_ref, bsre_ref, bsim_ref, cre_ref, cim_ref, toep_ref):
        ref[...] = jnp.zeros(ref.shape, BF16)
    r_i = lax.broadcasted_iota(jnp.int32, (q * B_CH, q * LANE), 0)
    c_i = lax.broadcasted_iota(jnp.int32, (q * B_CH, q * LANE), 1)
    d = lax.broadcasted_iota(jnp.int32, (2 * SUBLANE, h), 0).astype(F32)
    for g8 in range(S5_OCT):
        lr, li = lre_ref[g8], lim_ref[g8]
        step = jnp.exp(ldt_ref[g8])
        er = jnp.exp(lr * step * d)
        ang = li * step * d
        ar, ai = er * jnp.cos(ang), er * jnp.sin(ang)
        nr, ni = ar[1:2] - 1.0, ai[1:2]
        den = lr * lr + li * li
        wr, wi = (nr * lr + ni * li) / den, (ni * lr - nr * li) / den
        btr, bti = btr_ref[g8], bti_ref[g8]
        bbr, bbi = wr * btr - wi * bti, wr * bti + wi * btr
        cr, ci = cr_ref[g8], ci_ref[g8]
        bb_scr[:, 0:h] = bbr
        bb_scr[:, h:2 * h] = bbi
        cols = slice(g8 * h, (g8 + 1) * h)
        for s in range(q):
            rows = slice(s * LANE + g8 * B_CH, s * LANE + (g8 + 1) * B_CH)
            e = q - 1 - s
            bre_ref[0, rows, cols] = (ar[e:e + 1] * bbr - ai[e:e + 1] * bbi).astype(BF16)
            bim_ref[0, rows, cols] = (ai[e:e + 1] * bbr + ar[e:e + 1] * bbi).astype(BF16)
            if s < valid:
                e = valid - 1 - s
                bsre_ref[0, rows, cols] = (ar[e:e + 1] * bbr - ai[e:e + 1] * bbi).astype(BF16)
                bsim_ref[0, rows, cols] = (ai[e:e + 1] * bbr + ar[e:e + 1] * bbi).astype(BF16)
            cre_ref[0, rows, cols] = (cr * ar[s + 1:s + 2] - ci * ai[s + 1:s + 2]).astype(BF16)
            cim_ref[0, rows, cols] = (-(cr * ai[s + 1:s + 2] + ci * ar[s + 1:s + 2])).astype(BF16)
            cd_scr[s * B_CH:(s + 1) * B_CH, 0:h] = cr * ar[s:s + 1] - ci * ai[s:s + 1]
            cd_scr[s * B_CH:(s + 1) * B_CH, h:2 * h] = -(cr * ai[s:s + 1] + ci * ar[s:s + 1])
        kdt = _dot_nt(bb_scr[...], cd_scr[...], precision=HIGHEST)
        place = (c_i == (r_i >> 4) * LANE + g8 * B_CH + (r_i & (B_CH - 1))).astype(F32)
        slab = jnp.dot(kdt, place, preferred_element_type=F32, precision=HIGHEST).astype(BF16)
        for s in range(q):
            toep_ref[0, s * LANE + g8 * B_CH:s * LANE + (g8 + 1) * B_CH, s * LANE:] = slab[:, 0:(q - s) * LANE]
        ap_ref[g8, 0:1, 0:h] = ar[q:q + 1]
        ap_ref[g8, 0:1, h:2 * h] = ai[q:q + 1]
        ap_ref[g8, 1:2, 0:h] = ar[valid:valid + 1]
        ap_ref[g8, 1:2, h:2 * h] = ai[valid:valid + 1]


def _s5_weights(lam_re, lam_im, log_dt, b_re, b_im, c_re, c_im, valid):
    g, h, q = B_GROUPS, B_STATE, S5_Q
    vec = pl.BlockSpec((S5_OCT, 1, h), lambda i: (i, 0, 0))
    mat = pl.BlockSpec((S5_OCT, B_CH, h), lambda i: (i, 0, 0))
    st = pl.BlockSpec((1, q * LANE, S5_OCT * h), lambda i: (i, 0, 0))
    st_shape = jax.ShapeDtypeStruct((N_OCT, q * LANE, S5_OCT * h), BF16)
    bre, bim, bsre, bsim, cre, cim, toep, ap = pl.pallas_call(
        functools.partial(_s5w_body, valid=valid),
        grid=(N_OCT,),
        in_specs=[vec, vec, vec, mat, mat, mat, mat],
        out_specs=[st] * 6 + [pl.BlockSpec((1, q * LANE, q * LANE), lambda i: (i, 0, 0)),
                              pl.BlockSpec((S5_OCT, 2, 2 * h), lambda i: (i, 0, 0))],
        out_shape=[st_shape] * 6 + [jax.ShapeDtypeStruct((N_OCT, q * LANE, q * LANE), BF16),
                                    jax.ShapeDtypeStruct((g, 2, 2 * h), F32)],
        scratch_shapes=[pltpu.VMEM((q * B_CH, 2 * h), F32), pltpu.VMEM((B_CH, 2 * h), F32)],
        compiler_params=_cparams("parallel"),
        name="s5_weights",
    )(lam_re.reshape(g, 1, h), lam_im.reshape(g, 1, h), jnp.broadcast_to(log_dt.reshape(g, 1, 1), (g, 1, h)),
      jnp.swapaxes(b_re, 1, 2), jnp.swapaxes(b_im, 1, 2), c_re, c_im)
    return dict(
        bst_re=bre, bst_im=bim, bsts_re=bsre, bsts_im=bsim, cst_re=cre, cst_im=cim, toep=toep,
        aq_re=ap[:, 0, :h].reshape(1, S5_STATE_COLS), aq_im=ap[:, 0, h:].reshape(1, S5_STATE_COLS),
        av_re=ap[:, 1, :h].reshape(1, S5_STATE_COLS), av_im=ap[:, 1, h:].reshape(1, S5_STATE_COLS),
    )


def _oct_rows(u_ref, k8, rt):
    return jnp.concatenate([u_ref[k8, pl.ds(j, rt, stride=S5_Q), :] for j in range(S5_Q)], axis=1).astype(BF16)


def _toep_dot(uo, toep_ref, k8):
    two = 2 * LANE
    cols = [_dot(uo[:, :(t + 1) * two], toep_ref[k8, :(t + 1) * two, t * two:(t + 1) * two])
            for t in range(uo.shape[1] // two)]
    return jnp.concatenate(cols, axis=1)


def _s5_state_body(u_ref, bre_ref, bim_ref, sre_ref, sim_ref):
    w = S5_OCT * B_STATE
    for k8 in range(N_OCT):
        uo = _oct_rows(u_ref, k8, sre_ref.shape[0])
        sre_ref[:, k8 * w:(k8 + 1) * w] = _dot(uo, bre_ref[k8])
        sim_ref[:, k8 * w:(k8 + 1) * w] = _dot(uo, bim_ref[k8])


def _s5_state(u, bre, bim, rt):
    r = u.shape[1] // S5_Q
    wspec = _full((N_OCT, S5_Q * LANE, S5_OCT * B_STATE))
    ospec = pl.BlockSpec((rt, S5_STATE_COLS), lambda i: (i, 0))
    return pl.pallas_call(
        _s5_state_body,
        grid=(r // rt,),
        in_specs=[_tile_major_spec(rt * S5_Q), wspec, wspec],
        out_specs=[ospec, ospec],
        out_shape=[jax.ShapeDtypeStruct((r, S5_STATE_COLS), F32)] * 2,
        compiler_params=_cparams("parallel"),
        name="s5_state",
    )(u, bre, bim)


def _s5_scan_body(sre_ref, sim_ref, ar_ref, ai_ref, hre_ref, him_ref, fre_ref, fim_ref, cr_scr, ci_scr, *, rb, nblk):
    i = pl.program_id(0)
    nb = sre_ref.shape[0]

    @pl.when(i == 0)
    def _():
        cr_scr[...] = jnp.zeros_like(cr_scr)
        ci_scr[...] = jnp.zeros_like(ci_scr)

    ar, ai = ar_ref[...], ai_ref[...]

    def step(r, carry):
        out = []
        for b in range(nb):
            hr, hi = carry[2 * b], carry[2 * b + 1]
            hre_ref[b, pl.ds(r, 1), :] = hr
            him_ref[b, pl.ds(r, 1), :] = hi
            sr, si = sre_ref[b, pl.ds(r, 1), :], sim_ref[b, pl.ds(r, 1), :]
            out += [ar * hr - ai * hi + sr, ai * hr + ar * hi + si]
        return tuple(out)

    init = tuple(s[b] for b in range(nb) for s in (cr_scr, ci_scr))
    fin = lax.fori_loop(0, rb, step, init)
    for b in range(nb):
        cr_scr[b] = fin[2 * b]
        ci_scr[b] = fin[2 * b + 1]

    @pl.when(i == nblk - 1)
    def _():
        for b in range(nb):
            fre_ref[b] = fin[2 * b]
            fim_ref[b] = fin[2 * b + 1]


def _s5_scan(sre, sim, ar, ai, nb, rb):
    r = sre.shape[0] // nb
    nblk = r // rb
    blk = pl.BlockSpec((nb, rb, S5_STATE_COLS), lambda i: (0, i, 0))
    vec = _full((1, S5_STATE_COLS))
    fin = _full((nb, 1, S5_STATE_COLS))
    per_batch = lambda a: a.reshape(nb, r, S5_STATE_COLS)
    hre, him, fre, fim = pl.pallas_call(
        functools.partial(_s5_scan_body, rb=rb, nblk=nblk),
        grid=(nblk,),
        in_specs=[blk, blk, vec, vec],
        out_specs=[blk, blk, fin, fin],
        out_shape=[jax.ShapeDtypeStruct((nb, r, S5_STATE_COLS), F32)] * 2
        + [jax.ShapeDtypeStruct((nb, 1, S5_STATE_COLS), F32)] * 2,
        scratch_shapes=[pltpu.VMEM((nb, 1, S5_STATE_COLS), F32)] * 2,
        compiler_params=_cparams("arbitrary"),
        name="s5_scan",
    )(per_batch(sre), per_batch(sim), ar, ai)
    return hre.reshape(nb * r, S5_STATE_COLS), him.reshape(nb * r, S5_STATE_COLS), fre, fim


def _s5_sample_body(u_ref, toep_ref, cre_ref, cim_ref, bre_ref, bim_ref, hre_ref, him_ref, ar_ref, ai_ref,
                    y_ref, fre_ref, fim_ref, *, valid):
    w = S5_OCT * B_STATE
    ns = LANE
    for k8 in range(N_OCT):
        blocks = [u_ref[k8, t * ns:(t + 1) * ns, :] for t in range(valid)]
        uo = jnp.concatenate(blocks, axis=1).astype(BF16)
        uo_t = jnp.concatenate([b.T for b in blocks], axis=0).astype(BF16)
        hr, hi = hre_ref[0, k8 * w:(k8 + 1) * w, :], him_ref[0, k8 * w:(k8 + 1) * w, :]
        yo = (_toep_dot(uo, toep_ref, k8) + _dot_nt(hr.T.astype(BF16), cre_ref[k8])
              + _dot_nt(hi.T.astype(BF16), cim_ref[k8]))
        for t in range(valid):
            y_ref[k8, t * ns:(t + 1) * ns, :] = yo[:, t * LANE:(t + 1) * LANE]
        ar, ai = ar_ref[k8 * w:(k8 + 1) * w, :], ai_ref[k8 * w:(k8 + 1) * w, :]
        fre_ref[k8 * w:(k8 + 1) * w, :] = ar * hr - ai * hi + _dot_tn(bre_ref[k8], uo_t)
        fim_ref[k8 * w:(k8 + 1) * w, :] = ai * hr + ar * hi + _dot_tn(bim_ref[k8], uo_t)


def _s5_sample(u, s5w, hre_all, him_all, ar, ai, layer, valid):
    rows = valid * LANE
    vq = valid * LANE
    sub = lambda c: pl.BlockSpec((N_OCT, vq, c), lambda i: (0, 0, 0))
    hblk = pl.BlockSpec((1, S5_STATE_COLS, LANE), lambda i: (layer, 0, 0))
    st = _full((S5_STATE_COLS, LANE))
    return pl.pallas_call(
        functools.partial(_s5_sample_body, valid=valid),
        grid=(1,),
        in_specs=[_tile_major_spec(rows), sub(vq), sub(S5_OCT * B_STATE), sub(S5_OCT * B_STATE),
                  sub(S5_OCT * B_STATE), sub(S5_OCT * B_STATE), hblk, hblk, st, st],
        out_specs=[_tile_major_spec(rows), st, st],
        out_shape=[jax.ShapeDtypeStruct((N_OCT, rows, LANE), F32)] + [jax.ShapeDtypeStruct((S5_STATE_COLS, LANE), F32)] * 2,
        compiler_params=_cparams("arbitrary"),
        name="s5_sample",
    )(u, s5w["toep"], s5w["cst_re"], s5w["cst_im"], s5w["bsts_re"], s5w["bsts_im"], hre_all, him_all, ar, ai)


def _s5_out_body(u_ref, hre_ref, him_ref, toep_ref, cre_ref, cim_ref, y_ref):
    w = S5_OCT * B_STATE
    rt = hre_ref.shape[0]
    for k8 in range(N_OCT):
        uo = _oct_rows(u_ref, k8, rt)
        yo = (_toep_dot(uo, toep_ref, k8)
              + _dot_nt(hre_ref[:, k8 * w:(k8 + 1) * w].astype(BF16), cre_ref[k8])
              + _dot_nt(him_ref[:, k8 * w:(k8 + 1) * w].astype(BF16), cim_ref[k8]))
        for j in range(S5_Q):
            y_ref[k8, pl.ds(j, rt, stride=S5_Q), :] = yo[:, j * LANE:(j + 1) * LANE]


def _s5_out(u, hre, him, toep, cre, cim, rt):
    r = u.shape[1] // S5_Q
    hblk = pl.BlockSpec((rt, S5_STATE_COLS), lambda i: (i, 0))
    return pl.pallas_call(
        _s5_out_body,
        grid=(r // rt,),
        in_specs=[_tile_major_spec(rt * S5_Q), hblk, hblk, _full((N_OCT, S5_Q * LANE, S5_Q * LANE)),
                  _full((N_OCT, S5_Q * LANE, S5_OCT * B_STATE)), _full((N_OCT, S5_Q * LANE, S5_OCT * B_STATE))],
        out_specs=_tile_major_spec(rt * S5_Q),
        out_shape=jax.ShapeDtypeStruct(u.shape, F32),
        compiler_params=_cparams("parallel"),
        name="s5_out",
    )(u, hre, him, toep, cre, cim)


def _band_body(bucket_ref, table_ref, o_ref):
    bucket = bucket_ref[...]
    col = lax.broadcasted_iota(jnp.int32, (WINDOW, 2 * WINDOW), 1)
    for hq in range(C_HEADS):
        base = jnp.full((1, 2 * WINDOW), NEG, F32)
        for b in range(REL_BUCKETS):
            base = jnp.where(bucket == b, table_ref[b, hq], base)
        tile = pltpu.roll(jnp.broadcast_to(base, (WINDOW, 2 * WINDOW)), 0, 1, stride=1, stride_axis=0)
        o_ref[1, hq] = tile.T
        o_ref[0, hq] = jnp.where(col >= WINDOW, tile, NEG).T


def _bias_band(table):
    lane = np.arange(2 * WINDOW)
    n = WINDOW - lane
    exact = REL_BUCKETS // 2
    nf = np.maximum(n, 1).astype(np.float32)
    large = exact + (np.log(nf / np.float32(exact)) / np.float32(math.log(REL_MAX_DIST / exact))
                     * np.float32(REL_BUCKETS - exact)).astype(np.int32)
    bucket = np.where(n < exact, n, np.minimum(large, REL_BUCKETS - 1))
    bucket = np.where((n >= 0) & (n < WINDOW), bucket, -1).astype(np.int32)
    return pl.pallas_call(
        _band_body,
        in_specs=[pl.BlockSpec(memory_space=pltpu.VMEM), pl.BlockSpec(memory_space=pltpu.SMEM)],
        out_specs=pl.BlockSpec(memory_space=pltpu.VMEM),
        out_shape=jax.ShapeDtypeStruct((2, C_HEADS, 2 * WINDOW, WINDOW), F32),
        name="rel_bias_band",
    )(jnp.asarray(bucket).reshape(1, 2 * WINDOW), table)


def _attn_prompt_body(q_ref, kc_ref, kp_ref, vc_ref, vp_ref, b0_ref, br_ref, sink_ref, o_ref, s_scr, p_scr, *, nblk):
    k_all = jnp.concatenate([kp_ref[...], kc_ref[...]], axis=0) * (C_HEAD_DIM ** -0.5)
    v_all = jnp.concatenate([vp_ref[...], vc_ref[...]], axis=0)
    swapped = (pltpu.roll(k_all, C_HEAD_DIM, 1), pltpu.roll(v_all, C_HEAD_DIM, 1))
    low = lax.broadcasted_iota(jnp.int32, k_all.shape, 1) < C_HEAD_DIM
    kz, vz = [], []
    for hk in range(C_KV_HEADS):
        k_lo = k_all if hk == 0 else swapped[0]
        k_hi = swapped[0] if hk == 0 else k_all
        kz.append((jnp.where(low, k_lo, 0.0).astype(BF16), jnp.where(low, 0.0, k_hi).astype(BF16)))
        vz.append((v_all if hk == 0 else swapped[1])[:, :C_HEAD_DIM].astype(BF16))
    for i in range(nblk):
        keys = slice(i * WINDOW, (i + 2) * WINDOW)
        slot = i % 2
        for hq in range(C_HEADS):
            qt = q_ref[i * WINDOW:(i + 1) * WINDOW, (hq // 2) * LANE:(hq // 2 + 1) * LANE]
            s_scr[slot, hq] = _dot_nt(kz[hq // C_GQA][hq % 2][keys], qt)
        inv = []
        for hq in range(C_HEADS):
            sk = sink_ref[hq]
            s = s_scr[slot, hq] + (b0_ref[0, hq] if i == 0 else br_ref[0, hq])
            m = jnp.maximum(jnp.max(s, axis=0, keepdims=True), sk)
            p = jnp.exp(s - m)
            inv.append(1.0 / (jnp.sum(p, axis=0, keepdims=True) + jnp.exp(sk - m)))
            p_scr[slot, hq] = p.astype(BF16)
        for hq in range(C_HEADS):
            o = _dot_tn(vz[hq // C_GQA][keys], p_scr[slot, hq]) * inv[hq]
            o_ref[hq * C_HEAD_DIM:(hq + 1) * C_HEAD_DIM, i * WINDOW:(i + 1) * WINDOW] = o.astype(BF16)


def _attn_prompt(q, k, v, band, sinks, nb, nblk_seq, nblk):
    steps = nblk_seq // nblk
    cur = lambda b, n: (b * steps + n, 0)
    prev = lambda b, n: (b * nblk_seq + jnp.maximum(n * nblk - 1, 0), 0)
    bspec = lambda m: pl.BlockSpec((1, C_HEADS, 2 * WINDOW, WINDOW), m)
    return pl.pallas_call(
        functools.partial(_attn_prompt_body, nblk=nblk),
        grid=(nb, steps),
        in_specs=[pl.BlockSpec((nblk * WINDOW, C_WIDTH), cur),
                  pl.BlockSpec((nblk * WINDOW, C_KV_WIDTH), cur), pl.BlockSpec((WINDOW, C_KV_WIDTH), prev),
                  pl.BlockSpec((nblk * WINDOW, C_KV_WIDTH), cur), pl.BlockSpec((WINDOW, C_KV_WIDTH), prev),
                  bspec(lambda b, n: (jnp.minimum(n, 1), 0, 0, 0)), bspec(lambda b, n: (1, 0, 0, 0)),
                  pl.BlockSpec(memory_space=pltpu.SMEM)],
        out_specs=pl.BlockSpec((C_WIDTH, nblk * WINDOW), lambda b, n: (0, b * steps + n)),
        out_shape=jax.ShapeDtypeStruct((C_WIDTH, nb * nblk_seq * WINDOW), BF16),
        scratch_shapes=[pltpu.VMEM((2, C_HEADS, 2 * WINDOW, WINDOW), F32),
                        pltpu.VMEM((2, C_HEADS, 2 * WINDOW, WINDOW), BF16)],
        compiler_params=_cparams("parallel", "arbitrary"),
        name="attn_prompt",
    )(q, k, k, v, v, band, band, sinks)


def _attn_sample_body(*refs, valid, lb, nseq):
    q_ref, kn_ref, vn_ref, kc_ref, vc_ref, b1_ref, b2_ref, sink_ref = refs[:8]
    o_ref, ko_ref, vo_ref, s1_scr, s2_scr, inv_scr, o_scr = refs[-7:]
    d = C_HEAD_DIM
    rs = 2 * valid
    lane_c = lax.broadcasted_iota(jnp.int32, (C_KV_WIDTH, lb), 1)
    lane_n = lax.broadcasted_iota(jnp.int32, (SUBLANE, C_KV_WIDTH), 1)
    pad = jnp.zeros((SUBLANE - valid, C_KV_WIDTH), F32)
    zero_half = jnp.zeros((d, lb), BF16)
    combos = [(hk, par) for hk in range(C_KV_HEADS) for par in range(2)]

    def new_rows(ref, s):
        return jnp.concatenate([ref[:, s, :], pad], axis=0)

    for s in range(nseq):
        qs = q_ref[:, s, :].astype(F32)
        kn, vn = new_rows(kn_ref, s), new_rows(vn_ref, s)
        kt, vt = kc_ref[0, s], vc_ref[0, s]
        kn_t, vn_t = kn.T, vn.T
        new_k, new_v = pltpu.roll(kt, lb - valid, 1), pltpu.roll(vt, lb - valid, 1)
        for t in range(valid):
            at = lane_c == lb - valid + t
            new_k = jnp.where(at, jnp.broadcast_to(kn_t[:, t:t + 1], (C_KV_WIDTH, lb)), new_k)
            new_v = jnp.where(at, jnp.broadcast_to(vn_t[:, t:t + 1], (C_KV_WIDTH, lb)), new_v)
        ko_ref[0, s] = new_k
        vo_ref[0, s] = new_v
        kn_sw = pltpu.roll(kn, d, 1)
        for c, (hk, par) in enumerate(combos):
            kth = kt[hk * d:(hk + 1) * d].astype(BF16)
            kz = jnp.concatenate([kth, zero_half] if par == 0 else [zero_half, kth], axis=0)
            src = kn if hk == par else kn_sw
            knz = jnp.where((lane_n < d) if par == 0 else (lane_n >= d), src, 0.0).astype(BF16)
            qq = jnp.concatenate([qs[:, (hk * 2 + j) * LANE:(hk * 2 + j + 1) * LANE] for j in range(2)],
                                 axis=0).astype(BF16)
            s1_scr[c, s * rs:(s + 1) * rs, :] = _dot(qq, kz)
            s2_scr[c, s * rs:(s + 1) * rs, :] = _dot_nt(qq, knz)

    for c, (hk, par) in enumerate(combos):
        s1 = s1_scr[c] * (d ** -0.5) + b1_ref[hk, par]
        s2 = s2_scr[c] * (d ** -0.5) + b2_ref[hk, par]
        sk = sink_ref[hk, par]
        m = jnp.maximum(jnp.maximum(jnp.max(s1, axis=-1, keepdims=True), jnp.max(s2, axis=-1, keepdims=True)), sk)
        p1, p2 = jnp.exp(s1 - m), jnp.exp(s2 - m)
        den = jnp.sum(p1, axis=-1, keepdims=True) + jnp.sum(p2, axis=-1, keepdims=True) + jnp.exp(sk - m)
        s1_scr[c] = p1
        s2_scr[c] = p2
        inv_scr[c] = jnp.broadcast_to(1.0 / den, (nseq * rs, SUBLANE))

    for s in range(nseq):
        rows = slice(s * rs, (s + 1) * rs)
        vt = vc_ref[0, s]
        vn = new_rows(vn_ref, s)
        for c, (hk, par) in enumerate(combos):
            vth = vt[hk * d:(hk + 1) * d].astype(BF16)
            vnh = vn[:, hk * d:(hk + 1) * d].astype(BF16)
            o = _dot_nt(s1_scr[c, rows, :].astype(BF16), vth) + _dot(s2_scr[c, rows, :].astype(BF16), vnh)
            o_scr[hk, rows, par * d:(par + 1) * d] = o * inv_scr[c, rows, 0:1]

    for hk in range(C_KV_HEADS):
        for j in range(2):
            for t in range(valid):
                tile = hk * 2 + j
                o_ref[t, :, tile * LANE:(tile + 1) * LANE] = o_scr[hk, pl.ds(j * valid + t, nseq, stride=rs), :]


def _attn_sample(q3, kn3, vn3, kc_all, vc_all, b1, b2, sink_col, layer, k_prev, v_prev):
    valid, nsb, _ = q3.shape
    lb = kc_all.shape[-1]
    nseq = 2 * SUBLANE
    assert 2 * valid == SUBLANE
    rows = lambda w: pl.BlockSpec((valid, nseq, w), lambda i: (0, i, 0))
    cblk = pl.BlockSpec((1, nseq, C_KV_WIDTH, lb), lambda i: (layer, i, 0, 0))
    b1, b2, sink_col = (jnp.tile(a, (1, 1, nseq, 1)) for a in (b1, b2, sink_col))
    in_specs = [rows(C_WIDTH), rows(C_KV_WIDTH), rows(C_KV_WIDTH), cblk, cblk,
                _full(b1.shape), _full(b2.shape), _full(sink_col.shape)] + [pl.BlockSpec(memory_space=pl.ANY)] * 2
    args = [q3, kn3, vn3, kc_all, vc_all, b1, b2, sink_col, k_prev, v_prev]
    aliases = {len(args) - 2: 1, len(args) - 1: 2}
    return pl.pallas_call(
        functools.partial(_attn_sample_body, valid=valid, lb=lb, nseq=nseq),
        grid=(nsb // nseq,),
        in_specs=in_specs,
        out_specs=[rows(C_WIDTH), cblk, cblk],
        out_shape=[jax.ShapeDtypeStruct((valid, nsb, C_WIDTH), F32), jax.ShapeDtypeStruct(kc_all.shape, F32),
                   jax.ShapeDtypeStruct(vc_all.shape, F32)],
        scratch_shapes=[pltpu.VMEM((2 * C_KV_HEADS, nseq * SUBLANE, lb), F32),
                        pltpu.VMEM((2 * C_KV_HEADS, nseq * SUBLANE, SUBLANE), F32),
                        pltpu.VMEM((2 * C_KV_HEADS, nseq * SUBLANE, SUBLANE), F32),
                        pltpu.VMEM((C_KV_HEADS, nseq * SUBLANE, LANE), F32)],
        input_output_aliases=aliases,
        compiler_params=_cparams("arbitrary"),
        name="attn_sample",
    )(*args)


def _merge_body(x_ref, ya_ref, yb_ref, u_ref, yc_ref, nw_ref, wg_ref, d5_ref, wglu_ref,
                wa_ref, wb_ref, wc_ref, wo_ref, o_ref, *, ya_features_major, yc_features_major):
    x = x_ref[...]
    h = _rms(x, nw_ref[...]).astype(BF16)
    g = jax.nn.gelu(_lanes(yb_ref) + d5_ref[...] * _lanes(u_ref))
    gv = _dot(g.astype(BF16), wglu_ref[...])
    yb = (gv[:, :B_WIDTH] * jax.nn.sigmoid(gv[:, B_WIDTH:])).astype(BF16)

    def gate(i):
        return jax.nn.sigmoid(_dot_nt(h, wg_ref[i * D_MODEL:(i + 1) * D_MODEL, :]))

    def branch(y_ref, w_ref, features_major):
        y = y_ref[...].astype(BF16)
        return _dot_tn(y, w_ref[...]) if features_major else _dot(y, w_ref[...])

    mixed = gate(0) * branch(ya_ref, wa_ref, ya_features_major)
    mixed = mixed + gate(1) * _dot(yb, wb_ref[...])
    mixed = mixed + gate(2) * branch(yc_ref, wc_ref, yc_features_major)
    o_ref[...] = x + _dot(mixed.astype(BF16), wo_ref[...])


def _merge(x, ya, yb, u, yc, lw, tm, ya_features_major, yc_features_major):
    t = x.shape[0]
    rows = lambda w: pl.BlockSpec((tm, w), lambda i: (i, 0))
    branch_spec = lambda fm: pl.BlockSpec((A_WIDTH, tm), lambda i: (0, i)) if fm else rows(A_WIDTH)
    lwt = lambda shape: _layer_weight(shape, lw["layer"])
    return pl.pallas_call(
        functools.partial(_merge_body, ya_features_major=ya_features_major, yc_features_major=yc_features_major),
        grid=(t // tm,),
        in_specs=[rows(D_MODEL), branch_spec(ya_features_major), _tile_major_spec(tm), _tile_major_spec(tm),
                  branch_spec(yc_features_major),
                  _full((1, D_MODEL)), lwt((3 * D_MODEL, D_MODEL)), _full((1, B_WIDTH)),
                  lwt((B_WIDTH, 2 * B_WIDTH)), lwt((A_WIDTH, D_MODEL)), lwt((B_WIDTH, D_MODEL)),
                  lwt((C_WIDTH, D_MODEL)), lwt((D_MODEL, D_MODEL))],
        out_specs=rows(D_MODEL),
        out_shape=jax.ShapeDtypeStruct((t, D_MODEL), F32),
        compiler_params=_cparams("parallel"),
        name="merge",
    )(x, ya, yb, u, yc, lw["norm1_w"], lw["w_gate"], lw["s5_d"], lw["w_glu"],
      lw["w_br_a"], lw["w_br_b"], lw["w_br_c"], lw["w_out"])


def _ffn_body(x_ref, nw_ref, wup_ref, wdn_ref, fnw_ref, o_ref, act_scr, *, final_norm):
    x = x_ref[...]
    h = _rms(x, nw_ref[...]).astype(BF16)
    for c in range(D_FF // FF_CHUNK):
        lo, hi = c * FF_CHUNK, (c + 1) * FF_CHUNK
        a = _dot(h, wup_ref[:, lo:hi])
        b = _dot(h, wup_ref[:, D_FF + lo:D_FF + hi])
        act_scr[:, lo:hi] = (jax.nn.silu(a) * b).astype(BF16)
    y = x + _dot(act_scr[...], wdn_ref[...])
    if final_norm:
        y = _rms(y, fnw_ref[...])
    o_ref[...] = y


def _ffn(x, lw, fnw, tm, final_norm):
    t = x.shape[0]
    rows = pl.BlockSpec((tm, D_MODEL), lambda i: (i, 0))
    return pl.pallas_call(
        functools.partial(_ffn_body, final_norm=final_norm),
        grid=(t // tm,),
        in_specs=[rows, _full((1, D_MODEL)), _layer_weight((D_MODEL, 2 * D_FF), lw["layer"]),
                  _layer_weight((D_FF, D_MODEL), lw["layer"]), _full((1, D_MODEL))],
        out_specs=rows,
        out_shape=jax.ShapeDtypeStruct((t, D_MODEL), F32),
        scratch_shapes=[pltpu.VMEM((tm, D_FF), BF16)],
        compiler_params=_cparams("parallel"),
        name="ffn",
    )(x, lw["norm2_w"], lw["w_ffn_up"], lw["w_ffn_down"], fnw)


def _row_tile(t, pref):
    tm = pref
    while t % tm:
        tm //= 2
    return tm


def kernel(x_prompt, x_sample, state_ssd, state_conv, state_s5_re, state_s5_im, cache_k, cache_v, norm1_w, w_in, conv_w, conv_b, ssd_a_log, ssd_dt_bias, ssd_d, ssd_norm_w, s5_lam_re, s5_lam_im, s5_log_dt, s5_b_re, s5_b_im, s5_c_re, s5_c_im, s5_d, s5_w_glu, attn_sinks, w_br_a, w_br_b, w_br_c, w_out, norm2_w, w_ffn_up, w_ffn_down, rel_bias, final_norm_w):
    nb, seq, _ = x_prompt.shape
    nsb, valid, _ = x_sample.shape
    depth = w_in.shape[0]
    lb = cache_k.shape[2]
    assert seq % A_CHUNK == 0 and seq % WINDOW == 0 and seq % (S5_Q * SUBLANE) == 0
    assert nsb == LANE and A_CONV - 1 <= valid <= S5_Q and lb == WINDOW
    nchunk = seq // A_CHUNK
    tp = nb * seq
    ts = valid * nsb

    band = _bias_band(rel_bias)

    def pair_rows(a):
        a = a.reshape(C_KV_HEADS, 2, 2, valid, a.shape[-1])
        return jnp.swapaxes(a, 1, 2).reshape(C_KV_HEADS, 2, 2 * valid, a.shape[-1])

    band_s = pair_rows(jnp.swapaxes(band[1, :, :, :valid], 1, 2))
    bias_s1 = band_s[..., :lb]
    bias_s2 = band_s[..., lb:lb + SUBLANE]
    fnw = final_norm_w.reshape(1, D_MODEL)

    xp = x_prompt.reshape(tp, D_MODEL)
    xs = jnp.swapaxes(x_sample, 0, 1).reshape(ts, D_MODEL)
    conv_all = jnp.swapaxes(state_conv, 1, 2)
    ssd_all = jnp.transpose(state_ssd, (0, 2, 3, 4, 1))
    s5re_all = jnp.transpose(state_s5_re, (0, 2, 3, 1)).reshape(depth, S5_STATE_COLS, nsb)
    s5im_all = jnp.transpose(state_s5_im, (0, 2, 3, 1)).reshape(depth, S5_STATE_COLS, nsb)
    kc_all = jnp.transpose(cache_k, (0, 1, 3, 4, 2)).reshape(depth, nsb, C_KV_WIDTH, lb)
    vc_all = jnp.transpose(cache_v, (0, 1, 3, 4, 2)).reshape(depth, nsb, C_KV_WIDTH, lb)
    tm_p = _row_tile(seq, 512)
    tm_s = _row_tile(ts, 512)
    rt_p = _row_tile(tp // S5_Q, 256)
    rb_p = _row_tile(seq // S5_Q, 128)
    attn_blk = _row_tile(seq // WINDOW, 16)

    new_p, s5_s = [], []
    conv_new, ssd_new = jnp.zeros(conv_all.shape, F32), jnp.zeros(ssd_all.shape, F32)
    k_new, v_new = jnp.zeros(kc_all.shape, F32), jnp.zeros(vc_all.shape, F32)
    off = [0, 512, 1280, 1288, 1800, 2312, 2440, 2568, 5640]
    w_in_t = jnp.swapaxes(w_in, 1, 2)
    w_pack = jnp.concatenate(
        [w_in_t[:, off[0]:off[1]], w_in_t[:, off[1]:off[2]], w_in_t[:, off[3]:off[4]], w_in_t[:, off[4]:off[5]],
         w_in_t[:, off[5]:off[6]], w_in_t[:, off[6]:off[7]],
         jnp.pad(w_in_t[:, off[2]:off[3]], ((0, 0), (0, LANE - A_HEADS), (0, 0)))], axis=1).astype(BF16)
    stacked = dict(
        w_gate=w_in_t[:, off[7]:off[8]].astype(BF16), w_glu=s5_w_glu.astype(BF16),
        w_br_a=w_br_a.astype(BF16), w_br_b=w_br_b.astype(BF16), w_br_c=w_br_c.astype(BF16),
        w_out=w_out.astype(BF16), w_ffn_up=w_ffn_up.astype(BF16), w_ffn_down=w_ffn_down.astype(BF16))
    for i in range(depth):
        sink_rows = pair_rows(jnp.broadcast_to(attn_sinks[i].reshape(C_HEADS, 1, 1), (C_HEADS, valid, 1)))
        lw = dict(
            stacked, layer=i, norm1_w=norm1_w[i].reshape(1, D_MODEL),
            conv_w=conv_w[i], conv_b=conv_b[i].reshape(1, A_CONV_DIM),
            a_log=jnp.pad(ssd_a_log[i], (0, LANE - A_HEADS)).reshape(1, LANE),
            dt_bias=jnp.pad(ssd_dt_bias[i], (0, LANE - A_HEADS)).reshape(1, LANE),
            d_exp=jnp.repeat(ssd_d[i], A_HEAD_DIM).reshape(1, A_WIDTH),
            ssd_norm_w=ssd_norm_w[i].reshape(1, A_WIDTH),
            ssd_norm_wb=jnp.broadcast_to(ssd_norm_w[i].reshape(A_WIDTH, 1), (A_WIDTH, LANE)),
            s5_d=s5_d[i].reshape(1, B_WIDTH), norm2_w=norm2_w[i].reshape(1, D_MODEL),
        )
        s5w = _s5_weights(s5_lam_re[i], s5_lam_im[i], s5_log_dt[i], s5_b_re[i], s5_b_im[i],
                          s5_c_re[i], s5_c_im[i], valid)
        last = i == depth - 1

        u, q, k, v, ya, ssd_h, conv_tail = _inproj_ssd(xp, w_pack, lw, nb, tm_p)
        sre, sim = _s5_state(u, s5w["bst_re"], s5w["bst_im"], rt_p)
        hre, him, fre, fim = _s5_scan(sre, sim, s5w["aq_re"], s5w["aq_im"], nb, rb_p)
        yb = _s5_out(u, hre, him, s5w["toep"], s5w["cst_re"], s5w["cst_im"], rt_p)
        yc = _attn_prompt(q, k, v, band, attn_sinks[i], nb, seq // WINDOW, attn_blk)
        x1 = _merge(xp, ya, yb, u, yc, lw, tm_p, False, True)
        xp = _ffn(x1, lw, fnw, tm_p, last)
        new_p.append((
            ssd_h.reshape(nb, A_HEADS, A_HEAD_DIM, A_STATE),
            conv_tail[:, SUBLANE - (A_CONV - 1):],
            fre.reshape(nb, B_GROUPS, B_STATE), fim.reshape(nb, B_GROUPS, B_STATE),
            k.reshape(nb, seq, C_KV_WIDTH)[:, seq - WINDOW:].reshape(nb, WINDOW, C_KV_HEADS, C_HEAD_DIM),
            v.reshape(nb, seq, C_KV_WIDTH)[:, seq - WINDOW:].reshape(nb, WINDOW, C_KV_HEADS, C_HEAD_DIM)))

        z, xbc, u, q, k, v, dt = _inproj(xs, lw["norm1_w"], w_pack, i, tm_s)
        ya, conv_new, ssd_new = _ssd_sample(xbc, z, dt, conv_all, ssd_all, lw, i, valid, conv_new, ssd_new)
        av_re = jnp.broadcast_to(s5w["av_re"].reshape(S5_STATE_COLS, 1), (S5_STATE_COLS, nsb))
        av_im = jnp.broadcast_to(s5w["av_im"].reshape(S5_STATE_COLS, 1), (S5_STATE_COLS, nsb))
        yb, fre, fim = _s5_sample(u, s5w, s5re_all, s5im_all, av_re, av_im, i, valid)
        yc, k_new, v_new = _attn_sample(
            q.reshape(valid, nsb, C_WIDTH), k.reshape(valid, nsb, C_KV_WIDTH), v.reshape(valid, nsb, C_KV_WIDTH),
            kc_all, vc_all, bias_s1, bias_s2, sink_rows, i, k_new, v_new)
        x1 = _merge(xs, ya, yb, u, yc.reshape(ts, C_WIDTH), lw, tm_s, True, False)
        xs = _ffn(x1, lw, fnw, tm_s, last)
        s5_s.append((fre, fim))

    def stack(states, j):
        return jnp.stack([s[j] for s in states], axis=0)

    def s5_state(j):
        return jnp.transpose(stack(s5_s, j).reshape(depth, B_GROUPS, B_STATE, nsb), (0, 3, 1, 2))

    def cache(a):
        return jnp.transpose(a.reshape(depth, nsb, C_KV_HEADS, C_HEAD_DIM, lb), (0, 1, 4, 2, 3))

    y_prompt = xp.reshape(nb, seq, D_MODEL)
    y_sample = jnp.swapaxes(xs.reshape(valid, nsb, D_MODEL), 0, 1)
    return (y_prompt, y_sample,
            stack(new_p, 0), stack(new_p, 1), stack(new_p, 2), stack(new_p, 3), stack(new_p, 4), stack(new_p, 5),
            jnp.transpose(ssd_new, (0, 4, 1, 2, 3)), jnp.swapaxes(conv_new, 1, 2), s5_state(0), s5_state(1),
            cache(k_new), cache(v_new))
```

```python
import functools
import math

import numpy as np
import jax
import jax.numpy as jnp
from jax import lax
from jax.experimental import pallas as pl
from jax.experimental.pallas import tpu as pltpu

F32 = jnp.float32
BF16 = jnp.bfloat16
HIGHEST = lax.Precision.HIGHEST

D_MODEL = 1024
A_HEAD_DIM = 64
A_WIDTH = 512
A_HEADS = 8
A_GROUPS = 2
A_STATE = 64
A_CONV = 4
A_CONV_DIM = 768
A_CHUNK = 128
B_CH = 16
B_WIDTH = 512
B_GROUPS = 32
B_STATE = 64
C_HEAD_DIM = 64
C_WIDTH = 512
C_HEADS = 8
C_KV_HEADS = 2
C_GQA = 4
C_KV_WIDTH = 128
WINDOW = 128
REL_BUCKETS = 32
REL_MAX_DIST = 128
D_FF = 2816
EPS = 1e-6

P_Z = 0
P_XBC = P_Z + A_WIDTH
P_U = P_XBC + A_CONV_DIM
P_Q = P_U + B_WIDTH
P_K = P_Q + C_WIDTH
P_V = P_K + C_KV_WIDTH
P_DT = P_V + C_KV_WIDTH
P_COLS = P_DT + 128

LANE = 128
SUBLANE = 8
S5_Q = 8
S5_OCT = 8
N_OCT = B_GROUPS // S5_OCT
S5_STATE_COLS = B_GROUPS * B_STATE
NEG = -1e30
VMEM_LIMIT = 56 * 1024 * 1024
FF_CHUNK = 256


def _cparams(*sem):
    return pltpu.CompilerParams(dimension_semantics=sem, vmem_limit_bytes=VMEM_LIMIT)


def _rms(x, w):
    return x * lax.rsqrt(jnp.mean(x * x, axis=-1, keepdims=True) + EPS) * w


def _dot(a, b):
    return jnp.dot(a, b, preferred_element_type=F32)


def _dot_nt(a, b, precision=None):
    return lax.dot_general(a, b, (((1,), (1,)), ((), ())), preferred_element_type=F32, precision=precision)


def _dot_tn(a, b):
    return lax.dot_general(a, b, (((0,), (0,)), ((), ())), preferred_element_type=F32)


def _full(shape):
    n = len(shape)
    return pl.BlockSpec(shape, lambda *_: (0,) * n)


def _layer_weight(shape, layer):
    return pl.BlockSpec((None,) + shape, lambda *_: (layer,) + (0,) * len(shape), pipeline_mode=pl.Buffered(1))


def _tile_major_spec(rows):
    return pl.BlockSpec((N_OCT, rows, LANE), lambda i: (0, i, 0))


def _lanes(ref):
    return jnp.concatenate([ref[k8] for k8 in range(N_OCT)], axis=1)


def _inproj_body(x_ref, nw_ref, w_ref, z_ref, xbc_ref, u_ref, q_ref, k_ref, v_ref, dt_ref):
    h = _rms(x_ref[...], nw_ref[...]).astype(BF16)

    def proj(lo, hi):
        return _dot_nt(h, w_ref[lo:hi, :])

    z_ref[...] = proj(P_Z, P_XBC).astype(BF16)
    xbc_ref[...] = proj(P_XBC, P_U)
    u = proj(P_U, P_Q)
    for k8 in range(N_OCT):
        u_ref[k8] = u[:, k8 * LANE:(k8 + 1) * LANE]
    q_ref[...] = proj(P_Q, P_K).astype(BF16)
    kvdt = proj(P_K, P_COLS)
    k_ref[...] = kvdt[:, :C_KV_WIDTH]
    v_ref[...] = kvdt[:, C_KV_WIDTH:2 * C_KV_WIDTH]
    dt_ref[...] = kvdt[:, 2 * C_KV_WIDTH:]


def _inproj(x, nw, w, layer, tm):
    t = x.shape[0]
    widths = [(A_WIDTH, BF16), (A_CONV_DIM, F32), None, (C_WIDTH, BF16),
              (C_KV_WIDTH, F32), (C_KV_WIDTH, F32), (LANE, F32)]
    spec = lambda w: _tile_major_spec(tm) if w is None else pl.BlockSpec((tm, w[0]), lambda i: (i, 0))
    shape = lambda w: (jax.ShapeDtypeStruct((N_OCT, t, LANE), F32) if w is None
                       else jax.ShapeDtypeStruct((t, w[0]), w[1]))
    return pl.pallas_call(
        _inproj_body,
        grid=(t // tm,),
        in_specs=[pl.BlockSpec((tm, D_MODEL), lambda i: (i, 0)), _full((1, D_MODEL)),
                  _layer_weight((P_COLS, D_MODEL), layer)],
        out_specs=[spec(w) for w in widths],
        out_shape=[shape(w) for w in widths],
        compiler_params=_cparams("parallel"),
        name="inproj",
    )(x, nw, w)


def _ssd_chunk(xbc, z, dt_raw, cw_ref, cb_ref, alog_ref, dtb_ref, dexp_ref, nw_ref, h_scr, tail_scr):
    q = xbc.shape[0]
    tail_scr[SUBLANE:SUBLANE + q] = xbc
    acc = cb_ref[...] + cw_ref[A_CONV - 1:A_CONV, :] * xbc
    for j in range(1, A_CONV):
        acc = acc + cw_ref[A_CONV - 1 - j:A_CONV - j, :] * tail_scr[SUBLANE - j:SUBLANE - j + q]
    tail_scr[0:SUBLANE] = xbc[q - SUBLANE:q]
    xc = jax.nn.silu(acc)
    x = xc[:, :A_WIDTH]
    bm = xc[:, A_WIDTH:A_WIDTH + A_GROUPS * A_STATE].astype(BF16)
    cm = xc[:, A_WIDTH + A_GROUPS * A_STATE:].astype(BF16)

    dt = jax.nn.softplus(dt_raw + dtb_ref[...])
    dta = dt * (-jnp.exp(alog_ref[...]))
    row = lax.broadcasted_iota(jnp.int32, (q, q), 0)
    col = lax.broadcasted_iota(jnp.int32, (q, q), 1)
    tri = row >= col
    a_cum = jnp.dot(tri.astype(F32), dta, preferred_element_type=F32, precision=HIGHEST)

    def expand(v):
        return jnp.concatenate([jnp.broadcast_to(v[:, k:k + 1], (q, A_HEAD_DIM)) for k in range(A_HEADS)], axis=1)

    a_cum_e = expand(a_cum)
    xs = x * expand(dt)
    xs_b = xs.astype(BF16)
    a_cum_t = a_cum.T

    h = h_scr[...]
    h_b = h.astype(BF16)
    hpg = A_HEADS // A_GROUPS
    gw = hpg * A_HEAD_DIM
    y_diag, y_off = [], []
    for g in range(A_GROUPS):
        cg = cm[:, g * A_STATE:(g + 1) * A_STATE]
        bg = bm[:, g * A_STATE:(g + 1) * A_STATE]
        cb = _dot_nt(cg, bg)
        for kk in range(hpg):
            k = g * hpg + kk
            seg = a_cum[:, k:k + 1] - a_cum_t[k:k + 1, :]
            decay = jnp.exp(jnp.where(tri, seg, -jnp.inf))
            y_diag.append(_dot((cb * decay).astype(BF16), xs_b[:, k * A_HEAD_DIM:(k + 1) * A_HEAD_DIM]))
        y_off.append(_dot_nt(cg, h_b[g * gw:(g + 1) * gw]))
    y = (jnp.concatenate(y_diag, axis=1) + jnp.concatenate(y_off, axis=1) * jnp.exp(a_cum_e)
         + dexp_ref[...] * x)
    y = y * jax.nn.silu(z)

    last = a_cum_e[q - 1:q, :]
    xs_dec = (xs * jnp.exp(last - a_cum_e)).astype(BF16)
    dec_col = jnp.exp(a_cum_e.T[:, q - 1:q])
    upd = [_dot_tn(xs_dec[:, g * gw:(g + 1) * gw], bm[:, g * A_STATE:(g + 1) * A_STATE]) for g in range(A_GROUPS)]
    h_scr[...] = dec_col * h + jnp.concatenate(upd, axis=0)
    return _rms(y, nw_ref[...]).astype(BF16)


def _inproj_ssd_body(x_ref, nw_ref, w_ref, cw_ref, cb_ref, alog_ref, dtb_ref, dexp_ref, snw_ref,
                     u_ref, q_ref, k_ref, v_ref, ya_ref, hout_ref, tail_ref, h_scr, tail_scr, *, nsteps, chunk):
    c = pl.program_id(1)

    @pl.when(c == 0)
    def _():
        h_scr[...] = jnp.zeros_like(h_scr)
        tail_scr[0:SUBLANE] = jnp.zeros((SUBLANE, A_CONV_DIM), F32)

    h = _rms(x_ref[...], nw_ref[...]).astype(BF16)

    def proj(lo, hi):
        return _dot_nt(h, w_ref[lo:hi, :])

    z = proj(P_Z, P_XBC)
    xbc = proj(P_XBC, P_U)
    kvdt = proj(P_K, P_COLS)
    k_ref[...] = kvdt[:, :C_KV_WIDTH]
    v_ref[...] = kvdt[:, C_KV_WIDTH:2 * C_KV_WIDTH]
    dt = kvdt[:, 2 * C_KV_WIDTH:]
    for i in range(x_ref.shape[0] // chunk):
        rows = slice(i * chunk, (i + 1) * chunk)
        ya_ref[rows, :] = _ssd_chunk(xbc[rows], z[rows], dt[rows], cw_ref, cb_ref, alog_ref, dtb_ref, dexp_ref,
                                     snw_ref, h_scr, tail_scr)
    u = proj(P_U, P_Q)
    for k8 in range(N_OCT):
        u_ref[k8] = u[:, k8 * LANE:(k8 + 1) * LANE]
    q_ref[...] = proj(P_Q, P_K).astype(BF16)

    @pl.when(c == nsteps - 1)
    def _():
        hout_ref[0] = h_scr[...]
        tail_ref[0] = tail_scr[0:SUBLANE]


def _inproj_ssd(x, w, lw, nb, tm):
    t = x.shape[0]
    nsteps = t // nb // tm
    rmap = lambda b, c: (b * nsteps + c, 0)
    rows = lambda wd: pl.BlockSpec((tm, wd), rmap)
    per_seq = lambda s: pl.BlockSpec((1,) + s, lambda b, c: (b, 0, 0))
    return pl.pallas_call(
        functools.partial(_inproj_ssd_body, nsteps=nsteps, chunk=A_CHUNK),
        grid=(nb, nsteps),
        in_specs=[rows(D_MODEL), _full((1, D_MODEL)), _layer_weight((P_COLS, D_MODEL), lw["layer"]),
                  _full((A_CONV, A_CONV_DIM)), _full((1, A_CONV_DIM)), _full((1, LANE)), _full((1, LANE)),
                  _full((1, A_WIDTH)), _full((1, A_WIDTH))],
        out_specs=[pl.BlockSpec((N_OCT, tm, LANE), lambda b, c: (0, b * nsteps + c, 0)), rows(C_WIDTH),
                   rows(C_KV_WIDTH), rows(C_KV_WIDTH), rows(A_WIDTH),
                   per_seq((A_WIDTH, A_STATE)), per_seq((SUBLANE, A_CONV_DIM))],
        out_shape=[jax.ShapeDtypeStruct((N_OCT, t, LANE), F32), jax.ShapeDtypeStruct((t, C_WIDTH), BF16),
                   jax.ShapeDtypeStruct((t, C_KV_WIDTH), F32), jax.ShapeDtypeStruct((t, C_KV_WIDTH), F32),
                   jax.ShapeDtypeStruct((t, A_WIDTH), BF16), jax.ShapeDtypeStruct((nb, A_WIDTH, A_STATE), F32),
                   jax.ShapeDtypeStruct((nb, SUBLANE, A_CONV_DIM), F32)],
        scratch_shapes=[pltpu.VMEM((A_WIDTH, A_STATE), F32), pltpu.VMEM((SUBLANE + A_CHUNK, A_CONV_DIM), F32)],
        compiler_params=_cparams("parallel", "arbitrary"),
        name="inproj_ssd",
    )(x, lw["norm1_w"], w, lw["conv_w"], lw["conv_b"], lw["a_log"], lw["dt_bias"], lw["d_exp"], lw["ssd_norm_w"])


def _ssd_sample_body(*refs, valid):
    n_in = 13
    (xbc_ref, z_ref, dt_ref, conv0_ref, h0_ref, cw_ref, cb_ref, alog_ref, dtb_ref, dexp_ref, nwb_ref) = refs[:11]
    y_ref, conv_out_ref, h_out_ref = refs[n_in:n_in + 3]
    xs_scr, dx_scr, z_scr, y_scr, bt_scr, ct_scr, da_scr = refs[n_in + 3:]
    k = pl.program_id(0)
    ns = LANE

    @pl.when(k == 0)
    def _():
        rows = [conv0_ref[0, r] for r in range(A_CONV - 1)] + [xbc_ref[t * ns:(t + 1) * ns, :] for t in range(valid)]
        for t in range(valid):
            acc = cb_ref[...]
            for j in range(A_CONV):
                acc = acc + cw_ref[A_CONV - 1 - j:A_CONV - j, :] * rows[t + A_CONV - 1 - j]
            xc = jax.nn.silu(acc)
            dt = jax.nn.softplus(dt_ref[t * ns:(t + 1) * ns, :] + dtb_ref[...])
            dt_t = dt.T
            da_scr[t] = jnp.exp((dt * (-jnp.exp(alog_ref[...]))).T)
            x = xc[:, :A_WIDTH]
            x_t = x.T
            for kk in range(A_HEADS):
                hs = slice(kk * A_HEAD_DIM, (kk + 1) * A_HEAD_DIM)
                xs_scr[t, hs, :] = x_t[hs] * dt_t[kk:kk + 1, :]
            dx_scr[t] = (x * dexp_ref[...]).T
            bt_scr[t] = xc[:, A_WIDTH:A_WIDTH + A_GROUPS * A_STATE].T
            ct_scr[t] = xc[:, A_WIDTH + A_GROUPS * A_STATE:].T
            z_scr[t] = jax.nn.silu(z_ref[t * ns:(t + 1) * ns, :].astype(F32)).T
        for r in range(A_CONV - 1):
            conv_out_ref[0, r] = rows[valid + r]

    g0 = pl.multiple_of((k // (A_HEADS // A_GROUPS)) * A_STATE, A_STATE)

    def per_p(p, carry):
        h = h0_ref[0, 0, p]
        row = k * A_HEAD_DIM + p
        for t in range(valid):
            da = da_scr[t, pl.ds(k, 1), :]
            xr = xs_scr[t, pl.ds(row, 1), :]
            h = da * h + xr * bt_scr[t, pl.ds(g0, A_STATE), :]
            y_scr[t, pl.ds(row, 1), :] = jnp.sum(h * ct_scr[t, pl.ds(g0, A_STATE), :], axis=0, keepdims=True)
        h_out_ref[0, 0, p] = h
        return carry

    lax.fori_loop(0, A_HEAD_DIM, per_p, 0, unroll=2)

    @pl.when(k == A_HEADS - 1)
    def _():
        for t in range(valid):
            y = (y_scr[t] + dx_scr[t]) * z_scr[t]
            y = y * lax.rsqrt(jnp.mean(y * y, axis=0, keepdims=True) + EPS) * nwb_ref[...]
            y_ref[:, t * ns:(t + 1) * ns] = y.astype(BF16)


def _ssd_sample(xbc, z, dt, conv_all, h_all, lw, layer, valid, conv_prev, h_prev):
    rows = valid * LANE
    hblk = pl.BlockSpec((1, 1, A_HEAD_DIM, A_STATE, LANE), lambda k: (layer, k, 0, 0, 0))
    cblk = pl.BlockSpec((1, A_CONV - 1, LANE, A_CONV_DIM), lambda k: (layer, 0, 0, 0))
    in_specs = [_full((rows, A_CONV_DIM)), _full((rows, A_WIDTH)), _full((rows, LANE)), cblk, hblk,
                _full((A_CONV, A_CONV_DIM)), _full((1, A_CONV_DIM)), _full((1, LANE)), _full((1, LANE)),
                _full((1, A_WIDTH)), _full((A_WIDTH, LANE))] + [pl.BlockSpec(memory_space=pl.ANY)] * 2
    args = [xbc, z, dt, conv_all, h_all, lw["conv_w"], lw["conv_b"], lw["a_log"], lw["dt_bias"], lw["d_exp"],
            lw["ssd_norm_wb"], conv_prev, h_prev]
    aliases = {len(args) - 2: 1, len(args) - 1: 2}
    big = lambda n: pltpu.VMEM((valid, n, LANE), F32)
    return pl.pallas_call(
        functools.partial(_ssd_sample_body, valid=valid),
        grid=(A_HEADS,),
        in_specs=in_specs,
        out_specs=[_full((A_WIDTH, rows)), cblk, hblk],
        out_shape=[jax.ShapeDtypeStruct((A_WIDTH, rows), BF16), jax.ShapeDtypeStruct(conv_all.shape, F32),
                   jax.ShapeDtypeStruct(h_all.shape, F32)],
        scratch_shapes=[big(A_WIDTH), big(A_WIDTH), big(A_WIDTH), big(A_WIDTH), big(LANE), big(LANE), big(LANE)],
        input_output_aliases=aliases,
        compiler_params=_cparams("arbitrary"),
        name="ssd_sample",
    )(*args)


def _s5w_body(lre_ref, lim_ref, ldt_ref, btr_ref, bti_ref, cr_ref, ci_ref,
              bre_ref, bim_ref, bsre_ref, bsim_ref, cre_ref, cim_ref, toep_ref, ap_ref, cd_scr, bb_scr, *, valid):
    h, q = B_STATE, S5_Q
    for ref in (bre_ref, bim_ref, bsre_ref, bsim_ref, cre_ref, cim_ref, toep_ref):
        ref[...] = jnp.zeros(ref.shape, BF16)
    r_i = lax.broadcasted_iota(jnp.int32, (q * B_CH, q * LANE), 0)
    c_i = lax.broadcasted_iota(jnp.int32, (q * B_CH, q * LANE), 1)
    d = lax.broadcasted_iota(jnp.int32, (2 * SUBLANE, h), 0).astype(F32)
    for g8 in range(S5_OCT):
        lr, li = lre_ref[g8], lim_ref[g8]
        step = jnp.exp(ldt_ref[g8])
        er = jnp.exp(lr * step * d)
        ang = li * step * d
        ar, ai = er * jnp.cos(ang), er * jnp.sin(ang)
        nr, ni = ar[1:2] - 1.0, ai[1:2]
        den = lr * lr + li * li
        wr, wi = (nr * lr + ni * li) / den, (ni * lr - nr * li) / den
        btr, bti = btr_ref[g8], bti_ref[g8]
        bbr, bbi = wr * btr - wi * bti, wr * bti + wi * btr
        cr, ci = cr_ref[g8], ci_ref[g8]
        bb_scr[:, 0:h] = bbr
        bb_scr[:, h:2 * h] = bbi
        cols = slice(g8 * h, (g8 + 1) * h)
        for s in range(q):
            rows = slice(s * LANE + g8 * B_CH, s * LANE + (g8 + 1) * B_CH)
            e = q - 1 - s
            bre_ref[0, rows, cols] = (ar[e:e + 1] * bbr - ai[e:e + 1] * bbi).astype(BF16)
            bim_ref[0, rows, cols] = (ai[e:e + 1] * bbr + ar[e:e + 1] * bbi).astype(BF16)
            if s < valid:
                e = valid - 1 - s
                bsre_ref[0, rows, cols] = (ar[e:e + 1] * bbr - ai[e:e + 1] * bbi).astype(BF16)
                bsim_ref[0, rows, cols] = (ai[e:e + 1] * bbr + ar[e:e + 1] * bbi).astype(BF16)
            cre_ref[0, rows, cols] = (cr * ar[s + 1:s + 2] - ci * ai[s + 1:s + 2]).astype(BF16)
            cim_ref[0, rows, cols] = (-(cr * ai[s + 1:s + 2] + ci * ar[s + 1:s + 2])).astype(BF16)
            cd_scr[s * B_CH:(s + 1) * B_CH, 0:h] = cr * ar[s:s + 1] - ci * ai[s:s + 1]
            cd_scr[s * B_CH:(s + 1) * B_CH, h:2 * h] = -(cr * ai[s:s + 1] + ci * ar[s:s + 1])
        kdt = _dot_nt(bb_scr[...], cd_scr[...], precision=HIGHEST)
        place = (c_i == (r_i >> 4) * LANE + g8 * B_CH + (r_i & (B_CH - 1))).astype(F32)
        slab = jnp.dot(kdt, place, preferred_element_type=F32, precision=HIGHEST).astype(BF16)
        for s in range(q):
            toep_ref[0, s * LANE + g8 * B_CH:s * LANE + (g8 + 1) * B_CH, s * LANE:] = slab[:, 0:(q - s) * LANE]
        ap_ref[g8, 0:1, 0:h] = ar[q:q + 1]
        ap_ref[g8, 0:1, h:2 * h] = ai[q:q + 1]
        ap_ref[g8, 1:2, 0:h] = ar[valid:valid + 1]
        ap_ref[g8, 1:2, h:2 * h] = ai[valid:valid + 1]


def _s5_weights(lam_re, lam_im, log_dt, b_re, b_im, c_re, c_im, valid):
    g, h, q = B_GROUPS, B_STATE, S5_Q
    vec = pl.BlockSpec((S5_OCT, 1, h), lambda i: (i, 0, 0))
    mat = pl.BlockSpec((S5_OCT, B_CH, h), lambda i: (i, 0, 0))
    st = pl.BlockSpec((1, q * LANE, S5_OCT * h), lambda i: (i, 0, 0))
    st_shape = jax.ShapeDtypeStruct((N_OCT, q * LANE, S5_OCT * h), BF16)
    bre, bim, bsre, bsim, cre, cim, toep, ap = pl.pallas_call(
        functools.partial(_s5w_body, valid=valid),
        grid=(N_OCT,),
        in_specs=[vec, vec, vec, mat, mat, mat, mat],
        out_specs=[st] * 6 + [pl.BlockSpec((1, q * LANE, q * LANE), lambda i: (i, 0, 0)),
                              pl.BlockSpec((S5_OCT, 2, 2 * h), lambda i: (i, 0, 0))],
        out_shape=[st_shape] * 6 + [jax.ShapeDtypeStruct((N_OCT, q * LANE, q * LANE), BF16),
                                    jax.ShapeDtypeStruct((g, 2, 2 * h), F32)],
        scratch_shapes=[pltpu.VMEM((q * B_CH, 2 * h), F32), pltpu.VMEM((B_CH, 2 * h), F32)],
        compiler_params=_cparams("parallel"),
        name="s5_weights",
    )(lam_re.reshape(g, 1, h), lam_im.reshape(g, 1, h), jnp.broadcast_to(log_dt.reshape(g, 1, 1), (g, 1, h)),
      jnp.swapaxes(b_re, 1, 2), jnp.swapaxes(b_im, 1, 2), c_re, c_im)
    return dict(
        bst_re=bre, bst_im=bim, bsts_re=bsre, bsts_im=bsim, cst_re=cre, cst_im=cim, toep=toep,
        aq_re=ap[:, 0, :h].reshape(1, S5_STATE_COLS), aq_im=ap[:, 0, h:].reshape(1, S5_STATE_COLS),
        av_re=ap[:, 1, :h].reshape(1, S5_STATE_COLS), av_im=ap[:, 1, h:].reshape(1, S5_STATE_COLS),
    )


def _oct_rows(u_ref, k8, rt):
    return jnp.concatenate([u_ref[k8, pl.ds(j, rt, stride=S5_Q), :] for j in range(S5_Q)], axis=1).astype(BF16)


def _toep_dot(uo, toep_ref, k8):
    two = 2 * LANE
    cols = [_dot(uo[:, :(t + 1) * two], toep_ref[k8, :(t + 1) * two, t * two:(t + 1) * two])
            for t in range(uo.shape[1] // two)]
    return jnp.concatenate(cols, axis=1)


def _s5_state_body(u_ref, bre_ref, bim_ref, sre_ref, sim_ref):
    w = S5_OCT * B_STATE
    for k8 in range(N_OCT):
        uo = _oct_rows(u_ref, k8, sre_ref.shape[0])
        sre_ref[:, k8 * w:(k8 + 1) * w] = _dot(uo, bre_ref[k8])
        sim_ref[:, k8 * w:(k8 + 1) * w] = _dot(uo, bim_ref[k8])


def _s5_state(u, bre, bim, rt):
    r = u.shape[1] // S5_Q
    wspec = _full((N_OCT, S5_Q * LANE, S5_OCT * B_STATE))
    ospec = pl.BlockSpec((rt, S5_STATE_COLS), lambda i: (i, 0))
    return pl.pallas_call(
        _s5_state_body,
        grid=(r // rt,),
        in_specs=[_tile_major_spec(rt * S5_Q), wspec, wspec],
        out_specs=[ospec, ospec],
        out_shape=[jax.ShapeDtypeStruct((r, S5_STATE_COLS), F32)] * 2,
        compiler_params=_cparams("parallel"),
        name="s5_state",
    )(u, bre, bim)


def _s5_scan_body(sre_ref, sim_ref, ar_ref, ai_ref, hre_ref, him_ref, fre_ref, fim_ref, cr_scr, ci_scr, *, rb, nblk):
    i = pl.program_id(0)
    nb = sre_ref.shape[0]

    @pl.when(i == 0)
    def _():
        cr_scr[...] = jnp.zeros_like(cr_scr)
        ci_scr[...] = jnp.zeros_like(ci_scr)

    ar, ai = ar_ref[...], ai_ref[...]

    def step(r, carry):
        out = []
        for b in range(nb):
            hr, hi = carry[2 * b], carry[2 * b + 1]
            hre_ref[b, pl.ds(r, 1), :] = hr
            him_ref[b, pl.ds(r, 1), :] = hi
            sr, si = sre_ref[b, pl.ds(r, 1), :], sim_ref[b, pl.ds(r, 1), :]
            out += [ar * hr - ai * hi + sr, ai * hr + ar * hi + si]
        return tuple(out)

    init = tuple(s[b] for b in range(nb) for s in (cr_scr, ci_scr))
    fin = lax.fori_loop(0, rb, step, init)
    for b in range(nb):
        cr_scr[b] = fin[2 * b]
        ci_scr[b] = fin[2 * b + 1]

    @pl.when(i == nblk - 1)
    def _():
        for b in range(nb):
            fre_ref[b] = fin[2 * b]
            fim_ref[b] = fin[2 * b + 1]


def _s5_scan(sre, sim, ar, ai, nb, rb):
    r = sre.shape[0] // nb
    nblk = r // rb
    blk = pl.BlockSpec((nb, rb, S5_STATE_COLS), lambda i: (0, i, 0))
    vec = _full((1, S5_STATE_COLS))
    fin = _full((nb, 1, S5_STATE_COLS))
    per_batch = lambda a: a.reshape(nb, r, S5_STATE_COLS)
    hre, him, fre, fim = pl.pallas_call(
        functools.partial(_s5_scan_body, rb=rb, nblk=nblk),
        grid=(nblk,),
        in_specs=[blk, blk, vec, vec],
        out_specs=[blk, blk, fin, fin],
        out_shape=[jax.ShapeDtypeStruct((nb, r, S5_STATE_COLS), F32)] * 2
        + [jax.ShapeDtypeStruct((nb, 1, S5_STATE_COLS), F32)] * 2,
        scratch_shapes=[pltpu.VMEM((nb, 1, S5_STATE_COLS), F32)] * 2,
        compiler_params=_cparams("arbitrary"),
        name="s5_scan",
    )(per_batch(sre), per_batch(sim), ar, ai)
    return hre.reshape(nb * r, S5_STATE_COLS), him.reshape(nb * r, S5_STATE_COLS), fre, fim


def _s5_sample_body(u_ref, toep_ref, cre_ref, cim_ref, bre_ref, bim_ref, hre_ref, him_ref, ar_ref, ai_ref,
                    y_ref, fre_ref, fim_ref, *, valid):
    w = S5_OCT * B_STATE
    ns = LANE
    for k8 in range(N_OCT):
        blocks = [u_ref[k8, t * ns:(t + 1) * ns, :] for t in range(valid)]
        uo = jnp.concatenate(blocks, axis=1).astype(BF16)
        uo_t = jnp.concatenate([b.T for b in blocks], axis=0).astype(BF16)
        hr, hi = hre_ref[0, k8 * w:(k8 + 1) * w, :], him_ref[0, k8 * w:(k8 + 1) * w, :]
        yo = (_toep_dot(uo, toep_ref, k8) + _dot_nt(hr.T.astype(BF16), cre_ref[k8])
              + _dot_nt(hi.T.astype(BF16), cim_ref[k8]))
        for t in range(valid):
            y_ref[k8, t * ns:(t + 1) * ns, :] = yo[:, t * LANE:(t + 1) * LANE]
        ar, ai = ar_ref[k8 * w:(k8 + 1) * w, :], ai_ref[k8 * w:(k8 + 1) * w, :]
        fre_ref[k8 * w:(k8 + 1) * w, :] = ar * hr - ai * hi + _dot_tn(bre_ref[k8], uo_t)
        fim_ref[k8 * w:(k8 + 1) * w, :] = ai * hr + ar * hi + _dot_tn(bim_ref[k8], uo_t)


def _s5_sample(u, s5w, hre_all, him_all, ar, ai, layer, valid):
    rows = valid * LANE
    vq = valid * LANE
    sub = lambda c: pl.BlockSpec((N_OCT, vq, c), lambda i: (0, 0, 0))
    hblk = pl.BlockSpec((1, S5_STATE_COLS, LANE), lambda i: (layer, 0, 0))
    st = _full((S5_STATE_COLS, LANE))
    return pl.pallas_call(
        functools.partial(_s5_sample_body, valid=valid),
        grid=(1,),
        in_specs=[_tile_major_spec(rows), sub(vq), sub(S5_OCT * B_STATE), sub(S5_OCT * B_STATE),
                  sub(S5_OCT * B_STATE), sub(S5_OCT * B_STATE), hblk, hblk, st, st],
        out_specs=[_tile_major_spec(rows), st, st],
        out_shape=[jax.ShapeDtypeStruct((N_OCT, rows, LANE), F32)] + [jax.ShapeDtypeStruct((S5_STATE_COLS, LANE), F32)] * 2,
        compiler_params=_cparams("arbitrary"),
        name="s5_sample",
    )(u, s5w["toep"], s5w["cst_re"], s5w["cst_im"], s5w["bsts_re"], s5w["bsts_im"], hre_all, him_all, ar, ai)


def _s5_out_body(u_ref, hre_ref, him_ref, toep_ref, cre_ref, cim_ref, y_ref):
    w = S5_OCT * B_STATE
    rt = hre_ref.shape[0]
    for k8 in range(N_OCT):
        uo = _oct_rows(u_ref, k8, rt)
        yo = (_toep_dot(uo, toep_ref, k8)
              + _dot_nt(hre_ref[:, k8 * w:(k8 + 1) * w].astype(BF16), cre_ref[k8])
              + _dot_nt(him_ref[:, k8 * w:(k8 + 1) * w].astype(BF16), cim_ref[k8]))
        for j in range(S5_Q):
            y_ref[k8, pl.ds(j, rt, stride=S5_Q), :] = yo[:, j * LANE:(j + 1) * LANE]


def _s5_out(u, hre, him, toep, cre, cim, rt):
    r = u.shape[1] // S5_Q
    hblk = pl.BlockSpec((rt, S5_STATE_COLS), lambda i: (i, 0))
    return pl.pallas_call(
        _s5_out_body,
        grid=(r // rt,),
        in_specs=[_tile_major_spec(rt * S5_Q), hblk, hblk, _full((N_OCT, S5_Q * LANE, S5_Q * LANE)),
                  _full((N_OCT, S5_Q * LANE, S5_OCT * B_STATE)), _full((N_OCT, S5_Q * LANE, S5_OCT * B_STATE))],
        out_specs=_tile_major_spec(rt * S5_Q),
        out_shape=jax.ShapeDtypeStruct(u.shape, F32),
        compiler_params=_cparams("parallel"),
        name="s5_out",
    )(u, hre, him, toep, cre, cim)


def _band_body(bucket_ref, table_ref, o_ref):
    bucket = bucket_ref[...]
    col = lax.broadcasted_iota(jnp.int32, (WINDOW, 2 * WINDOW), 1)
    for hq in range(C_HEADS):
        base = jnp.full((1, 2 * WINDOW), NEG, F32)
        for b in range(REL_BUCKETS):
            base = jnp.where(bucket == b, table_ref[b, hq], base)
        tile = pltpu.roll(jnp.broadcast_to(base, (WINDOW, 2 * WINDOW)), 0, 1, stride=1, stride_axis=0)
        o_ref[1, hq] = tile.T
        o_ref[0, hq] = jnp.where(col >= WINDOW, tile, NEG).T


def _bias_band(table):
    lane = np.arange(2 * WINDOW)
    n = WINDOW - lane
    exact = REL_BUCKETS // 2
    nf = np.maximum(n, 1).astype(np.float32)
    large = exact + (np.log(nf / np.float32(exact)) / np.float32(math.log(REL_MAX_DIST / exact))
                     * np.float32(REL_BUCKETS - exact)).astype(np.int32)
    bucket = np.where(n < exact, n, np.minimum(large, REL_BUCKETS - 1))
    bucket = np.where((n >= 0) & (n < WINDOW), bucket, -1).astype(np.int32)
    return pl.pallas_call(
        _band_body,
        in_specs=[pl.BlockSpec(memory_space=pltpu.VMEM), pl.BlockSpec(memory_space=pltpu.SMEM)],
        out_specs=pl.BlockSpec(memory_space=pltpu.VMEM),
        out_shape=jax.ShapeDtypeStruct((2, C_HEADS, 2 * WINDOW, WINDOW), F32),
        name="rel_bias_band",
    )(jnp.asarray(bucket).reshape(1, 2 * WINDOW), table)


def _attn_prompt_body(q_ref, kc_ref, kp_ref, vc_ref, vp_ref, b0_ref, br_ref, sink_ref, o_ref, s_scr, p_scr, *, nblk):
    k_all = jnp.concatenate([kp_ref[...], kc_ref[...]], axis=0) * (C_HEAD_DIM ** -0.5)
    v_all = jnp.concatenate([vp_ref[...], vc_ref[...]], axis=0)
    swapped = (pltpu.roll(k_all, C_HEAD_DIM, 1), pltpu.roll(v_all, C_HEAD_DIM, 1))
    low = lax.broadcasted_iota(jnp.int32, k_all.shape, 1) < C_HEAD_DIM
    kz, vz = [], []
    for hk in range(C_KV_HEADS):
        k_lo = k_all if hk == 0 else swapped[0]
        k_hi = swapped[0] if hk == 0 else k_all
        kz.append((jnp.where(low, k_lo, 0.0).astype(BF16), jnp.where(low, 0.0, k_hi).astype(BF16)))
        vz.append((v_all if hk == 0 else swapped[1])[:, :C_HEAD_DIM].astype(BF16))
    for i in range(nblk):
        keys = slice(i * WINDOW, (i + 2) * WINDOW)
        slot = i % 2
        for hq in range(C_HEADS):
            qt = q_ref[i * WINDOW:(i + 1) * WINDOW, (hq // 2) * LANE:(hq // 2 + 1) * LANE]
            s_scr[slot, hq] = _dot_nt(kz[hq // C_GQA][hq % 2][keys], qt)
        inv = []
        for hq in range(C_HEADS):
            sk = sink_ref[hq]
            s = s_scr[slot, hq] + (b0_ref[0, hq] if i == 0 else br_ref[0, hq])
            m = jnp.maximum(jnp.max(s, axis=0, keepdims=True), sk)
            p = jnp.exp(s - m)
            inv.append(1.0 / (jnp.sum(p, axis=0, keepdims=True) + jnp.exp(sk - m)))
            p_scr[slot, hq] = p.astype(BF16)
        for hq in range(C_HEADS):
            o = _dot_tn(vz[hq // C_GQA][keys], p_scr[slot, hq]) * inv[hq]
            o_ref[hq * C_HEAD_DIM:(hq + 1) * C_HEAD_DIM, i * WINDOW:(i + 1) * WINDOW] = o.astype(BF16)


def _attn_prompt(q, k, v, band, sinks, nb, nblk_seq, nblk):
    steps = nblk_seq // nblk
    cur = lambda b, n: (b * steps + n, 0)
    prev = lambda b, n: (b * nblk_seq + jnp.maximum(n * nblk - 1, 0), 0)
    bspec = lambda m: pl.BlockSpec((1, C_HEADS, 2 * WINDOW, WINDOW), m)
    return pl.pallas_call(
        functools.partial(_attn_prompt_body, nblk=nblk),
        grid=(nb, steps),
        in_specs=[pl.BlockSpec((nblk * WINDOW, C_WIDTH), cur),
                  pl.BlockSpec((nblk * WINDOW, C_KV_WIDTH), cur), pl.BlockSpec((WINDOW, C_KV_WIDTH), prev),
                  pl.BlockSpec((nblk * WINDOW, C_KV_WIDTH), cur), pl.BlockSpec((WINDOW, C_KV_WIDTH), prev),
                  bspec(lambda b, n: (jnp.minimum(n, 1), 0, 0, 0)), bspec(lambda b, n: (1, 0, 0, 0)),
                  pl.BlockSpec(memory_space=pltpu.SMEM)],
        out_specs=pl.BlockSpec((C_WIDTH, nblk * WINDOW), lambda b, n: (0, b * steps + n)),
        out_shape=jax.ShapeDtypeStruct((C_WIDTH, nb * nblk_seq * WINDOW), BF16),
        scratch_shapes=[pltpu.VMEM((2, C_HEADS, 2 * WINDOW, WINDOW), F32),
                        pltpu.VMEM((2, C_HEADS, 2 * WINDOW, WINDOW), BF16)],
        compiler_params=_cparams("parallel", "arbitrary"),
        name="attn_prompt",
    )(q, k, k, v, v, band, band, sinks)


def _attn_sample_body(*refs, valid, lb, nseq):
    q_ref, kn_ref, vn_ref, kc_ref, vc_ref, b1_ref, b2_ref, sink_ref = refs[:8]
    o_ref, ko_ref, vo_ref, s1_scr, s2_scr, inv_scr, o_scr = refs[-7:]
    d = C_HEAD_DIM
    rs = 2 * valid
    lane_c = lax.broadcasted_iota(jnp.int32, (C_KV_WIDTH, lb), 1)
    lane_n = lax.broadcasted_iota(jnp.int32, (SUBLANE, C_KV_WIDTH), 1)
    pad = jnp.zeros((SUBLANE - valid, C_KV_WIDTH), F32)
    zero_half = jnp.zeros((d, lb), BF16)
    combos = [(hk, par) for hk in range(C_KV_HEADS) for par in range(2)]

    def new_rows(ref, s):
        return jnp.concatenate([ref[:, s, :], pad], axis=0)

    for s in range(nseq):
        qs = q_ref[:, s, :].astype(F32)
        kn, vn = new_rows(kn_ref, s), new_rows(vn_ref, s)
        kt, vt = kc_ref[0, s], vc_ref[0, s]
        kn_t, vn_t = kn.T, vn.T
        new_k, new_v = pltpu.roll(kt, lb - valid, 1), pltpu.roll(vt, lb - valid, 1)
        for t in range(valid):
            at = lane_c == lb - valid + t
            new_k = jnp.where(at, jnp.broadcast_to(kn_t[:, t:t + 1], (C_KV_WIDTH, lb)), new_k)
            new_v = jnp.where(at, jnp.broadcast_to(vn_t[:, t:t + 1], (C_KV_WIDTH, lb)), new_v)
        ko_ref[0, s] = new_k
        vo_ref[0, s] = new_v
        kn_sw = pltpu.roll(kn, d, 1)
        for c, (hk, par) in enumerate(combos):
            kth = kt[hk * d:(hk + 1) * d].astype(BF16)
            kz = jnp.concatenate([kth, zero_half] if par == 0 else [zero_half, kth], axis=0)
            src = kn if hk == par else kn_sw
            knz = jnp.where((lane_n < d) if par == 0 else (lane_n >= d), src, 0.0).astype(BF16)
            qq = jnp.concatenate([qs[:, (hk * 2 + j) * LANE:(hk * 2 + j + 1) * LANE] for j in range(2)],
                                 axis=0).astype(BF16)
            s1_scr[c, s * rs:(s + 1) * rs, :] = _dot(qq, kz)
            s2_scr[c, s * rs:(s + 1) * rs, :] = _dot_nt(qq, knz)

    for c, (hk, par) in enumerate(combos):
        s1 = s1_scr[c] * (d ** -0.5) + b1_ref[hk, par]
        s2 = s2_scr[c] * (d ** -0.5) + b2_ref[hk, par]
        sk = sink_ref[hk, par]
        m = jnp.maximum(jnp.maximum(jnp.max(s1, axis=-1, keepdims=True), jnp.max(s2, axis=-1, keepdims=True)), sk)
        p1, p2 = jnp.exp(s1 - m), jnp.exp(s2 - m)
        den = jnp.sum(p1, axis=-1, keepdims=True) + jnp.sum(p2, axis=-1, keepdims=True) + jnp.exp(sk - m)
        s1_scr[c] = p1
        s2_scr[c] = p2
        inv_scr[c] = jnp.broadcast_to(1.0 / den, (nseq * rs, SUBLANE))

    for s in range(nseq):
        rows = slice(s * rs, (s + 1) * rs)
        vt = vc_ref[0, s]
        vn = new_rows(vn_ref, s)
        for c, (hk, par) in enumerate(combos):
            vth = vt[hk * d:(hk + 1) * d].astype(BF16)
            vnh = vn[:, hk * d:(hk + 1) * d].astype(BF16)
            o = _dot_nt(s1_scr[c, rows, :].astype(BF16), vth) + _dot(s2_scr[c, rows, :].astype(BF16), vnh)
            o_scr[hk, rows, par * d:(par + 1) * d] = o * inv_scr[c, rows, 0:1]

    for hk in range(C_KV_HEADS):
        for j in range(2):
            for t in range(valid):
                tile = hk * 2 + j
                o_ref[t, :, tile * LANE:(tile + 1) * LANE] = o_scr[hk, pl.ds(j * valid + t, nseq, stride=rs), :]


def _attn_sample(q3, kn3, vn3, kc_all, vc_all, b1, b2, sink_col, layer, k_prev, v_prev):
    valid, nsb, _ = q3.shape
    lb = kc_all.shape[-1]
    nseq = 2 * SUBLANE
    assert 2 * valid == SUBLANE
    rows = lambda w: pl.BlockSpec((valid, nseq, w), lambda i: (0, i, 0))
    cblk = pl.BlockSpec((1, nseq, C_KV_WIDTH, lb), lambda i: (layer, i, 0, 0))
    b1, b2, sink_col = (jnp.tile(a, (1, 1, nseq, 1)) for a in (b1, b2, sink_col))
    in_specs = [rows(C_WIDTH), rows(C_KV_WIDTH), rows(C_KV_WIDTH), cblk, cblk,
                _full(b1.shape), _full(b2.shape), _full(sink_col.shape)] + [pl.BlockSpec(memory_space=pl.ANY)] * 2
    args = [q3, kn3, vn3, kc_all, vc_all, b1, b2, sink_col, k_prev, v_prev]
    aliases = {len(args) - 2: 1, len(args) - 1: 2}
    return pl.pallas_call(
        functools.partial(_attn_sample_body, valid=valid, lb=lb, nseq=nseq),
        grid=(nsb // nseq,),
        in_specs=in_specs,
        out_specs=[rows(C_WIDTH), cblk, cblk],
        out_shape=[jax.ShapeDtypeStruct((valid, nsb, C_WIDTH), F32), jax.ShapeDtypeStruct(kc_all.shape, F32),
                   jax.ShapeDtypeStruct(vc_all.shape, F32)],
        scratch_shapes=[pltpu.VMEM((2 * C_KV_HEADS, nseq * SUBLANE, lb), F32),
                        pltpu.VMEM((2 * C_KV_HEADS, nseq * SUBLANE, SUBLANE), F32),
                        pltpu.VMEM((2 * C_KV_HEADS, nseq * SUBLANE, SUBLANE), F32),
                        pltpu.VMEM((C_KV_HEADS, nseq * SUBLANE, LANE), F32)],
        input_output_aliases=aliases,
        compiler_params=_cparams("arbitrary"),
        name="attn_sample",
    )(*args)


def _merge_body(x_ref, ya_ref, yb_ref, u_ref, yc_ref, nw_ref, wg_ref, d5_ref, wglu_ref,
                wa_ref, wb_ref, wc_ref, wo_ref, o_ref, *, ya_features_major, yc_features_major):
    x = x_ref[...]
    h = _rms(x, nw_ref[...]).astype(BF16)
    g = jax.nn.gelu(_lanes(yb_ref) + d5_ref[...] * _lanes(u_ref))
    gv = _dot(g.astype(BF16), wglu_ref[...])
    yb = (gv[:, :B_WIDTH] * jax.nn.sigmoid(gv[:, B_WIDTH:])).astype(BF16)

    def gate(i):
        return jax.nn.sigmoid(_dot_nt(h, wg_ref[i * D_MODEL:(i + 1) * D_MODEL, :]))

    def branch(y_ref, w_ref, features_major):
        y = y_ref[...].astype(BF16)
        return _dot_tn(y, w_ref[...]) if features_major else _dot(y, w_ref[...])

    mixed = gate(0) * branch(ya_ref, wa_ref, ya_features_major)
    mixed = mixed + gate(1) * _dot(yb, wb_ref[...])
    mixed = mixed + gate(2) * branch(yc_ref, wc_ref, yc_features_major)
    o_ref[...] = x + _dot(mixed.astype(BF16), wo_ref[...])


def _merge(x, ya, yb, u, yc, lw, tm, ya_features_major, yc_features_major):
    t = x.shape[0]
    rows = lambda w: pl.BlockSpec((tm, w), lambda i: (i, 0))
    branch_spec = lambda fm: pl.BlockSpec((A_WIDTH, tm), lambda i: (0, i)) if fm else rows(A_WIDTH)
    lwt = lambda shape: _layer_weight(shape, lw["layer"])
    return pl.pallas_call(
        functools.partial(_merge_body, ya_features_major=ya_features_major, yc_features_major=yc_features_major),
        grid=(t // tm,),
        in_specs=[rows(D_MODEL), branch_spec(ya_features_major), _tile_major_spec(tm), _tile_major_spec(tm),
                  branch_spec(yc_features_major),
                  _full((1, D_MODEL)), lwt((3 * D_MODEL, D_MODEL)), _full((1, B_WIDTH)),
                  lwt((B_WIDTH, 2 * B_WIDTH)), lwt((A_WIDTH, D_MODEL)), lwt((B_WIDTH, D_MODEL)),
                  lwt((C_WIDTH, D_MODEL)), lwt((D_MODEL, D_MODEL))],
        out_specs=rows(D_MODEL),
        out_shape=jax.ShapeDtypeStruct((t, D_MODEL), F32),
        compiler_params=_cparams("parallel"),
        name="merge",
    )(x, ya, yb, u, yc, lw["norm1_w"], lw["w_gate"], lw["s5_d"], lw["w_glu"],
      lw["w_br_a"], lw["w_br_b"], lw["w_br_c"], lw["w_out"])


def _ffn_body(x_ref, nw_ref, wup_ref, wdn_ref, fnw_ref, o_ref, act_scr, *, final_norm):
    x = x_ref[...]
    h = _rms(x, nw_ref[...]).astype(BF16)
    for c in range(D_FF // FF_CHUNK):
        lo, hi = c * FF_CHUNK, (c + 1) * FF_CHUNK
        a = _dot(h, wup_ref[:, lo:hi])
        b = _dot(h, wup_ref[:, D_FF + lo:D_FF + hi])
        act_scr[:, lo:hi] = (jax.nn.silu(a) * b).astype(BF16)
    y = x + _dot(act_scr[...], wdn_ref[...])
    if final_norm:
        y = _rms(y, fnw_ref[...])
    o_ref[...] = y


def _ffn(x, lw, fnw, tm, final_norm):
    t = x.shape[0]
    rows = pl.BlockSpec((tm, D_MODEL), lambda i: (i, 0))
    return pl.pallas_call(
        functools.partial(_ffn_body, final_norm=final_norm),
        grid=(t // tm,),
        in_specs=[rows, _full((1, D_MODEL)), _layer_weight((D_MODEL, 2 * D_FF), lw["layer"]),
                  _layer_weight((D_FF, D_MODEL), lw["layer"]), _full((1, D_MODEL))],
        out_specs=rows,
        out_shape=jax.ShapeDtypeStruct((t, D_MODEL), F32),
        scratch_shapes=[pltpu.VMEM((tm, D_FF), BF16)],
        compiler_params=_cparams("parallel"),
        name="ffn",
    )(x, lw["norm2_w"], lw["w_ffn_up"], lw["w_ffn_down"], fnw)


def _row_tile(t, pref):
    tm = pref
    while t % tm:
        tm //= 2
    return tm


def kernel(x_prompt, x_sample, state_ssd, state_conv, state_s5_re, state_s5_im, cache_k, cache_v, norm1_w, w_in, conv_w, conv_b, ssd_a_log, ssd_dt_bias, ssd_d, ssd_norm_w, s5_lam_re, s5_lam_im, s5_log_dt, s5_b_re, s5_b_im, s5_c_re, s5_c_im, s5_d, s5_w_glu, attn_sinks, w_br_a, w_br_b, w_br_c, w_out, norm2_w, w_ffn_up, w_ffn_down, rel_bias, final_norm_w):
    nb, seq, _ = x_prompt.shape
    nsb, valid, _ = x_sample.shape
    depth = w_in.shape[0]
    lb = cache_k.shape[2]
    assert seq % A_CHUNK == 0 and seq % WINDOW == 0 and seq % (S5_Q * SUBLANE) == 0
    assert nsb == LANE and A_CONV - 1 <= valid <= S5_Q and lb == WINDOW
    nchunk = seq // A_CHUNK
    tp = nb * seq
    ts = valid * nsb

    band = _bias_band(rel_bias)

    def pair_rows(a):
        a = a.reshape(C_KV_HEADS, 2, 2, valid, a.shape[-1])
        return jnp.swapaxes(a, 1, 2).reshape(C_KV_HEADS, 2, 2 * valid, a.shape[-1])

    band_s = pair_rows(jnp.swapaxes(band[1, :, :, :valid], 1, 2))
    bias_s1 = band_s[..., :lb]
    bias_s2 = band_s[..., lb:lb + SUBLANE]
    fnw = final_norm_w.reshape(1, D_MODEL)

    xp = x_prompt.reshape(tp, D_MODEL)
    xs = jnp.swapaxes(x_sample, 0, 1).reshape(ts, D_MODEL)
    conv_all = jnp.swapaxes(state_conv, 1, 2)
    ssd_all = jnp.transpose(state_ssd, (0, 2, 3, 4, 1))
    s5re_all = jnp.transpose(state_s5_re, (0, 2, 3, 1)).reshape(depth, S5_STATE_COLS, nsb)
    s5im_all = jnp.transpose(state_s5_im, (0, 2, 3, 1)).reshape(depth, S5_STATE_COLS, nsb)
    kc_all = jnp.transpose(cache_k, (0, 1, 3, 4, 2)).reshape(depth, nsb, C_KV_WIDTH, lb)
    vc_all = jnp.transpose(cache_v, (0, 1, 3, 4, 2)).reshape(depth, nsb, C_KV_WIDTH, lb)
    tm_p = _row_tile(seq, 512)
    tm_s = _row_tile(ts, 512)
    rt_p = _row_tile(tp // S5_Q, 256)
    rb_p = _row_tile(seq // S5_Q, 128)
    attn_blk = _row_tile(seq // WINDOW, 16)

    new_p, s5_s = [], []
    conv_new, ssd_new = jnp.zeros(conv_all.shape, F32), jnp.zeros(ssd_all.shape, F32)
    k_new, v_new = jnp.zeros(kc_all.shape, F32), jnp.zeros(vc_all.shape, F32)
    off = [0, 512, 1280, 1288, 1800, 2312, 2440, 2568, 5640]
    w_in_t = jnp.swapaxes(w_in, 1, 2)
    w_pack = jnp.concatenate(
        [w_in_t[:, off[0]:off[1]], w_in_t[:, off[1]:off[2]], w_in_t[:, off[3]:off[4]], w_in_t[:, off[4]:off[5]],
         w_in_t[:, off[5]:off[6]], w_in_t[:, off[6]:off[7]],
         jnp.pad(w_in_t[:, off[2]:off[3]], ((0, 0), (0, LANE - A_HEADS), (0, 0)))], axis=1).astype(BF16)
    stacked = dict(
        w_gate=w_in_t[:, off[7]:off[8]].astype(BF16), w_glu=s5_w_glu.astype(BF16),
        w_br_a=w_br_a.astype(BF16), w_br_b=w_br_b.astype(BF16), w_br_c=w_br_c.astype(BF16),
        w_out=w_out.astype(BF16), w_ffn_up=w_ffn_up.astype(BF16), w_ffn_down=w_ffn_down.astype(BF16))
    for i in range(depth):
        sink_rows = pair_rows(jnp.broadcast_to(attn_sinks[i].reshape(C_HEADS, 1, 1), (C_HEADS, valid, 1)))
        lw = dict(
            stacked, layer=i, norm1_w=norm1_w[i].reshape(1, D_MODEL),
            conv_w=conv_w[i], conv_b=conv_b[i].reshape(1, A_CONV_DIM),
            a_log=jnp.pad(ssd_a_log[i], (0, LANE - A_HEADS)).reshape(1, LANE),
            dt_bias=jnp.pad(ssd_dt_bias[i], (0, LANE - A_HEADS)).reshape(1, LANE),
            d_exp=jnp.repeat(ssd_d[i], A_HEAD_DIM).reshape(1, A_WIDTH),
            ssd_norm_w=ssd_norm_w[i].reshape(1, A_WIDTH),
            ssd_norm_wb=jnp.broadcast_to(ssd_norm_w[i].reshape(A_WIDTH, 1), (A_WIDTH, LANE)),
            s5_d=s5_d[i].reshape(1, B_WIDTH), norm2_w=norm2_w[i].reshape(1, D_MODEL),
        )
        s5w = _s5_weights(s5_lam_re[i], s5_lam_im[i], s5_log_dt[i], s5_b_re[i], s5_b_im[i],
                          s5_c_re[i], s5_c_im[i], valid)
        last = i == depth - 1

        u, q, k, v, ya, ssd_h, conv_tail = _inproj_ssd(xp, w_pack, lw, nb, tm_p)
        sre, sim = _s5_state(u, s5w["bst_re"], s5w["bst_im"], rt_p)
        hre, him, fre, fim = _s5_scan(sre, sim, s5w["aq_re"], s5w["aq_im"], nb, rb_p)
        yb = _s5_out(u, hre, him, s5w["toep"], s5w["cst_re"], s5w["cst_im"], rt_p)
        yc = _attn_prompt(q, k, v, band, attn_sinks[i], nb, seq // WINDOW, attn_blk)
        x1 = _merge(xp, ya, yb, u, yc, lw, tm_p, False, True)
        xp = _ffn(x1, lw, fnw, _row_tile(seq, 2 * tm_p), last)
        new_p.append((
            ssd_h.reshape(nb, A_HEADS, A_HEAD_DIM, A_STATE),
            conv_tail[:, SUBLANE - (A_CONV - 1):],
            fre.reshape(nb, B_GROUPS, B_STATE), fim.reshape(nb, B_GROUPS, B_STATE),
            k.reshape(nb, seq, C_KV_WIDTH)[:, seq - WINDOW:].reshape(nb, WINDOW, C_KV_HEADS, C_HEAD_DIM),
            v.reshape(nb, seq, C_KV_WIDTH)[:, seq - WINDOW:].reshape(nb, WINDOW, C_KV_HEADS, C_HEAD_DIM)))

        z, xbc, u, q, k, v, dt = _inproj(xs, lw["norm1_w"], w_pack, i, tm_s)
        ya, conv_new, ssd_new = _ssd_sample(xbc, z, dt, conv_all, ssd_all, lw, i, valid, conv_new, ssd_new)
        av_re = jnp.broadcast_to(s5w["av_re"].reshape(S5_STATE_COLS, 1), (S5_STATE_COLS, nsb))
        av_im = jnp.broadcast_to(s5w["av_im"].reshape(S5_STATE_COLS, 1), (S5_STATE_COLS, nsb))
        yb, fre, fim = _s5_sample(u, s5w, s5re_all, s5im_all, av_re, av_im, i, valid)
        yc, k_new, v_new = _attn_sample(
            q.reshape(valid, nsb, C_WIDTH), k.reshape(valid, nsb, C_KV_WIDTH), v.reshape(valid, nsb, C_KV_WIDTH),
            kc_all, vc_all, bias_s1, bias_s2, sink_rows, i, k_new, v_new)
        x1 = _merge(xs, ya, yb, u, yc.reshape(ts, C_WIDTH), lw, tm_s, True, False)
        xs = _ffn(x1, lw, fnw, tm_s, last)
        s5_s.append((fre, fim))

    def stack(states, j):
        return jnp.stack([s[j] for s in states], axis=0)

    def s5_state(j):
        return jnp.transpose(stack(s5_s, j).reshape(depth, B_GROUPS, B_STATE, nsb), (0, 3, 1, 2))

    def cache(a):
        return jnp.transpose(a.reshape(depth, nsb, C_KV_HEADS, C_HEAD_DIM, lb), (0, 1, 4, 2, 3))

    y_prompt = xp.reshape(nb, seq, D_MODEL)
    y_sample = jnp.swapaxes(xs.reshape(valid, nsb, D_MODEL), 0, 1)
    return (y_prompt, y_sample,
            stack(new_p, 0), stack(new_p, 1), stack(new_p, 2), stack(new_p, 3), stack(new_p, 4), stack(new_p, 5),
            jnp.transpose(ssd_new, (0, 4, 1, 2, 3)), jnp.swapaxes(conv_new, 1, 2), s5_state(0), s5_state(1),
            cache(k_new), cache(v_new))
```

```python
import functools
import math

import numpy as np
import jax
import jax.numpy as jnp
from jax import lax
from jax.experimental import pallas as pl
from jax.experimental.pallas import tpu as pltpu

F32 = jnp.float32
BF16 = jnp.bfloat16
HIGHEST = lax.Precision.HIGHEST

D_MODEL = 1024
A_HEAD_DIM = 64
A_WIDTH = 512
A_HEADS = 8
A_GROUPS = 2
A_STATE = 64
A_CONV = 4
A_CONV_DIM = 768
A_CHUNK = 128
B_CH = 16
B_WIDTH = 512
B_GROUPS = 32
B_STATE = 64
C_HEAD_DIM = 64
C_WIDTH = 512
C_HEADS = 8
C_KV_HEADS = 2
C_GQA = 4
C_KV_WIDTH = 128
WINDOW = 128
REL_BUCKETS = 32
REL_MAX_DIST = 128
D_FF = 2816
EPS = 1e-6

P_Z = 0
P_XBC = P_Z + A_WIDTH
P_U = P_XBC + A_CONV_DIM
P_Q = P_U + B_WIDTH
P_K = P_Q + C_WIDTH
P_V = P_K + C_KV_WIDTH
P_DT = P_V + C_KV_WIDTH
P_COLS = P_DT + 128

LANE = 128
SUBLANE = 8
S5_Q = 8
S5_OCT = 8
N_OCT = B_GROUPS // S5_OCT
S5_STATE_COLS = B_GROUPS * B_STATE
NEG = -1e30
VMEM_LIMIT = 56 * 1024 * 1024
FF_CHUNK = 256


def _cparams(*sem):
    return pltpu.CompilerParams(dimension_semantics=sem, vmem_limit_bytes=VMEM_LIMIT)


def _rms(x, w):
    return x * lax.rsqrt(jnp.mean(x * x, axis=-1, keepdims=True) + EPS) * w


def _dot(a, b):
    return jnp.dot(a, b, preferred_element_type=F32)


def _dot_nt(a, b, precision=None):
    return lax.dot_general(a, b, (((1,), (1,)), ((), ())), preferred_element_type=F32, precision=precision)


def _dot_tn(a, b):
    return lax.dot_general(a, b, (((0,), (0,)), ((), ())), preferred_element_type=F32)


def _full(shape):
    n = len(shape)
    return pl.BlockSpec(shape, lambda *_: (0,) * n)


def _layer_weight(shape, layer):
    return pl.BlockSpec((None,) + shape, lambda *_: (layer,) + (0,) * len(shape), pipeline_mode=pl.Buffered(1))


def _tile_major_spec(rows):
    return pl.BlockSpec((N_OCT, rows, LANE), lambda i: (0, i, 0))


def _lanes(ref):
    return jnp.concatenate([ref[k8] for k8 in range(N_OCT)], axis=1)


def _inproj_body(x_ref, nw_ref, w_ref, z_ref, xbc_ref, u_ref, q_ref, k_ref, v_ref, dt_ref):
    h = _rms(x_ref[...], nw_ref[...]).astype(BF16)

    def proj(lo, hi):
        return _dot_nt(h, w_ref[lo:hi, :])

    z_ref[...] = proj(P_Z, P_XBC).astype(BF16)
    xbc_ref[...] = proj(P_XBC, P_U)
    u = proj(P_U, P_Q)
    for k8 in range(N_OCT):
        u_ref[k8] = u[:, k8 * LANE:(k8 + 1) * LANE]
    q_ref[...] = proj(P_Q, P_K).astype(BF16)
    kvdt = proj(P_K, P_COLS)
    k_ref[...] = kvdt[:, :C_KV_WIDTH]
    v_ref[...] = kvdt[:, C_KV_WIDTH:2 * C_KV_WIDTH]
    dt_ref[...] = kvdt[:, 2 * C_KV_WIDTH:]


def _inproj(x, nw, w, layer, tm):
    t = x.shape[0]
    widths = [(A_WIDTH, BF16), (A_CONV_DIM, F32), None, (C_WIDTH, BF16),
              (C_KV_WIDTH, F32), (C_KV_WIDTH, F32), (LANE, F32)]
    spec = lambda w: _tile_major_spec(tm) if w is None else pl.BlockSpec((tm, w[0]), lambda i: (i, 0))
    shape = lambda w: (jax.ShapeDtypeStruct((N_OCT, t, LANE), F32) if w is None
                       else jax.ShapeDtypeStruct((t, w[0]), w[1]))
    return pl.pallas_call(
        _inproj_body,
        grid=(t // tm,),
        in_specs=[pl.BlockSpec((tm, D_MODEL), lambda i: (i, 0)), _full((1, D_MODEL)),
                  _layer_weight((P_COLS, D_MODEL), layer)],
        out_specs=[spec(w) for w in widths],
        out_shape=[shape(w) for w in widths],
        compiler_params=_cparams("parallel"),
        name="inproj",
    )(x, nw, w)


def _ssd_chunk(xbc, z, dt_raw, cw_ref, cb_ref, alog_ref, dtb_ref, dexp_ref, nw_ref, h_scr, tail_scr):
    q = xbc.shape[0]
    tail_scr[SUBLANE:SUBLANE + q] = xbc
    acc = cb_ref[...] + cw_ref[A_CONV - 1:A_CONV, :] * xbc
    for j in range(1, A_CONV):
        acc = acc + cw_ref[A_CONV - 1 - j:A_CONV - j, :] * tail_scr[SUBLANE - j:SUBLANE - j + q]
    tail_scr[0:SUBLANE] = xbc[q - SUBLANE:q]
    xc = jax.nn.silu(acc)
    x = xc[:, :A_WIDTH]
    bm = xc[:, A_WIDTH:A_WIDTH + A_GROUPS * A_STATE].astype(BF16)
    cm = xc[:, A_WIDTH + A_GROUPS * A_STATE:].astype(BF16)

    dt = jax.nn.softplus(dt_raw + dtb_ref[...])
    dta = dt * (-jnp.exp(alog_ref[...]))
    row = lax.broadcasted_iota(jnp.int32, (q, q), 0)
    col = lax.broadcasted_iota(jnp.int32, (q, q), 1)
    tri = row >= col
    a_cum = jnp.dot(tri.astype(F32), dta, preferred_element_type=F32, precision=HIGHEST)

    def expand(v):
        return jnp.concatenate([jnp.broadcast_to(v[:, k:k + 1], (q, A_HEAD_DIM)) for k in range(A_HEADS)], axis=1)

    a_cum_e = expand(a_cum)
    xs = x * expand(dt)
    xs_b = xs.astype(BF16)
    a_cum_t = a_cum.T

    h = h_scr[...]
    h_b = h.astype(BF16)
    hpg = A_HEADS // A_GROUPS
    gw = hpg * A_HEAD_DIM
    y_diag, y_off = [], []
    for g in range(A_GROUPS):
        cg = cm[:, g * A_STATE:(g + 1) * A_STATE]
        bg = bm[:, g * A_STATE:(g + 1) * A_STATE]
        cb = _dot_nt(cg, bg)
        for kk in range(hpg):
            k = g * hpg + kk
            seg = a_cum[:, k:k + 1] - a_cum_t[k:k + 1, :]
            decay = jnp.exp(jnp.where(tri, seg, -jnp.inf))
            y_diag.append(_dot((cb * decay).astype(BF16), xs_b[:, k * A_HEAD_DIM:(k + 1) * A_HEAD_DIM]))
        y_off.append(_dot_nt(cg, h_b[g * gw:(g + 1) * gw]))
    y = (jnp.concatenate(y_diag, axis=1) + jnp.concatenate(y_off, axis=1) * jnp.exp(a_cum_e)
         + dexp_ref[...] * x)
    y = y * jax.nn.silu(z)

    last = a_cum_e[q - 1:q, :]
    xs_dec = (xs * jnp.exp(last - a_cum_e)).astype(BF16)
    dec_col = jnp.exp(a_cum_e.T[:, q - 1:q])
    upd = [_dot_tn(xs_dec[:, g * gw:(g + 1) * gw], bm[:, g * A_STATE:(g + 1) * A_STATE]) for g in range(A_GROUPS)]
    h_scr[...] = dec_col * h + jnp.concatenate(upd, axis=0)
    return _rms(y, nw_ref[...]).astype(BF16)


def _inproj_ssd_body(x_ref, nw_ref, w_ref, cw_ref, cb_ref, alog_ref, dtb_ref, dexp_ref, snw_ref,
                     u_ref, q_ref, k_ref, v_ref, ya_ref, hout_ref, tail_ref, h_scr, tail_scr, *, nsteps, chunk):
    c = pl.program_id(1)

    @pl.when(c == 0)
    def _():
        h_scr[...] = jnp.zeros_like(h_scr)
        tail_scr[0:SUBLANE] = jnp.zeros((SUBLANE, A_CONV_DIM), F32)

    h = _rms(x_ref[...], nw_ref[...]).astype(BF16)

    def proj(lo, hi):
        return _dot_nt(h, w_ref[lo:hi, :])

    z = proj(P_Z, P_XBC)
    xbc = proj(P_XBC, P_U)
    kvdt = proj(P_K, P_COLS)
    k_ref[...] = kvdt[:, :C_KV_WIDTH]
    v_ref[...] = kvdt[:, C_KV_WIDTH:2 * C_KV_WIDTH]
    dt = kvdt[:, 2 * C_KV_WIDTH:]
    for i in range(x_ref.shape[0] // chunk):
        rows = slice(i * chunk, (i + 1) * chunk)
        ya_ref[rows, :] = _ssd_chunk(xbc[rows], z[rows], dt[rows], cw_ref, cb_ref, alog_ref, dtb_ref, dexp_ref,
                                     snw_ref, h_scr, tail_scr)
    u = proj(P_U, P_Q)
    for k8 in range(N_OCT):
        u_ref[k8] = u[:, k8 * LANE:(k8 + 1) * LANE]
    q_ref[...] = proj(P_Q, P_K).astype(BF16)

    @pl.when(c == nsteps - 1)
    def _():
        hout_ref[0] = h_scr[...]
        tail_ref[0] = tail_scr[0:SUBLANE]


def _inproj_ssd(x, w, lw, nb, tm):
    t = x.shape[0]
    nsteps = t // nb // tm
    rmap = lambda b, c: (b * nsteps + c, 0)
    rows = lambda wd: pl.BlockSpec((tm, wd), rmap)
    per_seq = lambda s: pl.BlockSpec((1,) + s, lambda b, c: (b, 0, 0))
    return pl.pallas_call(
        functools.partial(_inproj_ssd_body, nsteps=nsteps, chunk=A_CHUNK),
        grid=(nb, nsteps),
        in_specs=[rows(D_MODEL), _full((1, D_MODEL)), _layer_weight((P_COLS, D_MODEL), lw["layer"]),
                  _full((A_CONV, A_CONV_DIM)), _full((1, A_CONV_DIM)), _full((1, LANE)), _full((1, LANE)),
                  _full((1, A_WIDTH)), _full((1, A_WIDTH))],
        out_specs=[pl.BlockSpec((N_OCT, tm, LANE), lambda b, c: (0, b * nsteps + c, 0)), rows(C_WIDTH),
                   rows(C_KV_WIDTH), rows(C_KV_WIDTH), rows(A_WIDTH),
                   per_seq((A_WIDTH, A_STATE)), per_seq((SUBLANE, A_CONV_DIM))],
        out_shape=[jax.ShapeDtypeStruct((N_OCT, t, LANE), F32), jax.ShapeDtypeStruct((t, C_WIDTH), BF16),
                   jax.ShapeDtypeStruct((t, C_KV_WIDTH), F32), jax.ShapeDtypeStruct((t, C_KV_WIDTH), F32),
                   jax.ShapeDtypeStruct((t, A_WIDTH), BF16), jax.ShapeDtypeStruct((nb, A_WIDTH, A_STATE), F32),
                   jax.ShapeDtypeStruct((nb, SUBLANE, A_CONV_DIM), F32)],
        scratch_shapes=[pltpu.VMEM((A_WIDTH, A_STATE), F32), pltpu.VMEM((SUBLANE + A_CHUNK, A_CONV_DIM), F32)],
        compiler_params=_cparams("parallel", "arbitrary"),
        name="inproj_ssd",
    )(x, lw["norm1_w"], w, lw["conv_w"], lw["conv_b"], lw["a_log"], lw["dt_bias"], lw["d_exp"], lw["ssd_norm_w"])


def _ssd_sample_body(*refs, valid):
    n_in = 13
    (xbc_ref, z_ref, dt_ref, conv0_ref, h0_ref, cw_ref, cb_ref, alog_ref, dtb_ref, dexp_ref, nwb_ref) = refs[:11]
    y_ref, conv_out_ref, h_out_ref = refs[n_in:n_in + 3]
    xs_scr, dx_scr, z_scr, y_scr, bt_scr, ct_scr, da_scr = refs[n_in + 3:]
    k = pl.program_id(0)
    ns = LANE

    @pl.when(k == 0)
    def _():
        rows = [conv0_ref[0, r] for r in range(A_CONV - 1)] + [xbc_ref[t * ns:(t + 1) * ns, :] for t in range(valid)]
        for t in range(valid):
            acc = cb_ref[...]
            for j in range(A_CONV):
                acc = acc + cw_ref[A_CONV - 1 - j:A_CONV - j, :] * rows[t + A_CONV - 1 - j]
            xc = jax.nn.silu(acc)
            dt = jax.nn.softplus(dt_ref[t * ns:(t + 1) * ns, :] + dtb_ref[...])
            dt_t = dt.T
            da_scr[t] = jnp.exp((dt * (-jnp.exp(alog_ref[...]))).T)
            x = xc[:, :A_WIDTH]
            x_t = x.T
            for kk in range(A_HEADS):
                hs = slice(kk * A_HEAD_DIM, (kk + 1) * A_HEAD_DIM)
                xs_scr[t, hs, :] = x_t[hs] * dt_t[kk:kk + 1, :]
            dx_scr[t] = (x * dexp_ref[...]).T
            bt_scr[t] = xc[:, A_WIDTH:A_WIDTH + A_GROUPS * A_STATE].T
            ct_scr[t] = xc[:, A_WIDTH + A_GROUPS * A_STATE:].T
            z_scr[t] = jax.nn.silu(z_ref[t * ns:(t + 1) * ns, :].astype(F32)).T
        for r in range(A_CONV - 1):
            conv_out_ref[0, r] = rows[valid + r]

    g0 = pl.multiple_of((k // (A_HEADS // A_GROUPS)) * A_STATE, A_STATE)

    def per_p(p, carry):
        h = h0_ref[0, 0, p]
        row = k * A_HEAD_DIM + p
        for t in range(valid):
            da = da_scr[t, pl.ds(k, 1), :]
            xr = xs_scr[t, pl.ds(row, 1), :]
            h = da * h + xr * bt_scr[t, pl.ds(g0, A_STATE), :]
            y_scr[t, pl.ds(row, 1), :] = jnp.sum(h * ct_scr[t, pl.ds(g0, A_STATE), :], axis=0, keepdims=True)
        h_out_ref[0, 0, p] = h
        return carry

    lax.fori_loop(0, A_HEAD_DIM, per_p, 0, unroll=2)

    @pl.when(k == A_HEADS - 1)
    def _():
        for t in range(valid):
            y = (y_scr[t] + dx_scr[t]) * z_scr[t]
            y = y * lax.rsqrt(jnp.mean(y * y, axis=0, keepdims=True) + EPS) * nwb_ref[...]
            y_ref[:, t * ns:(t + 1) * ns] = y.astype(BF16)


def _ssd_sample(xbc, z, dt, conv_all, h_all, lw, layer, valid, conv_prev, h_prev):
    rows = valid * LANE
    hblk = pl.BlockSpec((1, 1, A_HEAD_DIM, A_STATE, LANE), lambda k: (layer, k, 0, 0, 0))
    cblk = pl.BlockSpec((1, A_CONV - 1, LANE, A_CONV_DIM), lambda k: (layer, 0, 0, 0))
    in_specs = [_full((rows, A_CONV_DIM)), _full((rows, A_WIDTH)), _full((rows, LANE)), cblk, hblk,
                _full((A_CONV, A_CONV_DIM)), _full((1, A_CONV_DIM)), _full((1, LANE)), _full((1, LANE)),
                _full((1, A_WIDTH)), _full((A_WIDTH, LANE))] + [pl.BlockSpec(memory_space=pl.ANY)] * 2
    args = [xbc, z, dt, conv_all, h_all, lw["conv_w"], lw["conv_b"], lw["a_log"], lw["dt_bias"], lw["d_exp"],
            lw["ssd_norm_wb"], conv_prev, h_prev]
    aliases = {len(args) - 2: 1, len(args) - 1: 2}
    big = lambda n: pltpu.VMEM((valid, n, LANE), F32)
    return pl.pallas_call(
        functools.partial(_ssd_sample_body, valid=valid),
        grid=(A_HEADS,),
        in_specs=in_specs,
        out_specs=[_full((A_WIDTH, rows)), cblk, hblk],
        out_shape=[jax.ShapeDtypeStruct((A_WIDTH, rows), BF16), jax.ShapeDtypeStruct(conv_all.shape, F32),
                   jax.ShapeDtypeStruct(h_all.shape, F32)],
        scratch_shapes=[big(A_WIDTH), big(A_WIDTH), big(A_WIDTH), big(A_WIDTH), big(LANE), big(LANE), big(LANE)],
        input_output_aliases=aliases,
        compiler_params=_cparams("arbitrary"),
        name="ssd_sample",
    )(*args)


def _s5w_body(lre_ref, lim_ref, ldt_ref, btr_ref, bti_ref, cr_ref, ci_ref,
              bre_ref, bim_ref, bsre_ref, bsim_ref, cre_ref, cim_ref, toep_ref, ap_ref, cd_scr, bb_scr, *, valid):
    h, q = B_STATE, S5_Q
    for ref in (bre_ref, bim_ref, bsre_ref, bsim_ref, cre_ref, cim_ref, toep_ref):
        ref[...] = jnp.zeros(ref.shape, BF16)
    r_i = lax.broadcasted_iota(jnp.int32, (q * B_CH, q * LANE), 0)
    c_i = lax.broadcasted_iota(jnp.int32, (q * B_CH, q * LANE), 1)
    d = lax.broadcasted_iota(jnp.int32, (2 * SUBLANE, h), 0).astype(F32)
    for g8 in range(S5_OCT):
        lr, li = lre_ref[g8], lim_ref[g8]
        step = jnp.exp(ldt_ref[g8])
        er = jnp.exp(lr * step * d)
        ang = li * step * d
        ar, ai = er * jnp.cos(ang), er * jnp.sin(ang)
        nr, ni = ar[1:2] - 1.0, ai[1:2]
        den = lr * lr + li * li
        wr, wi = (nr * lr + ni * li) / den, (ni * lr - nr * li) / den
        btr, bti = btr_ref[g8], bti_ref[g8]
        bbr, bbi = wr * btr - wi * bti, wr * bti + wi * btr
        cr, ci = cr_ref[g8], ci_ref[g8]
        bb_scr[:, 0:h] = bbr
        bb_scr[:, h:2 * h] = bbi
        cols = slice(g8 * h, (g8 + 1) * h)
        for s in range(q):
            rows = slice(s * LANE + g8 * B_CH, s * LANE + (g8 + 1) * B_CH)
            e = q - 1 - s
            bre_ref[0, rows, cols] = (ar[e:e + 1] * bbr - ai[e:e + 1] * bbi).astype(BF16)
            bim_ref[0, rows, cols] = (ai[e:e + 1] * bbr + ar[e:e + 1] * bbi).astype(BF16)
            if s < valid:
                e = valid - 1 - s
                bsre_ref[0, rows, cols] = (ar[e:e + 1] * bbr - ai[e:e + 1] * bbi).astype(BF16)
                bsim_ref[0, rows, cols] = (ai[e:e + 1] * bbr + ar[e:e + 1] * bbi).astype(BF16)
            cre_ref[0, rows, cols] = (cr * ar[s + 1:s + 2] - ci * ai[s + 1:s + 2]).astype(BF16)
            cim_ref[0, rows, cols] = (-(cr * ai[s + 1:s + 2] + ci * ar[s + 1:s + 2])).astype(BF16)
            cd_scr[s * B_CH:(s + 1) * B_CH, 0:h] = cr * ar[s:s + 1] - ci * ai[s:s + 1]
            cd_scr[s * B_CH:(s + 1) * B_CH, h:2 * h] = -(cr * ai[s:s + 1] + ci * ar[s:s + 1])
        kdt = _dot_nt(bb_scr[...], cd_scr[...], precision=HIGHEST)
        place = (c_i == (r_i >> 4) * LANE + g8 * B_CH + (r_i & (B_CH - 1))).astype(F32)
        slab = jnp.dot(kdt, place, preferred_element_type=F32, precision=HIGHEST).astype(BF16)
        for s in range(q):
            toep_ref[0, s * LANE + g8 * B_CH:s * LANE + (g8 + 1) * B_CH, s * LANE:] = slab[:, 0:(q - s) * LANE]
        ap_ref[g8, 0:1, 0:h] = ar[q:q + 1]
        ap_ref[g8, 0:1, h:2 * h] = ai[q:q + 1]
        ap_ref[g8, 1:2, 0:h] = ar[valid:valid + 1]
        ap_ref[g8, 1:2, h:2 * h] = ai[valid:valid + 1]


def _s5_weights(lam_re, lam_im, log_dt, b_re, b_im, c_re, c_im, valid):
    g, h, q = B_GROUPS, B_STATE, S5_Q
    vec = pl.BlockSpec((S5_OCT, 1, h), lambda i: (i, 0, 0))
    mat = pl.BlockSpec((S5_OCT, B_CH, h), lambda i: (i, 0, 0))
    st = pl.BlockSpec((1, q * LANE, S5_OCT * h), lambda i: (i, 0, 0))
    st_shape = jax.ShapeDtypeStruct((N_OCT, q * LANE, S5_OCT * h), BF16)
    bre, bim, bsre, bsim, cre, cim, toep, ap = pl.pallas_call(
        functools.partial(_s5w_body, valid=valid),
        grid=(N_OCT,),
        in_specs=[vec, vec, vec, mat, mat, mat, mat],
        out_specs=[st] * 6 + [pl.BlockSpec((1, q * LANE, q * LANE), lambda i: (i, 0, 0)),
                              pl.BlockSpec((S5_OCT, 2, 2 * h), lambda i: (i, 0, 0))],
        out_shape=[st_shape] * 6 + [jax.ShapeDtypeStruct((N_OCT, q * LANE, q * LANE), BF16),
                                    jax.ShapeDtypeStruct((g, 2, 2 * h), F32)],
        scratch_shapes=[pltpu.VMEM((q * B_CH, 2 * h), F32), pltpu.VMEM((B_CH, 2 * h), F32)],
        compiler_params=_cparams("parallel"),
        name="s5_weights",
    )(lam_re.reshape(g, 1, h), lam_im.reshape(g, 1, h), jnp.broadcast_to(log_dt.reshape(g, 1, 1), (g, 1, h)),
      jnp.swapaxes(b_re, 1, 2), jnp.swapaxes(b_im, 1, 2), c_re, c_im)
    return dict(
        bst_re=bre, bst_im=bim, bsts_re=bsre, bsts_im=bsim, cst_re=cre, cst_im=cim, toep=toep,
        aq_re=ap[:, 0, :h].reshape(1, S5_STATE_COLS), aq_im=ap[:, 0, h:].reshape(1, S5_STATE_COLS),
        av_re=ap[:, 1, :h].reshape(1, S5_STATE_COLS), av_im=ap[:, 1, h:].reshape(1, S5_STATE_COLS),
    )


def _oct_rows(u_ref, k8, rt):
    return jnp.concatenate([u_ref[k8, pl.ds(j, rt, stride=S5_Q), :] for j in range(S5_Q)], axis=1).astype(BF16)


def _toep_dot(uo, toep_ref, k8):
    two = 2 * LANE
    cols = [_dot(uo[:, :(t + 1) * two], toep_ref[k8, :(t + 1) * two, t * two:(t + 1) * two])
            for t in range(uo.shape[1] // two)]
    return jnp.concatenate(cols, axis=1)


def _s5_state_body(u_ref, bre_ref, bim_ref, sre_ref, sim_ref):
    w = S5_OCT * B_STATE
    for k8 in range(N_OCT):
        uo = _oct_rows(u_ref, k8, sre_ref.shape[0])
        sre_ref[:, k8 * w:(k8 + 1) * w] = _dot(uo, bre_ref[k8])
        sim_ref[:, k8 * w:(k8 + 1) * w] = _dot(uo, bim_ref[k8])


def _s5_state(u, bre, bim, rt):
    r = u.shape[1] // S5_Q
    wspec = _full((N_OCT, S5_Q * LANE, S5_OCT * B_STATE))
    ospec = pl.BlockSpec((rt, S5_STATE_COLS), lambda i: (i, 0))
    return pl.pallas_call(
        _s5_state_body,
        grid=(r // rt,),
        in_specs=[_tile_major_spec(rt * S5_Q), wspec, wspec],
        out_specs=[ospec, ospec],
        out_shape=[jax.ShapeDtypeStruct((r, S5_STATE_COLS), F32)] * 2,
        compiler_params=_cparams("parallel"),
        name="s5_state",
    )(u, bre, bim)


def _s5_scan_body(sre_ref, sim_ref, ar_ref, ai_ref, hre_ref, him_ref, fre_ref, fim_ref, cr_scr, ci_scr, *, rb, nblk):
    i = pl.program_id(0)
    nb = sre_ref.shape[0]

    @pl.when(i == 0)
    def _():
        cr_scr[...] = jnp.zeros_like(cr_scr)
        ci_scr[...] = jnp.zeros_like(ci_scr)

    ar, ai = ar_ref[...], ai_ref[...]

    def step(r, carry):
        out = []
        for b in range(nb):
            hr, hi = carry[2 * b], carry[2 * b + 1]
            hre_ref[b, pl.ds(r, 1), :] = hr
            him_ref[b, pl.ds(r, 1), :] = hi
            sr, si = sre_ref[b, pl.ds(r, 1), :], sim_ref[b, pl.ds(r, 1), :]
            out += [ar * hr - ai * hi + sr, ai * hr + ar * hi + si]
        return tuple(out)

    init = tuple(s[b] for b in range(nb) for s in (cr_scr, ci_scr))
    fin = lax.fori_loop(0, rb, step, init)
    for b in range(nb):
        cr_scr[b] = fin[2 * b]
        ci_scr[b] = fin[2 * b + 1]

    @pl.when(i == nblk - 1)
    def _():
        for b in range(nb):
            fre_ref[b] = fin[2 * b]
            fim_ref[b] = fin[2 * b + 1]


def _s5_scan(sre, sim, ar, ai, nb, rb):
    r = sre.shape[0] // nb
    nblk = r // rb
    blk = pl.BlockSpec((nb, rb, S5_STATE_COLS), lambda i: (0, i, 0))
    vec = _full((1, S5_STATE_COLS))
    fin = _full((nb, 1, S5_STATE_COLS))
    per_batch = lambda a: a.reshape(nb, r, S5_STATE_COLS)
    hre, him, fre, fim = pl.pallas_call(
        functools.partial(_s5_scan_body, rb=rb, nblk=nblk),
        grid=(nblk,),
        in_specs=[blk, blk, vec, vec],
        out_specs=[blk, blk, fin, fin],
        out_shape=[jax.ShapeDtypeStruct((nb, r, S5_STATE_COLS), F32)] * 2
        + [jax.ShapeDtypeStruct((nb, 1, S5_STATE_COLS), F32)] * 2,
        scratch_shapes=[pltpu.VMEM((nb, 1, S5_STATE_COLS), F32)] * 2,
        compiler_params=_cparams("arbitrary"),
        name="s5_scan",
    )(per_batch(sre), per_batch(sim), ar, ai)
    return hre.reshape(nb * r, S5_STATE_COLS), him.reshape(nb * r, S5_STATE_COLS), fre, fim


def _s5_sample_body(u_ref, toep_ref, cre_ref, cim_ref, bre_ref, bim_ref, hre_ref, him_ref, ar_ref, ai_ref,
                    y_ref, fre_ref, fim_ref, *, valid):
    w = S5_OCT * B_STATE
    ns = LANE
    for k8 in range(N_OCT):
        blocks = [u_ref[k8, t * ns:(t + 1) * ns, :] for t in range(valid)]
        uo = jnp.concatenate(blocks, axis=1).astype(BF16)
        uo_t = jnp.concatenate([b.T for b in blocks], axis=0).astype(BF16)
        hr, hi = hre_ref[0, k8 * w:(k8 + 1) * w, :], him_ref[0, k8 * w:(k8 + 1) * w, :]
        yo = (_toep_dot(uo, toep_ref, k8) + _dot_nt(hr.T.astype(BF16), cre_ref[k8])
              + _dot_nt(hi.T.astype(BF16), cim_ref[k8]))
        for t in range(valid):
            y_ref[k8, t * ns:(t + 1) * ns, :] = yo[:, t * LANE:(t + 1) * LANE]
        ar, ai = ar_ref[k8 * w:(k8 + 1) * w, :], ai_ref[k8 * w:(k8 + 1) * w, :]
        fre_ref[k8 * w:(k8 + 1) * w, :] = ar * hr - ai * hi + _dot_tn(bre_ref[k8], uo_t)
        fim_ref[k8 * w:(k8 + 1) * w, :] = ai * hr + ar * hi + _dot_tn(bim_ref[k8], uo_t)


def _s5_sample(u, s5w, hre_all, him_all, ar, ai, layer, valid):
    rows = valid * LANE
    vq = valid * LANE
    sub = lambda c: pl.BlockSpec((N_OCT, vq, c), lambda i: (0, 0, 0))
    hblk = pl.BlockSpec((1, S5_STATE_COLS, LANE), lambda i: (layer, 0, 0))
    st = _full((S5_STATE_COLS, LANE))
    return pl.pallas_call(
        functools.partial(_s5_sample_body, valid=valid),
        grid=(1,),
        in_specs=[_tile_major_spec(rows), sub(vq), sub(S5_OCT * B_STATE), sub(S5_OCT * B_STATE),
                  sub(S5_OCT * B_STATE), sub(S5_OCT * B_STATE), hblk, hblk, st, st],
        out_specs=[_tile_major_spec(rows), st, st],
        out_shape=[jax.ShapeDtypeStruct((N_OCT, rows, LANE), F32)] + [jax.ShapeDtypeStruct((S5_STATE_COLS, LANE), F32)] * 2,
        compiler_params=_cparams("arbitrary"),
        name="s5_sample",
    )(u, s5w["toep"], s5w["cst_re"], s5w["cst_im"], s5w["bsts_re"], s5w["bsts_im"], hre_all, him_all, ar, ai)


def _s5_out_body(u_ref, hre_ref, him_ref, toep_ref, cre_ref, cim_ref, y_ref):
    w = S5_OCT * B_STATE
    rt = hre_ref.shape[0]
    for k8 in range(N_OCT):
        uo = _oct_rows(u_ref, k8, rt)
        yo = (_toep_dot(uo, toep_ref, k8)
              + _dot_nt(hre_ref[:, k8 * w:(k8 + 1) * w].astype(BF16), cre_ref[k8])
              + _dot_nt(him_ref[:, k8 * w:(k8 + 1) * w].astype(BF16), cim_ref[k8]))
        for j in range(S5_Q):
            y_ref[k8, pl.ds(j, rt, stride=S5_Q), :] = yo[:, j * LANE:(j + 1) * LANE]


def _s5_out(u, hre, him, toep, cre, cim, rt):
    r = u.shape[1] // S5_Q
    hblk = pl.BlockSpec((rt, S5_STATE_COLS), lambda i: (i, 0))
    return pl.pallas_call(
        _s5_out_body,
        grid=(r // rt,),
        in_specs=[_tile_major_spec(rt * S5_Q), hblk, hblk, _full((N_OCT, S5_Q * LANE, S5_Q * LANE)),
                  _full((N_OCT, S5_Q * LANE, S5_OCT * B_STATE)), _full((N_OCT, S5_Q * LANE, S5_OCT * B_STATE))],
        out_specs=_tile_major_spec(rt * S5_Q),
        out_shape=jax.ShapeDtypeStruct(u.shape, F32),
        compiler_params=_cparams("parallel"),
        name="s5_out",
    )(u, hre, him, toep, cre, cim)


def _band_body(bucket_ref, table_ref, o_ref):
    bucket = bucket_ref[...]
    col = lax.broadcasted_iota(jnp.int32, (WINDOW, 2 * WINDOW), 1)
    for hq in range(C_HEADS):
        base = jnp.full((1, 2 * WINDOW), NEG, F32)
        for b in range(REL_BUCKETS):
            base = jnp.where(bucket == b, table_ref[b, hq], base)
        tile = pltpu.roll(jnp.broadcast_to(base, (WINDOW, 2 * WINDOW)), 0, 1, stride=1, stride_axis=0)
        o_ref[1, hq] = tile.T
        o_ref[0, hq] = jnp.where(col >= WINDOW, tile, NEG).T


def _bias_band(table):
    lane = np.arange(2 * WINDOW)
    n = WINDOW - lane
    exact = REL_BUCKETS // 2
    nf = np.maximum(n, 1).astype(np.float32)
    large = exact + (np.log(nf / np.float32(exact)) / np.float32(math.log(REL_MAX_DIST / exact))
                     * np.float32(REL_BUCKETS - exact)).astype(np.int32)
    bucket = np.where(n < exact, n, np.minimum(large, REL_BUCKETS - 1))
    bucket = np.where((n >= 0) & (n < WINDOW), bucket, -1).astype(np.int32)
    return pl.pallas_call(
        _band_body,
        in_specs=[pl.BlockSpec(memory_space=pltpu.VMEM), pl.BlockSpec(memory_space=pltpu.SMEM)],
        out_specs=pl.BlockSpec(memory_space=pltpu.VMEM),
        out_shape=jax.ShapeDtypeStruct((2, C_HEADS, 2 * WINDOW, WINDOW), F32),
        name="rel_bias_band",
    )(jnp.asarray(bucket).reshape(1, 2 * WINDOW), table)


def _attn_prompt_body(q_ref, kc_ref, kp_ref, vc_ref, vp_ref, b0_ref, br_ref, sink_ref, o_ref, s_scr, p_scr, *, nblk):
    k_all = jnp.concatenate([kp_ref[...], kc_ref[...]], axis=0) * (C_HEAD_DIM ** -0.5)
    v_all = jnp.concatenate([vp_ref[...], vc_ref[...]], axis=0)
    swapped = (pltpu.roll(k_all, C_HEAD_DIM, 1), pltpu.roll(v_all, C_HEAD_DIM, 1))
    low = lax.broadcasted_iota(jnp.int32, k_all.shape, 1) < C_HEAD_DIM
    kz, vz = [], []
    for hk in range(C_KV_HEADS):
        k_lo = k_all if hk == 0 else swapped[0]
        k_hi = swapped[0] if hk == 0 else k_all
        kz.append((jnp.where(low, k_lo, 0.0).astype(BF16), jnp.where(low, 0.0, k_hi).astype(BF16)))
        vz.append((v_all if hk == 0 else swapped[1])[:, :C_HEAD_DIM].astype(BF16))
    for i in range(nblk):
        keys = slice(i * WINDOW, (i + 2) * WINDOW)
        slot = i % 2
        for hq in range(C_HEADS):
            qt = q_ref[i * WINDOW:(i + 1) * WINDOW, (hq // 2) * LANE:(hq // 2 + 1) * LANE]
            s_scr[slot, hq] = _dot_nt(kz[hq // C_GQA][hq % 2][keys], qt)
        inv = []
        for hq in range(C_HEADS):
            sk = sink_ref[hq]
            s = s_scr[slot, hq] + (b0_ref[0, hq] if i == 0 else br_ref[0, hq])
            m = jnp.maximum(jnp.max(s, axis=0, keepdims=True), sk)
            p = jnp.exp(s - m)
            inv.append(1.0 / (jnp.sum(p, axis=0, keepdims=True) + jnp.exp(sk - m)))
            p_scr[slot, hq] = p.astype(BF16)
        for hq in range(C_HEADS):
            o = _dot_tn(vz[hq // C_GQA][keys], p_scr[slot, hq]) * inv[hq]
            o_ref[hq * C_HEAD_DIM:(hq + 1) * C_HEAD_DIM, i * WINDOW:(i + 1) * WINDOW] = o.astype(BF16)


def _attn_prompt(q, k, v, band, sinks, nb, nblk_seq, nblk):
    steps = nblk_seq // nblk
    cur = lambda b, n: (b * steps + n, 0)
    prev = lambda b, n: (b * nblk_seq + jnp.maximum(n * nblk - 1, 0), 0)
    bspec = lambda m: pl.BlockSpec((1, C_HEADS, 2 * WINDOW, WINDOW), m)
    return pl.pallas_call(
        functools.partial(_attn_prompt_body, nblk=nblk),
        grid=(nb, steps),
        in_specs=[pl.BlockSpec((nblk * WINDOW, C_WIDTH), cur),
                  pl.BlockSpec((nblk * WINDOW, C_KV_WIDTH), cur), pl.BlockSpec((WINDOW, C_KV_WIDTH), prev),
                  pl.BlockSpec((nblk * WINDOW, C_KV_WIDTH), cur), pl.BlockSpec((WINDOW, C_KV_WIDTH), prev),
                  bspec(lambda b, n: (jnp.minimum(n, 1), 0, 0, 0)), bspec(lambda b, n: (1, 0, 0, 0)),
                  pl.BlockSpec(memory_space=pltpu.SMEM)],
        out_specs=pl.BlockSpec((C_WIDTH, nblk * WINDOW), lambda b, n: (0, b * steps + n)),
        out_shape=jax.ShapeDtypeStruct((C_WIDTH, nb * nblk_seq * WINDOW), BF16),
        scratch_shapes=[pltpu.VMEM((2, C_HEADS, 2 * WINDOW, WINDOW), F32),
                        pltpu.VMEM((2, C_HEADS, 2 * WINDOW, WINDOW), BF16)],
        compiler_params=_cparams("parallel", "arbitrary"),
        name="attn_prompt",
    )(q, k, k, v, v, band, band, sinks)


def _attn_sample_body(*refs, valid, lb, nseq):
    q_ref, kn_ref, vn_ref, kc_ref, vc_ref, b1_ref, b2_ref, sink_ref = refs[:8]
    o_ref, ko_ref, vo_ref, s1_scr, s2_scr, inv_scr, o_scr = refs[-7:]
    d = C_HEAD_DIM
    rs = 2 * valid
    lane_c = lax.broadcasted_iota(jnp.int32, (C_KV_WIDTH, lb), 1)
    lane_n = lax.broadcasted_iota(jnp.int32, (SUBLANE, C_KV_WIDTH), 1)
    pad = jnp.zeros((SUBLANE - valid, C_KV_WIDTH), F32)
    zero_half = jnp.zeros((d, lb), BF16)
    combos = [(hk, par) for hk in range(C_KV_HEADS) for par in range(2)]

    def new_rows(ref, s):
        return jnp.concatenate([ref[:, s, :], pad], axis=0)

    for s in range(nseq):
        qs = q_ref[:, s, :].astype(F32)
        kn, vn = new_rows(kn_ref, s), new_rows(vn_ref, s)
        kt, vt = kc_ref[0, s], vc_ref[0, s]
        kn_t, vn_t = kn.T, vn.T
        new_k, new_v = pltpu.roll(kt, lb - valid, 1), pltpu.roll(vt, lb - valid, 1)
        for t in range(valid):
            at = lane_c == lb - valid + t
            new_k = jnp.where(at, jnp.broadcast_to(kn_t[:, t:t + 1], (C_KV_WIDTH, lb)), new_k)
            new_v = jnp.where(at, jnp.broadcast_to(vn_t[:, t:t + 1], (C_KV_WIDTH, lb)), new_v)
        ko_ref[0, s] = new_k
        vo_ref[0, s] = new_v
        kn_sw = pltpu.roll(kn, d, 1)
        for c, (hk, par) in enumerate(combos):
            kth = kt[hk * d:(hk + 1) * d].astype(BF16)
            kz = jnp.concatenate([kth, zero_half] if par == 0 else [zero_half, kth], axis=0)
            src = kn if hk == par else kn_sw
            knz = jnp.where((lane_n < d) if par == 0 else (lane_n >= d), src, 0.0).astype(BF16)
            qq = jnp.concatenate([qs[:, (hk * 2 + j) * LANE:(hk * 2 + j + 1) * LANE] for j in range(2)],
                                 axis=0).astype(BF16)
            s1_scr[c, s * rs:(s + 1) * rs, :] = _dot(qq, kz)
            s2_scr[c, s * rs:(s + 1) * rs, :] = _dot_nt(qq, knz)

    for c, (hk, par) in enumerate(combos):
        s1 = s1_scr[c] * (d ** -0.5) + b1_ref[hk, par]
        s2 = s2_scr[c] * (d ** -0.5) + b2_ref[hk, par]
        sk = sink_ref[hk, par]
        m = jnp.maximum(jnp.maximum(jnp.max(s1, axis=-1, keepdims=True), jnp.max(s2, axis=-1, keepdims=True)), sk)
        p1, p2 = jnp.exp(s1 - m), jnp.exp(s2 - m)
        den = jnp.sum(p1, axis=-1, keepdims=True) + jnp.sum(p2, axis=-1, keepdims=True) + jnp.exp(sk - m)
        s1_scr[c] = p1
        s2_scr[c] = p2
        inv_scr[c] = jnp.broadcast_to(1.0 / den, (nseq * rs, SUBLANE))

    for s in range(nseq):
        rows = slice(s * rs, (s + 1) * rs)
        vt = vc_ref[0, s]
        vn = new_rows(vn_ref, s)
        for c, (hk, par) in enumerate(combos):
            vth = vt[hk * d:(hk + 1) * d].astype(BF16)
            vnh = vn[:, hk * d:(hk + 1) * d].astype(BF16)
            o = _dot_nt(s1_scr[c, rows, :].astype(BF16), vth) + _dot(s2_scr[c, rows, :].astype(BF16), vnh)
            o_scr[hk, rows, par * d:(par + 1) * d] = o * inv_scr[c, rows, 0:1]

    for hk in range(C_KV_HEADS):
        for j in range(2):
            for t in range(valid):
                tile = hk * 2 + j
                o_ref[t, :, tile * LANE:(tile + 1) * LANE] = o_scr[hk, pl.ds(j * valid + t, nseq, stride=rs), :]


def _attn_sample(q3, kn3, vn3, kc_all, vc_all, b1, b2, sink_col, layer, k_prev, v_prev):
    valid, nsb, _ = q3.shape
    lb = kc_all.shape[-1]
    nseq = 2 * SUBLANE
    assert 2 * valid == SUBLANE
    rows = lambda w: pl.BlockSpec((valid, nseq, w), lambda i: (0, i, 0))
    cblk = pl.BlockSpec((1, nseq, C_KV_WIDTH, lb), lambda i: (layer, i, 0, 0))
    b1, b2, sink_col = (jnp.tile(a, (1, 1, nseq, 1)) for a in (b1, b2, sink_col))
    in_specs = [rows(C_WIDTH), rows(C_KV_WIDTH), rows(C_KV_WIDTH), cblk, cblk,
                _full(b1.shape), _full(b2.shape), _full(sink_col.shape)] + [pl.BlockSpec(memory_space=pl.ANY)] * 2
    args = [q3, kn3, vn3, kc_all, vc_all, b1, b2, sink_col, k_prev, v_prev]
    aliases = {len(args) - 2: 1, len(args) - 1: 2}
    return pl.pallas_call(
        functools.partial(_attn_sample_body, valid=valid, lb=lb, nseq=nseq),
        grid=(nsb // nseq,),
        in_specs=in_specs,
        out_specs=[rows(C_WIDTH), cblk, cblk],
        out_shape=[jax.ShapeDtypeStruct((valid, nsb, C_WIDTH), F32), jax.ShapeDtypeStruct(kc_all.shape, F32),
                   jax.ShapeDtypeStruct(vc_all.shape, F32)],
        scratch_shapes=[pltpu.VMEM((2 * C_KV_HEADS, nseq * SUBLANE, lb), F32),
                        pltpu.VMEM((2 * C_KV_HEADS, nseq * SUBLANE, SUBLANE), F32),
                        pltpu.VMEM((2 * C_KV_HEADS, nseq * SUBLANE, SUBLANE), F32),
                        pltpu.VMEM((C_KV_HEADS, nseq * SUBLANE, LANE), F32)],
        input_output_aliases=aliases,
        compiler_params=_cparams("arbitrary"),
        name="attn_sample",
    )(*args)


def _merge_body(x_ref, ya_ref, yb_ref, u_ref, yc_ref, nw_ref, wg_ref, d5_ref, wglu_ref,
                wa_ref, wb_ref, wc_ref, wo_ref, o_ref, *, ya_features_major, yc_features_major):
    x = x_ref[...]
    h = _rms(x, nw_ref[...]).astype(BF16)
    g = jax.nn.gelu(_lanes(yb_ref) + d5_ref[...] * _lanes(u_ref))
    gv = _dot(g.astype(BF16), wglu_ref[...])
    yb = (gv[:, :B_WIDTH] * jax.nn.sigmoid(gv[:, B_WIDTH:])).astype(BF16)

    def gate(i):
        return jax.nn.sigmoid(_dot_nt(h, wg_ref[i * D_MODEL:(i + 1) * D_MODEL, :]))

    def branch(y_ref, w_ref, features_major):
        y = y_ref[...].astype(BF16)
        return _dot_tn(y, w_ref[...]) if features_major else _dot(y, w_ref[...])

    mixed = gate(0) * branch(ya_ref, wa_ref, ya_features_major)
    mixed = mixed + gate(1) * _dot(yb, wb_ref[...])
    mixed = mixed + gate(2) * branch(yc_ref, wc_ref, yc_features_major)
    o_ref[...] = x + _dot(mixed.astype(BF16), wo_ref[...])


def _merge(x, ya, yb, u, yc, lw, tm, ya_features_major, yc_features_major):
    t = x.shape[0]
    rows = lambda w: pl.BlockSpec((tm, w), lambda i: (i, 0))
    branch_spec = lambda fm: pl.BlockSpec((A_WIDTH, tm), lambda i: (0, i)) if fm else rows(A_WIDTH)
    lwt = lambda shape: _layer_weight(shape, lw["layer"])
    return pl.pallas_call(
        functools.partial(_merge_body, ya_features_major=ya_features_major, yc_features_major=yc_features_major),
        grid=(t // tm,),
        in_specs=[rows(D_MODEL), branch_spec(ya_features_major), _tile_major_spec(tm), _tile_major_spec(tm),
                  branch_spec(yc_features_major),
                  _full((1, D_MODEL)), lwt((3 * D_MODEL, D_MODEL)), _full((1, B_WIDTH)),
                  lwt((B_WIDTH, 2 * B_WIDTH)), lwt((A_WIDTH, D_MODEL)), lwt((B_WIDTH, D_MODEL)),
                  lwt((C_WIDTH, D_MODEL)), lwt((D_MODEL, D_MODEL))],
        out_specs=rows(D_MODEL),
        out_shape=jax.ShapeDtypeStruct((t, D_MODEL), F32),
        compiler_params=_cparams("parallel"),
        name="merge",
    )(x, ya, yb, u, yc, lw["norm1_w"], lw["w_gate"], lw["s5_d"], lw["w_glu"],
      lw["w_br_a"], lw["w_br_b"], lw["w_br_c"], lw["w_out"])


def _ffn_body(x_ref, nw_ref, wup_ref, wdn_ref, fnw_ref, o_ref, act_scr, *, final_norm):
    x = x_ref[...]
    h = _rms(x, nw_ref[...]).astype(BF16)
    for c in range(D_FF // FF_CHUNK):
        lo, hi = c * FF_CHUNK, (c + 1) * FF_CHUNK
        a = _dot(h, wup_ref[:, lo:hi])
        b = _dot(h, wup_ref[:, D_FF + lo:D_FF + hi])
        act_scr[:, lo:hi] = (jax.nn.silu(a) * b).astype(BF16)
    y = x + _dot(act_scr[...], wdn_ref[...])
    if final_norm:
        y = _rms(y, fnw_ref[...])
    o_ref[...] = y


def _ffn(x, lw, fnw, tm, final_norm):
    t = x.shape[0]
    rows = pl.BlockSpec((tm, D_MODEL), lambda i: (i, 0))
    return pl.pallas_call(
        functools.partial(_ffn_body, final_norm=final_norm),
        grid=(t // tm,),
        in_specs=[rows, _full((1, D_MODEL)), _layer_weight((D_MODEL, 2 * D_FF), lw["layer"]),
                  _layer_weight((D_FF, D_MODEL), lw["layer"]), _full((1, D_MODEL))],
        out_specs=rows,
        out_shape=jax.ShapeDtypeStruct((t, D_MODEL), F32),
        scratch_shapes=[pltpu.VMEM((tm, D_FF), BF16)],
        compiler_params=_cparams("parallel"),
        name="ffn",
    )(x, lw["norm2_w"], lw["w_ffn_up"], lw["w_ffn_down"], fnw)


def _row_tile(t, pref):
    tm = pref
    while t % tm:
        tm //= 2
    return tm


def kernel(x_prompt, x_sample, state_ssd, state_conv, state_s5_re, state_s5_im, cache_k, cache_v, norm1_w, w_in, conv_w, conv_b, ssd_a_log, ssd_dt_bias, ssd_d, ssd_norm_w, s5_lam_re, s5_lam_im, s5_log_dt, s5_b_re, s5_b_im, s5_c_re, s5_c_im, s5_d, s5_w_glu, attn_sinks, w_br_a, w_br_b, w_br_c, w_out, norm2_w, w_ffn_up, w_ffn_down, rel_bias, final_norm_w):
    nb, seq, _ = x_prompt.shape
    nsb, valid, _ = x_sample.shape
    depth = w_in.shape[0]
    lb = cache_k.shape[2]
    assert seq % A_CHUNK == 0 and seq % WINDOW == 0 and seq % (S5_Q * SUBLANE) == 0
    assert nsb == LANE and A_CONV - 1 <= valid <= S5_Q and lb == WINDOW
    nchunk = seq // A_CHUNK
    tp = nb * seq
    ts = valid * nsb

    band = _bias_band(rel_bias)

    def pair_rows(a):
        a = a.reshape(C_KV_HEADS, 2, 2, valid, a.shape[-1])
        return jnp.swapaxes(a, 1, 2).reshape(C_KV_HEADS, 2, 2 * valid, a.shape[-1])

    band_s = pair_rows(jnp.swapaxes(band[1, :, :, :valid], 1, 2))
    bias_s1 = band_s[..., :lb]
    bias_s2 = band_s[..., lb:lb + SUBLANE]
    fnw = final_norm_w.reshape(1, D_MODEL)

    xp = x_prompt.reshape(tp, D_MODEL)
    xs = jnp.swapaxes(x_sample, 0, 1).reshape(ts, D_MODEL)
    conv_all = jnp.swapaxes(state_conv, 1, 2)
    ssd_all = jnp.transpose(state_ssd, (0, 2, 3, 4, 1))
    s5re_all = jnp.transpose(state_s5_re, (0, 2, 3, 1)).reshape(depth, S5_STATE_COLS, nsb)
    s5im_all = jnp.transpose(state_s5_im, (0, 2, 3, 1)).reshape(depth, S5_STATE_COLS, nsb)
    kc_all = jnp.transpose(cache_k, (0, 1, 3, 4, 2)).reshape(depth, nsb, C_KV_WIDTH, lb)
    vc_all = jnp.transpose(cache_v, (0, 1, 3, 4, 2)).reshape(depth, nsb, C_KV_WIDTH, lb)
    tm_p = _row_tile(seq, 512)
    tm_s = _row_tile(ts, 512)
    rt_p = _row_tile(tp // S5_Q, 256)
    rb_p = _row_tile(seq // S5_Q, 128)
    attn_blk = _row_tile(seq // WINDOW, 16)

    new_p, s5_s = [], []
    conv_new, ssd_new = jnp.zeros(conv_all.shape, F32), jnp.zeros(ssd_all.shape, F32)
    k_new, v_new = jnp.zeros(kc_all.shape, F32), jnp.zeros(vc_all.shape, F32)
    off = [0, 512, 1280, 1288, 1800, 2312, 2440, 2568, 5640]
    w_in_t = jnp.swapaxes(w_in, 1, 2)
    w_pack = jnp.concatenate(
        [w_in_t[:, off[0]:off[1]], w_in_t[:, off[1]:off[2]], w_in_t[:, off[3]:off[4]], w_in_t[:, off[4]:off[5]],
         w_in_t[:, off[5]:off[6]], w_in_t[:, off[6]:off[7]],
         jnp.pad(w_in_t[:, off[2]:off[3]], ((0, 0), (0, LANE - A_HEADS), (0, 0)))], axis=1).astype(BF16)
    stacked = dict(
        w_gate=w_in_t[:, off[7]:off[8]].astype(BF16), w_glu=s5_w_glu.astype(BF16),
        w_br_a=w_br_a.astype(BF16), w_br_b=w_br_b.astype(BF16), w_br_c=w_br_c.astype(BF16),
        w_out=w_out.astype(BF16), w_ffn_up=w_ffn_up.astype(BF16), w_ffn_down=w_ffn_down.astype(BF16))
    for i in range(depth):
        sink_rows = pair_rows(jnp.broadcast_to(attn_sinks[i].reshape(C_HEADS, 1, 1), (C_HEADS, valid, 1)))
        lw = dict(
            stacked, layer=i, norm1_w=norm1_w[i].reshape(1, D_MODEL),
            conv_w=conv_w[i], conv_b=conv_b[i].reshape(1, A_CONV_DIM),
            a_log=jnp.pad(ssd_a_log[i], (0, LANE - A_HEADS)).reshape(1, LANE),
            dt_bias=jnp.pad(ssd_dt_bias[i], (0, LANE - A_HEADS)).reshape(1, LANE),
            d_exp=jnp.repeat(ssd_d[i], A_HEAD_DIM).reshape(1, A_WIDTH),
            ssd_norm_w=ssd_norm_w[i].reshape(1, A_WIDTH),
            ssd_norm_wb=jnp.broadcast_to(ssd_norm_w[i].reshape(A_WIDTH, 1), (A_WIDTH, LANE)),
            s5_d=s5_d[i].reshape(1, B_WIDTH), norm2_w=norm2_w[i].reshape(1, D_MODEL),
        )
        s5w = _s5_weights(s5_lam_re[i], s5_lam_im[i], s5_log_dt[i], s5_b_re[i], s5_b_im[i],
                          s5_c_re[i], s5_c_im[i], valid)
        last = i == depth - 1

        u, q, k, v, ya, ssd_h, conv_tail = _inproj_ssd(xp, w_pack, lw, nb, tm_p)
        sre, sim = _s5_state(u, s5w["bst_re"], s5w["bst_im"], rt_p)
        hre, him, fre, fim = _s5_scan(sre, sim, s5w["aq_re"], s5w["aq_im"], nb, rb_p)
        yb = _s5_out(u, hre, him, s5w["toep"], s5w["cst_re"], s5w["cst_im"], rt_p)
        yc = _attn_prompt(q, k, v, band, attn_sinks[i], nb, seq // WINDOW, attn_blk)
        x1 = _merge(xp, ya, yb, u, yc, lw, _row_tile(seq, 2 * tm_p), False, True)
        xp = _ffn(x1, lw, fnw, _row_tile(seq, 2 * tm_p), last)
        new_p.append((
            ssd_h.reshape(nb, A_HEADS, A_HEAD_DIM, A_STATE),
            conv_tail[:, SUBLANE - (A_CONV - 1):],
            fre.reshape(nb, B_GROUPS, B_STATE), fim.reshape(nb, B_GROUPS, B_STATE),
            k.reshape(nb, seq, C_KV_WIDTH)[:, seq - WINDOW:].reshape(nb, WINDOW, C_KV_HEADS, C_HEAD_DIM),
            v.reshape(nb, seq, C_KV_WIDTH)[:, seq - WINDOW:].reshape(nb, WINDOW, C_KV_HEADS, C_HEAD_DIM)))

        z, xbc, u, q, k, v, dt = _inproj(xs, lw["norm1_w"], w_pack, i, tm_s)
        ya, conv_new, ssd_new = _ssd_sample(xbc, z, dt, conv_all, ssd_all, lw, i, valid, conv_new, ssd_new)
        av_re = jnp.broadcast_to(s5w["av_re"].reshape(S5_STATE_COLS, 1), (S5_STATE_COLS, nsb))
        av_im = jnp.broadcast_to(s5w["av_im"].reshape(S5_STATE_COLS, 1), (S5_STATE_COLS, nsb))
        yb, fre, fim = _s5_sample(u, s5w, s5re_all, s5im_all, av_re, av_im, i, valid)
        yc, k_new, v_new = _attn_sample(
            q.reshape(valid, nsb, C_WIDTH), k.reshape(valid, nsb, C_KV_WIDTH), v.reshape(valid, nsb, C_KV_WIDTH),
            kc_all, vc_all, bias_s1, bias_s2, sink_rows, i, k_new, v_new)
        x1 = _merge(xs, ya, yb, u, yc.reshape(ts, C_WIDTH), lw, tm_s, True, False)
        xs = _ffn(x1, lw, fnw, tm_s, last)
        s5_s.append((fre, fim))

    def stack(states, j):
        return jnp.stack([s[j] for s in states], axis=0)

    def s5_state(j):
        return jnp.transpose(stack(s5_s, j).reshape(depth, B_GROUPS, B_STATE, nsb), (0, 3, 1, 2))

    def cache(a):
        return jnp.transpose(a.reshape(depth, nsb, C_KV_HEADS, C_HEAD_DIM, lb), (0, 1, 4, 2, 3))

    y_prompt = xp.reshape(nb, seq, D_MODEL)
    y_sample = jnp.swapaxes(xs.reshape(valid, nsb, D_MODEL), 0, 1)
    return (y_prompt, y_sample,
            stack(new_p, 0), stack(new_p, 1), stack(new_p, 2), stack(new_p, 3), stack(new_p, 4), stack(new_p, 5),
            jnp.transpose(ssd_new, (0, 4, 1, 2, 3)), jnp.swapaxes(conv_new, 1, 2), s5_state(0), s5_state(1),
            cache(k_new), cache(v_new))
```
